```python
import jax, jax.numpy as jnp
from jax import lax
import numpy as np

D_MODEL = 1024
BATCH = 4
SEQ = 8192
DEPTH = 1
DEC_BATCH = 128
DEC_SEQ = 4
PAST_LEN = 16384
PAGE_SIZE = 128

HEAD_DIM = 64
D_MIX = D_MODEL
D_CONV = (3 * D_MIX) // 8
N_MEM_HEADS = 4
D_MEMQ = N_MEM_HEADS * HEAD_DIM
D_SWA = D_MIX - D_CONV - D_MEMQ
N_SWA_HEADS = D_SWA // HEAD_DIM
N_SWA_KV = 2
SWA_GROUP = N_SWA_HEADS // N_SWA_KV
D_SWA_KV = N_SWA_KV * HEAD_DIM
N_MEM = 256
CONV_WIDTH = 3
WINDOW = 128
BLOCK = 128
RMS_EPS = 1e-6
NEG_INF = -1e30
SPLIT_SIZES = (D_CONV, D_CONV, D_CONV, D_CONV, D_SWA, D_SWA_KV, D_SWA_KV, D_SWA, D_MEMQ, D_MEMQ)
D_IN = sum(SPLIT_SIZES)

kernel_name = "hymba_conv_swa_sink_alibi_memxattn_step"


def rms_norm(x, g):
    xf = x.astype(jnp.float32)
    y = xf * lax.rsqrt(jnp.mean(xf * xf, axis=-1, keepdims=True) + RMS_EPS) * g.astype(jnp.float32)
    return y.astype(x.dtype)


def alibi_slopes():
    h = np.arange(1, N_SWA_HEADS + 1, dtype=np.float32)
    return jnp.asarray(np.power(np.float32(2.0), -8.0 * h / N_SWA_HEADS).astype(np.float32))


def project_in(x, pre_g, w_in):
    h = rms_norm(x, pre_g)
    z = h @ w_in
    points = [int(p) for p in np.cumsum(SPLIT_SIZES)[:-1]]
    return jnp.split(z, points, axis=-1)


def project_out(x, y_conv, y_swa, y_mem, post_g, w_out):
    y = jnp.concatenate([y_conv, y_swa, y_mem], axis=-1) @ w_out
    return x + rms_norm(y, post_g)


def short_conv(u_full, conv_w):
    T = u_full.shape[1] - (CONV_WIDTH - 1)
    out = conv_w[0] * u_full[:, 0:T]
    for j in range(1, CONV_WIDTH):
        out = out + conv_w[j] * u_full[:, j:j + T]
    return out


def swa_attention(q, k, v, q_pos, k_pos, sinks):
    N, nb, Tq = q.shape[:3]
    qg = q.reshape(N, nb, Tq, N_SWA_KV, SWA_GROUP, HEAD_DIM).astype(jnp.float32)
    s = jnp.einsum('nbqkgd,nbskd->nbkgqs', qg, k.astype(jnp.float32)) * (HEAD_DIM ** -0.5)
    dist = q_pos[:, :, None] - k_pos[:, None, :]
    valid = (dist >= 0) & (dist < WINDOW) & (k_pos[:, None, :] >= 0)
    slopes = alibi_slopes().reshape(N_SWA_KV, SWA_GROUP)
    s = s - slopes[None, None, :, :, None, None] * jnp.abs(dist).astype(jnp.float32)[None, :, None, None]
    s = jnp.where(valid[None, :, None, None], s, NEG_INF)
    sink = sinks.astype(jnp.float32).reshape(N_SWA_KV, SWA_GROUP)[None, None, :, :, None, None]
    sink = jnp.broadcast_to(sink, s.shape[:-1] + (1,))
    p = jax.nn.softmax(jnp.concatenate([s, sink], axis=-1), axis=-1)[..., :-1]
    o = jnp.einsum('nbkgqs,nbskd->nbqkgd', p, v.astype(jnp.float32))
    return o.reshape(N, nb, Tq, D_SWA).astype(q.dtype)


def mem_attention(q, mk, mv):
    N, T = q.shape[:2]
    qh = q.reshape(N, T, N_MEM_HEADS, HEAD_DIM).astype(jnp.float32)
    s = jnp.einsum('nthd,nmhd->nhtm', qh, mk.astype(jnp.float32)) * (HEAD_DIM ** -0.5)
    p = jax.nn.softmax(s, axis=-1)
    o = jnp.einsum('nhtm,nmhd->nthd', p, mv.astype(jnp.float32))
    return o.reshape(N, T, D_MEMQ).astype(q.dtype)


def mem_kv(mem, mem_g, w_mk, w_mv):
    N, M = mem.shape[:2]
    m = rms_norm(mem, mem_g)
    mk = (m @ w_mk).reshape(N, M, N_MEM_HEADS, HEAD_DIM)
    mv = (m @ w_mv).reshape(N, M, N_MEM_HEADS, HEAD_DIM)
    return mk, mv


def prompt_layer(x, mem, pre_g, post_g, w_in, conv_w, sinks, mem_g, w_mk, w_mv, w_out):
    B, T = x.shape[:2]
    cb, cc, ch, cz, q, k, v, sz, mq, mz = project_in(x, pre_g, w_in)
    u = cc * ch
    u_full = jnp.pad(u, ((0, 0), (CONV_WIDTH - 1, 0), (0, 0)))
    y_conv = cb * short_conv(u_full, conv_w) * jax.nn.silu(cz)
    conv_state = u[:, T - (CONV_WIDTH - 1):]
    nb = T // BLOCK
    k = k.reshape(B, T, N_SWA_KV, HEAD_DIM)
    v = v.reshape(B, T, N_SWA_KV, HEAD_DIM)
    pad = ((0, 0), (BLOCK, 0), (0, 0), (0, 0))
    kb = jnp.pad(k, pad).reshape(B, nb + 1, BLOCK, N_SWA_KV, HEAD_DIM)
    vb = jnp.pad(v, pad).reshape(B, nb + 1, BLOCK, N_SWA_KV, HEAD_DIM)
    k_band = jnp.concatenate([kb[:, :-1], kb[:, 1:]], axis=2)
    v_band = jnp.concatenate([vb[:, :-1], vb[:, 1:]], axis=2)
    qb = q.reshape(B, nb, BLOCK, N_SWA_HEADS, HEAD_DIM)
    q_pos = jnp.arange(T, dtype=jnp.int32).reshape(nb, BLOCK)
    k_pos = (jnp.arange(nb, dtype=jnp.int32)[:, None] * BLOCK - BLOCK
             + jnp.arange(2 * BLOCK, dtype=jnp.int32)[None, :])
    y_swa = swa_attention(qb, k_band, v_band, q_pos, k_pos, sinks).reshape(B, T, D_SWA) * jax.nn.silu(sz)
    keep = min(WINDOW, T)
    swa_k_state = k[:, T - keep:]
    swa_v_state = v[:, T - keep:]
    mk, mv = mem_kv(mem, mem_g, w_mk, w_mv)
    y_mem = mem_attention(mq, mk, mv) * jax.nn.silu(mz)
    y = project_out(x, y_conv, y_swa, y_mem, post_g, w_out)
    return y, conv_state, swa_k_state, swa_v_state, mk, mv


def sample_layer(x, conv_past, swa_k_past, swa_v_past, mk, mv, pre_g, post_g, w_in, conv_w, sinks, w_out):
    N, T = x.shape[:2]
    cb, cc, ch, cz, q, k, v, sz, mq, mz = project_in(x, pre_g, w_in)
    u = cc * ch
    u_full = jnp.concatenate([conv_past.astype(u.dtype), u], axis=1)
    y_conv = cb * short_conv(u_full, conv_w) * jax.nn.silu(cz)
    conv_state = u_full[:, u_full.shape[1] - (CONV_WIDTH - 1):]
    buf = swa_k_past.shape[1]
    k = k.reshape(N, T, N_SWA_KV, HEAD_DIM)
    v = v.reshape(N, T, N_SWA_KV, HEAD_DIM)
    k_all = jnp.concatenate([swa_k_past.astype(k.dtype), k], axis=1)
    v_all = jnp.concatenate([swa_v_past.astype(v.dtype), v], axis=1)
    q_pos = (PAST_LEN + jnp.arange(T, dtype=jnp.int32))[None]
    k_pos = jnp.concatenate([PAST_LEN - buf + jnp.arange(buf, dtype=jnp.int32),
                             PAST_LEN + jnp.arange(T, dtype=jnp.int32)])[None]
    qb = q.reshape(N, 1, T, N_SWA_HEADS, HEAD_DIM)
    y_swa = swa_attention(qb, k_all[:, None], v_all[:, None], q_pos, k_pos, sinks).reshape(N, T, D_SWA)
    y_swa = y_swa * jax.nn.silu(sz)
    swa_k_state = k_all[:, k_all.shape[1] - buf:]
    swa_v_state = v_all[:, v_all.shape[1] - buf:]
    y_mem = mem_attention(mq, mk, mv) * jax.nn.silu(mz)
    y = project_out(x, y_conv, y_swa, y_mem, post_g, w_out)
    return y, conv_state, swa_k_state, swa_v_state


def setup_inputs(seed: int = 0) -> dict:
    key = jax.random.key(seed)
    ks = jax.random.split(key, 20)
    f32 = jnp.float32
    swa_buf = min(WINDOW, PAST_LEN)
    nrm = lambda k, shape, s=1.0: jax.random.normal(k, shape, f32) * s
    return {
        "x_prompt": nrm(ks[0], (BATCH, SEQ, D_MODEL)),
        "x_sample": nrm(ks[1], (DEC_BATCH, DEC_SEQ, D_MODEL)),
        "mem_prompt": nrm(ks[2], (BATCH, N_MEM, D_MODEL)),
        "state_conv": nrm(ks[3], (DEPTH, DEC_BATCH, CONV_WIDTH - 1, D_CONV)),
        "cache_swa_k": nrm(ks[4], (DEPTH, DEC_BATCH, swa_buf, N_SWA_KV, HEAD_DIM)),
        "cache_swa_v": nrm(ks[5], (DEPTH, DEC_BATCH, swa_buf, N_SWA_KV, HEAD_DIM)),
        "cache_mem_k": nrm(ks[6], (DEPTH, DEC_BATCH, N_MEM, N_MEM_HEADS, HEAD_DIM)),
        "cache_mem_v": nrm(ks[7], (DEPTH, DEC_BATCH, N_MEM, N_MEM_HEADS, HEAD_DIM)),
        "pre_norm_g": 1.0 + nrm(ks[8], (DEPTH, D_MODEL), 0.05),
        "post_norm_g": 1.0 + nrm(ks[9], (DEPTH, D_MODEL), 0.05),
        "w_in": nrm(ks[10], (DEPTH, D_MODEL, D_IN), D_MODEL ** -0.5),
        "conv_w": nrm(ks[11], (DEPTH, CONV_WIDTH, D_CONV), CONV_WIDTH ** -0.5),
        "attn_sinks": nrm(ks[12], (DEPTH, N_SWA_HEADS), 0.5),
        "mem_norm_g": 1.0 + nrm(ks[13], (DEPTH, D_MODEL), 0.05),
        "w_mem_k": nrm(ks[14], (DEPTH, D_MODEL, D_MEMQ), D_MODEL ** -0.5),
        "w_mem_v": nrm(ks[15], (DEPTH, D_MODEL, D_MEMQ), D_MODEL ** -0.5),
        "w_out": nrm(ks[16], (DEPTH, D_MIX, D_MODEL), D_MIX ** -0.5),
    }


def reference(x_prompt, x_sample, mem_prompt, state_conv, cache_swa_k, cache_swa_v, cache_mem_k, cache_mem_v,
              pre_norm_g, post_norm_g, w_in, conv_w, attn_sinks, mem_norm_g, w_mem_k, w_mem_v, w_out):
    xp, xs = x_prompt, x_sample
    cp, kp, vp, mkp, mvp, cs, ks_, vs_ = [], [], [], [], [], [], [], []
    for l in range(DEPTH):
        xp, c1, k1, v1, mk1, mv1 = prompt_layer(xp, mem_prompt, pre_norm_g[l], post_norm_g[l], w_in[l], conv_w[l],
                                                attn_sinks[l], mem_norm_g[l], w_mem_k[l], w_mem_v[l], w_out[l])
        xs, c2, k2, v2 = sample_layer(xs, state_conv[l], cache_swa_k[l], cache_swa_v[l], cache_mem_k[l],
                                      cache_mem_v[l], pre_norm_g[l], post_norm_g[l], w_in[l], conv_w[l],
                                      attn_sinks[l], w_out[l])
        cp.append(c1); kp.append(k1); vp.append(v1); mkp.append(mk1); mvp.append(mv1)
        cs.append(c2); ks_.append(k2); vs_.append(v2)
    return (xp, xs, jnp.stack(cp), jnp.stack(kp), jnp.stack(vp), jnp.stack(mkp), jnp.stack(mvp),
            jnp.stack(cs), jnp.stack(ks_), jnp.stack(vs_))
```

```python
import functools
import math

import numpy as np
import jax
import jax.numpy as jnp
from jax import lax
from jax.experimental import pallas as pl
from jax.experimental.pallas import tpu as pltpu

D_MODEL = 1024
HEAD_DIM = 64
D_CONV = 384
N_MEM_HEADS = 4
D_MEMQ = N_MEM_HEADS * HEAD_DIM
D_SWA = 384
N_SWA_HEADS = 6
N_SWA_KV = 2
SWA_GROUP = N_SWA_HEADS // N_SWA_KV
D_SWA_KV = N_SWA_KV * HEAD_DIM
N_MEM = 256
CONV_WIDTH = 3
WINDOW = 128
BLOCK = 128
RMS_EPS = 1e-6
NEG_INF = -1e30
D_IN = 3072
QK_SCALE = HEAD_DIM ** -0.5

OFF_CB, OFF_CC, OFF_CH, OFF_CZ = 0, 384, 768, 1152
OFF_Q, OFF_K, OFF_V, OFF_SZ = 1536, 1920, 2048, 2176
OFF_MQ, OFF_MZ = 2560, 2816
YOFF_CONV, YOFF_SWA, YOFF_MEM = 0, 384, 768

V7X_VMEM_LIMIT_BYTES = 56 * 1024 * 1024

PROMPT_TQ = 512
SAMPLE_NS = 16
SAMPLE_ROWS = 8
SAMPLE_KEYS = 256


def _alibi_slope(h):
    return float(np.power(np.float32(2.0), np.float32(-8.0 * (h + 1) / N_SWA_HEADS)))


def _rms_norm(x, g):
    return x * lax.rsqrt(jnp.mean(x * x, axis=-1, keepdims=True) + RMS_EPS) * g


def _silu(x):
    return x * jax.nn.sigmoid(x)


def _mem_kv_kernel(mem_ref, g_ref, wk_ref, wv_ref, mk_ref, mv_ref, mkb_ref, mvt_ref):
    m = _rms_norm(mem_ref[0], g_ref[...]).astype(jnp.bfloat16)
    mk = jnp.dot(m, wk_ref[...], preferred_element_type=jnp.float32)
    mv = jnp.dot(m, wv_ref[...], preferred_element_type=jnp.float32)
    mk_ref[0] = mk
    mv_ref[0] = mv
    mkb_ref[0] = mk.astype(jnp.bfloat16)
    mvt_ref[0] = mv.T.astype(jnp.bfloat16)


def _mem_kv(mem, mem_g, w_mk, w_mv):
    B = mem.shape[0]
    full = lambda shape: pl.BlockSpec(shape, lambda b: (0,) * len(shape))
    return pl.pallas_call(
        _mem_kv_kernel,
        grid=(B,),
        in_specs=[
            pl.BlockSpec((1, N_MEM, D_MODEL), lambda b: (b, 0, 0)),
            full((1, D_MODEL)),
            full((D_MODEL, D_MEMQ)),
            full((D_MODEL, D_MEMQ)),
        ],
        out_specs=[
            pl.BlockSpec((1, N_MEM, D_MEMQ), lambda b: (b, 0, 0)),
            pl.BlockSpec((1, N_MEM, D_MEMQ), lambda b: (b, 0, 0)),
            pl.BlockSpec((1, N_MEM, D_MEMQ), lambda b: (b, 0, 0)),
            pl.BlockSpec((1, D_MEMQ, N_MEM), lambda b: (b, 0, 0)),
        ],
        out_shape=[
            jax.ShapeDtypeStruct((B, N_MEM, D_MEMQ), jnp.float32),
            jax.ShapeDtypeStruct((B, N_MEM, D_MEMQ), jnp.float32),
            jax.ShapeDtypeStruct((B, N_MEM, D_MEMQ), jnp.bfloat16),
            jax.ShapeDtypeStruct((B, D_MEMQ, N_MEM), jnp.bfloat16),
        ],
        compiler_params=pltpu.CompilerParams(dimension_semantics=("arbitrary",)),
        name="mem_kv",
    )(mem, mem_g.reshape(1, D_MODEL), w_mk.astype(jnp.bfloat16), w_mv.astype(jnp.bfloat16))


def _prompt_kernel(x_ref, pre_g_ref, post_g_ref, w_in_t_ref, conv_w_t_ref, sink_ref, mkb_ref, mvt_ref, w_out_ref,
                   y_ref, conv_state_ref, k_state_ref, v_state_ref,
                   zt_ref, ycat_ref, kbuf_ref, vbuf_ref, uprev_ref, bias_ref, *, tq):
    t = pl.program_id(1)
    nblk = tq // BLOCK

    @pl.when((pl.program_id(0) == 0) & (t == 0))
    def _():
        c = lax.broadcasted_iota(jnp.int32, (2 * BLOCK, BLOCK), 0)
        r = lax.broadcasted_iota(jnp.int32, (2 * BLOCK, BLOCK), 1)
        dist = r + BLOCK - c
        valid = (dist >= 0) & (dist < WINDOW)
        distf = dist.astype(jnp.float32)
        for h in range(N_SWA_HEADS):
            g, i = divmod(h, SWA_GROUP)
            bias_ref[g, :, i * BLOCK:(i + 1) * BLOCK] = jnp.where(valid, -_alibi_slope(h) * distf, NEG_INF)

    @pl.when(t == 0)
    def _():
        kbuf_ref[0:BLOCK, :] = jnp.zeros((BLOCK, D_SWA_KV), jnp.bfloat16)
        vbuf_ref[:, 0:BLOCK] = jnp.zeros((D_SWA_KV, BLOCK), jnp.bfloat16)
        uprev_ref[...] = jnp.zeros_like(uprev_ref)

    x = x_ref[0]
    h = _rms_norm(x, pre_g_ref[...]).astype(jnp.bfloat16)
    zt_ref[...] = lax.dot_general(w_in_t_ref[...], h, (((1,), (1,)), ((), ())),
                                  preferred_element_type=jnp.float32)

    u = zt_ref[OFF_CC:OFF_CC + D_CONV, :] * zt_ref[OFF_CH:OFF_CH + D_CONV, :]
    ucat = jnp.concatenate([uprev_ref[...], u], axis=1)
    cw = conv_w_t_ref[...]
    conv = (cw[:, 0:1] * pltpu.roll(ucat, 2, axis=1)[:, BLOCK:]
            + cw[:, 1:2] * pltpu.roll(ucat, 1, axis=1)[:, BLOCK:]
            + cw[:, 2:3] * u)
    y_conv = zt_ref[OFF_CB:OFF_CB + D_CONV, :] * conv * _silu(zt_ref[OFF_CZ:OFF_CZ + D_CONV, :])
    ycat_ref[YOFF_CONV:YOFF_CONV + D_CONV, :] = y_conv.astype(jnp.bfloat16)
    u_last = u[:, tq - BLOCK:]
    uprev_ref[...] = u_last
    conv_state_ref[0] = u_last.T[BLOCK - (CONV_WIDTH - 1):, :]

    k_nat = zt_ref[OFF_K:OFF_K + D_SWA_KV, :].T
    kbuf_ref[BLOCK:, :] = k_nat.astype(jnp.bfloat16)
    vbuf_ref[:, BLOCK:] = zt_ref[OFF_V:OFF_V + D_SWA_KV, :].astype(jnp.bfloat16)
    k_state_ref[0] = k_nat[tq - BLOCK:, :]
    v_state_ref[0] = zt_ref[OFF_V:OFF_V + D_SWA_KV, tq - BLOCK:].T

    first_pen = jnp.where(t == 0, NEG_INF, 0.0)
    key_is_prev = lax.broadcasted_iota(jnp.int32, (2 * BLOCK, SWA_GROUP * BLOCK), 0) < BLOCK
    q_zero = jnp.zeros((HEAD_DIM, SWA_GROUP * BLOCK), jnp.bfloat16)
    for j in range(nblk):
        cols = slice(j * BLOCK, (j + 1) * BLOCK)
        for g in range(N_SWA_KV):
            q0 = OFF_Q + g * SWA_GROUP * HEAD_DIM
            qt = jnp.concatenate(
                [zt_ref[q0 + i * HEAD_DIM:q0 + (i + 1) * HEAD_DIM, cols] for i in range(SWA_GROUP)],
                axis=1)
            qt = (qt * QK_SCALE).astype(jnp.bfloat16)
            qt = jnp.concatenate([qt, q_zero] if g == 0 else [q_zero, qt], axis=0)
            kband = kbuf_ref[j * BLOCK:(j + 2) * BLOCK, :]
            s = jnp.dot(kband, qt, preferred_element_type=jnp.float32)
            s = s + bias_ref[g]
            if j == 0:
                s = s + jnp.where(key_is_prev, first_pen, 0.0)
            sink = sink_ref[g:g + 1, :]
            m = jnp.maximum(jnp.max(s, axis=0, keepdims=True), sink)
            p = jnp.exp(s - m)
            l = jnp.sum(p, axis=0, keepdims=True) + jnp.exp(sink - m)
            vband = vbuf_ref[g * HEAD_DIM:(g + 1) * HEAD_DIM, j * BLOCK:(j + 2) * BLOCK]
            o = jnp.dot(vband, p.astype(jnp.bfloat16), preferred_element_type=jnp.float32)
            o = o / l
            for i in range(SWA_GROUP):
                hh = g * SWA_GROUP + i
                gate = _silu(zt_ref[OFF_SZ + hh * HEAD_DIM:OFF_SZ + (hh + 1) * HEAD_DIM, cols])
                ycat_ref[YOFF_SWA + hh * HEAD_DIM:YOFF_SWA + (hh + 1) * HEAD_DIM, cols] = (
                    o[:, i * BLOCK:(i + 1) * BLOCK] * gate).astype(jnp.bfloat16)

    kbuf_ref[0:BLOCK, :] = kbuf_ref[tq:tq + BLOCK, :]
    vbuf_ref[:, 0:BLOCK] = vbuf_ref[:, tq:tq + BLOCK]

    for hh in range(N_MEM_HEADS):
        rows = slice(OFF_MQ + hh * HEAD_DIM, OFF_MQ + (hh + 1) * HEAD_DIM)
        qt = (zt_ref[rows, :] * QK_SCALE).astype(jnp.bfloat16)
        pieces = [jnp.zeros((HEAD_DIM, tq), jnp.bfloat16)] * N_MEM_HEADS
        pieces[hh] = qt
        s = jnp.dot(mkb_ref[0], jnp.concatenate(pieces, axis=0),
                    preferred_element_type=jnp.float32)
        m = jnp.max(s, axis=0, keepdims=True)
        p = jnp.exp(s - m)
        l = jnp.sum(p, axis=0, keepdims=True)
        o = jnp.dot(mvt_ref[0, hh * HEAD_DIM:(hh + 1) * HEAD_DIM, :], p.astype(jnp.bfloat16),
                    preferred_element_type=jnp.float32)
        gate = _silu(zt_ref[OFF_MZ + hh * HEAD_DIM:OFF_MZ + (hh + 1) * HEAD_DIM, :])
        ycat_ref[YOFF_MEM + hh * HEAD_DIM:YOFF_MEM + (hh + 1) * HEAD_DIM, :] = (o / l * gate).astype(jnp.bfloat16)

    y = lax.dot_general(ycat_ref[...], w_out_ref[...], (((0,), (0,)), ((), ())),
                        preferred_element_type=jnp.float32)
    y_ref[0] = x + _rms_norm(y, post_g_ref[...])


def _prompt_layer(x, pre_g, post_g, w_in_t, conv_w_t, sink_rows, mkb, mvt, w_out, *, tq):
    B, T, _ = x.shape
    full = lambda shape: pl.BlockSpec(shape, lambda b, t: (0,) * len(shape))
    kernel = functools.partial(_prompt_kernel, tq=tq)
    return pl.pallas_call(
        kernel,
        grid=(B, T // tq),
        in_specs=[
            pl.BlockSpec((1, tq, D_MODEL), lambda b, t: (b, t, 0)),
            full((1, D_MODEL)),
            full((1, D_MODEL)),
            full((D_IN, D_MODEL)),
            full((D_CONV, CONV_WIDTH)),
            full((N_SWA_KV, SWA_GROUP * BLOCK)),
            pl.BlockSpec((1, N_MEM, D_MEMQ), lambda b, t: (b, 0, 0)),
            pl.BlockSpec((1, D_MEMQ, N_MEM), lambda b, t: (b, 0, 0)),
            full((D_MODEL, D_MODEL)),
        ],
        out_specs=[
            pl.BlockSpec((1, tq, D_MODEL), lambda b, t: (b, t, 0)),
            pl.BlockSpec((1, CONV_WIDTH - 1, D_CONV), lambda b, t: (b, 0, 0)),
            pl.BlockSpec((1, BLOCK, D_SWA_KV), lambda b, t: (b, 0, 0)),
            pl.BlockSpec((1, BLOCK, D_SWA_KV), lambda b, t: (b, 0, 0)),
        ],
        out_shape=[
            jax.ShapeDtypeStruct((B, T, D_MODEL), jnp.float32),
            jax.ShapeDtypeStruct((B, CONV_WIDTH - 1, D_CONV), jnp.float32),
            jax.ShapeDtypeStruct((B, BLOCK, D_SWA_KV), jnp.float32),
            jax.ShapeDtypeStruct((B, BLOCK, D_SWA_KV), jnp.float32),
        ],
        scratch_shapes=[
            pltpu.VMEM((D_IN, tq), jnp.float32),
            pltpu.VMEM((D_MODEL, tq), jnp.bfloat16),
            pltpu.VMEM((BLOCK + tq, D_SWA_KV), jnp.bfloat16),
            pltpu.VMEM((D_SWA_KV, BLOCK + tq), jnp.bfloat16),
            pltpu.VMEM((D_CONV, BLOCK), jnp.float32),
            pltpu.VMEM((N_SWA_KV, 2 * BLOCK, SWA_GROUP * BLOCK), jnp.float32),
        ],
        compiler_params=pltpu.CompilerParams(
            dimension_semantics=("arbitrary", "arbitrary"),
            vmem_limit_bytes=V7X_VMEM_LIMIT_BYTES),
        name="prompt_layer",
    )(x, pre_g, post_g, w_in_t, conv_w_t, sink_rows, mkb, mvt, w_out)


def _sample_kernel(x_ref, conv_past_ref, ck_ref, cv_ref, mk_ref, mv_ref,
                   pre_g_ref, post_g_ref, w_in_ref, conv_w_ref, sink_ref, w_out_ref,
                   y_ref, conv_state_ref, k_state_ref, v_state_ref,
                   z_ref, ycat_ref, kall_ref, vall_ref, *, ns):
    R = SAMPLE_ROWS
    half = R // 2

    @pl.when(pl.program_id(0) == 0)
    def _():
        kall_ref[...] = jnp.zeros_like(kall_ref)
        vall_ref[...] = jnp.zeros_like(vall_ref)

    x = x_ref[...]
    h = _rms_norm(x, pre_g_ref[...]).astype(jnp.bfloat16)
    z_ref[...] = jnp.dot(h, w_in_ref[...], preferred_element_type=jnp.float32)

    u = (z_ref[:, OFF_CC:OFF_CC + D_CONV] * z_ref[:, OFF_CH:OFF_CH + D_CONV]).reshape(ns, R, D_CONV)
    row3 = lax.broadcasted_iota(jnp.int32, (ns, R, D_CONV), 1)
    u_full = jnp.where(row3 < CONV_WIDTH - 1, conv_past_ref[...], pltpu.roll(u, CONV_WIDTH - 1, axis=1))
    cw = conv_w_ref[...]
    conv = (cw[0:1, :] * u_full
            + cw[1:2, :] * pltpu.roll(u_full, R - 1, axis=1)
            + cw[2:3, :] * pltpu.roll(u_full, R - 2, axis=1))
    conv_state_ref[...] = pltpu.roll(u_full, R - half, axis=1)[:, 0:CONV_WIDTH - 1, :]
    y_conv = (z_ref[:, OFF_CB:OFF_CB + D_CONV] * conv.reshape(ns * R, D_CONV)
              * _silu(z_ref[:, OFF_CZ:OFF_CZ + D_CONV]))
    ycat_ref[:, YOFF_CONV:YOFF_CONV + D_CONV] = y_conv

    row = lax.broadcasted_iota(jnp.int32, (R, 2 * HEAD_DIM), 0)
    lane = lax.broadcasted_iota(jnp.int32, (R, 2 * HEAD_DIM), 1)
    lo_row = row < half
    lo_lane = lane < HEAD_DIM
    diag = lo_row == lo_lane

    rr = lax.broadcasted_iota(jnp.int32, (R, SAMPLE_KEYS), 0)
    cc = lax.broadcasted_iota(jnp.int32, (R, SAMPLE_KEYS), 1)
    tok = rr % half
    dist = jnp.where(cc < WINDOW, tok + WINDOW - cc, tok - (cc - WINDOW))
    valid = (dist >= 0) & (dist < WINDOW) & (cc < WINDOW + half)
    distf = dist.astype(jnp.float32)
    swa_bias = []
    for pair in range(N_SWA_HEADS // 2):
        slope = jnp.where(rr < half, _alibi_slope(2 * pair), _alibi_slope(2 * pair + 1))
        swa_bias.append(jnp.where(valid, -slope * distf, NEG_INF))
    swa_bias = jnp.concatenate(swa_bias, axis=0)
    sink_col = sink_ref[...]

    def body(n, carry):
        r0 = pl.multiple_of(n * R, R)
        rows = pl.ds(r0, R)

        qa = z_ref[rows, OFF_Q:OFF_Q + 128] * QK_SCALE
        qb = z_ref[rows, OFF_Q + 128:OFF_Q + 256] * QK_SCALE
        qc = z_ref[rows, OFF_Q + 256:OFF_Q + 384] * QK_SCALE
        t0 = jnp.where(lo_lane, jnp.where(lo_row, qa, pltpu.roll(qa, HEAD_DIM, axis=1)), 0.0)
        t1 = jnp.where(diag, qb, 0.0)
        t2 = jnp.where(lo_lane, 0.0, jnp.where(lo_row, pltpu.roll(qc, HEAD_DIM, axis=1), qc))
        qs = jnp.concatenate([t0, t1, t2], axis=0).astype(jnp.bfloat16)

        k_new = z_ref[rows, OFF_K:OFF_K + D_SWA_KV]
        v_new = z_ref[rows, OFF_V:OFF_V + D_SWA_KV]
        k_old = ck_ref[n]
        v_old = cv_ref[n]
        kall_ref[0:WINDOW, :] = k_old.astype(jnp.bfloat16)
        vall_ref[0:WINDOW, :] = v_old.astype(jnp.bfloat16)
        kall_ref[WINDOW:WINDOW + R, :] = k_new.astype(jnp.bfloat16)
        vall_ref[WINDOW:WINDOW + R, :] = v_new.astype(jnp.bfloat16)

        s = lax.dot_general(qs, kall_ref[...], (((1,), (1,)), ((), ())),
                            preferred_element_type=jnp.float32)
        s = s + swa_bias
        m = jnp.maximum(jnp.max(s, axis=1, keepdims=True), sink_col)
        p = jnp.exp(s - m)
        l = jnp.sum(p, axis=1, keepdims=True) + jnp.exp(sink_col - m)
        o = jnp.dot(p.astype(jnp.bfloat16), vall_ref[...], preferred_element_type=jnp.float32) / l
        o0, o1, o2 = o[0:R], o[R:2 * R], o[2 * R:3 * R]
        ya = jnp.where(lo_lane, o0, pltpu.roll(pltpu.roll(o0, HEAD_DIM, axis=1), half, axis=0))
        yb = jnp.where(lo_lane, o1, pltpu.roll(o1, half, axis=0))
        yc = jnp.where(lo_lane, pltpu.roll(o2, HEAD_DIM, axis=1), pltpu.roll(o2, half, axis=0))
        y_swa = jnp.concatenate([ya, yb, yc], axis=1) * _silu(z_ref[rows, OFF_SZ:OFF_SZ + D_SWA])
        ycat_ref[rows, YOFF_SWA:YOFF_SWA + D_SWA] = y_swa

        k_shift = pltpu.roll(k_old, WINDOW - half, axis=0)
        v_shift = pltpu.roll(v_old, WINDOW - half, axis=0)
        k_state_ref[n, 0:WINDOW - R, :] = k_shift[0:WINDOW - R]
        v_state_ref[n, 0:WINDOW - R, :] = v_shift[0:WINDOW - R]
        k_state_ref[n, WINDOW - R:WINDOW, :] = jnp.where(lo_row, k_shift[WINDOW - R:], k_new)
        v_state_ref[n, WINDOW - R:WINDOW, :] = jnp.where(lo_row, v_shift[WINDOW - R:], v_new)

        m0 = z_ref[rows, OFF_MQ:OFF_MQ + 128] * QK_SCALE
        m1 = z_ref[rows, OFF_MQ + 128:OFF_MQ + 256] * QK_SCALE
        zero = jnp.zeros_like(m0)
        qm = jnp.concatenate(
            [jnp.concatenate([jnp.where(diag, m0, 0.0), zero], axis=1),
             jnp.concatenate([zero, jnp.where(diag, m1, 0.0)], axis=1)], axis=0).astype(jnp.bfloat16)
        s = lax.dot_general(qm, mk_ref[n].astype(jnp.bfloat16), (((1,), (1,)), ((), ())),
                            preferred_element_type=jnp.float32)
        m = jnp.max(s, axis=1, keepdims=True)
        p = jnp.exp(s - m)
        l = jnp.sum(p, axis=1, keepdims=True)
        o = jnp.dot(p.astype(jnp.bfloat16), mv_ref[n].astype(jnp.bfloat16),
                    preferred_element_type=jnp.float32) / l
        oa, ob = o[0:R, 0:128], o[R:2 * R, 128:256]
        y_mem = jnp.concatenate([jnp.where(lo_lane, oa, pltpu.roll(oa, half, axis=0)),
                                 jnp.where(lo_lane, ob, pltpu.roll(ob, half, axis=0))], axis=1)
        ycat_ref[rows, YOFF_MEM:YOFF_MEM + D_MEMQ] = y_mem * _silu(z_ref[rows, OFF_MZ:OFF_MZ + D_MEMQ])
        return carry

    lax.fori_loop(0, ns, body, 0)

    y = jnp.dot(ycat_ref[...].astype(jnp.bfloat16), w_out_ref[...], preferred_element_type=jnp.float32)
    out = x + _rms_norm(y, post_g_ref[...])
    y_ref[...] = out.reshape(ns, R, D_MODEL)[:, 0:half, :]


def _sample_layer(x8, conv_past8, ck, cv, mk, mv, pre_g, post_g, w_in, conv_w, sink_col, w_out, *, ns):
    N = ck.shape[0]
    R = SAMPLE_ROWS
    full = lambda shape: pl.BlockSpec(shape, lambda i: (0,) * len(shape))
    kernel = functools.partial(_sample_kernel, ns=ns)
    return pl.pallas_call(
        kernel,
        grid=(N // ns,),
        in_specs=[
            pl.BlockSpec((ns * R, D_MODEL), lambda i: (i, 0)),
            pl.BlockSpec((ns, R, D_CONV), lambda i: (i, 0, 0)),
            pl.BlockSpec((ns, WINDOW, D_SWA_KV), lambda i: (i, 0, 0)),
            pl.BlockSpec((ns, WINDOW, D_SWA_KV), lambda i: (i, 0, 0)),
            pl.BlockSpec((ns, N_MEM, D_MEMQ), lambda i: (i, 0, 0)),
            pl.BlockSpec((ns, N_MEM, D_MEMQ), lambda i: (i, 0, 0)),
            full((1, D_MODEL)),
            full((1, D_MODEL)),
            full((D_MODEL, D_IN)),
            full((CONV_WIDTH, D_CONV)),
            full((N_SWA_HEADS * R // 2, 1)),
            full((D_MODEL, D_MODEL)),
        ],
        out_specs=[
            pl.BlockSpec((ns, R // 2, D_MODEL), lambda i: (i, 0, 0)),
            pl.BlockSpec((ns, CONV_WIDTH - 1, D_CONV), lambda i: (i, 0, 0)),
            pl.BlockSpec((ns, WINDOW, D_SWA_KV), lambda i: (i, 0, 0)),
            pl.BlockSpec((ns, WINDOW, D_SWA_KV), lambda i: (i, 0, 0)),
        ],
        out_shape=[
            jax.ShapeDtypeStruct((N, R // 2, D_MODEL), jnp.float32),
            jax.ShapeDtypeStruct((N, CONV_WIDTH - 1, D_CONV), jnp.float32),
            jax.ShapeDtypeStruct((N, WINDOW, D_SWA_KV), jnp.float32),
            jax.ShapeDtypeStruct((N, WINDOW, D_SWA_KV), jnp.float32),
        ],
        scratch_shapes=[
            pltpu.VMEM((ns * R, D_IN), jnp.float32),
            pltpu.VMEM((ns * R, D_MODEL), jnp.float32),
            pltpu.VMEM((SAMPLE_KEYS, D_SWA_KV), jnp.bfloat16),
            pltpu.VMEM((SAMPLE_KEYS, D_SWA_KV), jnp.bfloat16),
        ],
        compiler_params=pltpu.CompilerParams(
            dimension_semantics=("arbitrary",),
            vmem_limit_bytes=V7X_VMEM_LIMIT_BYTES),
        name="sample_layer",
    )(x8, conv_past8, ck, cv, mk, mv, pre_g, post_g, w_in, conv_w, sink_col, w_out)


def kernel(x_prompt, x_sample, mem_prompt, state_conv, cache_swa_k, cache_swa_v, cache_mem_k, cache_mem_v,
           pre_norm_g, post_norm_g, w_in, conv_w, attn_sinks, mem_norm_g, w_mem_k, w_mem_v, w_out):
    depth = w_in.shape[0]
    assert depth == 1, "layer stacking is not needed for the stated shapes"
    B, T, _ = x_prompt.shape
    N, TS, _ = x_sample.shape
    assert TS == SAMPLE_ROWS // 2 and cache_swa_k.shape[2] == WINDOW
    l = 0

    pre_g = pre_norm_g[l].reshape(1, D_MODEL)
    post_g = post_norm_g[l].reshape(1, D_MODEL)
    w_in_bf = w_in[l].astype(jnp.bfloat16)
    w_out_bf = w_out[l].astype(jnp.bfloat16)
    sinks = attn_sinks[l].astype(jnp.float32)

    mk, mv, mkb, mvt = _mem_kv(mem_prompt, mem_norm_g[l], w_mem_k[l], w_mem_v[l])
    sink_rows = jnp.repeat(sinks.reshape(N_SWA_KV, SWA_GROUP), BLOCK, axis=1)
    y_p, conv_p, k_p, v_p = _prompt_layer(
        x_prompt, pre_g, post_g, w_in_bf.T, conv_w[l].T, sink_rows, mkb, mvt, w_out_bf, tq=PROMPT_TQ)

    x8 = jnp.concatenate([x_sample, x_sample], axis=1).reshape(N * SAMPLE_ROWS, D_MODEL)
    conv_past8 = jnp.pad(state_conv[l], ((0, 0), (0, SAMPLE_ROWS - (CONV_WIDTH - 1)), (0, 0)))
    sink_col = jnp.repeat(sinks, SAMPLE_ROWS // 2).reshape(N_SWA_HEADS * SAMPLE_ROWS // 2, 1)
    y_s, conv_s, k_s, v_s = _sample_layer(
        x8, conv_past8,
        cache_swa_k[l].reshape(N, WINDOW, D_SWA_KV), cache_swa_v[l].reshape(N, WINDOW, D_SWA_KV),
        cache_mem_k[l].reshape(N, N_MEM, D_MEMQ), cache_mem_v[l].reshape(N, N_MEM, D_MEMQ),
        pre_g, post_g, w_in_bf, conv_w[l], sink_col, w_out_bf, ns=SAMPLE_NS)

    return (y_p, y_s,
            conv_p[None],
            k_p.reshape(1, B, WINDOW, N_SWA_KV, HEAD_DIM), v_p.reshape(1, B, WINDOW, N_SWA_KV, HEAD_DIM),
            mk.reshape(1, B, N_MEM, N_MEM_HEADS, HEAD_DIM), mv.reshape(1, B, N_MEM, N_MEM_HEADS, HEAD_DIM),
            conv_s[None],
            k_s.reshape(1, N, WINDOW, N_SWA_KV, HEAD_DIM), v_s.reshape(1, N, WINDOW, N_SWA_KV, HEAD_DIM))
```

```python
import functools

import numpy as np
import jax
import jax.numpy as jnp
from jax import lax
from jax.experimental import pallas as pl
from jax.experimental.pallas import tpu as pltpu

D_MODEL = 1024
HEAD_DIM = 64
D_CONV = 384
N_MEM_HEADS = 4
D_MEMQ = N_MEM_HEADS * HEAD_DIM
D_SWA = 384
N_SWA_HEADS = 6
N_SWA_KV = 2
SWA_GROUP = N_SWA_HEADS // N_SWA_KV
D_SWA_KV = N_SWA_KV * HEAD_DIM
N_MEM = 256
CONV_WIDTH = 3
WINDOW = 128
BLOCK = 128
RMS_EPS = 1e-6
NEG_INF = -1e30
D_IN = 3072
QK_SCALE = HEAD_DIM ** -0.5

OFF_CB, OFF_CC, OFF_CH, OFF_CZ = 0, 384, 768, 1152
OFF_Q, OFF_K, OFF_V, OFF_SZ = 1536, 1920, 2048, 2176
OFF_MQ, OFF_MZ = 2560, 2816
YOFF_CONV, YOFF_SWA, YOFF_MEM = 0, 384, 768

V7X_VMEM_LIMIT_BYTES = 56 * 1024 * 1024

PROMPT_TQ = 512
SAMPLE_NS = 8
SAMPLE_GROUP = 4
SAMPLE_ROWS = 8
SAMPLE_NEW = 16


def _alibi_slope(h):
    return float(np.power(np.float32(2.0), np.float32(-8.0 * (h + 1) / N_SWA_HEADS)))


def _rms_norm(x, g):
    return x * lax.rsqrt(jnp.mean(x * x, axis=-1, keepdims=True) + RMS_EPS) * g


def _silu(x):
    return x * jax.nn.sigmoid(x)


_NT = (((1,), (1,)), ((), ()))
_TN = (((0,), (0,)), ((), ()))


def _mem_kv_kernel(mem_ref, g_ref, wk_ref, wv_ref, mkt_ref, mvt_ref, mkb_ref, mvtb_ref):
    m = _rms_norm(mem_ref[0], g_ref[...]).astype(jnp.bfloat16)
    mk = jnp.dot(m, wk_ref[...], preferred_element_type=jnp.float32)
    mv = jnp.dot(m, wv_ref[...], preferred_element_type=jnp.float32)
    mv_t = mv.T
    mkt_ref[0] = mk.T
    mvt_ref[0] = mv_t
    mkb_ref[0] = mk.astype(jnp.bfloat16)
    mvtb_ref[0] = mv_t.astype(jnp.bfloat16)


def _mem_kv(mem, mem_g, w_mk, w_mv):
    B = mem.shape[0]
    full = lambda shape: pl.BlockSpec(shape, lambda b: (0,) * len(shape))
    per_batch = pl.BlockSpec((1, N_MEM, D_MEMQ), lambda b: (b, 0, 0))
    return pl.pallas_call(
        _mem_kv_kernel,
        grid=(B,),
        in_specs=[
            pl.BlockSpec((1, N_MEM, D_MODEL), lambda b: (b, 0, 0)),
            full((1, D_MODEL)),
            full((D_MODEL, D_MEMQ)),
            full((D_MODEL, D_MEMQ)),
        ],
        out_specs=[per_batch] * 4,
        out_shape=[
            jax.ShapeDtypeStruct((B, D_MEMQ, N_MEM), jnp.float32),
            jax.ShapeDtypeStruct((B, D_MEMQ, N_MEM), jnp.float32),
            jax.ShapeDtypeStruct((B, N_MEM, D_MEMQ), jnp.bfloat16),
            jax.ShapeDtypeStruct((B, D_MEMQ, N_MEM), jnp.bfloat16),
        ],
        compiler_params=pltpu.CompilerParams(dimension_semantics=("arbitrary",)),
        name="mem_kv",
    )(mem, mem_g.reshape(1, D_MODEL), w_mk.astype(jnp.bfloat16), w_mv.astype(jnp.bfloat16))


def _prompt_kernel(x_ref, pre_g_ref, post_g_ref, w_in_t_ref, conv_w_t_ref, sink_ref, mkb_ref, mvt_ref, w_out_ref,
                   y_ref, conv_state_ref, k_state_ref, v_state_ref,
                   zt_ref, ycat_ref, kbuf_ref, vbuf_ref, uprev_ref, bias_ref, *, tq):
    t = pl.program_id(1)
    nblk = tq // BLOCK

    @pl.when((pl.program_id(0) == 0) & (t == 0))
    def _():
        c = lax.broadcasted_iota(jnp.int32, (2 * BLOCK, BLOCK), 0)
        r = lax.broadcasted_iota(jnp.int32, (2 * BLOCK, BLOCK), 1)
        dist = r + BLOCK - c
        valid = (dist >= 0) & (dist < WINDOW)
        distf = dist.astype(jnp.float32)
        for h in range(N_SWA_HEADS):
            g, i = divmod(h, SWA_GROUP)
            bias_ref[g, :, i * BLOCK:(i + 1) * BLOCK] = jnp.where(valid, -_alibi_slope(h) * distf, NEG_INF)

    @pl.when(t == 0)
    def _():
        kbuf_ref[0:BLOCK, :] = jnp.zeros((BLOCK, D_SWA_KV), jnp.bfloat16)
        vbuf_ref[:, 0:BLOCK] = jnp.zeros((D_SWA_KV, BLOCK), jnp.bfloat16)
        uprev_ref[...] = jnp.zeros_like(uprev_ref)

    x = x_ref[0]
    h = _rms_norm(x, pre_g_ref[...]).astype(jnp.bfloat16)
    zt_ref[...] = lax.dot_general(w_in_t_ref[...], h, _NT, preferred_element_type=jnp.float32)

    u = zt_ref[OFF_CC:OFF_CC + D_CONV, :] * zt_ref[OFF_CH:OFF_CH + D_CONV, :]
    ucat = jnp.concatenate([uprev_ref[...], u], axis=1)
    cw = conv_w_t_ref[...]
    conv = (cw[:, 0:1] * pltpu.roll(ucat, 2, axis=1)[:, BLOCK:]
            + cw[:, 1:2] * pltpu.roll(ucat, 1, axis=1)[:, BLOCK:]
            + cw[:, 2:3] * u)
    y_conv = zt_ref[OFF_CB:OFF_CB + D_CONV, :] * conv * _silu(zt_ref[OFF_CZ:OFF_CZ + D_CONV, :])
    ycat_ref[YOFF_CONV:YOFF_CONV + D_CONV, :] = y_conv.astype(jnp.bfloat16)
    u_last = u[:, tq - BLOCK:]
    uprev_ref[...] = u_last
    conv_state_ref[0] = u_last.T[BLOCK - (CONV_WIDTH - 1):, :]

    k_nat = zt_ref[OFF_K:OFF_K + D_SWA_KV, :].T
    kbuf_ref[BLOCK:, :] = k_nat.astype(jnp.bfloat16)
    vbuf_ref[:, BLOCK:] = zt_ref[OFF_V:OFF_V + D_SWA_KV, :].astype(jnp.bfloat16)
    k_state_ref[0] = zt_ref[OFF_K:OFF_K + D_SWA_KV, tq - BLOCK:]
    v_state_ref[0] = zt_ref[OFF_V:OFF_V + D_SWA_KV, tq - BLOCK:]

    first_pen = jnp.where(t == 0, NEG_INF, 0.0)
    key_is_prev = lax.broadcasted_iota(jnp.int32, (2 * BLOCK, SWA_GROUP * BLOCK), 0) < BLOCK
    q_zero = jnp.zeros((HEAD_DIM, SWA_GROUP * BLOCK), jnp.bfloat16)
    for j in range(nblk):
        cols = slice(j * BLOCK, (j + 1) * BLOCK)
        for g in range(N_SWA_KV):
            q0 = OFF_Q + g * SWA_GROUP * HEAD_DIM
            qt = jnp.concatenate(
                [zt_ref[q0 + i * HEAD_DIM:q0 + (i + 1) * HEAD_DIM, cols] for i in range(SWA_GROUP)],
                axis=1)
            qt = (qt * QK_SCALE).astype(jnp.bfloat16)
            qt = jnp.concatenate([qt, q_zero] if g == 0 else [q_zero, qt], axis=0)
            kband = kbuf_ref[j * BLOCK:(j + 2) * BLOCK, :]
            s = jnp.dot(kband, qt, preferred_element_type=jnp.float32)
            s = s + bias_ref[g]
            if j == 0:
                s = s + jnp.where(key_is_prev, first_pen, 0.0)
            sink = sink_ref[g:g + 1, :]
            m = jnp.maximum(jnp.max(s, axis=0, keepdims=True), sink)
            p = jnp.exp(s - m)
            l = jnp.sum(p, axis=0, keepdims=True) + jnp.exp(sink - m)
            vband = vbuf_ref[g * HEAD_DIM:(g + 1) * HEAD_DIM, j * BLOCK:(j + 2) * BLOCK]
            o = jnp.dot(vband, p.astype(jnp.bfloat16), preferred_element_type=jnp.float32)
            o = o / l
            for i in range(SWA_GROUP):
                hh = g * SWA_GROUP + i
                gate = _silu(zt_ref[OFF_SZ + hh * HEAD_DIM:OFF_SZ + (hh + 1) * HEAD_DIM, cols])
                ycat_ref[YOFF_SWA + hh * HEAD_DIM:YOFF_SWA + (hh + 1) * HEAD_DIM, cols] = (
                    o[:, i * BLOCK:(i + 1) * BLOCK] * gate).astype(jnp.bfloat16)

    kbuf_ref[0:BLOCK, :] = kbuf_ref[tq:tq + BLOCK, :]
    vbuf_ref[:, 0:BLOCK] = vbuf_ref[:, tq:tq + BLOCK]

    for hh in range(N_MEM_HEADS):
        rows = slice(OFF_MQ + hh * HEAD_DIM, OFF_MQ + (hh + 1) * HEAD_DIM)
        qt = (zt_ref[rows, :] * QK_SCALE).astype(jnp.bfloat16)
        pieces = [jnp.zeros((HEAD_DIM, tq), jnp.bfloat16)] * N_MEM_HEADS
        pieces[hh] = qt
        s = jnp.dot(mkb_ref[0], jnp.concatenate(pieces, axis=0),
                    preferred_element_type=jnp.float32)
        m = jnp.max(s, axis=0, keepdims=True)
        p = jnp.exp(s - m)
        l = jnp.sum(p, axis=0, keepdims=True)
        o = jnp.dot(mvt_ref[0, hh * HEAD_DIM:(hh + 1) * HEAD_DIM, :], p.astype(jnp.bfloat16),
                    preferred_element_type=jnp.float32)
        gate = _silu(zt_ref[OFF_MZ + hh * HEAD_DIM:OFF_MZ + (hh + 1) * HEAD_DIM, :])
        ycat_ref[YOFF_MEM + hh * HEAD_DIM:YOFF_MEM + (hh + 1) * HEAD_DIM, :] = (o / l * gate).astype(jnp.bfloat16)

    y = lax.dot_general(ycat_ref[...], w_out_ref[...], _TN, preferred_element_type=jnp.float32)
    y_ref[0] = x + _rms_norm(y, post_g_ref[...])


def _prompt_layer(x, pre_g, post_g, w_in_t, conv_w_t, sink_rows, mkb, mvtb, w_out, *, tq):
    B, T, _ = x.shape
    full = lambda shape: pl.BlockSpec(shape, lambda b, t: (0,) * len(shape))
    kernel = functools.partial(_prompt_kernel, tq=tq)
    return pl.pallas_call(
        kernel,
        grid=(B, T // tq),
        in_specs=[
            pl.BlockSpec((1, tq, D_MODEL), lambda b, t: (b, t, 0)),
            full((1, D_MODEL)),
            full((1, D_MODEL)),
            full((D_IN, D_MODEL)),
            full((D_CONV, CONV_WIDTH)),
            full((N_SWA_KV, SWA_GROUP * BLOCK)),
            pl.BlockSpec((1, N_MEM, D_MEMQ), lambda b, t: (b, 0, 0)),
            pl.BlockSpec((1, D_MEMQ, N_MEM), lambda b, t: (b, 0, 0)),
            full((D_MODEL, D_MODEL)),
        ],
        out_specs=[
            pl.BlockSpec((1, tq, D_MODEL), lambda b, t: (b, t, 0)),
            pl.BlockSpec((1, CONV_WIDTH - 1, D_CONV), lambda b, t: (b, 0, 0)),
            pl.BlockSpec((1, D_SWA_KV, BLOCK), lambda b, t: (b, 0, 0)),
            pl.BlockSpec((1, D_SWA_KV, BLOCK), lambda b, t: (b, 0, 0)),
        ],
        out_shape=[
            jax.ShapeDtypeStruct((B, T, D_MODEL), jnp.float32),
            jax.ShapeDtypeStruct((B, CONV_WIDTH - 1, D_CONV), jnp.float32),
            jax.ShapeDtypeStruct((B, D_SWA_KV, BLOCK), jnp.float32),
            jax.ShapeDtypeStruct((B, D_SWA_KV, BLOCK), jnp.float32),
        ],
        scratch_shapes=[
            pltpu.VMEM((D_IN, tq), jnp.float32),
            pltpu.VMEM((D_MODEL, tq), jnp.bfloat16),
            pltpu.VMEM((BLOCK + tq, D_SWA_KV), jnp.bfloat16),
            pltpu.VMEM((D_SWA_KV, BLOCK + tq), jnp.bfloat16),
            pltpu.VMEM((D_CONV, BLOCK), jnp.float32),
            pltpu.VMEM((N_SWA_KV, 2 * BLOCK, SWA_GROUP * BLOCK), jnp.float32),
        ],
        compiler_params=pltpu.CompilerParams(
            dimension_semantics=("arbitrary", "arbitrary"),
            vmem_limit_bytes=V7X_VMEM_LIMIT_BYTES),
        name="prompt_layer",
    )(x, pre_g, post_g, w_in_t, conv_w_t, sink_rows, mkb, mvtb, w_out)


def _sample_kernel(x_ref, conv_past_ref, ckt_ref, cvt_ref, mkt_ref, mvt_ref,
                   pre_g_ref, post_g_ref, w_in_t_ref, conv_w_ref, sink_ref, w_out_ref,
                   y_ref, conv_state_ref, kt_state_ref, vt_state_ref,
                   z_ref, ycat_ref, *, ns, group):
    R = SAMPLE_ROWS
    half = R // 2
    nseq = ns * group
    step = pl.program_id(1)

    @pl.when(step == 0)
    def _():
        h = _rms_norm(x_ref[...], pre_g_ref[...]).astype(jnp.bfloat16)
        z_ref[...] = lax.dot_general(h, w_in_t_ref[...], _NT, preferred_element_type=jnp.float32)
        u = (z_ref[:, OFF_CC:OFF_CC + D_CONV] * z_ref[:, OFF_CH:OFF_CH + D_CONV]).reshape(nseq, R, D_CONV)
        row3 = lax.broadcasted_iota(jnp.int32, (nseq, R, D_CONV), 1)
        u_full = jnp.where(row3 < CONV_WIDTH - 1, conv_past_ref[...], pltpu.roll(u, CONV_WIDTH - 1, axis=1))
        cw = conv_w_ref[...]
        conv = (cw[0:1, :] * u_full
                + cw[1:2, :] * pltpu.roll(u_full, R - 1, axis=1)
                + cw[2:3, :] * pltpu.roll(u_full, R - 2, axis=1))
        conv_state_ref[...] = pltpu.roll(u_full, R - half, axis=1)[:, 0:CONV_WIDTH - 1, :]
        y_conv = (z_ref[:, OFF_CB:OFF_CB + D_CONV] * conv.reshape(nseq * R, D_CONV)
                  * _silu(z_ref[:, OFF_CZ:OFF_CZ + D_CONV]))
        ycat_ref[:, YOFF_CONV:YOFF_CONV + D_CONV] = y_conv

    row = lax.broadcasted_iota(jnp.int32, (R, 2 * HEAD_DIM), 0)
    lane = lax.broadcasted_iota(jnp.int32, (R, 2 * HEAD_DIM), 1)
    lo_row = row < half
    lo_lane = lane < HEAD_DIM
    diag = lo_row == lo_lane
    lane_sq = lax.broadcasted_iota(jnp.int32, (WINDOW, WINDOW), 1)

    def pair_bias(ncols, dist_of):
        rr = lax.broadcasted_iota(jnp.int32, (R, ncols), 0)
        cc = lax.broadcasted_iota(jnp.int32, (R, ncols), 1)
        dist, valid = dist_of(rr % half, cc)
        distf = dist.astype(jnp.float32)
        tiles = []
        for pair in range(N_SWA_HEADS // 2):
            slope = jnp.where(rr < half, _alibi_slope(2 * pair), _alibi_slope(2 * pair + 1))
            tiles.append(jnp.where(valid, -slope * distf, NEG_INF))
        return jnp.concatenate(tiles, axis=0)

    def cached_dist(tok, c):
        d = tok + WINDOW - c
        return d, d < WINDOW

    def new_dist(tok, c):
        d = tok - c
        return d, (d >= 0) & (c < half)

    bias_c = pair_bias(WINDOW, cached_dist)
    bias_n = pair_bias(SAMPLE_NEW, new_dist)
    sink_col = sink_ref[...]
    state_pad = jnp.zeros((WINDOW - R, D_SWA_KV), jnp.float32)

    def body(n, carry):
        r0 = pl.multiple_of((step * ns + n) * R, R)
        rows = pl.ds(r0, R)

        qa = z_ref[rows, OFF_Q:OFF_Q + 128] * QK_SCALE
        qb = z_ref[rows, OFF_Q + 128:OFF_Q + 256] * QK_SCALE
        qc = z_ref[rows, OFF_Q + 256:OFF_Q + 384] * QK_SCALE
        t0 = jnp.where(lo_lane, jnp.where(lo_row, qa, pltpu.roll(qa, HEAD_DIM, axis=1)), 0.0)
        t1 = jnp.where(diag, qb, 0.0)
        t2 = jnp.where(lo_lane, 0.0, jnp.where(lo_row, pltpu.roll(qc, HEAD_DIM, axis=1), qc))
        qs = jnp.concatenate([t0, t1, t2], axis=0).astype(jnp.bfloat16)

        k_new = z_ref[rows, OFF_K:OFF_K + D_SWA_KV]
        v_new = z_ref[rows, OFF_V:OFF_V + D_SWA_KV]
        k_new_b = jnp.concatenate([k_new, k_new], axis=0).astype(jnp.bfloat16)
        v_new_b = jnp.concatenate([v_new, v_new], axis=0).astype(jnp.bfloat16)
        kt_old = ckt_ref[n]
        vt_old = cvt_ref[n]

        s_c = jnp.dot(qs, kt_old.astype(jnp.bfloat16), preferred_element_type=jnp.float32) + bias_c
        s_n = lax.dot_general(qs, k_new_b, _NT, preferred_element_type=jnp.float32) + bias_n
        m = jnp.maximum(jnp.maximum(jnp.max(s_c, axis=1, keepdims=True), jnp.max(s_n, axis=1, keepdims=True)),
                        sink_col)
        p_c = jnp.exp(s_c - m)
        p_n = jnp.exp(s_n - m)
        l = (jnp.sum(p_c, axis=1, keepdims=True) + jnp.sum(p_n, axis=1, keepdims=True) + jnp.exp(sink_col - m))
        o = (lax.dot_general(p_c.astype(jnp.bfloat16), vt_old.astype(jnp.bfloat16), _NT,
                             preferred_element_type=jnp.float32)
             + jnp.dot(p_n.astype(jnp.bfloat16), v_new_b, preferred_element_type=jnp.float32)) / l
        o0, o1, o2 = o[0:R], o[R:2 * R], o[2 * R:3 * R]
        ya = jnp.where(lo_lane, o0, pltpu.roll(pltpu.roll(o0, HEAD_DIM, axis=1), half, axis=0))
        yb = jnp.where(lo_lane, o1, pltpu.roll(o1, half, axis=0))
        yc = jnp.where(lo_lane, pltpu.roll(o2, HEAD_DIM, axis=1), pltpu.roll(o2, half, axis=0))
        y_swa = jnp.concatenate([ya, yb, yc], axis=1) * _silu(z_ref[rows, OFF_SZ:OFF_SZ + D_SWA])
        ycat_ref[rows, YOFF_SWA:YOFF_SWA + D_SWA] = y_swa

        k_new_t = jnp.concatenate([state_pad, k_new], axis=0).T
        v_new_t = jnp.concatenate([state_pad, v_new], axis=0).T
        keep = lane_sq < WINDOW - half
        kt_state_ref[n] = jnp.where(keep, pltpu.roll(kt_old, WINDOW - half, axis=1), k_new_t)
        vt_state_ref[n] = jnp.where(keep, pltpu.roll(vt_old, WINDOW - half, axis=1), v_new_t)

        m0 = z_ref[rows, OFF_MQ:OFF_MQ + 128] * QK_SCALE
        m1 = z_ref[rows, OFF_MQ + 128:OFF_MQ + 256] * QK_SCALE
        zero = jnp.zeros_like(m0)
        qm = jnp.concatenate(
            [jnp.concatenate([jnp.where(diag, m0, 0.0), zero], axis=1),
             jnp.concatenate([zero, jnp.where(diag, m1, 0.0)], axis=1)], axis=0).astype(jnp.bfloat16)
        s = jnp.dot(qm, mkt_ref[n].astype(jnp.bfloat16), preferred_element_type=jnp.float32)
        m = jnp.max(s, axis=1, keepdims=True)
        p = jnp.exp(s - m)
        l = jnp.sum(p, axis=1, keepdims=True)
        o = lax.dot_general(p.astype(jnp.bfloat16), mvt_ref[n].astype(jnp.bfloat16), _NT,
                            preferred_element_type=jnp.float32) / l
        oa, ob = o[0:R, 0:128], o[R:2 * R, 128:256]
        y_mem = jnp.concatenate([jnp.where(lo_lane, oa, pltpu.roll(oa, half, axis=0)),
                                 jnp.where(lo_lane, ob, pltpu.roll(ob, half, axis=0))], axis=1)
        ycat_ref[rows, YOFF_MEM:YOFF_MEM + D_MEMQ] = y_mem * _silu(z_ref[rows, OFF_MZ:OFF_MZ + D_MEMQ])
        return carry

    lax.fori_loop(0, ns, body, 0, unroll=2)

    @pl.when(step == group - 1)
    def _():
        y = jnp.dot(ycat_ref[...].astype(jnp.bfloat16), w_out_ref[...], preferred_element_type=jnp.float32)
        out = x_ref[...] + _rms_norm(y, post_g_ref[...])
        y_ref[...] = out.reshape(nseq, R, D_MODEL)[:, 0:half, :]


def _sample_layer(x8, conv_past8, ckt, cvt, mkt, mvt, pre_g, post_g, w_in_t, conv_w, sink_col, w_out, *, ns, group):
    N = ckt.shape[0]
    R = SAMPLE_ROWS
    nseq = ns * group
    full = lambda shape: pl.BlockSpec(shape, lambda o, i: (0,) * len(shape))
    per_group = lambda shape: pl.BlockSpec(shape, lambda o, i: (o,) + (0,) * (len(shape) - 1))
    per_step = lambda shape: pl.BlockSpec(shape, lambda o, i: (o * group + i,) + (0,) * (len(shape) - 1))
    kernel = functools.partial(_sample_kernel, ns=ns, group=group)
    return pl.pallas_call(
        kernel,
        grid=(N // nseq, group),
        in_specs=[
            per_group((nseq * R, D_MODEL)),
            per_group((nseq, R, D_CONV)),
            per_step((ns, D_SWA_KV, WINDOW)),
            per_step((ns, D_SWA_KV, WINDOW)),
            per_step((ns, D_MEMQ, N_MEM)),
            per_step((ns, D_MEMQ, N_MEM)),
            full((1, D_MODEL)),
            full((1, D_MODEL)),
            full((D_IN, D_MODEL)),
            full((CONV_WIDTH, D_CONV)),
            full((N_SWA_HEADS * R // 2, 1)),
            full((D_MODEL, D_MODEL)),
        ],
        out_specs=[
            per_group((nseq, R // 2, D_MODEL)),
            per_group((nseq, CONV_WIDTH - 1, D_CONV)),
            per_step((ns, D_SWA_KV, WINDOW)),
            per_step((ns, D_SWA_KV, WINDOW)),
        ],
        out_shape=[
            jax.ShapeDtypeStruct((N, R // 2, D_MODEL), jnp.float32),
            jax.ShapeDtypeStruct((N, CONV_WIDTH - 1, D_CONV), jnp.float32),
            jax.ShapeDtypeStruct((N, D_SWA_KV, WINDOW), jnp.float32),
            jax.ShapeDtypeStruct((N, D_SWA_KV, WINDOW), jnp.float32),
        ],
        scratch_shapes=[
            pltpu.VMEM((nseq * R, D_IN), jnp.float32),
            pltpu.VMEM((nseq * R, D_MODEL), jnp.float32),
        ],
        compiler_params=pltpu.CompilerParams(
            dimension_semantics=("arbitrary", "arbitrary"),
            vmem_limit_bytes=V7X_VMEM_LIMIT_BYTES),
        name="sample_layer",
    )(x8, conv_past8, ckt, cvt, mkt, mvt, pre_g, post_g, w_in_t, conv_w, sink_col, w_out)


def _heads_last_to_keys_last(a):
    n, keys, heads, dim = a.shape
    return jnp.transpose(a, (0, 2, 3, 1)).reshape(n, heads * dim, keys)


def _keys_last_to_heads_last(a, heads):
    n, hd, keys = a.shape
    return jnp.transpose(a.reshape(n, heads, hd // heads, keys), (0, 3, 1, 2))[None]


def kernel(x_prompt, x_sample, mem_prompt, state_conv, cache_swa_k, cache_swa_v, cache_mem_k, cache_mem_v,
           pre_norm_g, post_norm_g, w_in, conv_w, attn_sinks, mem_norm_g, w_mem_k, w_mem_v, w_out):
    assert w_in.shape[0] == 1, "one layer, as the problem states"
    N, TS, _ = x_sample.shape
    assert TS == SAMPLE_ROWS // 2 and cache_swa_k.shape[2] == WINDOW
    l = 0

    pre_g = pre_norm_g[l].reshape(1, D_MODEL)
    post_g = post_norm_g[l].reshape(1, D_MODEL)
    w_in_t = w_in[l].astype(jnp.bfloat16).T
    w_out_bf = w_out[l].astype(jnp.bfloat16)
    sinks = attn_sinks[l].astype(jnp.float32)

    mkt, mvt, mkb, mvtb = _mem_kv(mem_prompt, mem_norm_g[l], w_mem_k[l], w_mem_v[l])
    sink_rows = jnp.repeat(sinks.reshape(N_SWA_KV, SWA_GROUP), BLOCK, axis=1)
    y_p, conv_p, kt_p, vt_p = _prompt_layer(
        x_prompt, pre_g, post_g, w_in_t, conv_w[l].T, sink_rows, mkb, mvtb, w_out_bf, tq=PROMPT_TQ)

    x8 = jnp.concatenate([x_sample, x_sample], axis=1).reshape(N * SAMPLE_ROWS, D_MODEL)
    conv_past8 = jnp.pad(state_conv[l], ((0, 0), (0, SAMPLE_ROWS - (CONV_WIDTH - 1)), (0, 0)))
    sink_col = jnp.repeat(sinks, SAMPLE_ROWS // 2).reshape(N_SWA_HEADS * SAMPLE_ROWS // 2, 1)
    y_s, conv_s, kt_s, vt_s = _sample_layer(
        x8, conv_past8,
        _heads_last_to_keys_last(cache_swa_k[l]), _heads_last_to_keys_last(cache_swa_v[l]),
        _heads_last_to_keys_last(cache_mem_k[l]), _heads_last_to_keys_last(cache_mem_v[l]),
        pre_g, post_g, w_in_t, conv_w[l], sink_col, w_out_bf, ns=SAMPLE_NS, group=SAMPLE_GROUP)

    return (y_p, y_s,
            conv_p[None],
            _keys_last_to_heads_last(kt_p, N_SWA_KV), _keys_last_to_heads_last(vt_p, N_SWA_KV),
            _keys_last_to_heads_last(mkt, N_MEM_HEADS), _keys_last_to_heads_last(mvt, N_MEM_HEADS),
            conv_s[None],
            _keys_last_to_heads_last(kt_s, N_SWA_KV), _keys_last_to_heads_last(vt_s, N_SWA_KV))
```

```python
import functools

import numpy as np
import jax
import jax.numpy as jnp
from jax import lax
from jax.experimental import pallas as pl
from jax.experimental.pallas import tpu as pltpu

D_MODEL = 1024
HEAD_DIM = 64
D_CONV = 384
N_MEM_HEADS = 4
D_MEMQ = N_MEM_HEADS * HEAD_DIM
D_SWA = 384
N_SWA_HEADS = 6
N_SWA_KV = 2
SWA_GROUP = N_SWA_HEADS // N_SWA_KV
D_SWA_KV = N_SWA_KV * HEAD_DIM
N_MEM = 256
CONV_WIDTH = 3
WINDOW = 128
BLOCK = 128
RMS_EPS = 1e-6
NEG_INF = -1e30
D_IN = 3072
QK_SCALE = HEAD_DIM ** -0.5

OFF_CB, OFF_CC, OFF_CH, OFF_CZ = 0, 384, 768, 1152
OFF_Q, OFF_K, OFF_V, OFF_SZ = 1536, 1920, 2048, 2176
OFF_MQ, OFF_MZ = 2560, 2816
YOFF_CONV, YOFF_SWA, YOFF_MEM = 0, 384, 768

V7X_VMEM_LIMIT_BYTES = 56 * 1024 * 1024

PROMPT_TQ = 512
PROMPT_NSUB = 2
IN_PROJ_CHUNK = 512
ATTN_WIDTH = 2
SAMPLE_NS = 8
SAMPLE_GROUP = 4
SAMPLE_ROWS = 8
SAMPLE_NEW = 16


def _alibi_slope(h):
    return float(np.power(np.float32(2.0), np.float32(-8.0 * (h + 1) / N_SWA_HEADS)))


def _rms_norm(x, g):
    return x * lax.rsqrt(jnp.mean(x * x, axis=-1, keepdims=True) + RMS_EPS) * g


def _silu(x):
    return x * jax.nn.sigmoid(x)


_NT = (((1,), (1,)), ((), ()))
_TN = (((0,), (0,)), ((), ()))


def _mem_kv_kernel(mem_ref, g_ref, wk_ref, wv_ref, mkt_ref, mvt_ref, mkb_ref, mvtb_ref):
    m = _rms_norm(mem_ref[0], g_ref[...]).astype(jnp.bfloat16)
    mk = jnp.dot(m, wk_ref[...], preferred_element_type=jnp.float32)
    mv = jnp.dot(m, wv_ref[...], preferred_element_type=jnp.float32)
    mv_t = mv.T
    mkt_ref[0] = mk.T
    mvt_ref[0] = mv_t
    mkb_ref[0] = mk.astype(jnp.bfloat16)
    mvtb_ref[0] = mv_t.astype(jnp.bfloat16)


def _mem_kv(mem, mem_g, w_mk, w_mv):
    B = mem.shape[0]
    full = lambda shape: pl.BlockSpec(shape, lambda b: (0,) * len(shape))
    per_batch = pl.BlockSpec((1, N_MEM, D_MEMQ), lambda b: (b, 0, 0))
    return pl.pallas_call(
        _mem_kv_kernel,
        grid=(B,),
        in_specs=[
            pl.BlockSpec((1, N_MEM, D_MODEL), lambda b: (b, 0, 0)),
            full((1, D_MODEL)),
            full((D_MODEL, D_MEMQ)),
            full((D_MODEL, D_MEMQ)),
        ],
        out_specs=[per_batch] * 4,
        out_shape=[
            jax.ShapeDtypeStruct((B, D_MEMQ, N_MEM), jnp.float32),
            jax.ShapeDtypeStruct((B, D_MEMQ, N_MEM), jnp.float32),
            jax.ShapeDtypeStruct((B, N_MEM, D_MEMQ), jnp.bfloat16),
            jax.ShapeDtypeStruct((B, D_MEMQ, N_MEM), jnp.bfloat16),
        ],
        compiler_params=pltpu.CompilerParams(dimension_semantics=("arbitrary",)),
        name="mem_kv",
    )(mem, mem_g.reshape(1, D_MODEL), w_mk.astype(jnp.bfloat16), w_mv.astype(jnp.bfloat16))


def _prompt_kernel(x_ref, x_next_ref, pre_g_ref, post_g_ref, w_in_t_ref, conv_w_t_ref, sink_ref, mkb_ref, mvt_ref,
                   w_out_ref,
                   y_ref, conv_state_ref, k_state_ref, v_state_ref,
                   zt_ref, ycat_ref, kbuf_ref, vbuf_ref, uprev_ref, bias_ref, *, tq, nsub):
    assert nsub >= 2, "a sub-tile's z^T buffer is refilled while the other sub-tile's mixers run"
    t = pl.program_id(1)
    nblk = tq // BLOCK

    @pl.when((pl.program_id(0) == 0) & (t == 0))
    def _():
        c = lax.broadcasted_iota(jnp.int32, (2 * BLOCK, BLOCK), 0)
        r = lax.broadcasted_iota(jnp.int32, (2 * BLOCK, BLOCK), 1)
        dist = r + BLOCK - c
        valid = (dist >= 0) & (dist < WINDOW)
        distf = dist.astype(jnp.float32)
        for h in range(N_SWA_HEADS):
            g, i = divmod(h, SWA_GROUP)
            bias_ref[g, :, i * BLOCK:(i + 1) * BLOCK] = jnp.where(valid, -_alibi_slope(h) * distf, NEG_INF)

    @pl.when(t == 0)
    def _():
        kbuf_ref[0:BLOCK, :] = jnp.zeros((BLOCK, D_SWA_KV), jnp.bfloat16)
        vbuf_ref[:, 0:BLOCK] = jnp.zeros((D_SWA_KV, BLOCK), jnp.bfloat16)
        uprev_ref[...] = jnp.zeros_like(uprev_ref)

    first_pen = jnp.where(t == 0, NEG_INF, 0.0)
    key_is_prev = lax.broadcasted_iota(jnp.int32, (2 * BLOCK, SWA_GROUP * BLOCK), 0) < BLOCK
    q_zero = jnp.zeros((HEAD_DIM, SWA_GROUP * BLOCK), jnp.bfloat16)
    cw = conv_w_t_ref[...]
    carry = {"u_prev": uprev_ref[...]}

    def in_proj_chunks(x_rows, zt):
        state = {}

        def chunk(c):
            if c == 0:
                state["h"] = _rms_norm(x_rows(), pre_g_ref[...]).astype(jnp.bfloat16)
            rows = slice(c * IN_PROJ_CHUNK, (c + 1) * IN_PROJ_CHUNK)
            zt[rows, :] = lax.dot_general(w_in_t_ref[rows, :], state["h"], _NT,
                                          preferred_element_type=jnp.float32)

        return [functools.partial(chunk, c) for c in range(D_IN // IN_PROJ_CHUNK)]

    def mixer_units(sub):
        zt = zt_ref.at[sub]
        ycat = ycat_ref.at[sub]
        tok0 = sub * tq

        def conv_unit():
            u = zt[OFF_CC:OFF_CC + D_CONV, :] * zt[OFF_CH:OFF_CH + D_CONV, :]
            ucat = jnp.concatenate([carry["u_prev"], u], axis=1)
            conv = (cw[:, 0:1] * pltpu.roll(ucat, 2, axis=1)[:, BLOCK:]
                    + cw[:, 1:2] * pltpu.roll(ucat, 1, axis=1)[:, BLOCK:]
                    + cw[:, 2:3] * u)
            y_conv = zt[OFF_CB:OFF_CB + D_CONV, :] * conv * _silu(zt[OFF_CZ:OFF_CZ + D_CONV, :])
            ycat[YOFF_CONV:YOFF_CONV + D_CONV, :] = y_conv.astype(jnp.bfloat16)
            carry["u_prev"] = u[:, tq - BLOCK:]

        def kv_unit():
            k_nat = zt[OFF_K:OFF_K + D_SWA_KV, :].T
            kbuf_ref[BLOCK + tok0:BLOCK + tok0 + tq, :] = k_nat.astype(jnp.bfloat16)
            vbuf_ref[:, BLOCK + tok0:BLOCK + tok0 + tq] = zt[OFF_V:OFF_V + D_SWA_KV, :].astype(jnp.bfloat16)

        def swa_unit(j, g):
            cols = slice(j * BLOCK, (j + 1) * BLOCK)
            band = slice(tok0 + j * BLOCK, tok0 + (j + 2) * BLOCK)
            q0 = OFF_Q + g * SWA_GROUP * HEAD_DIM
            qt = jnp.concatenate(
                [zt[q0 + i * HEAD_DIM:q0 + (i + 1) * HEAD_DIM, cols] for i in range(SWA_GROUP)],
                axis=1)
            qt = (qt * QK_SCALE).astype(jnp.bfloat16)
            qt = jnp.concatenate([qt, q_zero] if g == 0 else [q_zero, qt], axis=0)
            s = jnp.dot(kbuf_ref[band, :], qt, preferred_element_type=jnp.float32)
            yield
            s = s + bias_ref[g]
            if sub == 0 and j == 0:
                s = s + jnp.where(key_is_prev, first_pen, 0.0)
            sink = sink_ref[g:g + 1, :]
            m = jnp.maximum(jnp.max(s, axis=0, keepdims=True), sink)
            p = jnp.exp(s - m)
            l = jnp.sum(p, axis=0, keepdims=True) + jnp.exp(sink - m)
            p = p.astype(jnp.bfloat16)
            yield
            vband = vbuf_ref[g * HEAD_DIM:(g + 1) * HEAD_DIM, band]
            o = jnp.dot(vband, p, preferred_element_type=jnp.float32)
            o = o / l
            for i in range(SWA_GROUP):
                hh = g * SWA_GROUP + i
                gate = _silu(zt[OFF_SZ + hh * HEAD_DIM:OFF_SZ + (hh + 1) * HEAD_DIM, cols])
                ycat[YOFF_SWA + hh * HEAD_DIM:YOFF_SWA + (hh + 1) * HEAD_DIM, cols] = (
                    o[:, i * BLOCK:(i + 1) * BLOCK] * gate).astype(jnp.bfloat16)

        def mem_unit(hh):
            rows = slice(OFF_MQ + hh * HEAD_DIM, OFF_MQ + (hh + 1) * HEAD_DIM)
            qt = (zt[rows, :] * QK_SCALE).astype(jnp.bfloat16)
            pieces = [jnp.zeros((HEAD_DIM, tq), jnp.bfloat16)] * N_MEM_HEADS
            pieces[hh] = qt
            s = jnp.dot(mkb_ref[0], jnp.concatenate(pieces, axis=0),
                        preferred_element_type=jnp.float32)
            yield
            m = jnp.max(s, axis=0, keepdims=True)
            p = jnp.exp(s - m)
            l = jnp.sum(p, axis=0, keepdims=True)
            p = p.astype(jnp.bfloat16)
            yield
            o = jnp.dot(mvt_ref[0, hh * HEAD_DIM:(hh + 1) * HEAD_DIM, :], p,
                        preferred_element_type=jnp.float32)
            gate = _silu(zt[OFF_MZ + hh * HEAD_DIM:OFF_MZ + (hh + 1) * HEAD_DIM, :])
            ycat[YOFF_MEM + hh * HEAD_DIM:YOFF_MEM + (hh + 1) * HEAD_DIM, :] = (
                o / l * gate).astype(jnp.bfloat16)

        def out_unit():
            y = lax.dot_general(ycat[...], w_out_ref[...], _TN, preferred_element_type=jnp.float32)
            y_ref[0, tok0:tok0 + tq, :] = x_ref[0, tok0:tok0 + tq, :] + _rms_norm(y, post_g_ref[...])

        attention = [functools.partial(swa_unit, j, g) for j in range(nblk) for g in range(N_SWA_KV)]
        attention += [functools.partial(mem_unit, hh) for hh in range(N_MEM_HEADS)]
        return [conv_unit, kv_unit], attention, out_unit

    def interleave(units, chunks):
        first, attention, last = units
        for unit in first:
            unit()
        todo = []
        every = max(1, len(attention) // len(chunks))
        pending = list(chunks)
        for i, unit in enumerate(attention):
            todo.append(unit)
            if pending and i % every == 0:
                todo.append(pending.pop(0))
        todo += pending
        active = []
        while todo or active:
            started = 0
            while todo and started < ATTN_WIDTH:
                nxt = todo.pop(0)()
                if nxt is not None:
                    active.append(nxt)
                    started += 1
            for gen in list(active):
                if next(gen, "done") == "done":
                    active.remove(gen)
        last()

    @pl.when((pl.program_id(0) == 0) & (t == 0))
    def _():
        for chunk in in_proj_chunks(lambda: x_ref[0, 0:tq, :], zt_ref.at[0]):
            chunk()

    for sub in range(nsub):
        if sub + 1 < nsub:
            nxt = in_proj_chunks(lambda sub=sub: x_ref[0, (sub + 1) * tq:(sub + 2) * tq, :], zt_ref.at[sub + 1])
        else:
            nxt = in_proj_chunks(lambda: x_next_ref[0], zt_ref.at[0])
        interleave(mixer_units(sub), nxt)

    tile = nsub * tq
    uprev_ref[...] = carry["u_prev"]
    conv_state_ref[0] = carry["u_prev"].T[BLOCK - (CONV_WIDTH - 1):, :]
    kbuf_ref[0:BLOCK, :] = kbuf_ref[tile:tile + BLOCK, :]
    vbuf_ref[:, 0:BLOCK] = vbuf_ref[:, tile:tile + BLOCK]
    k_state_ref[0] = zt_ref[nsub - 1, OFF_K:OFF_K + D_SWA_KV, tq - BLOCK:]
    v_state_ref[0] = zt_ref[nsub - 1, OFF_V:OFF_V + D_SWA_KV, tq - BLOCK:]


def _prompt_layer(x, pre_g, post_g, w_in_t, conv_w_t, sink_rows, mkb, mvtb, w_out, *, tq, nsub):
    B, T, _ = x.shape
    tile = tq * nsub
    full = lambda shape: pl.BlockSpec(shape, lambda b, t: (0,) * len(shape))
    kernel = functools.partial(_prompt_kernel, tq=tq, nsub=nsub)
    steps = T // tile

    def next_first_sub_tile(b, t):
        nxt = jnp.minimum(b * steps + t + 1, B * steps - 1)
        return nxt // steps, (nxt % steps) * nsub, 0

    return pl.pallas_call(
        kernel,
        grid=(B, steps),
        in_specs=[
            pl.BlockSpec((1, tile, D_MODEL), lambda b, t: (b, t, 0)),
            pl.BlockSpec((1, tq, D_MODEL), next_first_sub_tile),
            full((1, D_MODEL)),
            full((1, D_MODEL)),
            full((D_IN, D_MODEL)),
            full((D_CONV, CONV_WIDTH)),
            full((N_SWA_KV, SWA_GROUP * BLOCK)),
            pl.BlockSpec((1, N_MEM, D_MEMQ), lambda b, t: (b, 0, 0)),
            pl.BlockSpec((1, D_MEMQ, N_MEM), lambda b, t: (b, 0, 0)),
            full((D_MODEL, D_MODEL)),
        ],
        out_specs=[
            pl.BlockSpec((1, tile, D_MODEL), lambda b, t: (b, t, 0)),
            pl.BlockSpec((1, CONV_WIDTH - 1, D_CONV), lambda b, t: (b, 0, 0)),
            pl.BlockSpec((1, D_SWA_KV, BLOCK), lambda b, t: (b, 0, 0)),
            pl.BlockSpec((1, D_SWA_KV, BLOCK), lambda b, t: (b, 0, 0)),
        ],
        out_shape=[
            jax.ShapeDtypeStruct((B, T, D_MODEL), jnp.float32),
            jax.ShapeDtypeStruct((B, CONV_WIDTH - 1, D_CONV), jnp.float32),
            jax.ShapeDtypeStruct((B, D_SWA_KV, BLOCK), jnp.float32),
            jax.ShapeDtypeStruct((B, D_SWA_KV, BLOCK), jnp.float32),
        ],
        scratch_shapes=[
            pltpu.VMEM((nsub, D_IN, tq), jnp.float32),
            pltpu.VMEM((nsub, D_MODEL, tq), jnp.bfloat16),
            pltpu.VMEM((BLOCK + tile, D_SWA_KV), jnp.bfloat16),
            pltpu.VMEM((D_SWA_KV, BLOCK + tile), jnp.bfloat16),
            pltpu.VMEM((D_CONV, BLOCK), jnp.float32),
            pltpu.VMEM((N_SWA_KV, 2 * BLOCK, SWA_GROUP * BLOCK), jnp.float32),
        ],
        compiler_params=pltpu.CompilerParams(
            dimension_semantics=("arbitrary", "arbitrary"),
            vmem_limit_bytes=V7X_VMEM_LIMIT_BYTES),
        name="prompt_layer",
    )(x, x, pre_g, post_g, w_in_t, conv_w_t, sink_rows, mkb, mvtb, w_out)


def _sample_kernel(x_ref, conv_past_ref, ckt_ref, cvt_ref, mkt_ref, mvt_ref,
                   pre_g_ref, post_g_ref, w_in_t_ref, conv_w_ref, sink_ref, w_out_ref,
                   y_ref, conv_state_ref, kt_state_ref, vt_state_ref,
                   z_ref, ycat_ref, *, ns, group):
    R = SAMPLE_ROWS
    half = R // 2
    nseq = ns * group
    step = pl.program_id(1)

    @pl.when(step == 0)
    def _():
        h = _rms_norm(x_ref[...], pre_g_ref[...]).astype(jnp.bfloat16)
        z_ref[...] = lax.dot_general(h, w_in_t_ref[...], _NT, preferred_element_type=jnp.float32)
        u = (z_ref[:, OFF_CC:OFF_CC + D_CONV] * z_ref[:, OFF_CH:OFF_CH + D_CONV]).reshape(nseq, R, D_CONV)
        row3 = lax.broadcasted_iota(jnp.int32, (nseq, R, D_CONV), 1)
        u_full = jnp.where(row3 < CONV_WIDTH - 1, conv_past_ref[...], pltpu.roll(u, CONV_WIDTH - 1, axis=1))
        cw = conv_w_ref[...]
        conv = (cw[0:1, :] * u_full
                + cw[1:2, :] * pltpu.roll(u_full, R - 1, axis=1)
                + cw[2:3, :] * pltpu.roll(u_full, R - 2, axis=1))
        conv_state_ref[...] = pltpu.roll(u_full, R - half, axis=1)[:, 0:CONV_WIDTH - 1, :]
        y_conv = (z_ref[:, OFF_CB:OFF_CB + D_CONV] * conv.reshape(nseq * R, D_CONV)
                  * _silu(z_ref[:, OFF_CZ:OFF_CZ + D_CONV]))
        ycat_ref[:, YOFF_CONV:YOFF_CONV + D_CONV] = y_conv

    row = lax.broadcasted_iota(jnp.int32, (R, 2 * HEAD_DIM), 0)
    lane = lax.broadcasted_iota(jnp.int32, (R, 2 * HEAD_DIM), 1)
    lo_row = row < half
    lo_lane = lane < HEAD_DIM
    diag = lo_row == lo_lane
    lane_sq = lax.broadcasted_iota(jnp.int32, (WINDOW, WINDOW), 1)

    def pair_bias(ncols, dist_of):
        rr = lax.broadcasted_iota(jnp.int32, (R, ncols), 0)
        cc = lax.broadcasted_iota(jnp.int32, (R, ncols), 1)
        dist, valid = dist_of(rr % half, cc)
        distf = dist.astype(jnp.float32)
        tiles = []
        for pair in range(N_SWA_HEADS // 2):
            slope = jnp.where(rr < half, _alibi_slope(2 * pair), _alibi_slope(2 * pair + 1))
            tiles.append(jnp.where(valid, -slope * distf, NEG_INF))
        return jnp.concatenate(tiles, axis=0)

    def cached_dist(tok, c):
        d = tok + WINDOW - c
        return d, d < WINDOW

    def new_dist(tok, c):
        d = tok - c
        return d, (d >= 0) & (c < half)

    bias_c = pair_bias(WINDOW, cached_dist)
    bias_n = pair_bias(SAMPLE_NEW, new_dist)
    sink_col = sink_ref[...]
    state_pad = jnp.zeros((WINDOW - R, D_SWA_KV), jnp.float32)

    def body(n, carry):
        r0 = pl.multiple_of((step * ns + n) * R, R)
        rows = pl.ds(r0, R)

        qa = z_ref[rows, OFF_Q:OFF_Q + 128] * QK_SCALE
        qb = z_ref[rows, OFF_Q + 128:OFF_Q + 256] * QK_SCALE
        qc = z_ref[rows, OFF_Q + 256:OFF_Q + 384] * QK_SCALE
        t0 = jnp.where(lo_lane, jnp.where(lo_row, qa, pltpu.roll(qa, HEAD_DIM, axis=1)), 0.0)
        t1 = jnp.where(diag, qb, 0.0)
        t2 = jnp.where(lo_lane, 0.0, jnp.where(lo_row, pltpu.roll(qc, HEAD_DIM, axis=1), qc))
        qs = jnp.concatenate([t0, t1, t2], axis=0).astype(jnp.bfloat16)

        k_new = z_ref[rows, OFF_K:OFF_K + D_SWA_KV]
        v_new = z_ref[rows, OFF_V:OFF_V + D_SWA_KV]
        k_new_b = jnp.concatenate([k_new, k_new], axis=0).astype(jnp.bfloat16)
        v_new_b = jnp.concatenate([v_new, v_new], axis=0).astype(jnp.bfloat16)
        kt_old = ckt_ref[n]
        vt_old = cvt_ref[n]

        s_c = jnp.dot(qs, kt_old.astype(jnp.bfloat16), preferred_element_type=jnp.float32) + bias_c
        s_n = lax.dot_general(qs, k_new_b, _NT, preferred_element_type=jnp.float32) + bias_n
        m = jnp.maximum(jnp.maximum(jnp.max(s_c, axis=1, keepdims=True), jnp.max(s_n, axis=1, keepdims=True)),
                        sink_col)
        p_c = jnp.exp(s_c - m)
        p_n = jnp.exp(s_n - m)
        l = (jnp.sum(p_c, axis=1, keepdims=True) + jnp.sum(p_n, axis=1, keepdims=True) + jnp.exp(sink_col - m))
        o = (lax.dot_general(p_c.astype(jnp.bfloat16), vt_old.astype(jnp.bfloat16), _NT,
                             preferred_element_type=jnp.float32)
             + jnp.dot(p_n.astype(jnp.bfloat16), v_new_b, preferred_element_type=jnp.float32)) / l
        o0, o1, o2 = o[0:R], o[R:2 * R], o[2 * R:3 * R]
        ya = jnp.where(lo_lane, o0, pltpu.roll(pltpu.roll(o0, HEAD_DIM, axis=1), half, axis=0))
        yb = jnp.where(lo_lane, o1, pltpu.roll(o1, half, axis=0))
        yc = jnp.where(lo_lane, pltpu.roll(o2, HEAD_DIM, axis=1), pltpu.roll(o2, half, axis=0))
        y_swa = jnp.concatenate([ya, yb, yc], axis=1) * _silu(z_ref[rows, OFF_SZ:OFF_SZ + D_SWA])
        ycat_ref[rows, YOFF_SWA:YOFF_SWA + D_SWA] = y_swa

        k_new_t = jnp.concatenate([state_pad, k_new], axis=0).T
        v_new_t = jnp.concatenate([state_pad, v_new], axis=0).T
        keep = lane_sq < WINDOW - half
        kt_state_ref[n] = jnp.where(keep, pltpu.roll(kt_old, WINDOW - half, axis=1), k_new_t)
        vt_state_ref[n] = jnp.where(keep, pltpu.roll(vt_old, WINDOW - half, axis=1), v_new_t)

        m0 = z_ref[rows, OFF_MQ:OFF_MQ + 128] * QK_SCALE
        m1 = z_ref[rows, OFF_MQ + 128:OFF_MQ + 256] * QK_SCALE
        zero = jnp.zeros_like(m0)
        qm = jnp.concatenate(
            [jnp.concatenate([jnp.where(diag, m0, 0.0), zero], axis=1),
             jnp.concatenate([zero, jnp.where(diag, m1, 0.0)], axis=1)], axis=0).astype(jnp.bfloat16)
        s = jnp.dot(qm, mkt_ref[n].astype(jnp.bfloat16), preferred_element_type=jnp.float32)
        m = jnp.max(s, axis=1, keepdims=True)
        p = jnp.exp(s - m)
        l = jnp.sum(p, axis=1, keepdims=True)
        o = lax.dot_general(p.astype(jnp.bfloat16), mvt_ref[n].astype(jnp.bfloat16), _NT,
                            preferred_element_type=jnp.float32) / l
        oa, ob = o[0:R, 0:128], o[R:2 * R, 128:256]
        y_mem = jnp.concatenate([jnp.where(lo_lane, oa, pltpu.roll(oa, half, axis=0)),
                                 jnp.where(lo_lane, ob, pltpu.roll(ob, half, axis=0))], axis=1)
        ycat_ref[rows, YOFF_MEM:YOFF_MEM + D_MEMQ] = y_mem * _silu(z_ref[rows, OFF_MZ:OFF_MZ + D_MEMQ])
        return carry

    lax.fori_loop(0, ns, body, 0, unroll=2)

    @pl.when(step == group - 1)
    def _():
        y = jnp.dot(ycat_ref[...].astype(jnp.bfloat16), w_out_ref[...], preferred_element_type=jnp.float32)
        out = x_ref[...] + _rms_norm(y, post_g_ref[...])
        y_ref[...] = out.reshape(nseq, R, D_MODEL)[:, 0:half, :]


def _sample_layer(x8, conv_past8, ckt, cvt, mkt, mvt, pre_g, post_g, w_in_t, conv_w, sink_col, w_out, *, ns, group):
    N = ckt.shape[0]
    R = SAMPLE_ROWS
    nseq = ns * group
    full = lambda shape: pl.BlockSpec(shape, lambda o, i: (0,) * len(shape))
    per_group = lambda shape: pl.BlockSpec(shape, lambda o, i: (o,) + (0,) * (len(shape) - 1))
    per_step = lambda shape: pl.BlockSpec(shape, lambda o, i: (o * group + i,) + (0,) * (len(shape) - 1))
    kernel = functools.partial(_sample_kernel, ns=ns, group=group)
    return pl.pallas_call(
        kernel,
        grid=(N // nseq, group),
        in_specs=[
            per_group((nseq * R, D_MODEL)),
            per_group((nseq, R, D_CONV)),
            per_step((ns, D_SWA_KV, WINDOW)),
            per_step((ns, D_SWA_KV, WINDOW)),
            per_step((ns, D_MEMQ, N_MEM)),
            per_step((ns, D_MEMQ, N_MEM)),
            full((1, D_MODEL)),
            full((1, D_MODEL)),
            full((D_IN, D_MODEL)),
            full((CONV_WIDTH, D_CONV)),
            full((N_SWA_HEADS * R // 2, 1)),
            full((D_MODEL, D_MODEL)),
        ],
        out_specs=[
            per_group((nseq, R // 2, D_MODEL)),
            per_group((nseq, CONV_WIDTH - 1, D_CONV)),
            per_step((ns, D_SWA_KV, WINDOW)),
            per_step((ns, D_SWA_KV, WINDOW)),
        ],
        out_shape=[
            jax.ShapeDtypeStruct((N, R // 2, D_MODEL), jnp.float32),
            jax.ShapeDtypeStruct((N, CONV_WIDTH - 1, D_CONV), jnp.float32),
            jax.ShapeDtypeStruct((N, D_SWA_KV, WINDOW), jnp.float32),
            jax.ShapeDtypeStruct((N, D_SWA_KV, WINDOW), jnp.float32),
        ],
        scratch_shapes=[
            pltpu.VMEM((nseq * R, D_IN), jnp.float32),
            pltpu.VMEM((nseq * R, D_MODEL), jnp.float32),
        ],
        compiler_params=pltpu.CompilerParams(
            dimension_semantics=("arbitrary", "arbitrary"),
            vmem_limit_bytes=V7X_VMEM_LIMIT_BYTES),
        name="sample_layer",
    )(x8, conv_past8, ckt, cvt, mkt, mvt, pre_g, post_g, w_in_t, conv_w, sink_col, w_out)


def _heads_last_to_keys_last(a):
    n, keys, heads, dim = a.shape
    return jnp.transpose(a, (0, 2, 3, 1)).reshape(n, heads * dim, keys)


def _keys_last_to_heads_last(a, heads):
    n, hd, keys = a.shape
    return jnp.transpose(a.reshape(n, heads, hd // heads, keys), (0, 3, 1, 2))[None]


def kernel(x_prompt, x_sample, mem_prompt, state_conv, cache_swa_k, cache_swa_v, cache_mem_k, cache_mem_v,
           pre_norm_g, post_norm_g, w_in, conv_w, attn_sinks, mem_norm_g, w_mem_k, w_mem_v, w_out):
    assert w_in.shape[0] == 1, "one layer, as the problem states"
    N, TS, _ = x_sample.shape
    assert TS == SAMPLE_ROWS // 2 and cache_swa_k.shape[2] == WINDOW
    l = 0

    pre_g = pre_norm_g[l].reshape(1, D_MODEL)
    post_g = post_norm_g[l].reshape(1, D_MODEL)
    w_in_t = w_in[l].astype(jnp.bfloat16).T
    w_out_bf = w_out[l].astype(jnp.bfloat16)
    sinks = attn_sinks[l].astype(jnp.float32)

    mkt, mvt, mkb, mvtb = _mem_kv(mem_prompt, mem_norm_g[l], w_mem_k[l], w_mem_v[l])
    sink_rows = jnp.repeat(sinks.reshape(N_SWA_KV, SWA_GROUP), BLOCK, axis=1)
    y_p, conv_p, kt_p, vt_p = _prompt_layer(
        x_prompt, pre_g, post_g, w_in_t, conv_w[l].T, sink_rows, mkb, mvtb, w_out_bf, tq=PROMPT_TQ, nsub=PROMPT_NSUB)

    x8 = jnp.concatenate([x_sample, x_sample], axis=1).reshape(N * SAMPLE_ROWS, D_MODEL)
    conv_past8 = jnp.pad(state_conv[l], ((0, 0), (0, SAMPLE_ROWS - (CONV_WIDTH - 1)), (0, 0)))
    sink_col = jnp.repeat(sinks, SAMPLE_ROWS // 2).reshape(N_SWA_HEADS * SAMPLE_ROWS // 2, 1)
    y_s, conv_s, kt_s, vt_s = _sample_layer(
        x8, conv_past8,
        _heads_last_to_keys_last(cache_swa_k[l]), _heads_last_to_keys_last(cache_swa_v[l]),
        _heads_last_to_keys_last(cache_mem_k[l]), _heads_last_to_keys_last(cache_mem_v[l]),
        pre_g, post_g, w_in_t, conv_w[l], sink_col, w_out_bf, ns=SAMPLE_NS, group=SAMPLE_GROUP)

    return (y_p, y_s,
            conv_p[None],
            _keys_last_to_heads_last(kt_p, N_SWA_KV), _keys_last_to_heads_last(vt_p, N_SWA_KV),
            _keys_last_to_heads_last(mkt, N_MEM_HEADS), _keys_last_to_heads_last(mvt, N_MEM_HEADS),
            conv_s[None],
            _keys_last_to_heads_last(kt_s, N_SWA_KV), _keys_last_to_heads_last(vt_s, N_SWA_KV))
```

```python
import functools

import numpy as np
import jax
import jax.numpy as jnp
from jax import lax
from jax.experimental import pallas as pl
from jax.experimental.pallas import tpu as pltpu

D_MODEL = 1024
HEAD_DIM = 64
D_CONV = 384
N_MEM_HEADS = 4
D_MEMQ = N_MEM_HEADS * HEAD_DIM
D_SWA = 384
N_SWA_HEADS = 6
N_SWA_KV = 2
SWA_GROUP = N_SWA_HEADS // N_SWA_KV
D_SWA_KV = N_SWA_KV * HEAD_DIM
N_MEM = 256
CONV_WIDTH = 3
WINDOW = 128
BLOCK = 128
RMS_EPS = 1e-6
NEG_INF = -1e30
D_IN = 3072
QK_SCALE = HEAD_DIM ** -0.5

OFF_CB, OFF_CC, OFF_CH, OFF_CZ = 0, 384, 768, 1152
OFF_Q, OFF_K, OFF_V, OFF_SZ = 1536, 1920, 2048, 2176
OFF_MQ, OFF_MZ = 2560, 2816
YOFF_CONV, YOFF_SWA, YOFF_MEM = 0, 384, 768

V7X_VMEM_LIMIT_BYTES = 56 * 1024 * 1024

PROMPT_TQ = 512
PROMPT_NSUB = 2
IN_PROJ_CHUNK = 512
ATTN_WIDTH = 2
SAMPLE_NS = 8
SAMPLE_GROUP = 4
SAMPLE_ROWS = 8
SAMPLE_WIDTH = 16
SAMPLE_NEW = 16


def _alibi_slope(h):
    return float(np.power(np.float32(2.0), np.float32(-8.0 * (h + 1) / N_SWA_HEADS)))


def _rms_norm(x, g):
    return x * lax.rsqrt(jnp.mean(x * x, axis=-1, keepdims=True) + RMS_EPS) * g


def _silu(x):
    return x * jax.nn.sigmoid(x)


_NT = (((1,), (1,)), ((), ()))
_TN = (((0,), (0,)), ((), ()))


def _mem_kv_kernel(mem_ref, g_ref, wk_ref, wv_ref, mkt_ref, mvt_ref, mkb_ref, mvtb_ref):
    m = _rms_norm(mem_ref[0], g_ref[...]).astype(jnp.bfloat16)
    mk = jnp.dot(m, wk_ref[...], preferred_element_type=jnp.float32)
    mv = jnp.dot(m, wv_ref[...], preferred_element_type=jnp.float32)
    mv_t = mv.T
    mkt_ref[0] = mk.T
    mvt_ref[0] = mv_t
    mkb_ref[0] = mk.astype(jnp.bfloat16)
    mvtb_ref[0] = mv_t.astype(jnp.bfloat16)


def _mem_kv(mem, mem_g, w_mk, w_mv):
    B = mem.shape[0]
    full = lambda shape: pl.BlockSpec(shape, lambda b: (0,) * len(shape))
    per_batch = pl.BlockSpec((1, N_MEM, D_MEMQ), lambda b: (b, 0, 0))
    return pl.pallas_call(
        _mem_kv_kernel,
        grid=(B,),
        in_specs=[
            pl.BlockSpec((1, N_MEM, D_MODEL), lambda b: (b, 0, 0)),
            full((1, D_MODEL)),
            full((D_MODEL, D_MEMQ)),
            full((D_MODEL, D_MEMQ)),
        ],
        out_specs=[per_batch] * 4,
        out_shape=[
            jax.ShapeDtypeStruct((B, D_MEMQ, N_MEM), jnp.float32),
            jax.ShapeDtypeStruct((B, D_MEMQ, N_MEM), jnp.float32),
            jax.ShapeDtypeStruct((B, N_MEM, D_MEMQ), jnp.bfloat16),
            jax.ShapeDtypeStruct((B, D_MEMQ, N_MEM), jnp.bfloat16),
        ],
        compiler_params=pltpu.CompilerParams(dimension_semantics=("arbitrary",)),
        name="mem_kv",
    )(mem, mem_g.reshape(1, D_MODEL), w_mk.astype(jnp.bfloat16), w_mv.astype(jnp.bfloat16))


def _prompt_kernel(x_ref, x_next_ref, pre_g_ref, post_g_ref, w_in_t_ref, conv_w_t_ref, sink_ref, mkb_ref, mvt_ref,
                   w_out_ref,
                   y_ref, conv_state_ref, k_state_ref, v_state_ref,
                   zt_ref, ycat_ref, kbuf_ref, vbuf_ref, uprev_ref, bias_ref, *, tq, nsub):
    assert nsub >= 2, "a sub-tile's z^T buffer is refilled while the other sub-tile's mixers run"
    t = pl.program_id(1)
    nblk = tq // BLOCK

    @pl.when((pl.program_id(0) == 0) & (t == 0))
    def _():
        c = lax.broadcasted_iota(jnp.int32, (2 * BLOCK, BLOCK), 0)
        r = lax.broadcasted_iota(jnp.int32, (2 * BLOCK, BLOCK), 1)
        dist = r + BLOCK - c
        valid = (dist >= 0) & (dist < WINDOW)
        distf = dist.astype(jnp.float32)
        for h in range(N_SWA_HEADS):
            g, i = divmod(h, SWA_GROUP)
            bias_ref[g, :, i * BLOCK:(i + 1) * BLOCK] = jnp.where(valid, -_alibi_slope(h) * distf, NEG_INF)

    @pl.when(t == 0)
    def _():
        kbuf_ref[0:BLOCK, :] = jnp.zeros((BLOCK, D_SWA_KV), jnp.bfloat16)
        vbuf_ref[:, 0:BLOCK] = jnp.zeros((D_SWA_KV, BLOCK), jnp.bfloat16)
        uprev_ref[...] = jnp.zeros_like(uprev_ref)

    first_pen = jnp.where(t == 0, NEG_INF, 0.0)
    key_is_prev = lax.broadcasted_iota(jnp.int32, (2 * BLOCK, SWA_GROUP * BLOCK), 0) < BLOCK
    q_zero = jnp.zeros((HEAD_DIM, SWA_GROUP * BLOCK), jnp.bfloat16)
    cw = conv_w_t_ref[...]
    carry = {"u_prev": uprev_ref[...]}

    def in_proj_chunks(x_rows, zt):
        state = {}

        def chunk(c):
            if c == 0:
                state["h"] = _rms_norm(x_rows(), pre_g_ref[...]).astype(jnp.bfloat16)
            rows = slice(c * IN_PROJ_CHUNK, (c + 1) * IN_PROJ_CHUNK)
            zt[rows, :] = lax.dot_general(w_in_t_ref[rows, :], state["h"], _NT,
                                          preferred_element_type=jnp.float32)

        return [functools.partial(chunk, c) for c in range(D_IN // IN_PROJ_CHUNK)]

    def mixer_units(sub):
        zt = zt_ref.at[sub]
        ycat = ycat_ref.at[sub]
        tok0 = sub * tq

        def conv_unit():
            u = zt[OFF_CC:OFF_CC + D_CONV, :] * zt[OFF_CH:OFF_CH + D_CONV, :]
            ucat = jnp.concatenate([carry["u_prev"], u], axis=1)
            conv = (cw[:, 0:1] * pltpu.roll(ucat, 2, axis=1)[:, BLOCK:]
                    + cw[:, 1:2] * pltpu.roll(ucat, 1, axis=1)[:, BLOCK:]
                    + cw[:, 2:3] * u)
            y_conv = zt[OFF_CB:OFF_CB + D_CONV, :] * conv * _silu(zt[OFF_CZ:OFF_CZ + D_CONV, :])
            ycat[YOFF_CONV:YOFF_CONV + D_CONV, :] = y_conv.astype(jnp.bfloat16)
            carry["u_prev"] = u[:, tq - BLOCK:]

        def kv_unit():
            k_nat = zt[OFF_K:OFF_K + D_SWA_KV, :].T
            kbuf_ref[BLOCK + tok0:BLOCK + tok0 + tq, :] = k_nat.astype(jnp.bfloat16)
            vbuf_ref[:, BLOCK + tok0:BLOCK + tok0 + tq] = zt[OFF_V:OFF_V + D_SWA_KV, :].astype(jnp.bfloat16)

        def swa_unit(j, g):
            cols = slice(j * BLOCK, (j + 1) * BLOCK)
            band = slice(tok0 + j * BLOCK, tok0 + (j + 2) * BLOCK)
            q0 = OFF_Q + g * SWA_GROUP * HEAD_DIM
            qt = jnp.concatenate(
                [zt[q0 + i * HEAD_DIM:q0 + (i + 1) * HEAD_DIM, cols] for i in range(SWA_GROUP)],
                axis=1)
            qt = (qt * QK_SCALE).astype(jnp.bfloat16)
            qt = jnp.concatenate([qt, q_zero] if g == 0 else [q_zero, qt], axis=0)
            s = jnp.dot(kbuf_ref[band, :], qt, preferred_element_type=jnp.float32)
            yield
            s = s + bias_ref[g]
            if sub == 0 and j == 0:
                s = s + jnp.where(key_is_prev, first_pen, 0.0)
            sink = sink_ref[g:g + 1, :]
            m = jnp.maximum(jnp.max(s, axis=0, keepdims=True), sink)
            p = jnp.exp(s - m)
            l = jnp.sum(p, axis=0, keepdims=True) + jnp.exp(sink - m)
            p = p.astype(jnp.bfloat16)
            yield
            vband = vbuf_ref[g * HEAD_DIM:(g + 1) * HEAD_DIM, band]
            o = jnp.dot(vband, p, preferred_element_type=jnp.float32)
            o = o / l
            for i in range(SWA_GROUP):
                hh = g * SWA_GROUP + i
                gate = _silu(zt[OFF_SZ + hh * HEAD_DIM:OFF_SZ + (hh + 1) * HEAD_DIM, cols])
                ycat[YOFF_SWA + hh * HEAD_DIM:YOFF_SWA + (hh + 1) * HEAD_DIM, cols] = (
                    o[:, i * BLOCK:(i + 1) * BLOCK] * gate).astype(jnp.bfloat16)

        def mem_unit(hh):
            rows = slice(OFF_MQ + hh * HEAD_DIM, OFF_MQ + (hh + 1) * HEAD_DIM)
            qt = (zt[rows, :] * QK_SCALE).astype(jnp.bfloat16)
            pieces = [jnp.zeros((HEAD_DIM, tq), jnp.bfloat16)] * N_MEM_HEADS
            pieces[hh] = qt
            s = jnp.dot(mkb_ref[0], jnp.concatenate(pieces, axis=0),
                        preferred_element_type=jnp.float32)
            yield
            m = jnp.max(s, axis=0, keepdims=True)
            p = jnp.exp(s - m)
            l = jnp.sum(p, axis=0, keepdims=True)
            p = p.astype(jnp.bfloat16)
            yield
            o = jnp.dot(mvt_ref[0, hh * HEAD_DIM:(hh + 1) * HEAD_DIM, :], p,
                        preferred_element_type=jnp.float32)
            gate = _silu(zt[OFF_MZ + hh * HEAD_DIM:OFF_MZ + (hh + 1) * HEAD_DIM, :])
            ycat[YOFF_MEM + hh * HEAD_DIM:YOFF_MEM + (hh + 1) * HEAD_DIM, :] = (
                o / l * gate).astype(jnp.bfloat16)

        def out_unit():
            y = lax.dot_general(ycat[...], w_out_ref[...], _TN, preferred_element_type=jnp.float32)
            y_ref[0, tok0:tok0 + tq, :] = x_ref[0, tok0:tok0 + tq, :] + _rms_norm(y, post_g_ref[...])

        attention = [functools.partial(swa_unit, j, g) for j in range(nblk) for g in range(N_SWA_KV)]
        attention += [functools.partial(mem_unit, hh) for hh in range(N_MEM_HEADS)]
        return [conv_unit, kv_unit], attention, out_unit

    def interleave(units, chunks):
        (conv_unit, kv_unit), attention, last = units
        pending = list(chunks)
        conv_unit()
        pending.pop(0)()
        kv_unit()
        n_rounds = -(-len(attention) // ATTN_WIDTH) + 2
        chunk_rounds = [k * n_rounds // len(pending) for k in range(len(pending))]
        todo = list(attention)
        active = []
        for rnd in range(n_rounds):
            for _ in range(ATTN_WIDTH):
                if todo:
                    active.append(todo.pop(0)())
            for gen in list(active):
                if next(gen, "done") == "done":
                    active.remove(gen)
            for _ in range(chunk_rounds.count(rnd)):
                pending.pop(0)()
        assert not todo and not active and not pending
        last()

    @pl.when((pl.program_id(0) == 0) & (t == 0))
    def _():
        for chunk in in_proj_chunks(lambda: x_ref[0, 0:tq, :], zt_ref.at[0]):
            chunk()

    for sub in range(nsub):
        if sub + 1 < nsub:
            nxt = in_proj_chunks(lambda sub=sub: x_ref[0, (sub + 1) * tq:(sub + 2) * tq, :], zt_ref.at[sub + 1])
        else:
            nxt = in_proj_chunks(lambda: x_next_ref[0], zt_ref.at[0])
        interleave(mixer_units(sub), nxt)

    tile = nsub * tq
    uprev_ref[...] = carry["u_prev"]
    conv_state_ref[0] = carry["u_prev"].T[BLOCK - (CONV_WIDTH - 1):, :]
    kbuf_ref[0:BLOCK, :] = kbuf_ref[tile:tile + BLOCK, :]
    vbuf_ref[:, 0:BLOCK] = vbuf_ref[:, tile:tile + BLOCK]
    k_state_ref[0] = zt_ref[nsub - 1, OFF_K:OFF_K + D_SWA_KV, tq - BLOCK:]
    v_state_ref[0] = zt_ref[nsub - 1, OFF_V:OFF_V + D_SWA_KV, tq - BLOCK:]


def _prompt_layer(x, pre_g, post_g, w_in_t, conv_w_t, sink_rows, mkb, mvtb, w_out, *, tq, nsub):
    B, T, _ = x.shape
    tile = tq * nsub
    full = lambda shape: pl.BlockSpec(shape, lambda b, t: (0,) * len(shape))
    kernel = functools.partial(_prompt_kernel, tq=tq, nsub=nsub)
    steps = T // tile

    def next_first_sub_tile(b, t):
        nxt = jnp.minimum(b * steps + t + 1, B * steps - 1)
        return nxt // steps, (nxt % steps) * nsub, 0

    return pl.pallas_call(
        kernel,
        grid=(B, steps),
        in_specs=[
            pl.BlockSpec((1, tile, D_MODEL), lambda b, t: (b, t, 0)),
            pl.BlockSpec((1, tq, D_MODEL), next_first_sub_tile),
            full((1, D_MODEL)),
            full((1, D_MODEL)),
            full((D_IN, D_MODEL)),
            full((D_CONV, CONV_WIDTH)),
            full((N_SWA_KV, SWA_GROUP * BLOCK)),
            pl.BlockSpec((1, N_MEM, D_MEMQ), lambda b, t: (b, 0, 0)),
            pl.BlockSpec((1, D_MEMQ, N_MEM), lambda b, t: (b, 0, 0)),
            full((D_MODEL, D_MODEL)),
        ],
        out_specs=[
            pl.BlockSpec((1, tile, D_MODEL), lambda b, t: (b, t, 0)),
            pl.BlockSpec((1, CONV_WIDTH - 1, D_CONV), lambda b, t: (b, 0, 0)),
            pl.BlockSpec((1, D_SWA_KV, BLOCK), lambda b, t: (b, 0, 0)),
            pl.BlockSpec((1, D_SWA_KV, BLOCK), lambda b, t: (b, 0, 0)),
        ],
        out_shape=[
            jax.ShapeDtypeStruct((B, T, D_MODEL), jnp.float32),
            jax.ShapeDtypeStruct((B, CONV_WIDTH - 1, D_CONV), jnp.float32),
            jax.ShapeDtypeStruct((B, D_SWA_KV, BLOCK), jnp.float32),
            jax.ShapeDtypeStruct((B, D_SWA_KV, BLOCK), jnp.float32),
        ],
        scratch_shapes=[
            pltpu.VMEM((nsub, D_IN, tq), jnp.float32),
            pltpu.VMEM((nsub, D_MODEL, tq), jnp.bfloat16),
            pltpu.VMEM((BLOCK + tile, D_SWA_KV), jnp.bfloat16),
            pltpu.VMEM((D_SWA_KV, BLOCK + tile), jnp.bfloat16),
            pltpu.VMEM((D_CONV, BLOCK), jnp.float32),
            pltpu.VMEM((N_SWA_KV, 2 * BLOCK, SWA_GROUP * BLOCK), jnp.float32),
        ],
        compiler_params=pltpu.CompilerParams(
            dimension_semantics=("arbitrary", "arbitrary"),
            vmem_limit_bytes=V7X_VMEM_LIMIT_BYTES),
        name="prompt_layer",
    )(x, x, pre_g, post_g, w_in_t, conv_w_t, sink_rows, mkb, mvtb, w_out)


def _sample_kernel(x_ref, conv_past_ref, ckt_ref, cvt_ref, mkt_ref, mvt_ref,
                   pre_g_ref, post_g_ref, w_in_t_ref, conv_w_ref, sink_ref, w_out_ref,
                   y_ref, conv_state_ref, kt_state_ref, vt_state_ref,
                   z_ref, ycat_ref, *, ns, group):
    R = SAMPLE_ROWS
    half = R // 2
    nseq = ns * group
    step = pl.program_id(1)

    @pl.when(step == 0)
    def _():
        h = _rms_norm(x_ref[...], pre_g_ref[...]).astype(jnp.bfloat16)
        z_ref[...] = lax.dot_general(h, w_in_t_ref[...], _NT, preferred_element_type=jnp.float32)
        u = (z_ref[:, OFF_CC:OFF_CC + D_CONV] * z_ref[:, OFF_CH:OFF_CH + D_CONV]).reshape(nseq, R, D_CONV)
        row3 = lax.broadcasted_iota(jnp.int32, (nseq, R, D_CONV), 1)
        u_full = jnp.where(row3 < CONV_WIDTH - 1, conv_past_ref[...], pltpu.roll(u, CONV_WIDTH - 1, axis=1))
        cw = conv_w_ref[...]
        conv = (cw[0:1, :] * u_full
                + cw[1:2, :] * pltpu.roll(u_full, R - 1, axis=1)
                + cw[2:3, :] * pltpu.roll(u_full, R - 2, axis=1))
        conv_state_ref[...] = pltpu.roll(u_full, R - half, axis=1)[:, 0:CONV_WIDTH - 1, :]
        y_conv = (z_ref[:, OFF_CB:OFF_CB + D_CONV] * conv.reshape(nseq * R, D_CONV)
                  * _silu(z_ref[:, OFF_CZ:OFF_CZ + D_CONV]))
        ycat_ref[:, YOFF_CONV:YOFF_CONV + D_CONV] = y_conv

    row = lax.broadcasted_iota(jnp.int32, (R, 2 * HEAD_DIM), 0)
    lane = lax.broadcasted_iota(jnp.int32, (R, 2 * HEAD_DIM), 1)
    lo_row = row < half
    lo_lane = lane < HEAD_DIM
    diag = lo_row == lo_lane
    lane_sq = lax.broadcasted_iota(jnp.int32, (WINDOW, WINDOW), 1)

    def pair_bias(ncols, dist_of):
        rr = lax.broadcasted_iota(jnp.int32, (R, ncols), 0)
        cc = lax.broadcasted_iota(jnp.int32, (R, ncols), 1)
        dist, valid = dist_of(rr % half, cc)
        distf = dist.astype(jnp.float32)
        tiles = []
        for pair in range(N_SWA_HEADS // 2):
            slope = jnp.where(rr < half, _alibi_slope(2 * pair), _alibi_slope(2 * pair + 1))
            tiles.append(jnp.where(valid, -slope * distf, NEG_INF))
        return jnp.concatenate(tiles, axis=0)

    def cached_dist(tok, c):
        d = tok + WINDOW - c
        return d, d < WINDOW

    def new_dist(tok, c):
        d = tok - c
        return d, (d >= 0) & (c < half)

    bias_c = pair_bias(WINDOW, cached_dist)
    bias_n = pair_bias(SAMPLE_NEW, new_dist)
    sink_col = sink_ref[...]
    state_pad = jnp.zeros((WINDOW - R, D_SWA_KV), jnp.float32)

    def seq_rows(n):
        return pl.ds(pl.multiple_of((step * ns + n) * R, R), R)

    def swa_unit(n):
        rows = seq_rows(n)
        qa = z_ref[rows, OFF_Q:OFF_Q + 128] * QK_SCALE
        qb = z_ref[rows, OFF_Q + 128:OFF_Q + 256] * QK_SCALE
        qc = z_ref[rows, OFF_Q + 256:OFF_Q + 384] * QK_SCALE
        t0 = jnp.where(lo_lane, jnp.where(lo_row, qa, pltpu.roll(qa, HEAD_DIM, axis=1)), 0.0)
        t1 = jnp.where(diag, qb, 0.0)
        t2 = jnp.where(lo_lane, 0.0, jnp.where(lo_row, pltpu.roll(qc, HEAD_DIM, axis=1), qc))
        qs = jnp.concatenate([t0, t1, t2], axis=0).astype(jnp.bfloat16)

        k_new = z_ref[rows, OFF_K:OFF_K + D_SWA_KV]
        v_new = z_ref[rows, OFF_V:OFF_V + D_SWA_KV]
        k_new_b = jnp.concatenate([k_new, k_new], axis=0).astype(jnp.bfloat16)
        v_new_b = jnp.concatenate([v_new, v_new], axis=0).astype(jnp.bfloat16)
        kt_old = ckt_ref[n].astype(jnp.bfloat16)

        s_c = jnp.dot(qs, kt_old, preferred_element_type=jnp.float32) + bias_c
        s_n = lax.dot_general(qs, k_new_b, _NT, preferred_element_type=jnp.float32) + bias_n
        yield
        m = jnp.maximum(jnp.maximum(jnp.max(s_c, axis=1, keepdims=True), jnp.max(s_n, axis=1, keepdims=True)),
                        sink_col)
        p_c = jnp.exp(s_c - m)
        p_n = jnp.exp(s_n - m)
        l = (jnp.sum(p_c, axis=1, keepdims=True) + jnp.sum(p_n, axis=1, keepdims=True) + jnp.exp(sink_col - m))
        p_c = p_c.astype(jnp.bfloat16)
        p_n = p_n.astype(jnp.bfloat16)
        yield
        o = (lax.dot_general(p_c, cvt_ref[n].astype(jnp.bfloat16), _NT, preferred_element_type=jnp.float32)
             + jnp.dot(p_n, v_new_b, preferred_element_type=jnp.float32)) / l
        yield
        o0, o1, o2 = o[0:R], o[R:2 * R], o[2 * R:3 * R]
        ya = jnp.where(lo_lane, o0, pltpu.roll(pltpu.roll(o0, HEAD_DIM, axis=1), half, axis=0))
        yb = jnp.where(lo_lane, o1, pltpu.roll(o1, half, axis=0))
        yc = jnp.where(lo_lane, pltpu.roll(o2, HEAD_DIM, axis=1), pltpu.roll(o2, half, axis=0))
        y_swa = jnp.concatenate([ya, yb, yc], axis=1) * _silu(z_ref[rows, OFF_SZ:OFF_SZ + D_SWA])
        ycat_ref[rows, YOFF_SWA:YOFF_SWA + D_SWA] = y_swa

    def state_unit(n):
        rows = seq_rows(n)
        k_new_t = jnp.concatenate([state_pad, z_ref[rows, OFF_K:OFF_K + D_SWA_KV]], axis=0).T
        v_new_t = jnp.concatenate([state_pad, z_ref[rows, OFF_V:OFF_V + D_SWA_KV]], axis=0).T
        keep = lane_sq < WINDOW - half
        kt_state_ref[n] = jnp.where(keep, pltpu.roll(ckt_ref[n], WINDOW - half, axis=1), k_new_t)
        vt_state_ref[n] = jnp.where(keep, pltpu.roll(cvt_ref[n], WINDOW - half, axis=1), v_new_t)

    def mem_unit(n):
        rows = seq_rows(n)
        m0 = z_ref[rows, OFF_MQ:OFF_MQ + 128] * QK_SCALE
        m1 = z_ref[rows, OFF_MQ + 128:OFF_MQ + 256] * QK_SCALE
        zero = jnp.zeros_like(m0)
        qm = jnp.concatenate(
            [jnp.concatenate([jnp.where(diag, m0, 0.0), zero], axis=1),
             jnp.concatenate([zero, jnp.where(diag, m1, 0.0)], axis=1)], axis=0).astype(jnp.bfloat16)
        s = jnp.dot(qm, mkt_ref[n].astype(jnp.bfloat16), preferred_element_type=jnp.float32)
        yield
        m = jnp.max(s, axis=1, keepdims=True)
        p = jnp.exp(s - m)
        l = jnp.sum(p, axis=1, keepdims=True)
        p = p.astype(jnp.bfloat16)
        yield
        o = lax.dot_general(p, mvt_ref[n].astype(jnp.bfloat16), _NT,
                            preferred_element_type=jnp.float32) / l
        yield
        oa, ob = o[0:R, 0:128], o[R:2 * R, 128:256]
        y_mem = jnp.concatenate([jnp.where(lo_lane, oa, pltpu.roll(oa, half, axis=0)),
                                 jnp.where(lo_lane, ob, pltpu.roll(ob, half, axis=0))], axis=1)
        ycat_ref[rows, YOFF_MEM:YOFF_MEM + D_MEMQ] = y_mem * _silu(z_ref[rows, OFF_MZ:OFF_MZ + D_MEMQ])

    todo = [functools.partial(unit, n) for n in range(ns) for unit in (swa_unit, mem_unit)]
    states = [functools.partial(state_unit, n) for n in range(ns)]
    active = []
    while todo or active:
        for _ in range(SAMPLE_WIDTH):
            if todo:
                active.append(todo.pop(0)())
        for gen in list(active):
            if next(gen, "done") == "done":
                active.remove(gen)
        if states:
            states.pop(0)()
    for unit in states:
        unit()

    @pl.when(step == group - 1)
    def _():
        y = jnp.dot(ycat_ref[...].astype(jnp.bfloat16), w_out_ref[...], preferred_element_type=jnp.float32)
        out = x_ref[...] + _rms_norm(y, post_g_ref[...])
        y_ref[...] = out.reshape(nseq, R, D_MODEL)[:, 0:half, :]


def _sample_layer(x8, conv_past8, ckt, cvt, mkt, mvt, pre_g, post_g, w_in_t, conv_w, sink_col, w_out, *, ns, group):
    N = ckt.shape[0]
    R = SAMPLE_ROWS
    nseq = ns * group
    full = lambda shape: pl.BlockSpec(shape, lambda o, i: (0,) * len(shape))
    per_group = lambda shape: pl.BlockSpec(shape, lambda o, i: (o,) + (0,) * (len(shape) - 1))
    per_step = lambda shape: pl.BlockSpec(shape, lambda o, i: (o * group + i,) + (0,) * (len(shape) - 1))
    kernel = functools.partial(_sample_kernel, ns=ns, group=group)
    return pl.pallas_call(
        kernel,
        grid=(N // nseq, group),
        in_specs=[
            per_group((nseq * R, D_MODEL)),
            per_group((nseq, R, D_CONV)),
            per_step((ns, D_SWA_KV, WINDOW)),
            per_step((ns, D_SWA_KV, WINDOW)),
            per_step((ns, D_MEMQ, N_MEM)),
            per_step((ns, D_MEMQ, N_MEM)),
            full((1, D_MODEL)),
            full((1, D_MODEL)),
            full((D_IN, D_MODEL)),
            full((CONV_WIDTH, D_CONV)),
            full((N_SWA_HEADS * R // 2, 1)),
            full((D_MODEL, D_MODEL)),
        ],
        out_specs=[
            per_group((nseq, R // 2, D_MODEL)),
            per_group((nseq, CONV_WIDTH - 1, D_CONV)),
            per_step((ns, D_SWA_KV, WINDOW)),
            per_step((ns, D_SWA_KV, WINDOW)),
        ],
        out_shape=[
            jax.ShapeDtypeStruct((N, R // 2, D_MODEL), jnp.float32),
            jax.ShapeDtypeStruct((N, CONV_WIDTH - 1, D_CONV), jnp.float32),
            jax.ShapeDtypeStruct((N, D_SWA_KV, WINDOW), jnp.float32),
            jax.ShapeDtypeStruct((N, D_SWA_KV, WINDOW), jnp.float32),
        ],
        scratch_shapes=[
            pltpu.VMEM((nseq * R, D_IN), jnp.float32),
            pltpu.VMEM((nseq * R, D_MODEL), jnp.float32),
        ],
        compiler_params=pltpu.CompilerParams(
            dimension_semantics=("arbitrary", "arbitrary"),
            vmem_limit_bytes=V7X_VMEM_LIMIT_BYTES),
        name="sample_layer",
    )(x8, conv_past8, ckt, cvt, mkt, mvt, pre_g, post_g, w_in_t, conv_w, sink_col, w_out)


def _heads_last_to_keys_last(a):
    n, keys, heads, dim = a.shape
    return jnp.transpose(a, (0, 2, 3, 1)).reshape(n, heads * dim, keys)


def _keys_last_to_heads_last(a, heads):
    n, hd, keys = a.shape
    return jnp.transpose(a.reshape(n, heads, hd // heads, keys), (0, 3, 1, 2))[None]


def kernel(x_prompt, x_sample, mem_prompt, state_conv, cache_swa_k, cache_swa_v, cache_mem_k, cache_mem_v,
           pre_norm_g, post_norm_g, w_in, conv_w, attn_sinks, mem_norm_g, w_mem_k, w_mem_v, w_out):
    assert w_in.shape[0] == 1, "one layer, as the problem states"
    N, TS, _ = x_sample.shape
    assert TS == SAMPLE_ROWS // 2 and cache_swa_k.shape[2] == WINDOW
    l = 0

    pre_g = pre_norm_g[l].reshape(1, D_MODEL)
    post_g = post_norm_g[l].reshape(1, D_MODEL)
    w_in_t = w_in[l].astype(jnp.bfloat16).T
    w_out_bf = w_out[l].astype(jnp.bfloat16)
    sinks = attn_sinks[l].astype(jnp.float32)

    mkt, mvt, mkb, mvtb = _mem_kv(mem_prompt, mem_norm_g[l], w_mem_k[l], w_mem_v[l])
    sink_rows = jnp.repeat(sinks.reshape(N_SWA_KV, SWA_GROUP), BLOCK, axis=1)
    y_p, conv_p, kt_p, vt_p = _prompt_layer(
        x_prompt, pre_g, post_g, w_in_t, conv_w[l].T, sink_rows, mkb, mvtb, w_out_bf, tq=PROMPT_TQ, nsub=PROMPT_NSUB)

    x8 = jnp.concatenate([x_sample, x_sample], axis=1).reshape(N * SAMPLE_ROWS, D_MODEL)
    conv_past8 = jnp.pad(state_conv[l], ((0, 0), (0, SAMPLE_ROWS - (CONV_WIDTH - 1)), (0, 0)))
    sink_col = jnp.repeat(sinks, SAMPLE_ROWS // 2).reshape(N_SWA_HEADS * SAMPLE_ROWS // 2, 1)
    y_s, conv_s, kt_s, vt_s = _sample_layer(
        x8, conv_past8,
        _heads_last_to_keys_last(cache_swa_k[l]), _heads_last_to_keys_last(cache_swa_v[l]),
        _heads_last_to_keys_last(cache_mem_k[l]), _heads_last_to_keys_last(cache_mem_v[l]),
        pre_g, post_g, w_in_t, conv_w[l], sink_col, w_out_bf, ns=SAMPLE_NS, group=SAMPLE_GROUP)

    return (y_p, y_s,
            conv_p[None],
            _keys_last_to_heads_last(kt_p, N_SWA_KV), _keys_last_to_heads_last(vt_p, N_SWA_KV),
            _keys_last_to_heads_last(mkt, N_MEM_HEADS), _keys_last_to_heads_last(mvt, N_MEM_HEADS),
            conv_s[None],
            _keys_last_to_heads_last(kt_s, N_SWA_KV), _keys_last_to_heads_last(vt_s, N_SWA_KV))
```

```python
import functools

import numpy as np
import jax
import jax.numpy as jnp
from jax import lax
from jax.experimental import pallas as pl
from jax.experimental.pallas import tpu as pltpu

D_MODEL = 1024
HEAD_DIM = 64
D_CONV = 384
N_MEM_HEADS = 4
D_MEMQ = N_MEM_HEADS * HEAD_DIM
D_SWA = 384
N_SWA_HEADS = 6
N_SWA_KV = 2
SWA_GROUP = N_SWA_HEADS // N_SWA_KV
D_SWA_KV = N_SWA_KV * HEAD_DIM
N_MEM = 256
CONV_WIDTH = 3
WINDOW = 128
BLOCK = 128
RMS_EPS = 1e-6
NEG_INF = -1e30
D_IN = 3072
QK_SCALE = HEAD_DIM ** -0.5

OFF_CB, OFF_CC, OFF_CH, OFF_CZ = 0, 384, 768, 1152
OFF_Q, OFF_K, OFF_V, OFF_SZ = 1536, 1920, 2048, 2176
OFF_MQ, OFF_MZ = 2560, 2816
YOFF_CONV, YOFF_SWA, YOFF_MEM = 0, 384, 768

V7X_VMEM_LIMIT_BYTES = 56 * 1024 * 1024

MEM_KV_BATCHES = 2
PROMPT_TQ = 512
PROMPT_NSUB = 2
IN_PROJ_CHUNK = 512
ATTN_WIDTH = 2
SAMPLE_NS = 8
SAMPLE_GROUP = 4
SAMPLE_ROWS = 8
SAMPLE_WIDTH = 16
SAMPLE_NEW = 16


def _alibi_slope(h):
    return float(np.power(np.float32(2.0), np.float32(-8.0 * (h + 1) / N_SWA_HEADS)))


def _rms_norm(x, g):
    return x * lax.rsqrt(jnp.mean(x * x, axis=-1, keepdims=True) + RMS_EPS) * g


def _silu(x):
    return x * jax.nn.sigmoid(x)


_NT = (((1,), (1,)), ((), ()))
_TN = (((0,), (0,)), ((), ()))


def _mem_kv_kernel(mem_ref, g_ref, wk_ref, wv_ref, mkt_ref, mvt_ref, mkb_ref, mvtb_ref, *, nb):
    mem = mem_ref[...].reshape(nb * N_MEM, D_MODEL)
    m = _rms_norm(mem, g_ref[...]).astype(jnp.bfloat16)
    mk = jnp.dot(m, wk_ref[...].astype(jnp.bfloat16), preferred_element_type=jnp.float32)
    mv = jnp.dot(m, wv_ref[...].astype(jnp.bfloat16), preferred_element_type=jnp.float32)
    for b in range(nb):
        mk_b = mk[b * N_MEM:(b + 1) * N_MEM]
        mv_t = mv[b * N_MEM:(b + 1) * N_MEM].T
        mkt_ref[b] = mk_b.T
        mvt_ref[b] = mv_t
        mkb_ref[b] = mk_b.astype(jnp.bfloat16)
        mvtb_ref[b] = mv_t.astype(jnp.bfloat16)


def _mem_kv(mem, mem_g, w_mk, w_mv):
    B = mem.shape[0]
    nb = MEM_KV_BATCHES
    full = lambda shape: pl.BlockSpec(shape, lambda b: (0,) * len(shape))
    per_batch = pl.BlockSpec((nb, N_MEM, D_MEMQ), lambda b: (b, 0, 0))
    return pl.pallas_call(
        functools.partial(_mem_kv_kernel, nb=nb),
        grid=(B // nb,),
        in_specs=[
            pl.BlockSpec((nb, N_MEM, D_MODEL), lambda b: (b, 0, 0)),
            full((1, D_MODEL)),
            full((D_MODEL, D_MEMQ)),
            full((D_MODEL, D_MEMQ)),
        ],
        out_specs=[per_batch] * 4,
        out_shape=[
            jax.ShapeDtypeStruct((B, D_MEMQ, N_MEM), jnp.float32),
            jax.ShapeDtypeStruct((B, D_MEMQ, N_MEM), jnp.float32),
            jax.ShapeDtypeStruct((B, N_MEM, D_MEMQ), jnp.bfloat16),
            jax.ShapeDtypeStruct((B, D_MEMQ, N_MEM), jnp.bfloat16),
        ],
        compiler_params=pltpu.CompilerParams(dimension_semantics=("arbitrary",)),
        name="mem_kv",
    )(mem, mem_g.reshape(1, D_MODEL), w_mk, w_mv)


def _prompt_kernel(x_ref, x_next_ref, pre_g_ref, post_g_ref, w_in_t_ref, conv_w_t_ref, sink_ref, mkb_ref, mvt_ref,
                   w_out_ref,
                   y_ref, conv_state_ref, k_state_ref, v_state_ref,
                   zt_ref, ycat_ref, kbuf_ref, vbuf_ref, uprev_ref, bias_ref, *, tq, nsub):
    assert nsub >= 2, "a sub-tile's z^T buffer is refilled while the other sub-tile's mixers run"
    t = pl.program_id(1)
    nblk = tq // BLOCK

    @pl.when((pl.program_id(0) == 0) & (t == 0))
    def _():
        c = lax.broadcasted_iota(jnp.int32, (2 * BLOCK, BLOCK), 0)
        r = lax.broadcasted_iota(jnp.int32, (2 * BLOCK, BLOCK), 1)
        dist = r + BLOCK - c
        valid = (dist >= 0) & (dist < WINDOW)
        distf = dist.astype(jnp.float32)
        for h in range(N_SWA_HEADS):
            g, i = divmod(h, SWA_GROUP)
            bias_ref[g, :, i * BLOCK:(i + 1) * BLOCK] = jnp.where(valid, -_alibi_slope(h) * distf, NEG_INF)

    @pl.when(t == 0)
    def _():
        kbuf_ref[0:BLOCK, :] = jnp.zeros((BLOCK, D_SWA_KV), jnp.bfloat16)
        vbuf_ref[:, 0:BLOCK] = jnp.zeros((D_SWA_KV, BLOCK), jnp.bfloat16)
        uprev_ref[...] = jnp.zeros_like(uprev_ref)

    first_pen = jnp.where(t == 0, NEG_INF, 0.0)
    key_is_prev = lax.broadcasted_iota(jnp.int32, (2 * BLOCK, SWA_GROUP * BLOCK), 0) < BLOCK
    q_zero = jnp.zeros((HEAD_DIM, SWA_GROUP * BLOCK), jnp.bfloat16)
    cw = conv_w_t_ref[...]
    carry = {"u_prev": uprev_ref[...]}
    head_of_lane = lax.broadcasted_iota(jnp.int32, (1, SWA_GROUP * BLOCK), 1) // BLOCK
    sink_rows = []
    for g in range(N_SWA_KV):
        row = jnp.full((1, SWA_GROUP * BLOCK), sink_ref[g * SWA_GROUP], jnp.float32)
        for i in range(1, SWA_GROUP):
            row = jnp.where(head_of_lane == i, sink_ref[g * SWA_GROUP + i], row)
        sink_rows.append(row)

    def in_proj_chunks(x_rows, zt):
        state = {}

        def norm():
            state["h"] = _rms_norm(x_rows(), pre_g_ref[...]).astype(jnp.bfloat16)

        def chunk(c):
            rows = slice(c * IN_PROJ_CHUNK, (c + 1) * IN_PROJ_CHUNK)
            zt[rows, :] = lax.dot_general(w_in_t_ref[rows, :], state["h"], _NT,
                                          preferred_element_type=jnp.float32)

        return norm, [functools.partial(chunk, c) for c in range(D_IN // IN_PROJ_CHUNK)]

    def mixer_units(sub):
        zt = zt_ref.at[sub]
        ycat = ycat_ref.at[sub]
        tok0 = sub * tq

        def conv_unit():
            u = zt[OFF_CC:OFF_CC + D_CONV, :] * zt[OFF_CH:OFF_CH + D_CONV, :]
            ucat = jnp.concatenate([carry["u_prev"], u], axis=1)
            conv = (cw[:, 0:1] * pltpu.roll(ucat, 2, axis=1)[:, BLOCK:]
                    + cw[:, 1:2] * pltpu.roll(ucat, 1, axis=1)[:, BLOCK:]
                    + cw[:, 2:3] * u)
            y_conv = zt[OFF_CB:OFF_CB + D_CONV, :] * conv * _silu(zt[OFF_CZ:OFF_CZ + D_CONV, :])
            ycat[YOFF_CONV:YOFF_CONV + D_CONV, :] = y_conv.astype(jnp.bfloat16)
            carry["u_prev"] = u[:, tq - BLOCK:]

        def kv_unit():
            k_nat = zt[OFF_K:OFF_K + D_SWA_KV, :].T
            kbuf_ref[BLOCK + tok0:BLOCK + tok0 + tq, :] = k_nat.astype(jnp.bfloat16)
            vbuf_ref[:, BLOCK + tok0:BLOCK + tok0 + tq] = zt[OFF_V:OFF_V + D_SWA_KV, :].astype(jnp.bfloat16)

        def swa_unit(j, g):
            cols = slice(j * BLOCK, (j + 1) * BLOCK)
            band = slice(tok0 + j * BLOCK, tok0 + (j + 2) * BLOCK)
            q0 = OFF_Q + g * SWA_GROUP * HEAD_DIM
            qt = jnp.concatenate(
                [zt[q0 + i * HEAD_DIM:q0 + (i + 1) * HEAD_DIM, cols] for i in range(SWA_GROUP)],
                axis=1)
            qt = (qt * QK_SCALE).astype(jnp.bfloat16)
            qt = jnp.concatenate([qt, q_zero] if g == 0 else [q_zero, qt], axis=0)
            s = jnp.dot(kbuf_ref[band, :], qt, preferred_element_type=jnp.float32)
            yield
            s = s + bias_ref[g]
            if sub == 0 and j == 0:
                s = s + jnp.where(key_is_prev, first_pen, 0.0)
            sink = sink_rows[g]
            m = jnp.maximum(jnp.max(s, axis=0, keepdims=True), sink)
            p = jnp.exp(s - m)
            l = jnp.sum(p, axis=0, keepdims=True) + jnp.exp(sink - m)
            p = p.astype(jnp.bfloat16)
            yield
            vband = vbuf_ref[g * HEAD_DIM:(g + 1) * HEAD_DIM, band]
            o = jnp.dot(vband, p, preferred_element_type=jnp.float32)
            o = o / l
            for i in range(SWA_GROUP):
                hh = g * SWA_GROUP + i
                gate = _silu(zt[OFF_SZ + hh * HEAD_DIM:OFF_SZ + (hh + 1) * HEAD_DIM, cols])
                ycat[YOFF_SWA + hh * HEAD_DIM:YOFF_SWA + (hh + 1) * HEAD_DIM, cols] = (
                    o[:, i * BLOCK:(i + 1) * BLOCK] * gate).astype(jnp.bfloat16)

        def mem_unit(hh):
            rows = slice(OFF_MQ + hh * HEAD_DIM, OFF_MQ + (hh + 1) * HEAD_DIM)
            qt = (zt[rows, :] * QK_SCALE).astype(jnp.bfloat16)
            pieces = [jnp.zeros((HEAD_DIM, tq), jnp.bfloat16)] * N_MEM_HEADS
            pieces[hh] = qt
            s = jnp.dot(mkb_ref[0], jnp.concatenate(pieces, axis=0),
                        preferred_element_type=jnp.float32)
            yield
            m = jnp.max(s, axis=0, keepdims=True)
            p = jnp.exp(s - m)
            l = jnp.sum(p, axis=0, keepdims=True)
            p = p.astype(jnp.bfloat16)
            yield
            o = jnp.dot(mvt_ref[0, hh * HEAD_DIM:(hh + 1) * HEAD_DIM, :], p,
                        preferred_element_type=jnp.float32)
            gate = _silu(zt[OFF_MZ + hh * HEAD_DIM:OFF_MZ + (hh + 1) * HEAD_DIM, :])
            ycat[YOFF_MEM + hh * HEAD_DIM:YOFF_MEM + (hh + 1) * HEAD_DIM, :] = (
                o / l * gate).astype(jnp.bfloat16)

        def out_unit():
            y = lax.dot_general(ycat[...], w_out_ref[...], _TN, preferred_element_type=jnp.float32)
            y_ref[0, tok0:tok0 + tq, :] = x_ref[0, tok0:tok0 + tq, :] + _rms_norm(y, post_g_ref[...])

        attention = [functools.partial(swa_unit, j, g) for j in range(nblk) for g in range(N_SWA_KV)]
        attention += [functools.partial(mem_unit, hh) for hh in range(N_MEM_HEADS)]
        return [conv_unit, kv_unit], attention, out_unit

    def interleave(units, chunks, norm_after):
        (conv_unit, kv_unit), attention, last = units
        pending, tail_chunk = list(chunks[:-1]), chunks[-1]
        kv_unit()
        conv_unit()
        n_rounds = -(-len(attention) // ATTN_WIDTH) + 2
        chunk_rounds = [k * n_rounds // len(pending) for k in range(len(pending))]
        todo = list(attention)
        active = []
        for rnd in range(n_rounds):
            for _ in range(ATTN_WIDTH):
                if todo:
                    active.append(todo.pop(0)())
            for gen in list(active):
                if next(gen, "done") == "done":
                    active.remove(gen)
            for _ in range(chunk_rounds.count(rnd)):
                pending.pop(0)()
        assert not todo and not active and not pending
        if norm_after is not None:
            norm_after()
        last()
        tail_chunk()

    @pl.when((pl.program_id(0) == 0) & (t == 0))
    def _():
        norm, chunks = in_proj_chunks(lambda: x_ref[0, 0:tq, :], zt_ref.at[0])
        norm()
        for chunk in chunks:
            chunk()

    proj = [in_proj_chunks(lambda sub=sub: x_ref[0, sub * tq:(sub + 1) * tq, :], zt_ref.at[sub])
            for sub in range(1, nsub)]
    proj.append(in_proj_chunks(lambda: x_next_ref[0], zt_ref.at[0]))
    proj[0][0]()
    for sub in range(nsub):
        norm_after = proj[sub + 1][0] if sub + 1 < nsub else None
        interleave(mixer_units(sub), proj[sub][1], norm_after)

    tile = nsub * tq
    uprev_ref[...] = carry["u_prev"]
    conv_state_ref[0] = carry["u_prev"].T[BLOCK - (CONV_WIDTH - 1):, :]
    kbuf_ref[0:BLOCK, :] = kbuf_ref[tile:tile + BLOCK, :]
    vbuf_ref[:, 0:BLOCK] = vbuf_ref[:, tile:tile + BLOCK]
    k_state_ref[0] = zt_ref[nsub - 1, OFF_K:OFF_K + D_SWA_KV, tq - BLOCK:]
    v_state_ref[0] = zt_ref[nsub - 1, OFF_V:OFF_V + D_SWA_KV, tq - BLOCK:]


def _prompt_layer(x, pre_g, post_g, w_in_t, conv_w_t, sinks, mkb, mvtb, w_out, *, tq, nsub):
    B, T, _ = x.shape
    tile = tq * nsub
    full = lambda shape: pl.BlockSpec(shape, lambda b, t: (0,) * len(shape))
    kernel = functools.partial(_prompt_kernel, tq=tq, nsub=nsub)
    steps = T // tile

    def next_first_sub_tile(b, t):
        nxt = jnp.minimum(b * steps + t + 1, B * steps - 1)
        return nxt // steps, (nxt % steps) * nsub, 0

    return pl.pallas_call(
        kernel,
        grid=(B, steps),
        in_specs=[
            pl.BlockSpec((1, tile, D_MODEL), lambda b, t: (b, t, 0)),
            pl.BlockSpec((1, tq, D_MODEL), next_first_sub_tile),
            full((1, D_MODEL)),
            full((1, D_MODEL)),
            full((D_IN, D_MODEL)),
            full((D_CONV, CONV_WIDTH)),
            pl.BlockSpec(memory_space=pltpu.SMEM),
            pl.BlockSpec((1, N_MEM, D_MEMQ), lambda b, t: (b, 0, 0)),
            pl.BlockSpec((1, D_MEMQ, N_MEM), lambda b, t: (b, 0, 0)),
            full((D_MODEL, D_MODEL)),
        ],
        out_specs=[
            pl.BlockSpec((1, tile, D_MODEL), lambda b, t: (b, t, 0)),
            pl.BlockSpec((1, CONV_WIDTH - 1, D_CONV), lambda b, t: (b, 0, 0)),
            pl.BlockSpec((1, D_SWA_KV, BLOCK), lambda b, t: (b, 0, 0)),
            pl.BlockSpec((1, D_SWA_KV, BLOCK), lambda b, t: (b, 0, 0)),
        ],
        out_shape=[
            jax.ShapeDtypeStruct((B, T, D_MODEL), jnp.float32),
            jax.ShapeDtypeStruct((B, CONV_WIDTH - 1, D_CONV), jnp.float32),
            jax.ShapeDtypeStruct((B, D_SWA_KV, BLOCK), jnp.float32),
            jax.ShapeDtypeStruct((B, D_SWA_KV, BLOCK), jnp.float32),
        ],
        scratch_shapes=[
            pltpu.VMEM((nsub, D_IN, tq), jnp.float32),
            pltpu.VMEM((nsub, D_MODEL, tq), jnp.bfloat16),
            pltpu.VMEM((BLOCK + tile, D_SWA_KV), jnp.bfloat16),
            pltpu.VMEM((D_SWA_KV, BLOCK + tile), jnp.bfloat16),
            pltpu.VMEM((D_CONV, BLOCK), jnp.float32),
            pltpu.VMEM((N_SWA_KV, 2 * BLOCK, SWA_GROUP * BLOCK), jnp.float32),
        ],
        compiler_params=pltpu.CompilerParams(
            dimension_semantics=("arbitrary", "arbitrary"),
            vmem_limit_bytes=V7X_VMEM_LIMIT_BYTES),
        name="prompt_layer",
    )(x, x, pre_g, post_g, w_in_t, conv_w_t, sinks, mkb, mvtb, w_out)


def _sample_kernel(x_ref, conv_past_ref, ckt_ref, cvt_ref, mkt_ref, mvt_ref,
                   pre_g_ref, post_g_ref, w_in_t_ref, conv_w_ref, sink_ref, w_out_ref,
                   y_ref, conv_state_ref, kt_state_ref, vt_state_ref,
                   z_ref, ycat_ref, *, ns, group):
    R = SAMPLE_ROWS
    half = R // 2
    nseq = ns * group
    step = pl.program_id(1)

    @pl.when(step == 0)
    def _():
        h = _rms_norm(x_ref[...], pre_g_ref[...]).astype(jnp.bfloat16)
        z_ref[...] = lax.dot_general(h, w_in_t_ref[...], _NT, preferred_element_type=jnp.float32)
        u = (z_ref[:, OFF_CC:OFF_CC + D_CONV] * z_ref[:, OFF_CH:OFF_CH + D_CONV]).reshape(nseq, R, D_CONV)
        row3 = lax.broadcasted_iota(jnp.int32, (nseq, R, D_CONV), 1)
        u_full = jnp.where(row3 < CONV_WIDTH - 1, conv_past_ref[...], pltpu.roll(u, CONV_WIDTH - 1, axis=1))
        cw = conv_w_ref[...]
        conv = (cw[0:1, :] * u_full
                + cw[1:2, :] * pltpu.roll(u_full, R - 1, axis=1)
                + cw[2:3, :] * pltpu.roll(u_full, R - 2, axis=1))
        conv_state_ref[...] = pltpu.roll(u_full, R - half, axis=1)[:, 0:CONV_WIDTH - 1, :]
        y_conv = (z_ref[:, OFF_CB:OFF_CB + D_CONV] * conv.reshape(nseq * R, D_CONV)
                  * _silu(z_ref[:, OFF_CZ:OFF_CZ + D_CONV]))
        ycat_ref[:, YOFF_CONV:YOFF_CONV + D_CONV] = y_conv

    row = lax.broadcasted_iota(jnp.int32, (R, 2 * HEAD_DIM), 0)
    lane = lax.broadcasted_iota(jnp.int32, (R, 2 * HEAD_DIM), 1)
    lo_row = row < half
    lo_lane = lane < HEAD_DIM
    diag = lo_row == lo_lane
    lane_sq = lax.broadcasted_iota(jnp.int32, (WINDOW, WINDOW), 1)

    def pair_bias(ncols, dist_of):
        rr = lax.broadcasted_iota(jnp.int32, (R, ncols), 0)
        cc = lax.broadcasted_iota(jnp.int32, (R, ncols), 1)
        dist, valid = dist_of(rr % half, cc)
        distf = dist.astype(jnp.float32)
        tiles = []
        for pair in range(N_SWA_HEADS // 2):
            slope = jnp.where(rr < half, _alibi_slope(2 * pair), _alibi_slope(2 * pair + 1))
            tiles.append(jnp.where(valid, -slope * distf, NEG_INF))
        return jnp.concatenate(tiles, axis=0)

    def cached_dist(tok, c):
        d = tok + WINDOW - c
        return d, d < WINDOW

    def new_dist(tok, c):
        d = tok - c
        return d, (d >= 0) & (c < half)

    bias_c = pair_bias(WINDOW, cached_dist)
    bias_n = pair_bias(SAMPLE_NEW, new_dist)
    head_of_row = lax.broadcasted_iota(jnp.int32, (N_SWA_HEADS * half, 1), 0) // half
    sink_col = jnp.full((N_SWA_HEADS * half, 1), sink_ref[0], jnp.float32)
    for hh in range(1, N_SWA_HEADS):
        sink_col = jnp.where(head_of_row == hh, sink_ref[hh], sink_col)
    state_pad = jnp.zeros((WINDOW - R, D_SWA_KV), jnp.float32)

    def seq_rows(n):
        return pl.ds(pl.multiple_of((step * ns + n) * R, R), R)

    def swa_unit(n):
        rows = seq_rows(n)
        qa = z_ref[rows, OFF_Q:OFF_Q + 128] * QK_SCALE
        qb = z_ref[rows, OFF_Q + 128:OFF_Q + 256] * QK_SCALE
        qc = z_ref[rows, OFF_Q + 256:OFF_Q + 384] * QK_SCALE
        t0 = jnp.where(lo_lane, jnp.where(lo_row, qa, pltpu.roll(qa, HEAD_DIM, axis=1)), 0.0)
        t1 = jnp.where(diag, qb, 0.0)
        t2 = jnp.where(lo_lane, 0.0, jnp.where(lo_row, pltpu.roll(qc, HEAD_DIM, axis=1), qc))
        qs = jnp.concatenate([t0, t1, t2], axis=0).astype(jnp.bfloat16)

        k_new = z_ref[rows, OFF_K:OFF_K + D_SWA_KV]
        v_new = z_ref[rows, OFF_V:OFF_V + D_SWA_KV]
        k_new_b = jnp.concatenate([k_new, k_new], axis=0).astype(jnp.bfloat16)
        v_new_b = jnp.concatenate([v_new, v_new], axis=0).astype(jnp.bfloat16)
        kt_old = ckt_ref[n].astype(jnp.bfloat16)

        s_c = jnp.dot(qs, kt_old, preferred_element_type=jnp.float32) + bias_c
        s_n = lax.dot_general(qs, k_new_b, _NT, preferred_element_type=jnp.float32) + bias_n
        yield
        m = jnp.maximum(jnp.maximum(jnp.max(s_c, axis=1, keepdims=True), jnp.max(s_n, axis=1, keepdims=True)),
                        sink_col)
        p_c = jnp.exp(s_c - m)
        p_n = jnp.exp(s_n - m)
        l = (jnp.sum(p_c, axis=1, keepdims=True) + jnp.sum(p_n, axis=1, keepdims=True) + jnp.exp(sink_col - m))
        p_c = p_c.astype(jnp.bfloat16)
        p_n = p_n.astype(jnp.bfloat16)
        yield
        o = (lax.dot_general(p_c, cvt_ref[n].astype(jnp.bfloat16), _NT, preferred_element_type=jnp.float32)
             + jnp.dot(p_n, v_new_b, preferred_element_type=jnp.float32)) / l
        yield
        o0, o1, o2 = o[0:R], o[R:2 * R], o[2 * R:3 * R]
        ya = jnp.where(lo_lane, o0, pltpu.roll(pltpu.roll(o0, HEAD_DIM, axis=1), half, axis=0))
        yb = jnp.where(lo_lane, o1, pltpu.roll(o1, half, axis=0))
        yc = jnp.where(lo_lane, pltpu.roll(o2, HEAD_DIM, axis=1), pltpu.roll(o2, half, axis=0))
        y_swa = jnp.concatenate([ya, yb, yc], axis=1) * _silu(z_ref[rows, OFF_SZ:OFF_SZ + D_SWA])
        ycat_ref[rows, YOFF_SWA:YOFF_SWA + D_SWA] = y_swa

    def state_unit(n):
        rows = seq_rows(n)
        k_new_t = jnp.concatenate([state_pad, z_ref[rows, OFF_K:OFF_K + D_SWA_KV]], axis=0).T
        v_new_t = jnp.concatenate([state_pad, z_ref[rows, OFF_V:OFF_V + D_SWA_KV]], axis=0).T
        keep = lane_sq < WINDOW - half
        kt_state_ref[n] = jnp.where(keep, pltpu.roll(ckt_ref[n], WINDOW - half, axis=1), k_new_t)
        vt_state_ref[n] = jnp.where(keep, pltpu.roll(cvt_ref[n], WINDOW - half, axis=1), v_new_t)

    def mem_unit(n):
        rows = seq_rows(n)
        m0 = z_ref[rows, OFF_MQ:OFF_MQ + 128] * QK_SCALE
        m1 = z_ref[rows, OFF_MQ + 128:OFF_MQ + 256] * QK_SCALE
        zero = jnp.zeros_like(m0)
        qm = jnp.concatenate(
            [jnp.concatenate([jnp.where(diag, m0, 0.0), zero], axis=1),
             jnp.concatenate([zero, jnp.where(diag, m1, 0.0)], axis=1)], axis=0).astype(jnp.bfloat16)
        s = jnp.dot(qm, mkt_ref[n].astype(jnp.bfloat16), preferred_element_type=jnp.float32)
        yield
        m = jnp.max(s, axis=1, keepdims=True)
        p = jnp.exp(s - m)
        l = jnp.sum(p, axis=1, keepdims=True)
        p = p.astype(jnp.bfloat16)
        yield
        o = lax.dot_general(p, mvt_ref[n].astype(jnp.bfloat16), _NT,
                            preferred_element_type=jnp.float32) / l
        yield
        oa, ob = o[0:R, 0:128], o[R:2 * R, 128:256]
        y_mem = jnp.concatenate([jnp.where(lo_lane, oa, pltpu.roll(oa, half, axis=0)),
                                 jnp.where(lo_lane, ob, pltpu.roll(ob, half, axis=0))], axis=1)
        ycat_ref[rows, YOFF_MEM:YOFF_MEM + D_MEMQ] = y_mem * _silu(z_ref[rows, OFF_MZ:OFF_MZ + D_MEMQ])

    todo = [functools.partial(unit, n) for n in range(ns) for unit in (swa_unit, mem_unit)]
    states = [functools.partial(state_unit, n) for n in range(ns)]
    active = []
    while todo or active:
        for _ in range(SAMPLE_WIDTH):
            if todo:
                active.append(todo.pop(0)())
        for gen in list(active):
            if next(gen, "done") == "done":
                active.remove(gen)
        if states:
            states.pop(0)()
    for unit in states:
        unit()

    @pl.when(step == group - 1)
    def _():
        y = jnp.dot(ycat_ref[...].astype(jnp.bfloat16), w_out_ref[...], preferred_element_type=jnp.float32)
        out = x_ref[...] + _rms_norm(y, post_g_ref[...])
        y_ref[...] = out.reshape(nseq, R, D_MODEL)[:, 0:half, :]


def _sample_layer(x8, conv_past8, ckt, cvt, mkt, mvt, pre_g, post_g, w_in_t, conv_w, sinks, w_out, *, ns, group):
    N = ckt.shape[0]
    R = SAMPLE_ROWS
    nseq = ns * group
    full = lambda shape: pl.BlockSpec(shape, lambda o, i: (0,) * len(shape))
    per_group = lambda shape: pl.BlockSpec(shape, lambda o, i: (o,) + (0,) * (len(shape) - 1))
    per_step = lambda shape: pl.BlockSpec(shape, lambda o, i: (o * group + i,) + (0,) * (len(shape) - 1))
    kernel = functools.partial(_sample_kernel, ns=ns, group=group)
    return pl.pallas_call(
        kernel,
        grid=(N // nseq, group),
        in_specs=[
            per_group((nseq * R, D_MODEL)),
            per_group((nseq, R, D_CONV)),
            per_step((ns, D_SWA_KV, WINDOW)),
            per_step((ns, D_SWA_KV, WINDOW)),
            per_step((ns, D_MEMQ, N_MEM)),
            per_step((ns, D_MEMQ, N_MEM)),
            full((1, D_MODEL)),
            full((1, D_MODEL)),
            full((D_IN, D_MODEL)),
            full((CONV_WIDTH, D_CONV)),
            pl.BlockSpec(memory_space=pltpu.SMEM),
            full((D_MODEL, D_MODEL)),
        ],
        out_specs=[
            per_group((nseq, R // 2, D_MODEL)),
            per_group((nseq, CONV_WIDTH - 1, D_CONV)),
            per_step((ns, D_SWA_KV, WINDOW)),
            per_step((ns, D_SWA_KV, WINDOW)),
        ],
        out_shape=[
            jax.ShapeDtypeStruct((N, R // 2, D_MODEL), jnp.float32),
            jax.ShapeDtypeStruct((N, CONV_WIDTH - 1, D_CONV), jnp.float32),
            jax.ShapeDtypeStruct((N, D_SWA_KV, WINDOW), jnp.float32),
            jax.ShapeDtypeStruct((N, D_SWA_KV, WINDOW), jnp.float32),
        ],
        scratch_shapes=[
            pltpu.VMEM((nseq * R, D_IN), jnp.float32),
            pltpu.VMEM((nseq * R, D_MODEL), jnp.float32),
        ],
        compiler_params=pltpu.CompilerParams(
            dimension_semantics=("arbitrary", "arbitrary"),
            vmem_limit_bytes=V7X_VMEM_LIMIT_BYTES),
        name="sample_layer",
    )(x8, conv_past8, ckt, cvt, mkt, mvt, pre_g, post_g, w_in_t, conv_w, sinks, w_out)


def _heads_last_to_keys_last(a):
    n, keys, heads, dim = a.shape
    return jnp.transpose(a, (0, 2, 3, 1)).reshape(n, heads * dim, keys)


def _keys_last_to_heads_last(a, heads):
    n, hd, keys = a.shape
    return jnp.transpose(a.reshape(n, heads, hd // heads, keys), (0, 3, 1, 2))[None]


def kernel(x_prompt, x_sample, mem_prompt, state_conv, cache_swa_k, cache_swa_v, cache_mem_k, cache_mem_v,
           pre_norm_g, post_norm_g, w_in, conv_w, attn_sinks, mem_norm_g, w_mem_k, w_mem_v, w_out):
    assert w_in.shape[0] == 1, "one layer, as the problem states"
    N, TS, _ = x_sample.shape
    assert TS == SAMPLE_ROWS // 2 and cache_swa_k.shape[2] == WINDOW
    l = 0

    pre_g = pre_norm_g[l].reshape(1, D_MODEL)
    post_g = post_norm_g[l].reshape(1, D_MODEL)
    w_in_t = w_in[l].astype(jnp.bfloat16).T
    w_out_bf = w_out[l].astype(jnp.bfloat16)
    sinks = attn_sinks[l].astype(jnp.float32)

    mkt, mvt, mkb, mvtb = _mem_kv(mem_prompt, mem_norm_g[l], w_mem_k[l], w_mem_v[l])
    y_p, conv_p, kt_p, vt_p = _prompt_layer(
        x_prompt, pre_g, post_g, w_in_t, conv_w[l].T, sinks, mkb, mvtb, w_out_bf, tq=PROMPT_TQ, nsub=PROMPT_NSUB)

    x8 = jnp.concatenate([x_sample, x_sample], axis=1).reshape(N * SAMPLE_ROWS, D_MODEL)
    conv_past8 = jnp.pad(state_conv[l], ((0, 0), (0, SAMPLE_ROWS - (CONV_WIDTH - 1)), (0, 0)))
    y_s, conv_s, kt_s, vt_s = _sample_layer(
        x8, conv_past8,
        _heads_last_to_keys_last(cache_swa_k[l]), _heads_last_to_keys_last(cache_swa_v[l]),
        _heads_last_to_keys_last(cache_mem_k[l]), _heads_last_to_keys_last(cache_mem_v[l]),
        pre_g, post_g, w_in_t, conv_w[l], sinks, w_out_bf, ns=SAMPLE_NS, group=SAMPLE_GROUP)

    return (y_p, y_s,
            conv_p[None],
            _keys_last_to_heads_last(kt_p, N_SWA_KV), _keys_last_to_heads_last(vt_p, N_SWA_KV),
            _keys_last_to_heads_last(mkt, N_MEM_HEADS), _keys_last_to_heads_last(mvt, N_MEM_HEADS),
            conv_s[None],
            _keys_last_to_heads_last(kt_s, N_SWA_KV), _keys_last_to_heads_last(vt_s, N_SWA_KV))
```

```python
import functools

import numpy as np
import jax
import jax.numpy as jnp
from jax import lax
from jax.experimental import pallas as pl
from jax.experimental.pallas import tpu as pltpu

D_MODEL = 1024
HEAD_DIM = 64
D_CONV = 384
N_MEM_HEADS = 4
D_MEMQ = N_MEM_HEADS * HEAD_DIM
D_SWA = 384
N_SWA_HEADS = 6
N_SWA_KV = 2
SWA_GROUP = N_SWA_HEADS // N_SWA_KV
D_SWA_KV = N_SWA_KV * HEAD_DIM
N_MEM = 256
CONV_WIDTH = 3
WINDOW = 128
BLOCK = 128
RMS_EPS = 1e-6
NEG_INF = -1e30
D_IN = 3072
QK_SCALE = HEAD_DIM ** -0.5
LOG2_E = float(np.log2(np.e))
QK_SCALE_LOG2 = QK_SCALE * LOG2_E

OFF_CB, OFF_CC, OFF_CH, OFF_CZ = 0, 384, 768, 1152
OFF_Q, OFF_K, OFF_V, OFF_SZ = 1536, 1920, 2048, 2176
OFF_MQ, OFF_MZ = 2560, 2816
YOFF_CONV, YOFF_SWA, YOFF_MEM = 0, 384, 768

V7X_VMEM_LIMIT_BYTES = 56 * 1024 * 1024

MEM_KV_BATCHES = 2
PROMPT_TQ = 512
PROMPT_NSUB = 2
IN_PROJ_CHUNK = 512
ATTN_WIDTH = 2
SAMPLE_NS = 8
SAMPLE_GROUP = 4
SAMPLE_ROWS = 8
SAMPLE_WIDTH = 16
SAMPLE_NEW = 16


def _alibi_slope(h):
    return float(np.power(np.float32(2.0), np.float32(-8.0 * (h + 1) / N_SWA_HEADS)))


def _rms_norm(x, g):
    return x * lax.rsqrt(jnp.mean(x * x, axis=-1, keepdims=True) + RMS_EPS) * g


def _silu(x):
    return x * jax.nn.sigmoid(x)


_NT = (((1,), (1,)), ((), ()))
_TN = (((0,), (0,)), ((), ()))


def _mem_kv_kernel(mem_ref, g_ref, wk_ref, wv_ref, mkt_ref, mvt_ref, mkb_ref, mvtb_ref, *, nb):
    mem = mem_ref[...].reshape(nb * N_MEM, D_MODEL)
    m = _rms_norm(mem, g_ref[...]).astype(jnp.bfloat16)
    mk = jnp.dot(m, wk_ref[...].astype(jnp.bfloat16), preferred_element_type=jnp.float32)
    mv = jnp.dot(m, wv_ref[...].astype(jnp.bfloat16), preferred_element_type=jnp.float32)
    for b in range(nb):
        mk_b = mk[b * N_MEM:(b + 1) * N_MEM]
        mv_t = mv[b * N_MEM:(b + 1) * N_MEM].T
        mkt_ref[b] = mk_b.T
        mvt_ref[b] = mv_t
        mkb_ref[b] = mk_b.astype(jnp.bfloat16)
        mvtb_ref[b] = mv_t.astype(jnp.bfloat16)


def _mem_kv(mem, mem_g, w_mk, w_mv):
    B = mem.shape[0]
    nb = MEM_KV_BATCHES
    full = lambda shape: pl.BlockSpec(shape, lambda b: (0,) * len(shape))
    per_batch = pl.BlockSpec((nb, N_MEM, D_MEMQ), lambda b: (b, 0, 0))
    return pl.pallas_call(
        functools.partial(_mem_kv_kernel, nb=nb),
        grid=(B // nb,),
        in_specs=[
            pl.BlockSpec((nb, N_MEM, D_MODEL), lambda b: (b, 0, 0)),
            full((1, D_MODEL)),
            full((D_MODEL, D_MEMQ)),
            full((D_MODEL, D_MEMQ)),
        ],
        out_specs=[per_batch] * 4,
        out_shape=[
            jax.ShapeDtypeStruct((B, D_MEMQ, N_MEM), jnp.float32),
            jax.ShapeDtypeStruct((B, D_MEMQ, N_MEM), jnp.float32),
            jax.ShapeDtypeStruct((B, N_MEM, D_MEMQ), jnp.bfloat16),
            jax.ShapeDtypeStruct((B, D_MEMQ, N_MEM), jnp.bfloat16),
        ],
        compiler_params=pltpu.CompilerParams(dimension_semantics=("arbitrary",)),
        name="mem_kv",
    )(mem, mem_g.reshape(1, D_MODEL), w_mk, w_mv)


def _prompt_kernel(x_ref, x_next_ref, pre_g_ref, post_g_ref, w_in_t_ref, conv_w_t_ref, sink_ref, mkb_ref, mvt_ref,
                   w_out_ref,
                   y_ref, conv_state_ref, k_state_ref, v_state_ref,
                   zt_ref, ycat_ref, kbuf_ref, vbuf_ref, uprev_ref, bias_ref, *, tq, nsub):
    assert nsub >= 2, "a sub-tile's z^T buffer is refilled while the other sub-tile's mixers run"
    t = pl.program_id(1)
    nblk = tq // BLOCK

    @pl.when((pl.program_id(0) == 0) & (t == 0))
    def _():
        c = lax.broadcasted_iota(jnp.int32, (BLOCK, BLOCK), 0)
        r = lax.broadcasted_iota(jnp.int32, (BLOCK, BLOCK), 1)
        distf = (r - c + jnp.where(c > r, BLOCK, 0)).astype(jnp.float32)
        for h in range(N_SWA_HEADS):
            g, i = divmod(h, SWA_GROUP)
            bias_ref[g, :, i * BLOCK:(i + 1) * BLOCK] = (-_alibi_slope(h) * LOG2_E) * distf

    @pl.when(t == 0)
    def _():
        kbuf_ref[0:BLOCK, :] = jnp.zeros((BLOCK, D_SWA_KV), jnp.bfloat16)
        vbuf_ref[:, 0:BLOCK] = jnp.zeros((D_SWA_KV, BLOCK), jnp.bfloat16)
        uprev_ref[...] = jnp.zeros_like(uprev_ref)

    first_pen = jnp.where(t == 0, NEG_INF, 0.0)
    slot = lax.broadcasted_iota(jnp.int32, (BLOCK, SWA_GROUP * BLOCK), 0)
    query = lax.broadcasted_iota(jnp.int32, (BLOCK, SWA_GROUP * BLOCK), 1) % BLOCK
    from_prev = slot > query
    q_zero = jnp.zeros((HEAD_DIM, SWA_GROUP * BLOCK), jnp.bfloat16)
    cw = conv_w_t_ref[...]
    carry = {"u_prev": uprev_ref[...]}
    head_of_lane = lax.broadcasted_iota(jnp.int32, (1, SWA_GROUP * BLOCK), 1) // BLOCK
    sink_rows = []
    for g in range(N_SWA_KV):
        row = jnp.full((1, SWA_GROUP * BLOCK), sink_ref[g * SWA_GROUP], jnp.float32)
        for i in range(1, SWA_GROUP):
            row = jnp.where(head_of_lane == i, sink_ref[g * SWA_GROUP + i], row)
        sink_rows.append(row * LOG2_E)

    def in_proj_chunks(x_rows, zt):
        state = {}

        def norm():
            state["h"] = _rms_norm(x_rows(), pre_g_ref[...]).astype(jnp.bfloat16)

        def chunk(c):
            rows = slice(c * IN_PROJ_CHUNK, (c + 1) * IN_PROJ_CHUNK)
            zt[rows, :] = lax.dot_general(w_in_t_ref[rows, :], state["h"], _NT,
                                          preferred_element_type=jnp.float32)

        return norm, [functools.partial(chunk, c) for c in range(D_IN // IN_PROJ_CHUNK)]

    def mixer_units(sub):
        zt = zt_ref.at[sub]
        ycat = ycat_ref.at[sub]
        tok0 = sub * tq

        def conv_unit():
            u = zt[OFF_CC:OFF_CC + D_CONV, :] * zt[OFF_CH:OFF_CH + D_CONV, :]
            ucat = jnp.concatenate([carry["u_prev"], u], axis=1)
            conv = (cw[:, 0:1] * pltpu.roll(ucat, 2, axis=1)[:, BLOCK:]
                    + cw[:, 1:2] * pltpu.roll(ucat, 1, axis=1)[:, BLOCK:]
                    + cw[:, 2:3] * u)
            y_conv = zt[OFF_CB:OFF_CB + D_CONV, :] * conv * _silu(zt[OFF_CZ:OFF_CZ + D_CONV, :])
            ycat[YOFF_CONV:YOFF_CONV + D_CONV, :] = y_conv.astype(jnp.bfloat16)
            carry["u_prev"] = u[:, tq - BLOCK:]

        def kv_unit():
            k_nat = zt[OFF_K:OFF_K + D_SWA_KV, :].T
            kbuf_ref[BLOCK + tok0:BLOCK + tok0 + tq, :] = k_nat.astype(jnp.bfloat16)
            vbuf_ref[:, BLOCK + tok0:BLOCK + tok0 + tq] = zt[OFF_V:OFF_V + D_SWA_KV, :].astype(jnp.bfloat16)

        def swa_unit(j, g):
            cols = slice(j * BLOCK, (j + 1) * BLOCK)
            band = slice(tok0 + j * BLOCK, tok0 + (j + 2) * BLOCK)
            q0 = OFF_Q + g * SWA_GROUP * HEAD_DIM
            qt = jnp.concatenate(
                [zt[q0 + i * HEAD_DIM:q0 + (i + 1) * HEAD_DIM, cols] for i in range(SWA_GROUP)],
                axis=1)
            qt = (qt * QK_SCALE_LOG2).astype(jnp.bfloat16)
            qt = jnp.concatenate([qt, q_zero] if g == 0 else [q_zero, qt], axis=0)
            s = jnp.dot(kbuf_ref[band, :], qt, preferred_element_type=jnp.float32)
            yield
            s = jnp.where(from_prev, s[0:BLOCK], s[BLOCK:]) + bias_ref[g]
            if sub == 0 and j == 0:
                s = s + jnp.where(from_prev, first_pen, 0.0)
            sink = sink_rows[g]
            m = jnp.maximum(jnp.max(s, axis=0, keepdims=True), sink)
            p = jnp.exp2(s - m)
            l = jnp.sum(p, axis=0, keepdims=True) + jnp.exp2(sink - m)
            p = jnp.concatenate([jnp.where(from_prev, p, 0.0), jnp.where(from_prev, 0.0, p)],
                                axis=0).astype(jnp.bfloat16)
            yield
            vband = vbuf_ref[g * HEAD_DIM:(g + 1) * HEAD_DIM, band]
            o = jnp.dot(vband, p, preferred_element_type=jnp.float32)
            o = o / l
            for i in range(SWA_GROUP):
                hh = g * SWA_GROUP + i
                gate = _silu(zt[OFF_SZ + hh * HEAD_DIM:OFF_SZ + (hh + 1) * HEAD_DIM, cols])
                ycat[YOFF_SWA + hh * HEAD_DIM:YOFF_SWA + (hh + 1) * HEAD_DIM, cols] = (
                    o[:, i * BLOCK:(i + 1) * BLOCK] * gate).astype(jnp.bfloat16)

        def mem_unit(hh):
            rows = slice(OFF_MQ + hh * HEAD_DIM, OFF_MQ + (hh + 1) * HEAD_DIM)
            qt = (zt[rows, :] * QK_SCALE_LOG2).astype(jnp.bfloat16)
            pieces = [jnp.zeros((HEAD_DIM, tq), jnp.bfloat16)] * N_MEM_HEADS
            pieces[hh] = qt
            s = jnp.dot(mkb_ref[0], jnp.concatenate(pieces, axis=0),
                        preferred_element_type=jnp.float32)
            yield
            m = jnp.max(s, axis=0, keepdims=True)
            p = jnp.exp2(s - m)
            l = jnp.sum(p, axis=0, keepdims=True)
            p = p.astype(jnp.bfloat16)
            yield
            o = jnp.dot(mvt_ref[0, hh * HEAD_DIM:(hh + 1) * HEAD_DIM, :], p,
                        preferred_element_type=jnp.float32)
            gate = _silu(zt[OFF_MZ + hh * HEAD_DIM:OFF_MZ + (hh + 1) * HEAD_DIM, :])
            ycat[YOFF_MEM + hh * HEAD_DIM:YOFF_MEM + (hh + 1) * HEAD_DIM, :] = (
                o / l * gate).astype(jnp.bfloat16)

        def out_unit():
            y = lax.dot_general(ycat[...], w_out_ref[...], _TN, preferred_element_type=jnp.float32)
            y_ref[0, tok0:tok0 + tq, :] = x_ref[0, tok0:tok0 + tq, :] + _rms_norm(y, post_g_ref[...])

        attention = [functools.partial(swa_unit, j, g) for j in range(nblk) for g in range(N_SWA_KV)]
        attention += [functools.partial(mem_unit, hh) for hh in range(N_MEM_HEADS)]
        return [conv_unit, kv_unit], attention, out_unit

    def interleave(units, chunks, norm_after):
        (conv_unit, kv_unit), attention, last = units
        pending = list(chunks)
        conv_unit()
        pending.pop(0)()
        kv_unit()
        n_rounds = -(-len(attention) // ATTN_WIDTH) + 2
        chunk_rounds = [k * n_rounds // len(pending) for k in range(len(pending))]
        todo = list(attention)
        active = []
        for rnd in range(n_rounds):
            for _ in range(ATTN_WIDTH):
                if todo:
                    active.append(todo.pop(0)())
            for gen in list(active):
                if next(gen, "done") == "done":
                    active.remove(gen)
            for _ in range(chunk_rounds.count(rnd)):
                pending.pop(0)()
        assert not todo and not active and not pending
        if norm_after is not None:
            norm_after()
        last()

    @pl.when((pl.program_id(0) == 0) & (t == 0))
    def _():
        norm, chunks = in_proj_chunks(lambda: x_ref[0, 0:tq, :], zt_ref.at[0])
        norm()
        for chunk in chunks:
            chunk()

    proj = [in_proj_chunks(lambda sub=sub: x_ref[0, sub * tq:(sub + 1) * tq, :], zt_ref.at[sub])
            for sub in range(1, nsub)]
    proj.append(in_proj_chunks(lambda: x_next_ref[0], zt_ref.at[0]))
    proj[0][0]()
    for sub in range(nsub):
        norm_after = proj[sub + 1][0] if sub + 1 < nsub else None
        interleave(mixer_units(sub), proj[sub][1], norm_after)

    tile = nsub * tq
    uprev_ref[...] = carry["u_prev"]
    conv_state_ref[0] = carry["u_prev"].T[BLOCK - (CONV_WIDTH - 1):, :]
    kbuf_ref[0:BLOCK, :] = kbuf_ref[tile:tile + BLOCK, :]
    vbuf_ref[:, 0:BLOCK] = vbuf_ref[:, tile:tile + BLOCK]
    k_state_ref[0] = zt_ref[nsub - 1, OFF_K:OFF_K + D_SWA_KV, tq - BLOCK:]
    v_state_ref[0] = zt_ref[nsub - 1, OFF_V:OFF_V + D_SWA_KV, tq - BLOCK:]


def _prompt_layer(x, pre_g, post_g, w_in_t, conv_w_t, sinks, mkb, mvtb, w_out, *, tq, nsub):
    B, T, _ = x.shape
    tile = tq * nsub
    full = lambda shape: pl.BlockSpec(shape, lambda b, t: (0,) * len(shape))
    kernel = functools.partial(_prompt_kernel, tq=tq, nsub=nsub)
    steps = T // tile

    def next_first_sub_tile(b, t):
        nxt = jnp.minimum(b * steps + t + 1, B * steps - 1)
        return nxt // steps, (nxt % steps) * nsub, 0

    return pl.pallas_call(
        kernel,
        grid=(B, steps),
        in_specs=[
            pl.BlockSpec((1, tile, D_MODEL), lambda b, t: (b, t, 0)),
            pl.BlockSpec((1, tq, D_MODEL), next_first_sub_tile),
            full((1, D_MODEL)),
            full((1, D_MODEL)),
            full((D_IN, D_MODEL)),
            full((D_CONV, CONV_WIDTH)),
            pl.BlockSpec(memory_space=pltpu.SMEM),
            pl.BlockSpec((1, N_MEM, D_MEMQ), lambda b, t: (b, 0, 0)),
            pl.BlockSpec((1, D_MEMQ, N_MEM), lambda b, t: (b, 0, 0)),
            full((D_MODEL, D_MODEL)),
        ],
        out_specs=[
            pl.BlockSpec((1, tile, D_MODEL), lambda b, t: (b, t, 0)),
            pl.BlockSpec((1, CONV_WIDTH - 1, D_CONV), lambda b, t: (b, 0, 0)),
            pl.BlockSpec((1, D_SWA_KV, BLOCK), lambda b, t: (b, 0, 0)),
            pl.BlockSpec((1, D_SWA_KV, BLOCK), lambda b, t: (b, 0, 0)),
        ],
        out_shape=[
            jax.ShapeDtypeStruct((B, T, D_MODEL), jnp.float32),
            jax.ShapeDtypeStruct((B, CONV_WIDTH - 1, D_CONV), jnp.float32),
            jax.ShapeDtypeStruct((B, D_SWA_KV, BLOCK), jnp.float32),
            jax.ShapeDtypeStruct((B, D_SWA_KV, BLOCK), jnp.float32),
        ],
        scratch_shapes=[
            pltpu.VMEM((nsub, D_IN, tq), jnp.float32),
            pltpu.VMEM((nsub, D_MODEL, tq), jnp.bfloat16),
            pltpu.VMEM((BLOCK + tile, D_SWA_KV), jnp.bfloat16),
            pltpu.VMEM((D_SWA_KV, BLOCK + tile), jnp.bfloat16),
            pltpu.VMEM((D_CONV, BLOCK), jnp.float32),
            pltpu.VMEM((N_SWA_KV, BLOCK, SWA_GROUP * BLOCK), jnp.float32),
        ],
        compiler_params=pltpu.CompilerParams(
            dimension_semantics=("arbitrary", "arbitrary"),
            vmem_limit_bytes=V7X_VMEM_LIMIT_BYTES),
        name="prompt_layer",
    )(x, x, pre_g, post_g, w_in_t, conv_w_t, sinks, mkb, mvtb, w_out)


def _sample_kernel(x_ref, conv_past_ref, ckt_ref, cvt_ref, mkt_ref, mvt_ref,
                   pre_g_ref, post_g_ref, w_in_t_ref, conv_w_ref, sink_ref, w_out_ref,
                   y_ref, conv_state_ref, kt_state_ref, vt_state_ref,
                   z_ref, ycat_ref, *, ns, group):
    R = SAMPLE_ROWS
    half = R // 2
    nseq = ns * group
    step = pl.program_id(1)

    @pl.when(step == 0)
    def _():
        h = _rms_norm(x_ref[...], pre_g_ref[...]).astype(jnp.bfloat16)
        z_ref[...] = lax.dot_general(h, w_in_t_ref[...], _NT, preferred_element_type=jnp.float32)
        u = (z_ref[:, OFF_CC:OFF_CC + D_CONV] * z_ref[:, OFF_CH:OFF_CH + D_CONV]).reshape(nseq, R, D_CONV)
        row3 = lax.broadcasted_iota(jnp.int32, (nseq, R, D_CONV), 1)
        u_full = jnp.where(row3 < CONV_WIDTH - 1, conv_past_ref[...], pltpu.roll(u, CONV_WIDTH - 1, axis=1))
        cw = conv_w_ref[...]
        conv = (cw[0:1, :] * u_full
                + cw[1:2, :] * pltpu.roll(u_full, R - 1, axis=1)
                + cw[2:3, :] * pltpu.roll(u_full, R - 2, axis=1))
        conv_state_ref[...] = pltpu.roll(u_full, R - half, axis=1)[:, 0:CONV_WIDTH - 1, :]
        y_conv = (z_ref[:, OFF_CB:OFF_CB + D_CONV] * conv.reshape(nseq * R, D_CONV)
                  * _silu(z_ref[:, OFF_CZ:OFF_CZ + D_CONV]))
        ycat_ref[:, YOFF_CONV:YOFF_CONV + D_CONV] = y_conv

    row = lax.broadcasted_iota(jnp.int32, (R, 2 * HEAD_DIM), 0)
    lane = lax.broadcasted_iota(jnp.int32, (R, 2 * HEAD_DIM), 1)
    lo_row = row < half
    lo_lane = lane < HEAD_DIM
    diag = lo_row == lo_lane
    lane_sq = lax.broadcasted_iota(jnp.int32, (WINDOW, WINDOW), 1)

    def pair_bias(ncols, dist_of):
        rr = lax.broadcasted_iota(jnp.int32, (R, ncols), 0)
        cc = lax.broadcasted_iota(jnp.int32, (R, ncols), 1)
        dist, valid = dist_of(rr % half, cc)
        distf = dist.astype(jnp.float32)
        tiles = []
        for pair in range(N_SWA_HEADS // 2):
            slope = jnp.where(rr < half, _alibi_slope(2 * pair), _alibi_slope(2 * pair + 1))
            tiles.append(jnp.where(valid, -slope * distf, NEG_INF))
        return jnp.concatenate(tiles, axis=0)

    def cached_dist(tok, c):
        d = tok + WINDOW - c
        return d, d < WINDOW

    def new_dist(tok, c):
        d = tok - c
        return d, (d >= 0) & (c < half)

    bias_c = pair_bias(WINDOW, cached_dist)
    bias_n = pair_bias(SAMPLE_NEW, new_dist)
    head_of_row = lax.broadcasted_iota(jnp.int32, (N_SWA_HEADS * half, 1), 0) // half
    sink_col = jnp.full((N_SWA_HEADS * half, 1), sink_ref[0], jnp.float32)
    for hh in range(1, N_SWA_HEADS):
        sink_col = jnp.where(head_of_row == hh, sink_ref[hh], sink_col)
    state_pad = jnp.zeros((WINDOW - R, D_SWA_KV), jnp.float32)

    def seq_rows(n):
        return pl.ds(pl.multiple_of((step * ns + n) * R, R), R)

    def swa_unit(n):
        rows = seq_rows(n)
        qa = z_ref[rows, OFF_Q:OFF_Q + 128] * QK_SCALE
        qb = z_ref[rows, OFF_Q + 128:OFF_Q + 256] * QK_SCALE
        qc = z_ref[rows, OFF_Q + 256:OFF_Q + 384] * QK_SCALE
        t0 = jnp.where(lo_lane, jnp.where(lo_row, qa, pltpu.roll(qa, HEAD_DIM, axis=1)), 0.0)
        t1 = jnp.where(diag, qb, 0.0)
        t2 = jnp.where(lo_lane, 0.0, jnp.where(lo_row, pltpu.roll(qc, HEAD_DIM, axis=1), qc))
        qs = jnp.concatenate([t0, t1, t2], axis=0).astype(jnp.bfloat16)

        k_new = z_ref[rows, OFF_K:OFF_K + D_SWA_KV]
        v_new = z_ref[rows, OFF_V:OFF_V + D_SWA_KV]
        k_new_b = jnp.concatenate([k_new, k_new], axis=0).astype(jnp.bfloat16)
        v_new_b = jnp.concatenate([v_new, v_new], axis=0).astype(jnp.bfloat16)
        kt_old = ckt_ref[n].astype(jnp.bfloat16)

        s_c = jnp.dot(qs, kt_old, preferred_element_type=jnp.float32) + bias_c
        s_n = lax.dot_general(qs, k_new_b, _NT, preferred_element_type=jnp.float32) + bias_n
        yield
        m = jnp.maximum(jnp.maximum(jnp.max(s_c, axis=1, keepdims=True), jnp.max(s_n, axis=1, keepdims=True)),
                        sink_col)
        p_c = jnp.exp(s_c - m)
        p_n = jnp.exp(s_n - m)
        l = (jnp.sum(p_c, axis=1, keepdims=True) + jnp.sum(p_n, axis=1, keepdims=True) + jnp.exp(sink_col - m))
        p_c = p_c.astype(jnp.bfloat16)
        p_n = p_n.astype(jnp.bfloat16)
        yield
        o = (lax.dot_general(p_c, cvt_ref[n].astype(jnp.bfloat16), _NT, preferred_element_type=jnp.float32)
             + jnp.dot(p_n, v_new_b, preferred_element_type=jnp.float32)) / l
        yield
        o0, o1, o2 = o[0:R], o[R:2 * R], o[2 * R:3 * R]
        ya = jnp.where(lo_lane, o0, pltpu.roll(pltpu.roll(o0, HEAD_DIM, axis=1), half, axis=0))
        yb = jnp.where(lo_lane, o1, pltpu.roll(o1, half, axis=0))
        yc = jnp.where(lo_lane, pltpu.roll(o2, HEAD_DIM, axis=1), pltpu.roll(o2, half, axis=0))
        y_swa = jnp.concatenate([ya, yb, yc], axis=1) * _silu(z_ref[rows, OFF_SZ:OFF_SZ + D_SWA])
        ycat_ref[rows, YOFF_SWA:YOFF_SWA + D_SWA] = y_swa

    def state_unit(n):
        rows = seq_rows(n)
        k_new_t = jnp.concatenate([state_pad, z_ref[rows, OFF_K:OFF_K + D_SWA_KV]], axis=0).T
        v_new_t = jnp.concatenate([state_pad, z_ref[rows, OFF_V:OFF_V + D_SWA_KV]], axis=0).T
        keep = lane_sq < WINDOW - half
        kt_state_ref[n] = jnp.where(keep, pltpu.roll(ckt_ref[n], WINDOW - half, axis=1), k_new_t)
        vt_state_ref[n] = jnp.where(keep, pltpu.roll(cvt_ref[n], WINDOW - half, axis=1), v_new_t)

    def mem_unit(n):
        rows = seq_rows(n)
        m0 = z_ref[rows, OFF_MQ:OFF_MQ + 128] * QK_SCALE
        m1 = z_ref[rows, OFF_MQ + 128:OFF_MQ + 256] * QK_SCALE
        zero = jnp.zeros_like(m0)
        qm = jnp.concatenate(
            [jnp.concatenate([jnp.where(diag, m0, 0.0), zero], axis=1),
             jnp.concatenate([zero, jnp.where(diag, m1, 0.0)], axis=1)], axis=0).astype(jnp.bfloat16)
        s = jnp.dot(qm, mkt_ref[n].astype(jnp.bfloat16), preferred_element_type=jnp.float32)
        yield
        m = jnp.max(s, axis=1, keepdims=True)
        p = jnp.exp(s - m)
        l = jnp.sum(p, axis=1, keepdims=True)
        p = p.astype(jnp.bfloat16)
        yield
        o = lax.dot_general(p, mvt_ref[n].astype(jnp.bfloat16), _NT,
                            preferred_element_type=jnp.float32) / l
        yield
        oa, ob = o[0:R, 0:128], o[R:2 * R, 128:256]
        y_mem = jnp.concatenate([jnp.where(lo_lane, oa, pltpu.roll(oa, half, axis=0)),
                                 jnp.where(lo_lane, ob, pltpu.roll(ob, half, axis=0))], axis=1)
        ycat_ref[rows, YOFF_MEM:YOFF_MEM + D_MEMQ] = y_mem * _silu(z_ref[rows, OFF_MZ:OFF_MZ + D_MEMQ])

    todo = [functools.partial(unit, n) for n in range(ns) for unit in (swa_unit, mem_unit)]
    states = [functools.partial(state_unit, n) for n in range(ns)]
    active = []
    while todo or active:
        for _ in range(SAMPLE_WIDTH):
            if todo:
                active.append(todo.pop(0)())
        for gen in list(active):
            if next(gen, "done") == "done":
                active.remove(gen)
        if states:
            states.pop(0)()
    for unit in states:
        unit()

    @pl.when(step == group - 1)
    def _():
        y = jnp.dot(ycat_ref[...].astype(jnp.bfloat16), w_out_ref[...], preferred_element_type=jnp.float32)
        out = x_ref[...] + _rms_norm(y, post_g_ref[...])
        y_ref[...] = out.reshape(nseq, R, D_MODEL)[:, 0:half, :]


def _sample_layer(x8, conv_past8, ckt, cvt, mkt, mvt, pre_g, post_g, w_in_t, conv_w, sinks, w_out, *, ns, group):
    N = ckt.shape[0]
    R = SAMPLE_ROWS
    nseq = ns * group
    full = lambda shape: pl.BlockSpec(shape, lambda o, i: (0,) * len(shape))
    per_group = lambda shape: pl.BlockSpec(shape, lambda o, i: (o,) + (0,) * (len(shape) - 1))
    per_step = lambda shape: pl.BlockSpec(shape, lambda o, i: (o * group + i,) + (0,) * (len(shape) - 1))
    kernel = functools.partial(_sample_kernel, ns=ns, group=group)
    return pl.pallas_call(
        kernel,
        grid=(N // nseq, group),
        in_specs=[
            per_group((nseq * R, D_MODEL)),
            per_group((nseq, R, D_CONV)),
            per_step((ns, D_SWA_KV, WINDOW)),
            per_step((ns, D_SWA_KV, WINDOW)),
            per_step((ns, D_MEMQ, N_MEM)),
            per_step((ns, D_MEMQ, N_MEM)),
            full((1, D_MODEL)),
            full((1, D_MODEL)),
            full((D_IN, D_MODEL)),
            full((CONV_WIDTH, D_CONV)),
            pl.BlockSpec(memory_space=pltpu.SMEM),
            full((D_MODEL, D_MODEL)),
        ],
        out_specs=[
            per_group((nseq, R // 2, D_MODEL)),
            per_group((nseq, CONV_WIDTH - 1, D_CONV)),
            per_step((ns, D_SWA_KV, WINDOW)),
            per_step((ns, D_SWA_KV, WINDOW)),
        ],
        out_shape=[
            jax.ShapeDtypeStruct((N, R // 2, D_MODEL), jnp.float32),
            jax.ShapeDtypeStruct((N, CONV_WIDTH - 1, D_CONV), jnp.float32),
            jax.ShapeDtypeStruct((N, D_SWA_KV, WINDOW), jnp.float32),
            jax.ShapeDtypeStruct((N, D_SWA_KV, WINDOW), jnp.float32),
        ],
        scratch_shapes=[
            pltpu.VMEM((nseq * R, D_IN), jnp.float32),
            pltpu.VMEM((nseq * R, D_MODEL), jnp.float32),
        ],
        compiler_params=pltpu.CompilerParams(
            dimension_semantics=("arbitrary", "arbitrary"),
            vmem_limit_bytes=V7X_VMEM_LIMIT_BYTES),
        name="sample_layer",
    )(x8, conv_past8, ckt, cvt, mkt, mvt, pre_g, post_g, w_in_t, conv_w, sinks, w_out)


def _heads_last_to_keys_last(a):
    n, keys, heads, dim = a.shape
    return jnp.transpose(a, (0, 2, 3, 1)).reshape(n, heads * dim, keys)


def _keys_last_to_heads_last(a, heads):
    n, hd, keys = a.shape
    return jnp.transpose(a.reshape(n, heads, hd // heads, keys), (0, 3, 1, 2))[None]


def kernel(x_prompt, x_sample, mem_prompt, state_conv, cache_swa_k, cache_swa_v, cache_mem_k, cache_mem_v,
           pre_norm_g, post_norm_g, w_in, conv_w, attn_sinks, mem_norm_g, w_mem_k, w_mem_v, w_out):
    assert w_in.shape[0] == 1, "one layer, as the problem states"
    N, TS, _ = x_sample.shape
    assert TS == SAMPLE_ROWS // 2 and cache_swa_k.shape[2] == WINDOW
    l = 0

    pre_g = pre_norm_g[l].reshape(1, D_MODEL)
    post_g = post_norm_g[l].reshape(1, D_MODEL)
    w_in_t = w_in[l].astype(jnp.bfloat16).T
    w_out_bf = w_out[l].astype(jnp.bfloat16)
    sinks = attn_sinks[l].astype(jnp.float32)

    mkt, mvt, mkb, mvtb = _mem_kv(mem_prompt, mem_norm_g[l], w_mem_k[l], w_mem_v[l])
    y_p, conv_p, kt_p, vt_p = _prompt_layer(
        x_prompt, pre_g, post_g, w_in_t, conv_w[l].T, sinks, mkb, mvtb, w_out_bf, tq=PROMPT_TQ, nsub=PROMPT_NSUB)

    x8 = jnp.concatenate([x_sample, x_sample], axis=1).reshape(N * SAMPLE_ROWS, D_MODEL)
    conv_past8 = jnp.pad(state_conv[l], ((0, 0), (0, SAMPLE_ROWS - (CONV_WIDTH - 1)), (0, 0)))
    y_s, conv_s, kt_s, vt_s = _sample_layer(
        x8, conv_past8,
        _heads_last_to_keys_last(cache_swa_k[l]), _heads_last_to_keys_last(cache_swa_v[l]),
        _heads_last_to_keys_last(cache_mem_k[l]), _heads_last_to_keys_last(cache_mem_v[l]),
        pre_g, post_g, w_in_t, conv_w[l], sinks, w_out_bf, ns=SAMPLE_NS, group=SAMPLE_GROUP)

    return (y_p, y_s,
            conv_p[None],
            _keys_last_to_heads_last(kt_p, N_SWA_KV), _keys_last_to_heads_last(vt_p, N_SWA_KV),
            _keys_last_to_heads_last(mkt, N_MEM_HEADS), _keys_last_to_heads_last(mvt, N_MEM_HEADS),
            conv_s[None],
            _keys_last_to_heads_last(kt_s, N_SWA_KV), _keys_last_to_heads_last(vt_s, N_SWA_KV))
```

```python
import functools

import numpy as np
import jax
import jax.numpy as jnp
from jax import lax
from jax.experimental import pallas as pl
from jax.experimental.pallas import tpu as pltpu

D_MODEL = 1024
HEAD_DIM = 64
D_CONV = 384
N_MEM_HEADS = 4
D_MEMQ = N_MEM_HEADS * HEAD_DIM
D_SWA = 384
N_SWA_HEADS = 6
N_SWA_KV = 2
SWA_GROUP = N_SWA_HEADS // N_SWA_KV
D_SWA_KV = N_SWA_KV * HEAD_DIM
N_MEM = 256
CONV_WIDTH = 3
WINDOW = 128
BLOCK = 128
RMS_EPS = 1e-6
NEG_INF = -1e30
D_IN = 3072
QK_SCALE = HEAD_DIM ** -0.5
LOG2_E = float(np.log2(np.e))
QK_SCALE_LOG2 = QK_SCALE * LOG2_E

OFF_CB, OFF_CC, OFF_CH, OFF_CZ = 0, 384, 768, 1152
OFF_Q, OFF_K, OFF_V, OFF_SZ = 1536, 1920, 2048, 2176
OFF_MQ, OFF_MZ = 2560, 2816
YOFF_CONV, YOFF_SWA, YOFF_MEM = 0, 384, 768

V7X_VMEM_LIMIT_BYTES = 56 * 1024 * 1024

MEM_KV_BATCHES = 2
PROMPT_TQ = 512
PROMPT_NSUB = 2
IN_PROJ_CHUNK = 512
IN_PROJ_TOKEN_SPLIT = 1
TAIL_CHUNKS = 0
ATTN_WIDTH = 2
SAMPLE_NS = 16
SAMPLE_GROUP = 2
SAMPLE_ROWS = 8
SAMPLE_WIDTH = 16
SAMPLE_NEW = 16


def _alibi_slope(h):
    return float(np.power(np.float32(2.0), np.float32(-8.0 * (h + 1) / N_SWA_HEADS)))


def _rms_norm(x, g):
    return x * lax.rsqrt(jnp.mean(x * x, axis=-1, keepdims=True) + RMS_EPS) * g


def _silu(x):
    return x * jax.nn.sigmoid(x)


_NT = (((1,), (1,)), ((), ()))
_TN = (((0,), (0,)), ((), ()))


def _mem_kv_kernel(mem_ref, g_ref, wk_ref, wv_ref, mkt_ref, mvt_ref, mkb_ref, mvtb_ref, *, nb):
    mem = mem_ref[...].reshape(nb * N_MEM, D_MODEL)
    m = _rms_norm(mem, g_ref[...]).astype(jnp.bfloat16)
    mk = jnp.dot(m, wk_ref[...].astype(jnp.bfloat16), preferred_element_type=jnp.float32)
    mv = jnp.dot(m, wv_ref[...].astype(jnp.bfloat16), preferred_element_type=jnp.float32)
    for b in range(nb):
        mk_b = mk[b * N_MEM:(b + 1) * N_MEM]
        mv_t = mv[b * N_MEM:(b + 1) * N_MEM].T
        mkt_ref[b] = mk_b.T
        mvt_ref[b] = mv_t
        mkb_ref[b] = mk_b.astype(jnp.bfloat16)
        mvtb_ref[b] = mv_t.astype(jnp.bfloat16)


def _mem_kv(mem, mem_g, w_mk, w_mv):
    B = mem.shape[0]
    nb = MEM_KV_BATCHES
    full = lambda shape: pl.BlockSpec(shape, lambda b: (0,) * len(shape))
    per_batch = pl.BlockSpec((nb, N_MEM, D_MEMQ), lambda b: (b, 0, 0))
    return pl.pallas_call(
        functools.partial(_mem_kv_kernel, nb=nb),
        grid=(B // nb,),
        in_specs=[
            pl.BlockSpec((nb, N_MEM, D_MODEL), lambda b: (b, 0, 0)),
            full((1, D_MODEL)),
            full((D_MODEL, D_MEMQ)),
            full((D_MODEL, D_MEMQ)),
        ],
        out_specs=[per_batch] * 4,
        out_shape=[
            jax.ShapeDtypeStruct((B, D_MEMQ, N_MEM), jnp.float32),
            jax.ShapeDtypeStruct((B, D_MEMQ, N_MEM), jnp.float32),
            jax.ShapeDtypeStruct((B, N_MEM, D_MEMQ), jnp.bfloat16),
            jax.ShapeDtypeStruct((B, D_MEMQ, N_MEM), jnp.bfloat16),
        ],
        compiler_params=pltpu.CompilerParams(dimension_semantics=("arbitrary",)),
        name="mem_kv",
    )(mem, mem_g.reshape(1, D_MODEL), w_mk, w_mv)


def _prompt_kernel(x_ref, x_next_ref, pre_g_ref, post_g_ref, w_in_t_ref, conv_w_t_ref, sink_ref, mkb_ref, mvt_ref,
                   w_out_ref,
                   y_ref, conv_state_ref, k_state_ref, v_state_ref,
                   zt_ref, ycat_ref, kbuf_ref, vbuf_ref, uprev_ref, bias_ref, *, tq, nsub):
    assert nsub >= 2, "a sub-tile's z^T buffer is refilled while the other sub-tile's mixers run"
    t = pl.program_id(1)
    nblk = tq // BLOCK

    @pl.when((pl.program_id(0) == 0) & (t == 0))
    def _():
        c = lax.broadcasted_iota(jnp.int32, (BLOCK, BLOCK), 0)
        r = lax.broadcasted_iota(jnp.int32, (BLOCK, BLOCK), 1)
        distf = (r - c + jnp.where(c > r, BLOCK, 0)).astype(jnp.float32)
        for h in range(N_SWA_HEADS):
            g, i = divmod(h, SWA_GROUP)
            bias_ref[g, :, i * BLOCK:(i + 1) * BLOCK] = (-_alibi_slope(h) * LOG2_E) * distf

    @pl.when(t == 0)
    def _():
        kbuf_ref[0:BLOCK, :] = jnp.zeros((BLOCK, D_SWA_KV), jnp.bfloat16)
        vbuf_ref[:, 0:BLOCK] = jnp.zeros((D_SWA_KV, BLOCK), jnp.bfloat16)
        uprev_ref[...] = jnp.zeros_like(uprev_ref)

    first_pen = jnp.where(t == 0, NEG_INF, 0.0)
    slot = lax.broadcasted_iota(jnp.int32, (BLOCK, SWA_GROUP * BLOCK), 0)
    query = lax.broadcasted_iota(jnp.int32, (BLOCK, SWA_GROUP * BLOCK), 1) % BLOCK
    from_prev = slot > query
    q_zero = jnp.zeros((HEAD_DIM, SWA_GROUP * BLOCK), jnp.bfloat16)
    cw = conv_w_t_ref[...]
    carry = {"u_prev": uprev_ref[...]}
    head_of_lane = lax.broadcasted_iota(jnp.int32, (1, SWA_GROUP * BLOCK), 1) // BLOCK
    sink_rows = []
    for g in range(N_SWA_KV):
        row = jnp.full((1, SWA_GROUP * BLOCK), sink_ref[g * SWA_GROUP], jnp.float32)
        for i in range(1, SWA_GROUP):
            row = jnp.where(head_of_lane == i, sink_ref[g * SWA_GROUP + i], row)
        sink_rows.append(row * LOG2_E)

    def in_proj_chunks(x_rows, zt):
        h = {}
        th = tq // IN_PROJ_TOKEN_SPLIT

        def norm(part=0):
            if part not in h:
                h[part] = _rms_norm(x_rows(part * th, (part + 1) * th), pre_g_ref[...]).astype(jnp.bfloat16)

        def chunk(c, part):
            norm(part)
            rows = slice(c * IN_PROJ_CHUNK, (c + 1) * IN_PROJ_CHUNK)
            zt[rows, part * th:(part + 1) * th] = lax.dot_general(w_in_t_ref[rows, :], h[part], _NT,
                                                                  preferred_element_type=jnp.float32)

        return norm, [functools.partial(chunk, c, part)
                      for c in range(D_IN // IN_PROJ_CHUNK) for part in range(IN_PROJ_TOKEN_SPLIT)]

    def mixer_units(sub):
        zt = zt_ref.at[sub]
        ycat = ycat_ref.at[sub]
        tok0 = sub * tq

        def conv_unit():
            u = zt[OFF_CC:OFF_CC + D_CONV, :] * zt[OFF_CH:OFF_CH + D_CONV, :]
            ucat = jnp.concatenate([carry["u_prev"], u], axis=1)
            conv = (cw[:, 0:1] * pltpu.roll(ucat, 2, axis=1)[:, BLOCK:]
                    + cw[:, 1:2] * pltpu.roll(ucat, 1, axis=1)[:, BLOCK:]
                    + cw[:, 2:3] * u)
            y_conv = zt[OFF_CB:OFF_CB + D_CONV, :] * conv * _silu(zt[OFF_CZ:OFF_CZ + D_CONV, :])
            ycat[YOFF_CONV:YOFF_CONV + D_CONV, :] = y_conv.astype(jnp.bfloat16)
            carry["u_prev"] = u[:, tq - BLOCK:]

        def kv_unit():
            k_nat = zt[OFF_K:OFF_K + D_SWA_KV, :].T
            kbuf_ref[BLOCK + tok0:BLOCK + tok0 + tq, :] = k_nat.astype(jnp.bfloat16)
            vbuf_ref[:, BLOCK + tok0:BLOCK + tok0 + tq] = zt[OFF_V:OFF_V + D_SWA_KV, :].astype(jnp.bfloat16)

        def swa_unit(j, g):
            cols = slice(j * BLOCK, (j + 1) * BLOCK)
            band = slice(tok0 + j * BLOCK, tok0 + (j + 2) * BLOCK)
            q0 = OFF_Q + g * SWA_GROUP * HEAD_DIM
            qt = jnp.concatenate(
                [zt[q0 + i * HEAD_DIM:q0 + (i + 1) * HEAD_DIM, cols] for i in range(SWA_GROUP)],
                axis=1)
            qt = (qt * QK_SCALE_LOG2).astype(jnp.bfloat16)
            qt = jnp.concatenate([qt, q_zero] if g == 0 else [q_zero, qt], axis=0)
            s = jnp.dot(kbuf_ref[band, :], qt, preferred_element_type=jnp.float32)
            yield
            s = jnp.where(from_prev, s[0:BLOCK], s[BLOCK:]) + bias_ref[g]
            if sub == 0 and j == 0:
                s = s + jnp.where(from_prev, first_pen, 0.0)
            sink = sink_rows[g]
            m = jnp.maximum(jnp.max(s, axis=0, keepdims=True), sink)
            p = jnp.exp2(s - m)
            l = jnp.sum(p, axis=0, keepdims=True) + jnp.exp2(sink - m)
            p = jnp.concatenate([jnp.where(from_prev, p, 0.0), jnp.where(from_prev, 0.0, p)],
                                axis=0).astype(jnp.bfloat16)
            yield
            vband = vbuf_ref[g * HEAD_DIM:(g + 1) * HEAD_DIM, band]
            o = jnp.dot(vband, p, preferred_element_type=jnp.float32)
            o = o / l
            for i in range(SWA_GROUP):
                hh = g * SWA_GROUP + i
                gate = _silu(zt[OFF_SZ + hh * HEAD_DIM:OFF_SZ + (hh + 1) * HEAD_DIM, cols])
                ycat[YOFF_SWA + hh * HEAD_DIM:YOFF_SWA + (hh + 1) * HEAD_DIM, cols] = (
                    o[:, i * BLOCK:(i + 1) * BLOCK] * gate).astype(jnp.bfloat16)

        def mem_unit(hh):
            rows = slice(OFF_MQ + hh * HEAD_DIM, OFF_MQ + (hh + 1) * HEAD_DIM)
            qt = (zt[rows, :] * QK_SCALE_LOG2).astype(jnp.bfloat16)
            pieces = [jnp.zeros((HEAD_DIM, tq), jnp.bfloat16)] * N_MEM_HEADS
            pieces[hh] = qt
            s = jnp.dot(mkb_ref[0], jnp.concatenate(pieces, axis=0),
                        preferred_element_type=jnp.float32)
            yield
            m = jnp.max(s, axis=0, keepdims=True)
            p = jnp.exp2(s - m)
            l = jnp.sum(p, axis=0, keepdims=True)
            p = p.astype(jnp.bfloat16)
            yield
            o = jnp.dot(mvt_ref[0, hh * HEAD_DIM:(hh + 1) * HEAD_DIM, :], p,
                        preferred_element_type=jnp.float32)
            gate = _silu(zt[OFF_MZ + hh * HEAD_DIM:OFF_MZ + (hh + 1) * HEAD_DIM, :])
            ycat[YOFF_MEM + hh * HEAD_DIM:YOFF_MEM + (hh + 1) * HEAD_DIM, :] = (
                o / l * gate).astype(jnp.bfloat16)

        def out_unit():
            y = lax.dot_general(ycat[...], w_out_ref[...], _TN, preferred_element_type=jnp.float32)
            y_ref[0, tok0:tok0 + tq, :] = x_ref[0, tok0:tok0 + tq, :] + _rms_norm(y, post_g_ref[...])

        attention = [functools.partial(swa_unit, j, g) for j in range(nblk) for g in range(N_SWA_KV)]
        attention += [functools.partial(mem_unit, hh) for hh in range(N_MEM_HEADS)]
        return [conv_unit, kv_unit], attention, out_unit

    def interleave(units, chunks, norm_after, n_tail):
        (conv_unit, kv_unit), attention, last = units
        pending, tail = list(chunks[:len(chunks) - n_tail]), list(chunks[len(chunks) - n_tail:])
        conv_unit()
        pending.pop(0)()
        kv_unit()
        n_rounds = -(-len(attention) // ATTN_WIDTH) + 2
        chunk_rounds = [k * n_rounds // len(pending) for k in range(len(pending))]
        todo = list(attention)
        active = []
        for rnd in range(n_rounds):
            for _ in range(ATTN_WIDTH):
                if todo:
                    active.append(todo.pop(0)())
            for gen in list(active):
                if next(gen, "done") == "done":
                    active.remove(gen)
            for _ in range(chunk_rounds.count(rnd)):
                pending.pop(0)()
        assert not todo and not active and not pending
        if norm_after is not None:
            norm_after()
        last()
        for chunk in tail:
            chunk()

    @pl.when((pl.program_id(0) == 0) & (t == 0))
    def _():
        norm, chunks = in_proj_chunks(lambda lo, hi: x_ref[0, lo:hi, :], zt_ref.at[0])
        norm()
        for chunk in chunks:
            chunk()

    proj = [in_proj_chunks(lambda lo, hi, sub=sub: x_ref[0, sub * tq + lo:sub * tq + hi, :], zt_ref.at[sub])
            for sub in range(1, nsub)]
    proj.append(in_proj_chunks(lambda lo, hi: x_next_ref[0, lo:hi, :], zt_ref.at[0]))
    proj[0][0]()
    for sub in range(nsub):
        norm_after = proj[sub + 1][0] if sub + 1 < nsub else None
        interleave(mixer_units(sub), proj[sub][1], norm_after, TAIL_CHUNKS if sub + 1 == nsub else 0)

    tile = nsub * tq
    uprev_ref[...] = carry["u_prev"]
    conv_state_ref[0] = carry["u_prev"].T[BLOCK - (CONV_WIDTH - 1):, :]
    kbuf_ref[0:BLOCK, :] = kbuf_ref[tile:tile + BLOCK, :]
    vbuf_ref[:, 0:BLOCK] = vbuf_ref[:, tile:tile + BLOCK]
    k_state_ref[0] = zt_ref[nsub - 1, OFF_K:OFF_K + D_SWA_KV, tq - BLOCK:]
    v_state_ref[0] = zt_ref[nsub - 1, OFF_V:OFF_V + D_SWA_KV, tq - BLOCK:]


def _prompt_layer(x, pre_g, post_g, w_in_t, conv_w_t, sinks, mkb, mvtb, w_out, *, tq, nsub):
    B, T, _ = x.shape
    tile = tq * nsub
    full = lambda shape: pl.BlockSpec(shape, lambda b, t: (0,) * len(shape))
    kernel = functools.partial(_prompt_kernel, tq=tq, nsub=nsub)
    steps = T // tile

    def next_first_sub_tile(b, t):
        nxt = jnp.minimum(b * steps + t + 1, B * steps - 1)
        return nxt // steps, (nxt % steps) * nsub, 0

    return pl.pallas_call(
        kernel,
        grid=(B, steps),
        in_specs=[
            pl.BlockSpec((1, tile, D_MODEL), lambda b, t: (b, t, 0)),
            pl.BlockSpec((1, tq, D_MODEL), next_first_sub_tile),
            full((1, D_MODEL)),
            full((1, D_MODEL)),
            full((D_IN, D_MODEL)),
            full((D_CONV, CONV_WIDTH)),
            pl.BlockSpec(memory_space=pltpu.SMEM),
            pl.BlockSpec((1, N_MEM, D_MEMQ), lambda b, t: (b, 0, 0)),
            pl.BlockSpec((1, D_MEMQ, N_MEM), lambda b, t: (b, 0, 0)),
            full((D_MODEL, D_MODEL)),
        ],
        out_specs=[
            pl.BlockSpec((1, tile, D_MODEL), lambda b, t: (b, t, 0)),
            pl.BlockSpec((1, CONV_WIDTH - 1, D_CONV), lambda b, t: (b, 0, 0)),
            pl.BlockSpec((1, D_SWA_KV, BLOCK), lambda b, t: (b, 0, 0)),
            pl.BlockSpec((1, D_SWA_KV, BLOCK), lambda b, t: (b, 0, 0)),
        ],
        out_shape=[
            jax.ShapeDtypeStruct((B, T, D_MODEL), jnp.float32),
            jax.ShapeDtypeStruct((B, CONV_WIDTH - 1, D_CONV), jnp.float32),
            jax.ShapeDtypeStruct((B, D_SWA_KV, BLOCK), jnp.float32),
            jax.ShapeDtypeStruct((B, D_SWA_KV, BLOCK), jnp.float32),
        ],
        scratch_shapes=[
            pltpu.VMEM((nsub, D_IN, tq), jnp.float32),
            pltpu.VMEM((nsub, D_MODEL, tq), jnp.bfloat16),
            pltpu.VMEM((BLOCK + tile, D_SWA_KV), jnp.bfloat16),
            pltpu.VMEM((D_SWA_KV, BLOCK + tile), jnp.bfloat16),
            pltpu.VMEM((D_CONV, BLOCK), jnp.float32),
            pltpu.VMEM((N_SWA_KV, BLOCK, SWA_GROUP * BLOCK), jnp.float32),
        ],
        compiler_params=pltpu.CompilerParams(
            dimension_semantics=("arbitrary", "arbitrary"),
            vmem_limit_bytes=V7X_VMEM_LIMIT_BYTES),
        name="prompt_layer",
    )(x, x, pre_g, post_g, w_in_t, conv_w_t, sinks, mkb, mvtb, w_out)


def _sample_kernel(x_ref, conv_past_ref, ckt_ref, cvt_ref, mkt_ref, mvt_ref,
                   pre_g_ref, post_g_ref, w_in_t_ref, conv_w_ref, sink_ref, w_out_ref,
                   y_ref, conv_state_ref, kt_state_ref, vt_state_ref,
                   z_ref, ycat_ref, *, ns, group):
    R = SAMPLE_ROWS
    half = R // 2
    nseq = ns * group
    step = pl.program_id(1)

    @pl.when(step == 0)
    def _():
        h = _rms_norm(x_ref[...], pre_g_ref[...]).astype(jnp.bfloat16)
        z_ref[...] = lax.dot_general(h, w_in_t_ref[...], _NT, preferred_element_type=jnp.float32)
        u = (z_ref[:, OFF_CC:OFF_CC + D_CONV] * z_ref[:, OFF_CH:OFF_CH + D_CONV]).reshape(nseq, R, D_CONV)
        row3 = lax.broadcasted_iota(jnp.int32, (nseq, R, D_CONV), 1)
        u_full = jnp.where(row3 < CONV_WIDTH - 1, conv_past_ref[...], pltpu.roll(u, CONV_WIDTH - 1, axis=1))
        cw = conv_w_ref[...]
        conv = (cw[0:1, :] * u_full
                + cw[1:2, :] * pltpu.roll(u_full, R - 1, axis=1)
                + cw[2:3, :] * pltpu.roll(u_full, R - 2, axis=1))
        conv_state_ref[...] = pltpu.roll(u_full, R - half, axis=1)[:, 0:CONV_WIDTH - 1, :]
        y_conv = (z_ref[:, OFF_CB:OFF_CB + D_CONV] * conv.reshape(nseq * R, D_CONV)
                  * _silu(z_ref[:, OFF_CZ:OFF_CZ + D_CONV]))
        ycat_ref[:, YOFF_CONV:YOFF_CONV + D_CONV] = y_conv

    row = lax.broadcasted_iota(jnp.int32, (R, 2 * HEAD_DIM), 0)
    lane = lax.broadcasted_iota(jnp.int32, (R, 2 * HEAD_DIM), 1)
    lo_row = row < half
    lo_lane = lane < HEAD_DIM
    diag = lo_row == lo_lane
    lane_sq = lax.broadcasted_iota(jnp.int32, (WINDOW, WINDOW), 1)

    def pair_bias(ncols, dist_of):
        rr = lax.broadcasted_iota(jnp.int32, (R, ncols), 0)
        cc = lax.broadcasted_iota(jnp.int32, (R, ncols), 1)
        dist, valid = dist_of(rr % half, cc)
        distf = dist.astype(jnp.float32)
        tiles = []
        for pair in range(N_SWA_HEADS // 2):
            slope = jnp.where(rr < half, _alibi_slope(2 * pair), _alibi_slope(2 * pair + 1))
            tiles.append(jnp.where(valid, -slope * distf, NEG_INF))
        return jnp.concatenate(tiles, axis=0)

    def cached_dist(tok, c):
        d = tok + WINDOW - c
        return d, d < WINDOW

    def new_dist(tok, c):
        d = tok - c
        return d, (d >= 0) & (c < half)

    bias_c = pair_bias(WINDOW, cached_dist)
    bias_n = pair_bias(SAMPLE_NEW, new_dist)
    head_of_row = lax.broadcasted_iota(jnp.int32, (N_SWA_HEADS * half, 1), 0) // half
    sink_col = jnp.full((N_SWA_HEADS * half, 1), sink_ref[0], jnp.float32)
    for hh in range(1, N_SWA_HEADS):
        sink_col = jnp.where(head_of_row == hh, sink_ref[hh], sink_col)
    state_pad = jnp.zeros((WINDOW - R, D_SWA_KV), jnp.float32)

    def seq_rows(n):
        return pl.ds(pl.multiple_of((step * ns + n) * R, R), R)

    def swa_unit(n):
        rows = seq_rows(n)
        qa = z_ref[rows, OFF_Q:OFF_Q + 128] * QK_SCALE
        qb = z_ref[rows, OFF_Q + 128:OFF_Q + 256] * QK_SCALE
        qc = z_ref[rows, OFF_Q + 256:OFF_Q + 384] * QK_SCALE
        t0 = jnp.where(lo_lane, jnp.where(lo_row, qa, pltpu.roll(qa, HEAD_DIM, axis=1)), 0.0)
        t1 = jnp.where(diag, qb, 0.0)
        t2 = jnp.where(lo_lane, 0.0, jnp.where(lo_row, pltpu.roll(qc, HEAD_DIM, axis=1), qc))
        qs = jnp.concatenate([t0, t1, t2], axis=0).astype(jnp.bfloat16)

        k_new = z_ref[rows, OFF_K:OFF_K + D_SWA_KV]
        v_new = z_ref[rows, OFF_V:OFF_V + D_SWA_KV]
        k_new_b = jnp.concatenate([k_new, k_new], axis=0).astype(jnp.bfloat16)
        v_new_b = jnp.concatenate([v_new, v_new], axis=0).astype(jnp.bfloat16)
        kt_old = ckt_ref[n].astype(jnp.bfloat16)

        s_c = jnp.dot(qs, kt_old, preferred_element_type=jnp.float32) + bias_c
        s_n = lax.dot_general(qs, k_new_b, _NT, preferred_element_type=jnp.float32) + bias_n
        yield
        m = jnp.maximum(jnp.maximum(jnp.max(s_c, axis=1, keepdims=True), jnp.max(s_n, axis=1, keepdims=True)),
                        sink_col)
        p_c = jnp.exp(s_c - m)
        p_n = jnp.exp(s_n - m)
        l = (jnp.sum(p_c, axis=1, keepdims=True) + jnp.sum(p_n, axis=1, keepdims=True) + jnp.exp(sink_col - m))
        p_c = p_c.astype(jnp.bfloat16)
        p_n = p_n.astype(jnp.bfloat16)
        yield
        o = (lax.dot_general(p_c, cvt_ref[n].astype(jnp.bfloat16), _NT, preferred_element_type=jnp.float32)
             + jnp.dot(p_n, v_new_b, preferred_element_type=jnp.float32)) / l
        yield
        o0, o1, o2 = o[0:R], o[R:2 * R], o[2 * R:3 * R]
        ya = jnp.where(lo_lane, o0, pltpu.roll(pltpu.roll(o0, HEAD_DIM, axis=1), half, axis=0))
        yb = jnp.where(lo_lane, o1, pltpu.roll(o1, half, axis=0))
        yc = jnp.where(lo_lane, pltpu.roll(o2, HEAD_DIM, axis=1), pltpu.roll(o2, half, axis=0))
        y_swa = jnp.concatenate([ya, yb, yc], axis=1) * _silu(z_ref[rows, OFF_SZ:OFF_SZ + D_SWA])
        ycat_ref[rows, YOFF_SWA:YOFF_SWA + D_SWA] = y_swa

    def state_unit(n):
        rows = seq_rows(n)
        k_new_t = jnp.concatenate([state_pad, z_ref[rows, OFF_K:OFF_K + D_SWA_KV]], axis=0).T
        v_new_t = jnp.concatenate([state_pad, z_ref[rows, OFF_V:OFF_V + D_SWA_KV]], axis=0).T
        keep = lane_sq < WINDOW - half
        kt_state_ref[n] = jnp.where(keep, pltpu.roll(ckt_ref[n], WINDOW - half, axis=1), k_new_t)
        vt_state_ref[n] = jnp.where(keep, pltpu.roll(cvt_ref[n], WINDOW - half, axis=1), v_new_t)

    def mem_unit(n):
        rows = seq_rows(n)
        m0 = z_ref[rows, OFF_MQ:OFF_MQ + 128] * QK_SCALE
        m1 = z_ref[rows, OFF_MQ + 128:OFF_MQ + 256] * QK_SCALE
        zero = jnp.zeros_like(m0)
        qm = jnp.concatenate(
            [jnp.concatenate([jnp.where(diag, m0, 0.0), zero], axis=1),
             jnp.concatenate([zero, jnp.where(diag, m1, 0.0)], axis=1)], axis=0).astype(jnp.bfloat16)
        s = jnp.dot(qm, mkt_ref[n].astype(jnp.bfloat16), preferred_element_type=jnp.float32)
        yield
        m = jnp.max(s, axis=1, keepdims=True)
        p = jnp.exp(s - m)
        l = jnp.sum(p, axis=1, keepdims=True)
        p = p.astype(jnp.bfloat16)
        yield
        o = lax.dot_general(p, mvt_ref[n].astype(jnp.bfloat16), _NT,
                            preferred_element_type=jnp.float32) / l
        yield
        oa, ob = o[0:R, 0:128], o[R:2 * R, 128:256]
        y_mem = jnp.concatenate([jnp.where(lo_lane, oa, pltpu.roll(oa, half, axis=0)),
                                 jnp.where(lo_lane, ob, pltpu.roll(ob, half, axis=0))], axis=1)
        ycat_ref[rows, YOFF_MEM:YOFF_MEM + D_MEMQ] = y_mem * _silu(z_ref[rows, OFF_MZ:OFF_MZ + D_MEMQ])

    todo = [functools.partial(unit, n) for n in range(ns) for unit in (swa_unit, mem_unit)]
    states = [functools.partial(state_unit, n) for n in range(ns)]
    active = []
    while todo or active:
        for _ in range(SAMPLE_WIDTH):
            if todo:
                active.append(todo.pop(0)())
        for gen in list(active):
            if next(gen, "done") == "done":
                active.remove(gen)
        if states:
            states.pop(0)()
    for unit in states:
        unit()

    @pl.when(step == group - 1)
    def _():
        y = jnp.dot(ycat_ref[...].astype(jnp.bfloat16), w_out_ref[...], preferred_element_type=jnp.float32)
        out = x_ref[...] + _rms_norm(y, post_g_ref[...])
        y_ref[...] = out.reshape(nseq, R, D_MODEL)[:, 0:half, :]


def _sample_layer(x8, conv_past8, ckt, cvt, mkt, mvt, pre_g, post_g, w_in_t, conv_w, sinks, w_out, *, ns, group):
    N = ckt.shape[0]
    R = SAMPLE_ROWS
    nseq = ns * group
    full = lambda shape: pl.BlockSpec(shape, lambda o, i: (0,) * len(shape))
    per_group = lambda shape: pl.BlockSpec(shape, lambda o, i: (o,) + (0,) * (len(shape) - 1))
    per_step = lambda shape: pl.BlockSpec(shape, lambda o, i: (o * group + i,) + (0,) * (len(shape) - 1))
    kernel = functools.partial(_sample_kernel, ns=ns, group=group)
    return pl.pallas_call(
        kernel,
        grid=(N // nseq, group),
        in_specs=[
            per_group((nseq * R, D_MODEL)),
            per_group((nseq, R, D_CONV)),
            per_step((ns, D_SWA_KV, WINDOW)),
            per_step((ns, D_SWA_KV, WINDOW)),
            per_step((ns, D_MEMQ, N_MEM)),
            per_step((ns, D_MEMQ, N_MEM)),
            full((1, D_MODEL)),
            full((1, D_MODEL)),
            full((D_IN, D_MODEL)),
            full((CONV_WIDTH, D_CONV)),
            pl.BlockSpec(memory_space=pltpu.SMEM),
            full((D_MODEL, D_MODEL)),
        ],
        out_specs=[
            per_group((nseq, R // 2, D_MODEL)),
            per_group((nseq, CONV_WIDTH - 1, D_CONV)),
            per_step((ns, D_SWA_KV, WINDOW)),
            per_step((ns, D_SWA_KV, WINDOW)),
        ],
        out_shape=[
            jax.ShapeDtypeStruct((N, R // 2, D_MODEL), jnp.float32),
            jax.ShapeDtypeStruct((N, CONV_WIDTH - 1, D_CONV), jnp.float32),
            jax.ShapeDtypeStruct((N, D_SWA_KV, WINDOW), jnp.float32),
            jax.ShapeDtypeStruct((N, D_SWA_KV, WINDOW), jnp.float32),
        ],
        scratch_shapes=[
            pltpu.VMEM((nseq * R, D_IN), jnp.float32),
            pltpu.VMEM((nseq * R, D_MODEL), jnp.float32),
        ],
        compiler_params=pltpu.CompilerParams(
            dimension_semantics=("arbitrary", "arbitrary"),
            vmem_limit_bytes=V7X_VMEM_LIMIT_BYTES),
        name="sample_layer",
    )(x8, conv_past8, ckt, cvt, mkt, mvt, pre_g, post_g, w_in_t, conv_w, sinks, w_out)


def _heads_last_to_keys_last(a):
    n, keys, heads, dim = a.shape
    return jnp.transpose(a, (0, 2, 3, 1)).reshape(n, heads * dim, keys)


def _keys_last_to_heads_last(a, heads):
    n, hd, keys = a.shape
    return jnp.transpose(a.reshape(n, heads, hd // heads, keys), (0, 3, 1, 2))[None]


def kernel(x_prompt, x_sample, mem_prompt, state_conv, cache_swa_k, cache_swa_v, cache_mem_k, cache_mem_v,
           pre_norm_g, post_norm_g, w_in, conv_w, attn_sinks, mem_norm_g, w_mem_k, w_mem_v, w_out):
    assert w_in.shape[0] == 1, "one layer, as the problem states"
    N, TS, _ = x_sample.shape
    assert TS == SAMPLE_ROWS // 2 and cache_swa_k.shape[2] == WINDOW
    l = 0

    pre_g = pre_norm_g[l].reshape(1, D_MODEL)
    post_g = post_norm_g[l].reshape(1, D_MODEL)
    w_in_t = w_in[l].astype(jnp.bfloat16).T
    w_out_bf = w_out[l].astype(jnp.bfloat16)
    sinks = attn_sinks[l].astype(jnp.float32)

    mkt, mvt, mkb, mvtb = _mem_kv(mem_prompt, mem_norm_g[l], w_mem_k[l], w_mem_v[l])
    y_p, conv_p, kt_p, vt_p = _prompt_layer(
        x_prompt, pre_g, post_g, w_in_t, conv_w[l].T, sinks, mkb, mvtb, w_out_bf, tq=PROMPT_TQ, nsub=PROMPT_NSUB)

    x8 = jnp.concatenate([x_sample, x_sample], axis=1).reshape(N * SAMPLE_ROWS, D_MODEL)
    conv_past8 = jnp.pad(state_conv[l], ((0, 0), (0, SAMPLE_ROWS - (CONV_WIDTH - 1)), (0, 0)))
    y_s, conv_s, kt_s, vt_s = _sample_layer(
        x8, conv_past8,
        _heads_last_to_keys_last(cache_swa_k[l]), _heads_last_to_keys_last(cache_swa_v[l]),
        _heads_last_to_keys_last(cache_mem_k[l]), _heads_last_to_keys_last(cache_mem_v[l]),
        pre_g, post_g, w_in_t, conv_w[l], sinks, w_out_bf, ns=SAMPLE_NS, group=SAMPLE_GROUP)

    return (y_p, y_s,
            conv_p[None],
            _keys_last_to_heads_last(kt_p, N_SWA_KV), _keys_last_to_heads_last(vt_p, N_SWA_KV),
            _keys_last_to_heads_last(mkt, N_MEM_HEADS), _keys_last_to_heads_last(mvt, N_MEM_HEADS),
            conv_s[None],
            _keys_last_to_heads_last(kt_s, N_SWA_KV), _keys_last_to_heads_last(vt_s, N_SWA_KV))
```

```python
import functools

import numpy as np
import jax
import jax.numpy as jnp
from jax import lax
from jax.experimental import pallas as pl
from jax.experimental.pallas import tpu as pltpu

D_MODEL = 1024
HEAD_DIM = 64
D_CONV = 384
N_MEM_HEADS = 4
D_MEMQ = N_MEM_HEADS * HEAD_DIM
D_SWA = 384
N_SWA_HEADS = 6
N_SWA_KV = 2
SWA_GROUP = N_SWA_HEADS // N_SWA_KV
D_SWA_KV = N_SWA_KV * HEAD_DIM
N_MEM = 256
CONV_WIDTH = 3
WINDOW = 128
BLOCK = 128
RMS_EPS = 1e-6
NEG_INF = -1e30
D_IN = 3072
QK_SCALE = HEAD_DIM ** -0.5
LOG2_E = float(np.log2(np.e))
QK_SCALE_LOG2 = QK_SCALE * LOG2_E

OFF_CB, OFF_CC, OFF_CH, OFF_CZ = 0, 384, 768, 1152
OFF_Q, OFF_K, OFF_V, OFF_SZ = 1536, 1920, 2048, 2176
OFF_MQ, OFF_MZ = 2560, 2816
YOFF_CONV, YOFF_SWA, YOFF_MEM = 0, 384, 768

V7X_VMEM_LIMIT_BYTES = 56 * 1024 * 1024

MEM_KV_BATCHES = 2
PROMPT_TQ = 512
PROMPT_NSUB = 2
IN_PROJ_CHUNK = 512
ATTN_WIDTH = 2
SAMPLE_NS = 16
SAMPLE_GROUP = 2
SAMPLE_ROWS = 8
SAMPLE_WIDTH = 16
SAMPLE_NEW = 16


def _alibi_slope(h):
    return float(np.power(np.float32(2.0), np.float32(-8.0 * (h + 1) / N_SWA_HEADS)))


def _rms_norm(x, g):
    return x * lax.rsqrt(jnp.mean(x * x, axis=-1, keepdims=True) + RMS_EPS) * g


def _silu(x):
    return x * jax.nn.sigmoid(x)


_NT = (((1,), (1,)), ((), ()))
_TN = (((0,), (0,)), ((), ()))


def _mem_kv_kernel(mem_ref, g_ref, wk_ref, wv_ref, mkt_ref, mvt_ref, mkb_ref, mvtb_ref, *, nb):
    mem = mem_ref[...].reshape(nb * N_MEM, D_MODEL)
    m = _rms_norm(mem, g_ref[...]).astype(jnp.bfloat16)
    mk = jnp.dot(m, wk_ref[...].astype(jnp.bfloat16), preferred_element_type=jnp.float32)
    mv = jnp.dot(m, wv_ref[...].astype(jnp.bfloat16), preferred_element_type=jnp.float32)
    for b in range(nb):
        mk_b = mk[b * N_MEM:(b + 1) * N_MEM]
        mv_t = mv[b * N_MEM:(b + 1) * N_MEM].T
        mkt_ref[b] = mk_b.T
        mvt_ref[b] = mv_t
        mkb_ref[b] = mk_b.astype(jnp.bfloat16)
        mvtb_ref[b] = mv_t.astype(jnp.bfloat16)


def _mem_kv(mem, mem_g, w_mk, w_mv):
    B = mem.shape[0]
    nb = MEM_KV_BATCHES
    full = lambda shape: pl.BlockSpec(shape, lambda b: (0,) * len(shape))
    per_batch = pl.BlockSpec((nb, N_MEM, D_MEMQ), lambda b: (b, 0, 0))
    return pl.pallas_call(
        functools.partial(_mem_kv_kernel, nb=nb),
        grid=(B // nb,),
        in_specs=[
            pl.BlockSpec((nb, N_MEM, D_MODEL), lambda b: (b, 0, 0)),
            full((1, D_MODEL)),
            full((D_MODEL, D_MEMQ)),
            full((D_MODEL, D_MEMQ)),
        ],
        out_specs=[per_batch] * 4,
        out_shape=[
            jax.ShapeDtypeStruct((B, D_MEMQ, N_MEM), jnp.float32),
            jax.ShapeDtypeStruct((B, D_MEMQ, N_MEM), jnp.float32),
            jax.ShapeDtypeStruct((B, N_MEM, D_MEMQ), jnp.bfloat16),
            jax.ShapeDtypeStruct((B, D_MEMQ, N_MEM), jnp.bfloat16),
        ],
        compiler_params=pltpu.CompilerParams(dimension_semantics=("arbitrary",)),
        name="mem_kv",
    )(mem, mem_g.reshape(1, D_MODEL), w_mk, w_mv)


def _prompt_kernel(x_ref, x_next_ref, pre_g_ref, post_g_ref, w_in_t_ref, conv_w_t_ref, sink_ref, mkb_ref, mvt_ref,
                   w_out_ref,
                   y_ref, conv_state_ref, k_state_ref, v_state_ref,
                   zt_ref, ycat_ref, kbuf_ref, vbuf_ref, uprev_ref, bias_ref, *, tq, nsub):
    assert nsub % 2 == 0
    t = pl.program_id(1)
    nblk = tq // BLOCK

    @pl.when((pl.program_id(0) == 0) & (t == 0))
    def _():
        c = lax.broadcasted_iota(jnp.int32, (BLOCK, BLOCK), 0)
        r = lax.broadcasted_iota(jnp.int32, (BLOCK, BLOCK), 1)
        distf = (r - c + jnp.where(c > r, BLOCK, 0)).astype(jnp.float32)
        for h in range(N_SWA_HEADS):
            g, i = divmod(h, SWA_GROUP)
            bias_ref[g, :, i * BLOCK:(i + 1) * BLOCK] = (-_alibi_slope(h) * LOG2_E) * distf

    @pl.when(t == 0)
    def _():
        kbuf_ref[0:BLOCK, :] = jnp.zeros((BLOCK, D_SWA_KV), jnp.bfloat16)
        vbuf_ref[:, 0:BLOCK] = jnp.zeros((D_SWA_KV, BLOCK), jnp.bfloat16)
        uprev_ref[...] = jnp.zeros_like(uprev_ref)

    first_pen = jnp.where(t == 0, NEG_INF, 0.0)
    slot = lax.broadcasted_iota(jnp.int32, (BLOCK, SWA_GROUP * BLOCK), 0)
    query = lax.broadcasted_iota(jnp.int32, (BLOCK, SWA_GROUP * BLOCK), 1) % BLOCK
    from_prev = slot > query
    q_zero = jnp.zeros((HEAD_DIM, SWA_GROUP * BLOCK), jnp.bfloat16)
    cw = conv_w_t_ref[...]
    carry = {"u_prev": uprev_ref[...]}
    head_of_lane = lax.broadcasted_iota(jnp.int32, (1, SWA_GROUP * BLOCK), 1) // BLOCK
    sink_rows = []
    for g in range(N_SWA_KV):
        row = jnp.full((1, SWA_GROUP * BLOCK), sink_ref[g * SWA_GROUP], jnp.float32)
        for i in range(1, SWA_GROUP):
            row = jnp.where(head_of_lane == i, sink_ref[g * SWA_GROUP + i], row)
        sink_rows.append(row * LOG2_E)

    def in_proj_chunks(x_rows, zt):
        state = {}

        def norm():
            state["h"] = _rms_norm(x_rows(), pre_g_ref[...]).astype(jnp.bfloat16)

        def chunk(lo):
            rows = slice(lo, lo + IN_PROJ_CHUNK)
            zt[rows, :] = lax.dot_general(w_in_t_ref[rows, :], state["h"], _NT,
                                          preferred_element_type=jnp.float32)

        return norm, [functools.partial(chunk, lo) for lo in range(0, D_IN, IN_PROJ_CHUNK)]

    def mixer_units(sub):
        zt = zt_ref.at[sub % 2]
        ycat = ycat_ref.at[sub % 2]
        tok0 = sub * tq

        def conv_unit():
            u = zt[OFF_CC:OFF_CC + D_CONV, :] * zt[OFF_CH:OFF_CH + D_CONV, :]
            ucat = jnp.concatenate([carry["u_prev"], u], axis=1)
            conv = (cw[:, 0:1] * pltpu.roll(ucat, 2, axis=1)[:, BLOCK:]
                    + cw[:, 1:2] * pltpu.roll(ucat, 1, axis=1)[:, BLOCK:]
                    + cw[:, 2:3] * u)
            y_conv = zt[OFF_CB:OFF_CB + D_CONV, :] * conv * _silu(zt[OFF_CZ:OFF_CZ + D_CONV, :])
            ycat[YOFF_CONV:YOFF_CONV + D_CONV, :] = y_conv.astype(jnp.bfloat16)
            carry["u_prev"] = u[:, tq - BLOCK:]

        def kv_unit():
            k_nat = zt[OFF_K:OFF_K + D_SWA_KV, :].T
            kbuf_ref[BLOCK + tok0:BLOCK + tok0 + tq, :] = k_nat.astype(jnp.bfloat16)
            vbuf_ref[:, BLOCK + tok0:BLOCK + tok0 + tq] = zt[OFF_V:OFF_V + D_SWA_KV, :].astype(jnp.bfloat16)

        def swa_unit(j, g):
            cols = slice(j * BLOCK, (j + 1) * BLOCK)
            band = slice(tok0 + j * BLOCK, tok0 + (j + 2) * BLOCK)
            q0 = OFF_Q + g * SWA_GROUP * HEAD_DIM
            qt = jnp.concatenate(
                [zt[q0 + i * HEAD_DIM:q0 + (i + 1) * HEAD_DIM, cols] for i in range(SWA_GROUP)],
                axis=1)
            qt = (qt * QK_SCALE_LOG2).astype(jnp.bfloat16)
            qt = jnp.concatenate([qt, q_zero] if g == 0 else [q_zero, qt], axis=0)
            s = jnp.dot(kbuf_ref[band, :], qt, preferred_element_type=jnp.float32)
            yield
            s = jnp.where(from_prev, s[0:BLOCK], s[BLOCK:]) + bias_ref[g]
            if sub == 0 and j == 0:
                s = s + jnp.where(from_prev, first_pen, 0.0)
            sink = sink_rows[g]
            m = jnp.maximum(jnp.max(s, axis=0, keepdims=True), sink)
            p = jnp.exp2(s - m)
            l = jnp.sum(p, axis=0, keepdims=True) + jnp.exp2(sink - m)
            p = jnp.concatenate([jnp.where(from_prev, p, 0.0), jnp.where(from_prev, 0.0, p)],
                                axis=0).astype(jnp.bfloat16)
            yield
            vband = vbuf_ref[g * HEAD_DIM:(g + 1) * HEAD_DIM, band]
            o = jnp.dot(vband, p, preferred_element_type=jnp.float32)
            o = o / l
            for i in range(SWA_GROUP):
                hh = g * SWA_GROUP + i
                gate = _silu(zt[OFF_SZ + hh * HEAD_DIM:OFF_SZ + (hh + 1) * HEAD_DIM, cols])
                ycat[YOFF_SWA + hh * HEAD_DIM:YOFF_SWA + (hh + 1) * HEAD_DIM, cols] = (
                    o[:, i * BLOCK:(i + 1) * BLOCK] * gate).astype(jnp.bfloat16)

        def mem_unit(hh):
            rows = slice(OFF_MQ + hh * HEAD_DIM, OFF_MQ + (hh + 1) * HEAD_DIM)
            qt = (zt[rows, :] * QK_SCALE_LOG2).astype(jnp.bfloat16)
            pieces = [jnp.zeros((HEAD_DIM, tq), jnp.bfloat16)] * N_MEM_HEADS
            pieces[hh] = qt
            s = jnp.dot(mkb_ref[0], jnp.concatenate(pieces, axis=0),
                        preferred_element_type=jnp.float32)
            yield
            m = jnp.max(s, axis=0, keepdims=True)
            p = jnp.exp2(s - m)
            l = jnp.sum(p, axis=0, keepdims=True)
            p = p.astype(jnp.bfloat16)
            yield
            o = jnp.dot(mvt_ref[0, hh * HEAD_DIM:(hh + 1) * HEAD_DIM, :], p,
                        preferred_element_type=jnp.float32)
            gate = _silu(zt[OFF_MZ + hh * HEAD_DIM:OFF_MZ + (hh + 1) * HEAD_DIM, :])
            ycat[YOFF_MEM + hh * HEAD_DIM:YOFF_MEM + (hh + 1) * HEAD_DIM, :] = (
                o / l * gate).astype(jnp.bfloat16)

        def out_proj():
            carry["y"] = lax.dot_general(ycat[...], w_out_ref[...], _TN,
                                         preferred_element_type=jnp.float32)

        def post_norm(blk):
            rows = slice(blk * BLOCK, (blk + 1) * BLOCK)
            out_rows = slice(tok0 + blk * BLOCK, tok0 + (blk + 1) * BLOCK)
            y_ref[0, out_rows, :] = x_ref[0, out_rows, :] + _rms_norm(carry["y"][rows], post_g_ref[...])

        attention = [functools.partial(swa_unit, j, g) for j in range(nblk) for g in range(N_SWA_KV)]
        attention += [functools.partial(mem_unit, hh) for hh in range(N_MEM_HEADS)]
        return [conv_unit, kv_unit], attention, (out_proj, [functools.partial(post_norm, b) for b in range(nblk)])

    def interleave(units, chunks, norm_first, norm_after):
        (conv_unit, kv_unit), attention, (out_proj, post_norms) = units
        pending = list(chunks)
        kv_unit()
        n_rounds = -(-len(attention) // ATTN_WIDTH) + 2
        chunk_rounds = [k * n_rounds // len(pending) for k in range(len(pending))]
        todo = list(attention)
        active = []
        for rnd in range(n_rounds):
            for _ in range(ATTN_WIDTH):
                if todo:
                    active.append(todo.pop(0)())
            for gen in list(active):
                if next(gen, "done") == "done":
                    active.remove(gen)
            if rnd == 0 and norm_first is not None:
                norm_first()
            for _ in range(chunk_rounds.count(rnd)):
                pending.pop(0)()
            if rnd == 0:
                conv_unit()
        assert not todo and not active and not pending
        if norm_after is not None:
            norm_after()
        out_proj()
        for post_norm in post_norms:
            post_norm()

    @pl.when((pl.program_id(0) == 0) & (t == 0))
    def _():
        norm, chunks = in_proj_chunks(lambda: x_ref[0, 0:tq, :], zt_ref.at[0])
        norm()
        for chunk in chunks:
            chunk()

    proj = [in_proj_chunks(lambda sub=sub: x_ref[0, sub * tq:(sub + 1) * tq, :], zt_ref.at[sub % 2])
            for sub in range(1, nsub)]
    proj.append(in_proj_chunks(lambda: x_next_ref[0], zt_ref.at[0]))
    for sub in range(nsub):
        norm_first = proj[0][0] if sub == 0 else None
        norm_after = proj[sub + 1][0] if sub + 1 < nsub else None
        interleave(mixer_units(sub), proj[sub][1], norm_first, norm_after)

    tile = nsub * tq
    uprev_ref[...] = carry["u_prev"]
    conv_state_ref[0] = carry["u_prev"].T[BLOCK - (CONV_WIDTH - 1):, :]
    kbuf_ref[0:BLOCK, :] = kbuf_ref[tile:tile + BLOCK, :]
    vbuf_ref[:, 0:BLOCK] = vbuf_ref[:, tile:tile + BLOCK]
    k_state_ref[0] = zt_ref[1, OFF_K:OFF_K + D_SWA_KV, tq - BLOCK:]
    v_state_ref[0] = zt_ref[1, OFF_V:OFF_V + D_SWA_KV, tq - BLOCK:]


def _prompt_layer(x, pre_g, post_g, w_in_t, conv_w_t, sinks, mkb, mvtb, w_out, *, tq, nsub):
    B, T, _ = x.shape
    tile = tq * nsub
    full = lambda shape: pl.BlockSpec(shape, lambda b, t: (0,) * len(shape))
    kernel = functools.partial(_prompt_kernel, tq=tq, nsub=nsub)
    steps = T // tile

    def next_first_sub_tile(b, t):
        nxt = jnp.minimum(b * steps + t + 1, B * steps - 1)
        return nxt // steps, (nxt % steps) * nsub, 0

    return pl.pallas_call(
        kernel,
        grid=(B, steps),
        in_specs=[
            pl.BlockSpec((1, tile, D_MODEL), lambda b, t: (b, t, 0)),
            pl.BlockSpec((1, tq, D_MODEL), next_first_sub_tile),
            full((1, D_MODEL)),
            full((1, D_MODEL)),
            full((D_IN, D_MODEL)),
            full((D_CONV, CONV_WIDTH)),
            pl.BlockSpec(memory_space=pltpu.SMEM),
            pl.BlockSpec((1, N_MEM, D_MEMQ), lambda b, t: (b, 0, 0)),
            pl.BlockSpec((1, D_MEMQ, N_MEM), lambda b, t: (b, 0, 0)),
            full((D_MODEL, D_MODEL)),
        ],
        out_specs=[
            pl.BlockSpec((1, tile, D_MODEL), lambda b, t: (b, t, 0)),
            pl.BlockSpec((1, CONV_WIDTH - 1, D_CONV), lambda b, t: (b, 0, 0)),
            pl.BlockSpec((1, D_SWA_KV, BLOCK), lambda b, t: (b, 0, 0)),
            pl.BlockSpec((1, D_SWA_KV, BLOCK), lambda b, t: (b, 0, 0)),
        ],
        out_shape=[
            jax.ShapeDtypeStruct((B, T, D_MODEL), jnp.float32),
            jax.ShapeDtypeStruct((B, CONV_WIDTH - 1, D_CONV), jnp.float32),
            jax.ShapeDtypeStruct((B, D_SWA_KV, BLOCK), jnp.float32),
            jax.ShapeDtypeStruct((B, D_SWA_KV, BLOCK), jnp.float32),
        ],
        scratch_shapes=[
            pltpu.VMEM((2, D_IN, tq), jnp.float32),
            pltpu.VMEM((2, D_MODEL, tq), jnp.bfloat16),
            pltpu.VMEM((BLOCK + tile, D_SWA_KV), jnp.bfloat16),
            pltpu.VMEM((D_SWA_KV, BLOCK + tile), jnp.bfloat16),
            pltpu.VMEM((D_CONV, BLOCK), jnp.float32),
            pltpu.VMEM((N_SWA_KV, BLOCK, SWA_GROUP * BLOCK), jnp.float32),
        ],
        compiler_params=pltpu.CompilerParams(
            dimension_semantics=("arbitrary", "arbitrary"),
            vmem_limit_bytes=V7X_VMEM_LIMIT_BYTES),
        name="prompt_layer",
    )(x, x, pre_g, post_g, w_in_t, conv_w_t, sinks, mkb, mvtb, w_out)


def _sample_kernel(x_ref, conv_past_ref, ckt_ref, cvt_ref, mkt_ref, mvt_ref,
                   pre_g_ref, post_g_ref, w_in_t_ref, conv_w_ref, sink_ref, w_out_ref,
                   y_ref, conv_state_ref, kt_state_ref, vt_state_ref,
                   z_ref, ycat_ref, *, ns, group):
    R = SAMPLE_ROWS
    half = R // 2
    nseq = ns * group
    step = pl.program_id(1)

    @pl.when(step == 0)
    def _():
        h = _rms_norm(x_ref[...], pre_g_ref[...]).astype(jnp.bfloat16)
        z_ref[...] = lax.dot_general(h, w_in_t_ref[...], _NT, preferred_element_type=jnp.float32)
        u = (z_ref[:, OFF_CC:OFF_CC + D_CONV] * z_ref[:, OFF_CH:OFF_CH + D_CONV]).reshape(nseq, R, D_CONV)
        row3 = lax.broadcasted_iota(jnp.int32, (nseq, R, D_CONV), 1)
        u_full = jnp.where(row3 < CONV_WIDTH - 1, conv_past_ref[...], pltpu.roll(u, CONV_WIDTH - 1, axis=1))
        cw = conv_w_ref[...]
        conv = (cw[0:1, :] * u_full
                + cw[1:2, :] * pltpu.roll(u_full, R - 1, axis=1)
                + cw[2:3, :] * pltpu.roll(u_full, R - 2, axis=1))
        conv_state_ref[...] = pltpu.roll(u_full, R - half, axis=1)[:, 0:CONV_WIDTH - 1, :]
        y_conv = (z_ref[:, OFF_CB:OFF_CB + D_CONV] * conv.reshape(nseq * R, D_CONV)
                  * _silu(z_ref[:, OFF_CZ:OFF_CZ + D_CONV]))
        ycat_ref[:, YOFF_CONV:YOFF_CONV + D_CONV] = y_conv

    row = lax.broadcasted_iota(jnp.int32, (R, 2 * HEAD_DIM), 0)
    lane = lax.broadcasted_iota(jnp.int32, (R, 2 * HEAD_DIM), 1)
    lo_row = row < half
    lo_lane = lane < HEAD_DIM
    diag = lo_row == lo_lane
    lane_sq = lax.broadcasted_iota(jnp.int32, (WINDOW, WINDOW), 1)

    def pair_bias(ncols, dist_of):
        rr = lax.broadcasted_iota(jnp.int32, (R, ncols), 0)
        cc = lax.broadcasted_iota(jnp.int32, (R, ncols), 1)
        dist, valid = dist_of(rr % half, cc)
        distf = dist.astype(jnp.float32)
        tiles = []
        for pair in range(N_SWA_HEADS // 2):
            slope = jnp.where(rr < half, _alibi_slope(2 * pair), _alibi_slope(2 * pair + 1))
            tiles.append(jnp.where(valid, -slope * distf, NEG_INF))
        return jnp.concatenate(tiles, axis=0)

    def cached_dist(tok, c):
        d = tok + WINDOW - c
        return d, d < WINDOW

    def new_dist(tok, c):
        d = tok - c
        return d, (d >= 0) & (c < half)

    bias_c = pair_bias(WINDOW, cached_dist)
    bias_n = pair_bias(SAMPLE_NEW, new_dist)
    head_of_row = lax.broadcasted_iota(jnp.int32, (N_SWA_HEADS * half, 1), 0) // half
    sink_col = jnp.full((N_SWA_HEADS * half, 1), sink_ref[0], jnp.float32)
    for hh in range(1, N_SWA_HEADS):
        sink_col = jnp.where(head_of_row == hh, sink_ref[hh], sink_col)
    state_pad = jnp.zeros((WINDOW - R, D_SWA_KV), jnp.float32)

    def seq_rows(n):
        return pl.ds(pl.multiple_of((step * ns + n) * R, R), R)

    def swa_unit(n):
        rows = seq_rows(n)
        qa = z_ref[rows, OFF_Q:OFF_Q + 128] * QK_SCALE
        qb = z_ref[rows, OFF_Q + 128:OFF_Q + 256] * QK_SCALE
        qc = z_ref[rows, OFF_Q + 256:OFF_Q + 384] * QK_SCALE
        t0 = jnp.where(lo_lane, jnp.where(lo_row, qa, pltpu.roll(qa, HEAD_DIM, axis=1)), 0.0)
        t1 = jnp.where(diag, qb, 0.0)
        t2 = jnp.where(lo_lane, 0.0, jnp.where(lo_row, pltpu.roll(qc, HEAD_DIM, axis=1), qc))
        qs = jnp.concatenate([t0, t1, t2], axis=0).astype(jnp.bfloat16)

        k_new = z_ref[rows, OFF_K:OFF_K + D_SWA_KV]
        v_new = z_ref[rows, OFF_V:OFF_V + D_SWA_KV]
        k_new_b = jnp.concatenate([k_new, k_new], axis=0).astype(jnp.bfloat16)
        v_new_b = jnp.concatenate([v_new, v_new], axis=0).astype(jnp.bfloat16)
        kt_old = ckt_ref[n].astype(jnp.bfloat16)

        s_c = jnp.dot(qs, kt_old, preferred_element_type=jnp.float32) + bias_c
        s_n = lax.dot_general(qs, k_new_b, _NT, preferred_element_type=jnp.float32) + bias_n
        yield
        m = jnp.maximum(jnp.maximum(jnp.max(s_c, axis=1, keepdims=True), jnp.max(s_n, axis=1, keepdims=True)),
                        sink_col)
        p_c = jnp.exp(s_c - m)
        p_n = jnp.exp(s_n - m)
        l = (jnp.sum(p_c, axis=1, keepdims=True) + jnp.sum(p_n, axis=1, keepdims=True) + jnp.exp(sink_col - m))
        p_c = p_c.astype(jnp.bfloat16)
        p_n = p_n.astype(jnp.bfloat16)
        yield
        o = (lax.dot_general(p_c, cvt_ref[n].astype(jnp.bfloat16), _NT, preferred_element_type=jnp.float32)
             + jnp.dot(p_n, v_new_b, preferred_element_type=jnp.float32)) / l
        yield
        o0, o1, o2 = o[0:R], o[R:2 * R], o[2 * R:3 * R]
        ya = jnp.where(lo_lane, o0, pltpu.roll(pltpu.roll(o0, HEAD_DIM, axis=1), half, axis=0))
        yb = jnp.where(lo_lane, o1, pltpu.roll(o1, half, axis=0))
        yc = jnp.where(lo_lane, pltpu.roll(o2, HEAD_DIM, axis=1), pltpu.roll(o2, half, axis=0))
        y_swa = jnp.concatenate([ya, yb, yc], axis=1) * _silu(z_ref[rows, OFF_SZ:OFF_SZ + D_SWA])
        ycat_ref[rows, YOFF_SWA:YOFF_SWA + D_SWA] = y_swa

    def state_unit(n):
        rows = seq_rows(n)
        k_new_t = jnp.concatenate([state_pad, z_ref[rows, OFF_K:OFF_K + D_SWA_KV]], axis=0).T
        v_new_t = jnp.concatenate([state_pad, z_ref[rows, OFF_V:OFF_V + D_SWA_KV]], axis=0).T
        keep = lane_sq < WINDOW - half
        kt_state_ref[n] = jnp.where(keep, pltpu.roll(ckt_ref[n], WINDOW - half, axis=1), k_new_t)
        vt_state_ref[n] = jnp.where(keep, pltpu.roll(cvt_ref[n], WINDOW - half, axis=1), v_new_t)

    def mem_unit(n):
        rows = seq_rows(n)
        m0 = z_ref[rows, OFF_MQ:OFF_MQ + 128] * QK_SCALE
        m1 = z_ref[rows, OFF_MQ + 128:OFF_MQ + 256] * QK_SCALE
        zero = jnp.zeros_like(m0)
        qm = jnp.concatenate(
            [jnp.concatenate([jnp.where(diag, m0, 0.0), zero], axis=1),
             jnp.concatenate([zero, jnp.where(diag, m1, 0.0)], axis=1)], axis=0).astype(jnp.bfloat16)
        s = jnp.dot(qm, mkt_ref[n].astype(jnp.bfloat16), preferred_element_type=jnp.float32)
        yield
        m = jnp.max(s, axis=1, keepdims=True)
        p = jnp.exp(s - m)
        l = jnp.sum(p, axis=1, keepdims=True)
        p = p.astype(jnp.bfloat16)
        yield
        o = lax.dot_general(p, mvt_ref[n].astype(jnp.bfloat16), _NT,
                            preferred_element_type=jnp.float32) / l
        yield
        oa, ob = o[0:R, 0:128], o[R:2 * R, 128:256]
        y_mem = jnp.concatenate([jnp.where(lo_lane, oa, pltpu.roll(oa, half, axis=0)),
                                 jnp.where(lo_lane, ob, pltpu.roll(ob, half, axis=0))], axis=1)
        ycat_ref[rows, YOFF_MEM:YOFF_MEM + D_MEMQ] = y_mem * _silu(z_ref[rows, OFF_MZ:OFF_MZ + D_MEMQ])

    todo = [functools.partial(unit, n) for n in range(ns) for unit in (swa_unit, mem_unit)]
    states = [functools.partial(state_unit, n) for n in range(ns)]
    active = []
    while todo or active:
        for _ in range(SAMPLE_WIDTH):
            if todo:
                active.append(todo.pop(0)())
        for gen in list(active):
            if next(gen, "done") == "done":
                active.remove(gen)
        if states:
            states.pop(0)()
    for unit in states:
        unit()

    @pl.when(step == group - 1)
    def _():
        y = jnp.dot(ycat_ref[...].astype(jnp.bfloat16), w_out_ref[...], preferred_element_type=jnp.float32)
        out = x_ref[...] + _rms_norm(y, post_g_ref[...])
        y_ref[...] = out.reshape(nseq, R, D_MODEL)[:, 0:half, :]


def _sample_layer(x8, conv_past8, ckt, cvt, mkt, mvt, pre_g, post_g, w_in_t, conv_w, sinks, w_out, *, ns, group):
    N = ckt.shape[0]
    R = SAMPLE_ROWS
    nseq = ns * group
    full = lambda shape: pl.BlockSpec(shape, lambda o, i: (0,) * len(shape))
    per_group = lambda shape: pl.BlockSpec(shape, lambda o, i: (o,) + (0,) * (len(shape) - 1))
    per_step = lambda shape: pl.BlockSpec(shape, lambda o, i: (o * group + i,) + (0,) * (len(shape) - 1))
    kernel = functools.partial(_sample_kernel, ns=ns, group=group)
    return pl.pallas_call(
        kernel,
        grid=(N // nseq, group),
        in_specs=[
            per_group((nseq * R, D_MODEL)),
            per_group((nseq, R, D_CONV)),
            per_step((ns, D_SWA_KV, WINDOW)),
            per_step((ns, D_SWA_KV, WINDOW)),
            per_step((ns, D_MEMQ, N_MEM)),
            per_step((ns, D_MEMQ, N_MEM)),
            full((1, D_MODEL)),
            full((1, D_MODEL)),
            full((D_IN, D_MODEL)),
            full((CONV_WIDTH, D_CONV)),
            pl.BlockSpec(memory_space=pltpu.SMEM),
            full((D_MODEL, D_MODEL)),
        ],
        out_specs=[
            per_group((nseq, R // 2, D_MODEL)),
            per_group((nseq, CONV_WIDTH - 1, D_CONV)),
            per_step((ns, D_SWA_KV, WINDOW)),
            per_step((ns, D_SWA_KV, WINDOW)),
        ],
        out_shape=[
            jax.ShapeDtypeStruct((N, R // 2, D_MODEL), jnp.float32),
            jax.ShapeDtypeStruct((N, CONV_WIDTH - 1, D_CONV), jnp.float32),
            jax.ShapeDtypeStruct((N, D_SWA_KV, WINDOW), jnp.float32),
            jax.ShapeDtypeStruct((N, D_SWA_KV, WINDOW), jnp.float32),
        ],
        scratch_shapes=[
            pltpu.VMEM((nseq * R, D_IN), jnp.float32),
            pltpu.VMEM((nseq * R, D_MODEL), jnp.float32),
        ],
        compiler_params=pltpu.CompilerParams(
            dimension_semantics=("arbitrary", "arbitrary"),
            vmem_limit_bytes=V7X_VMEM_LIMIT_BYTES),
        name="sample_layer",
    )(x8, conv_past8, ckt, cvt, mkt, mvt, pre_g, post_g, w_in_t, conv_w, sinks, w_out)


def _heads_last_to_keys_last(a):
    n, keys, heads, dim = a.shape
    return jnp.transpose(a, (0, 2, 3, 1)).reshape(n, heads * dim, keys)


def _keys_last_to_heads_last(a, heads):
    n, hd, keys = a.shape
    return jnp.transpose(a.reshape(n, heads, hd // heads, keys), (0, 3, 1, 2))[None]


def kernel(x_prompt, x_sample, mem_prompt, state_conv, cache_swa_k, cache_swa_v, cache_mem_k, cache_mem_v,
           pre_norm_g, post_norm_g, w_in, conv_w, attn_sinks, mem_norm_g, w_mem_k, w_mem_v, w_out):
    assert w_in.shape[0] == 1, "one layer, as the problem states"
    N, TS, _ = x_sample.shape
    assert TS == SAMPLE_ROWS // 2 and cache_swa_k.shape[2] == WINDOW
    l = 0

    pre_g = pre_norm_g[l].reshape(1, D_MODEL)
    post_g = post_norm_g[l].reshape(1, D_MODEL)
    w_in_t = w_in[l].astype(jnp.bfloat16).T
    w_out_bf = w_out[l].astype(jnp.bfloat16)
    sinks = attn_sinks[l].astype(jnp.float32)

    mkt, mvt, mkb, mvtb = _mem_kv(mem_prompt, mem_norm_g[l], w_mem_k[l], w_mem_v[l])
    y_p, conv_p, kt_p, vt_p = _prompt_layer(
        x_prompt, pre_g, post_g, w_in_t, conv_w[l].T, sinks, mkb, mvtb, w_out_bf, tq=PROMPT_TQ, nsub=PROMPT_NSUB)

    x8 = jnp.concatenate([x_sample, x_sample], axis=1).reshape(N * SAMPLE_ROWS, D_MODEL)
    conv_past8 = jnp.pad(state_conv[l], ((0, 0), (0, SAMPLE_ROWS - (CONV_WIDTH - 1)), (0, 0)))
    y_s, conv_s, kt_s, vt_s = _sample_layer(
        x8, conv_past8,
        _heads_last_to_keys_last(cache_swa_k[l]), _heads_last_to_keys_last(cache_swa_v[l]),
        _heads_last_to_keys_last(cache_mem_k[l]), _heads_last_to_keys_last(cache_mem_v[l]),
        pre_g, post_g, w_in_t, conv_w[l], sinks, w_out_bf, ns=SAMPLE_NS, group=SAMPLE_GROUP)

    return (y_p, y_s,
            conv_p[None],
            _keys_last_to_heads_last(kt_p, N_SWA_KV), _keys_last_to_heads_last(vt_p, N_SWA_KV),
            _keys_last_to_heads_last(mkt, N_MEM_HEADS), _keys_last_to_heads_last(mvt, N_MEM_HEADS),
            conv_s[None],
            _keys_last_to_heads_last(kt_s, N_SWA_KV), _keys_last_to_heads_last(vt_s, N_SWA_KV))
```

```python
import functools

import numpy as np
import jax
import jax.numpy as jnp
from jax import lax
from jax.experimental import pallas as pl
from jax.experimental.pallas import tpu as pltpu

D_MODEL = 1024
HEAD_DIM = 64
D_CONV = 384
N_MEM_HEADS = 4
D_MEMQ = N_MEM_HEADS * HEAD_DIM
D_SWA = 384
N_SWA_HEADS = 6
N_SWA_KV = 2
SWA_GROUP = N_SWA_HEADS // N_SWA_KV
D_SWA_KV = N_SWA_KV * HEAD_DIM
N_MEM = 256
CONV_WIDTH = 3
WINDOW = 128
BLOCK = 128
RMS_EPS = 1e-6
NEG_INF = -1e30
D_IN = 3072
QK_SCALE = HEAD_DIM ** -0.5
LOG2_E = float(np.log2(np.e))
QK_SCALE_LOG2 = QK_SCALE * LOG2_E

OFF_CB, OFF_CC, OFF_CH, OFF_CZ = 0, 384, 768, 1152
OFF_Q, OFF_K, OFF_V, OFF_SZ = 1536, 1920, 2048, 2176
OFF_MQ, OFF_MZ = 2560, 2816
YOFF_CONV, YOFF_SWA, YOFF_MEM = 0, 384, 768

MIB = 1024 * 1024
PROMPT_VMEM_LIMIT_BYTES = 44 * MIB
SAMPLE_VMEM_LIMIT_BYTES = 48 * MIB

MEM_KV_BATCHES = 2
PROMPT_TQ = 512
PROMPT_NSUB = 2
IN_PROJ_CHUNK = 512
ATTN_WIDTH = 2
SAMPLE_NS = 16
SAMPLE_GROUP = 2
SAMPLE_ROWS = 8
SAMPLE_WIDTH = 16
SAMPLE_NEW = 16


def _alibi_slope(h):
    return float(np.power(np.float32(2.0), np.float32(-8.0 * (h + 1) / N_SWA_HEADS)))


def _rms_norm(x, g):
    return x * lax.rsqrt(jnp.mean(x * x, axis=-1, keepdims=True) + RMS_EPS) * g


def _silu(x):
    return x * jax.nn.sigmoid(x)


_NT = (((1,), (1,)), ((), ()))
_TN = (((0,), (0,)), ((), ()))


def _mem_kv_kernel(mem_ref, g_ref, wk_ref, wv_ref, mkt_ref, mvt_ref, mkb_ref, mvtb_ref, *, nb):
    mem = mem_ref[...].reshape(nb * N_MEM, D_MODEL)
    m = _rms_norm(mem, g_ref[...]).astype(jnp.bfloat16)
    mk = jnp.dot(m, wk_ref[...].astype(jnp.bfloat16), preferred_element_type=jnp.float32)
    mv = jnp.dot(m, wv_ref[...].astype(jnp.bfloat16), preferred_element_type=jnp.float32)
    for b in range(nb):
        mk_b = mk[b * N_MEM:(b + 1) * N_MEM]
        mv_t = mv[b * N_MEM:(b + 1) * N_MEM].T
        mkt_ref[b] = mk_b.T
        mvt_ref[b] = mv_t
        mkb_ref[b] = mk_b.astype(jnp.bfloat16)
        mvtb_ref[b] = mv_t.astype(jnp.bfloat16)


def _mem_kv(mem, mem_g, w_mk, w_mv):
    B = mem.shape[0]
    nb = MEM_KV_BATCHES
    full = lambda shape: pl.BlockSpec(shape, lambda b: (0,) * len(shape))
    per_batch = pl.BlockSpec((nb, N_MEM, D_MEMQ), lambda b: (b, 0, 0))
    return pl.pallas_call(
        functools.partial(_mem_kv_kernel, nb=nb),
        grid=(B // nb,),
        in_specs=[
            pl.BlockSpec((nb, N_MEM, D_MODEL), lambda b: (b, 0, 0)),
            full((1, D_MODEL)),
            full((D_MODEL, D_MEMQ)),
            full((D_MODEL, D_MEMQ)),
        ],
        out_specs=[per_batch] * 4,
        out_shape=[
            jax.ShapeDtypeStruct((B, D_MEMQ, N_MEM), jnp.float32),
            jax.ShapeDtypeStruct((B, D_MEMQ, N_MEM), jnp.float32),
            jax.ShapeDtypeStruct((B, N_MEM, D_MEMQ), jnp.bfloat16),
            jax.ShapeDtypeStruct((B, D_MEMQ, N_MEM), jnp.bfloat16),
        ],
        compiler_params=pltpu.CompilerParams(dimension_semantics=("arbitrary",)),
        name="mem_kv",
    )(mem, mem_g.reshape(1, D_MODEL), w_mk, w_mv)


def _prompt_kernel(x_ref, x_next_ref, pre_g_ref, post_g_ref, w_in_t_ref, conv_w_t_ref, sink_ref, mkb_ref, mvt_ref,
                   w_out_ref,
                   y_ref, conv_state_ref, k_state_ref, v_state_ref,
                   zt_ref, ycat_ref, kbuf_ref, vbuf_ref, uprev_ref, bias_ref, *, tq, nsub):
    assert nsub % 2 == 0
    t = pl.program_id(1)
    nblk = tq // BLOCK

    @pl.when((pl.program_id(0) == 0) & (t == 0))
    def _():
        c = lax.broadcasted_iota(jnp.int32, (BLOCK, BLOCK), 0)
        r = lax.broadcasted_iota(jnp.int32, (BLOCK, BLOCK), 1)
        distf = (r - c + jnp.where(c > r, BLOCK, 0)).astype(jnp.float32)
        for h in range(N_SWA_HEADS):
            g, i = divmod(h, SWA_GROUP)
            bias_ref[g, :, i * BLOCK:(i + 1) * BLOCK] = (-_alibi_slope(h) * LOG2_E) * distf

    @pl.when(t == 0)
    def _():
        kbuf_ref[0:BLOCK, :] = jnp.zeros((BLOCK, D_SWA_KV), jnp.bfloat16)
        vbuf_ref[:, 0:BLOCK] = jnp.zeros((D_SWA_KV, BLOCK), jnp.bfloat16)
        uprev_ref[...] = jnp.zeros_like(uprev_ref)

    first_pen = jnp.where(t == 0, NEG_INF, 0.0)
    slot = lax.broadcasted_iota(jnp.int32, (BLOCK, SWA_GROUP * BLOCK), 0)
    query = lax.broadcasted_iota(jnp.int32, (BLOCK, SWA_GROUP * BLOCK), 1) % BLOCK
    from_prev = slot > query
    q_zero = jnp.zeros((HEAD_DIM, SWA_GROUP * BLOCK), jnp.bfloat16)
    cw = conv_w_t_ref[...]
    carry = {"u_prev": uprev_ref[...]}
    head_of_lane = lax.broadcasted_iota(jnp.int32, (1, SWA_GROUP * BLOCK), 1) // BLOCK
    sink_rows = []
    for g in range(N_SWA_KV):
        row = jnp.full((1, SWA_GROUP * BLOCK), sink_ref[g * SWA_GROUP], jnp.float32)
        for i in range(1, SWA_GROUP):
            row = jnp.where(head_of_lane == i, sink_ref[g * SWA_GROUP + i], row)
        sink_rows.append(row * LOG2_E)

    def in_proj_chunks(x_rows, zt):
        state = {}

        def norm():
            state["h"] = _rms_norm(x_rows(), pre_g_ref[...]).astype(jnp.bfloat16)

        def chunk(lo):
            rows = slice(lo, lo + IN_PROJ_CHUNK)
            zt[rows, :] = lax.dot_general(w_in_t_ref[rows, :], state["h"], _NT,
                                          preferred_element_type=jnp.float32)

        return norm, [functools.partial(chunk, lo) for lo in range(0, D_IN, IN_PROJ_CHUNK)]

    def mixer_units(sub):
        zt = zt_ref.at[sub % 2]
        ycat = ycat_ref.at[sub % 2]
        tok0 = sub * tq

        def conv_unit():
            u = zt[OFF_CC:OFF_CC + D_CONV, :] * zt[OFF_CH:OFF_CH + D_CONV, :]
            ucat = jnp.concatenate([carry["u_prev"], u], axis=1)
            conv = (cw[:, 0:1] * pltpu.roll(ucat, 2, axis=1)[:, BLOCK:]
                    + cw[:, 1:2] * pltpu.roll(ucat, 1, axis=1)[:, BLOCK:]
                    + cw[:, 2:3] * u)
            y_conv = zt[OFF_CB:OFF_CB + D_CONV, :] * conv * _silu(zt[OFF_CZ:OFF_CZ + D_CONV, :])
            ycat[YOFF_CONV:YOFF_CONV + D_CONV, :] = y_conv.astype(jnp.bfloat16)
            carry["u_prev"] = u[:, tq - BLOCK:]

        def kv_unit():
            k_nat = zt[OFF_K:OFF_K + D_SWA_KV, :].T
            kbuf_ref[BLOCK + tok0:BLOCK + tok0 + tq, :] = k_nat.astype(jnp.bfloat16)
            vbuf_ref[:, BLOCK + tok0:BLOCK + tok0 + tq] = zt[OFF_V:OFF_V + D_SWA_KV, :].astype(jnp.bfloat16)

        def swa_unit(j, g):
            cols = slice(j * BLOCK, (j + 1) * BLOCK)
            band = slice(tok0 + j * BLOCK, tok0 + (j + 2) * BLOCK)
            q0 = OFF_Q + g * SWA_GROUP * HEAD_DIM
            qt = jnp.concatenate(
                [zt[q0 + i * HEAD_DIM:q0 + (i + 1) * HEAD_DIM, cols] for i in range(SWA_GROUP)],
                axis=1)
            qt = (qt * QK_SCALE_LOG2).astype(jnp.bfloat16)
            qt = jnp.concatenate([qt, q_zero] if g == 0 else [q_zero, qt], axis=0)
            s = jnp.dot(kbuf_ref[band, :], qt, preferred_element_type=jnp.float32)
            yield
            s = jnp.where(from_prev, s[0:BLOCK], s[BLOCK:]) + bias_ref[g]
            if sub == 0 and j == 0:
                s = s + jnp.where(from_prev, first_pen, 0.0)
            sink = sink_rows[g]
            m = jnp.maximum(jnp.max(s, axis=0, keepdims=True), sink)
            p = jnp.exp2(s - m)
            l = jnp.sum(p, axis=0, keepdims=True) + jnp.exp2(sink - m)
            p = jnp.concatenate([jnp.where(from_prev, p, 0.0), jnp.where(from_prev, 0.0, p)],
                                axis=0).astype(jnp.bfloat16)
            yield
            vband = vbuf_ref[g * HEAD_DIM:(g + 1) * HEAD_DIM, band]
            o = jnp.dot(vband, p, preferred_element_type=jnp.float32)
            o = o / l
            for i in range(SWA_GROUP):
                hh = g * SWA_GROUP + i
                gate = _silu(zt[OFF_SZ + hh * HEAD_DIM:OFF_SZ + (hh + 1) * HEAD_DIM, cols])
                ycat[YOFF_SWA + hh * HEAD_DIM:YOFF_SWA + (hh + 1) * HEAD_DIM, cols] = (
                    o[:, i * BLOCK:(i + 1) * BLOCK] * gate).astype(jnp.bfloat16)

        def mem_unit(hh):
            rows = slice(OFF_MQ + hh * HEAD_DIM, OFF_MQ + (hh + 1) * HEAD_DIM)
            qt = (zt[rows, :] * QK_SCALE_LOG2).astype(jnp.bfloat16)
            pieces = [jnp.zeros((HEAD_DIM, tq), jnp.bfloat16)] * N_MEM_HEADS
            pieces[hh] = qt
            s = jnp.dot(mkb_ref[0], jnp.concatenate(pieces, axis=0),
                        preferred_element_type=jnp.float32)
            yield
            m = jnp.max(s, axis=0, keepdims=True)
            p = jnp.exp2(s - m)
            l = jnp.sum(p, axis=0, keepdims=True)
            p = p.astype(jnp.bfloat16)
            yield
            o = jnp.dot(mvt_ref[0, hh * HEAD_DIM:(hh + 1) * HEAD_DIM, :], p,
                        preferred_element_type=jnp.float32)
            gate = _silu(zt[OFF_MZ + hh * HEAD_DIM:OFF_MZ + (hh + 1) * HEAD_DIM, :])
            ycat[YOFF_MEM + hh * HEAD_DIM:YOFF_MEM + (hh + 1) * HEAD_DIM, :] = (
                o / l * gate).astype(jnp.bfloat16)

        def out_proj():
            carry["y"] = lax.dot_general(ycat[...], w_out_ref[...], _TN,
                                         preferred_element_type=jnp.float32)

        def post_norm(blk):
            rows = slice(blk * BLOCK, (blk + 1) * BLOCK)
            out_rows = slice(tok0 + blk * BLOCK, tok0 + (blk + 1) * BLOCK)
            y_ref[0, out_rows, :] = x_ref[0, out_rows, :] + _rms_norm(carry["y"][rows], post_g_ref[...])

        attention = [functools.partial(swa_unit, j, g) for j in range(nblk) for g in range(N_SWA_KV)]
        attention += [functools.partial(mem_unit, hh) for hh in range(N_MEM_HEADS)]
        return [conv_unit, kv_unit], attention, (out_proj, [functools.partial(post_norm, b) for b in range(nblk)])

    def interleave(units, chunks, norm_first, norm_after):
        (conv_unit, kv_unit), attention, (out_proj, post_norms) = units
        pending = list(chunks)
        kv_unit()
        n_rounds = -(-len(attention) // ATTN_WIDTH) + 2
        chunk_rounds = [k * n_rounds // len(pending) for k in range(len(pending))]
        todo = list(attention)
        active = []
        for rnd in range(n_rounds):
            for _ in range(ATTN_WIDTH):
                if todo:
                    active.append(todo.pop(0)())
            for gen in list(active):
                if next(gen, "done") == "done":
                    active.remove(gen)
            if rnd == 0 and norm_first is not None:
                norm_first()
            for _ in range(chunk_rounds.count(rnd)):
                pending.pop(0)()
            if rnd == 0:
                conv_unit()
        assert not todo and not active and not pending
        if norm_after is not None:
            norm_after()
        out_proj()
        for post_norm in post_norms:
            post_norm()

    @pl.when((pl.program_id(0) == 0) & (t == 0))
    def _():
        norm, chunks = in_proj_chunks(lambda: x_ref[0, 0:tq, :], zt_ref.at[0])
        norm()
        for chunk in chunks:
            chunk()

    proj = [in_proj_chunks(lambda sub=sub: x_ref[0, sub * tq:(sub + 1) * tq, :], zt_ref.at[sub % 2])
            for sub in range(1, nsub)]
    proj.append(in_proj_chunks(lambda: x_next_ref[0], zt_ref.at[0]))
    for sub in range(nsub):
        norm_first = proj[0][0] if sub == 0 else None
        norm_after = proj[sub + 1][0] if sub + 1 < nsub else None
        interleave(mixer_units(sub), proj[sub][1], norm_first, norm_after)

    tile = nsub * tq
    uprev_ref[...] = carry["u_prev"]
    conv_state_ref[0] = carry["u_prev"].T[BLOCK - (CONV_WIDTH - 1):, :]
    kbuf_ref[0:BLOCK, :] = kbuf_ref[tile:tile + BLOCK, :]
    vbuf_ref[:, 0:BLOCK] = vbuf_ref[:, tile:tile + BLOCK]
    k_state_ref[0] = zt_ref[1, OFF_K:OFF_K + D_SWA_KV, tq - BLOCK:]
    v_state_ref[0] = zt_ref[1, OFF_V:OFF_V + D_SWA_KV, tq - BLOCK:]


def _prompt_layer(x, pre_g, post_g, w_in_t, conv_w_t, sinks, mkb, mvtb, w_out, *, tq, nsub):
    B, T, _ = x.shape
    tile = tq * nsub
    full = lambda shape: pl.BlockSpec(shape, lambda b, t: (0,) * len(shape))
    kernel = functools.partial(_prompt_kernel, tq=tq, nsub=nsub)
    steps = T // tile

    def next_first_sub_tile(b, t):
        nxt = jnp.minimum(b * steps + t + 1, B * steps - 1)
        return nxt // steps, (nxt % steps) * nsub, 0

    return pl.pallas_call(
        kernel,
        grid=(B, steps),
        in_specs=[
            pl.BlockSpec((1, tile, D_MODEL), lambda b, t: (b, t, 0)),
            pl.BlockSpec((1, tq, D_MODEL), next_first_sub_tile),
            full((1, D_MODEL)),
            full((1, D_MODEL)),
            full((D_IN, D_MODEL)),
            full((D_CONV, CONV_WIDTH)),
            pl.BlockSpec(memory_space=pltpu.SMEM),
            pl.BlockSpec((1, N_MEM, D_MEMQ), lambda b, t: (b, 0, 0)),
            pl.BlockSpec((1, D_MEMQ, N_MEM), lambda b, t: (b, 0, 0)),
            full((D_MODEL, D_MODEL)),
        ],
        out_specs=[
            pl.BlockSpec((1, tile, D_MODEL), lambda b, t: (b, t, 0)),
            pl.BlockSpec((1, CONV_WIDTH - 1, D_CONV), lambda b, t: (b, 0, 0)),
            pl.BlockSpec((1, D_SWA_KV, BLOCK), lambda b, t: (b, 0, 0)),
            pl.BlockSpec((1, D_SWA_KV, BLOCK), lambda b, t: (b, 0, 0)),
        ],
        out_shape=[
            jax.ShapeDtypeStruct((B, T, D_MODEL), jnp.float32),
            jax.ShapeDtypeStruct((B, CONV_WIDTH - 1, D_CONV), jnp.float32),
            jax.ShapeDtypeStruct((B, D_SWA_KV, BLOCK), jnp.float32),
            jax.ShapeDtypeStruct((B, D_SWA_KV, BLOCK), jnp.float32),
        ],
        scratch_shapes=[
            pltpu.VMEM((2, D_IN, tq), jnp.float32),
            pltpu.VMEM((2, D_MODEL, tq), jnp.bfloat16),
            pltpu.VMEM((BLOCK + tile, D_SWA_KV), jnp.bfloat16),
            pltpu.VMEM((D_SWA_KV, BLOCK + tile), jnp.bfloat16),
            pltpu.VMEM((D_CONV, BLOCK), jnp.float32),
            pltpu.VMEM((N_SWA_KV, BLOCK, SWA_GROUP * BLOCK), jnp.float32),
        ],
        compiler_params=pltpu.CompilerParams(
            dimension_semantics=("arbitrary", "arbitrary"),
            vmem_limit_bytes=PROMPT_VMEM_LIMIT_BYTES),
        name="prompt_layer",
    )(x, x, pre_g, post_g, w_in_t, conv_w_t, sinks, mkb, mvtb, w_out)


def _sample_kernel(x_ref, conv_past_ref, ckt_ref, cvt_ref, mkt_ref, mvt_ref,
                   pre_g_ref, post_g_ref, w_in_t_ref, conv_w_ref, sink_ref, w_out_ref,
                   y_ref, conv_state_ref, kt_state_ref, vt_state_ref,
                   z_ref, ycat_ref, *, ns, group):
    R = SAMPLE_ROWS
    half = R // 2
    nseq = ns * group
    step = pl.program_id(1)

    @pl.when(step == 0)
    def _():
        h = _rms_norm(x_ref[...], pre_g_ref[...]).astype(jnp.bfloat16)
        z_ref[...] = lax.dot_general(h, w_in_t_ref[...], _NT, preferred_element_type=jnp.float32)
        u = (z_ref[:, OFF_CC:OFF_CC + D_CONV] * z_ref[:, OFF_CH:OFF_CH + D_CONV]).reshape(nseq, R, D_CONV)
        row3 = lax.broadcasted_iota(jnp.int32, (nseq, R, D_CONV), 1)
        u_full = jnp.where(row3 < CONV_WIDTH - 1, conv_past_ref[...], pltpu.roll(u, CONV_WIDTH - 1, axis=1))
        cw = conv_w_ref[...]
        conv = (cw[0:1, :] * u_full
                + cw[1:2, :] * pltpu.roll(u_full, R - 1, axis=1)
                + cw[2:3, :] * pltpu.roll(u_full, R - 2, axis=1))
        conv_state_ref[...] = pltpu.roll(u_full, R - half, axis=1)[:, 0:CONV_WIDTH - 1, :]
        y_conv = (z_ref[:, OFF_CB:OFF_CB + D_CONV] * conv.reshape(nseq * R, D_CONV)
                  * _silu(z_ref[:, OFF_CZ:OFF_CZ + D_CONV]))
        ycat_ref[:, YOFF_CONV:YOFF_CONV + D_CONV] = y_conv

    row = lax.broadcasted_iota(jnp.int32, (R, 2 * HEAD_DIM), 0)
    lane = lax.broadcasted_iota(jnp.int32, (R, 2 * HEAD_DIM), 1)
    lo_row = row < half
    lo_lane = lane < HEAD_DIM
    diag = lo_row == lo_lane
    lane_sq = lax.broadcasted_iota(jnp.int32, (WINDOW, WINDOW), 1)

    def pair_bias(ncols, dist_of):
        rr = lax.broadcasted_iota(jnp.int32, (R, ncols), 0)
        cc = lax.broadcasted_iota(jnp.int32, (R, ncols), 1)
        dist, valid = dist_of(rr % half, cc)
        distf = dist.astype(jnp.float32)
        tiles = []
        for pair in range(N_SWA_HEADS // 2):
            slope = jnp.where(rr < half, _alibi_slope(2 * pair), _alibi_slope(2 * pair + 1))
            tiles.append(jnp.where(valid, -slope * distf, NEG_INF))
        return jnp.concatenate(tiles, axis=0)

    def cached_dist(tok, c):
        d = tok + WINDOW - c
        return d, d < WINDOW

    def new_dist(tok, c):
        d = tok - c
        return d, (d >= 0) & (c < half)

    bias_c = pair_bias(WINDOW, cached_dist)
    bias_n = pair_bias(SAMPLE_NEW, new_dist)
    head_of_row = lax.broadcasted_iota(jnp.int32, (N_SWA_HEADS * half, 1), 0) // half
    sink_col = jnp.full((N_SWA_HEADS * half, 1), sink_ref[0], jnp.float32)
    for hh in range(1, N_SWA_HEADS):
        sink_col = jnp.where(head_of_row == hh, sink_ref[hh], sink_col)
    state_pad = jnp.zeros((WINDOW - R, D_SWA_KV), jnp.float32)

    def seq_rows(n):
        return pl.ds(pl.multiple_of((step * ns + n) * R, R), R)

    def swa_unit(n):
        rows = seq_rows(n)
        qa = z_ref[rows, OFF_Q:OFF_Q + 128] * QK_SCALE
        qb = z_ref[rows, OFF_Q + 128:OFF_Q + 256] * QK_SCALE
        qc = z_ref[rows, OFF_Q + 256:OFF_Q + 384] * QK_SCALE
        t0 = jnp.where(lo_lane, jnp.where(lo_row, qa, pltpu.roll(qa, HEAD_DIM, axis=1)), 0.0)
        t1 = jnp.where(diag, qb, 0.0)
        t2 = jnp.where(lo_lane, 0.0, jnp.where(lo_row, pltpu.roll(qc, HEAD_DIM, axis=1), qc))
        qs = jnp.concatenate([t0, t1, t2], axis=0).astype(jnp.bfloat16)

        k_new = z_ref[rows, OFF_K:OFF_K + D_SWA_KV]
        v_new = z_ref[rows, OFF_V:OFF_V + D_SWA_KV]
        k_new_b = jnp.concatenate([k_new, k_new], axis=0).astype(jnp.bfloat16)
        v_new_b = jnp.concatenate([v_new, v_new], axis=0).astype(jnp.bfloat16)
        kt_old = ckt_ref[n].astype(jnp.bfloat16)

        s_c = jnp.dot(qs, kt_old, preferred_element_type=jnp.float32) + bias_c
        s_n = lax.dot_general(qs, k_new_b, _NT, preferred_element_type=jnp.float32) + bias_n
        yield
        m = jnp.maximum(jnp.maximum(jnp.max(s_c, axis=1, keepdims=True), jnp.max(s_n, axis=1, keepdims=True)),
                        sink_col)
        p_c = jnp.exp(s_c - m)
        p_n = jnp.exp(s_n - m)
        l = (jnp.sum(p_c, axis=1, keepdims=True) + jnp.sum(p_n, axis=1, keepdims=True) + jnp.exp(sink_col - m))
        p_c = p_c.astype(jnp.bfloat16)
        p_n = p_n.astype(jnp.bfloat16)
        yield
        o = (lax.dot_general(p_c, cvt_ref[n].astype(jnp.bfloat16), _NT, preferred_element_type=jnp.float32)
             + jnp.dot(p_n, v_new_b, preferred_element_type=jnp.float32)) / l
        yield
        o0, o1, o2 = o[0:R], o[R:2 * R], o[2 * R:3 * R]
        ya = jnp.where(lo_lane, o0, pltpu.roll(pltpu.roll(o0, HEAD_DIM, axis=1), half, axis=0))
        yb = jnp.where(lo_lane, o1, pltpu.roll(o1, half, axis=0))
        yc = jnp.where(lo_lane, pltpu.roll(o2, HEAD_DIM, axis=1), pltpu.roll(o2, half, axis=0))
        y_swa = jnp.concatenate([ya, yb, yc], axis=1) * _silu(z_ref[rows, OFF_SZ:OFF_SZ + D_SWA])
        ycat_ref[rows, YOFF_SWA:YOFF_SWA + D_SWA] = y_swa

    def state_unit(n):
        rows = seq_rows(n)
        k_new_t = jnp.concatenate([state_pad, z_ref[rows, OFF_K:OFF_K + D_SWA_KV]], axis=0).T
        v_new_t = jnp.concatenate([state_pad, z_ref[rows, OFF_V:OFF_V + D_SWA_KV]], axis=0).T
        keep = lane_sq < WINDOW - half
        kt_state_ref[n] = jnp.where(keep, pltpu.roll(ckt_ref[n], WINDOW - half, axis=1), k_new_t)
        vt_state_ref[n] = jnp.where(keep, pltpu.roll(cvt_ref[n], WINDOW - half, axis=1), v_new_t)

    def mem_unit(n):
        rows = seq_rows(n)
        m0 = z_ref[rows, OFF_MQ:OFF_MQ + 128] * QK_SCALE
        m1 = z_ref[rows, OFF_MQ + 128:OFF_MQ + 256] * QK_SCALE
        zero = jnp.zeros_like(m0)
        qm = jnp.concatenate(
            [jnp.concatenate([jnp.where(diag, m0, 0.0), zero], axis=1),
             jnp.concatenate([zero, jnp.where(diag, m1, 0.0)], axis=1)], axis=0).astype(jnp.bfloat16)
        s = jnp.dot(qm, mkt_ref[n].astype(jnp.bfloat16), preferred_element_type=jnp.float32)
        yield
        m = jnp.max(s, axis=1, keepdims=True)
        p = jnp.exp(s - m)
        l = jnp.sum(p, axis=1, keepdims=True)
        p = p.astype(jnp.bfloat16)
        yield
        o = lax.dot_general(p, mvt_ref[n].astype(jnp.bfloat16), _NT,
                            preferred_element_type=jnp.float32) / l
        yield
        oa, ob = o[0:R, 0:128], o[R:2 * R, 128:256]
        y_mem = jnp.concatenate([jnp.where(lo_lane, oa, pltpu.roll(oa, half, axis=0)),
                                 jnp.where(lo_lane, ob, pltpu.roll(ob, half, axis=0))], axis=1)
        ycat_ref[rows, YOFF_MEM:YOFF_MEM + D_MEMQ] = y_mem * _silu(z_ref[rows, OFF_MZ:OFF_MZ + D_MEMQ])

    todo = [functools.partial(unit, n) for n in range(ns) for unit in (swa_unit, mem_unit)]
    states = [functools.partial(state_unit, n) for n in range(ns)]
    active = []
    while todo or active:
        for _ in range(SAMPLE_WIDTH):
            if todo:
                active.append(todo.pop(0)())
        for gen in list(active):
            if next(gen, "done") == "done":
                active.remove(gen)
        if states:
            states.pop(0)()
    for unit in states:
        unit()

    @pl.when(step == group - 1)
    def _():
        y = jnp.dot(ycat_ref[...].astype(jnp.bfloat16), w_out_ref[...], preferred_element_type=jnp.float32)
        out = x_ref[...] + _rms_norm(y, post_g_ref[...])
        y_ref[...] = out.reshape(nseq, R, D_MODEL)[:, 0:half, :]


def _sample_layer(x8, conv_past8, ckt, cvt, mkt, mvt, pre_g, post_g, w_in_t, conv_w, sinks, w_out, *, ns, group):
    N = ckt.shape[0]
    R = SAMPLE_ROWS
    nseq = ns * group
    full = lambda shape: pl.BlockSpec(shape, lambda o, i: (0,) * len(shape))
    per_group = lambda shape: pl.BlockSpec(shape, lambda o, i: (o,) + (0,) * (len(shape) - 1))
    per_step = lambda shape: pl.BlockSpec(shape, lambda o, i: (o * group + i,) + (0,) * (len(shape) - 1))
    kernel = functools.partial(_sample_kernel, ns=ns, group=group)
    return pl.pallas_call(
        kernel,
        grid=(N // nseq, group),
        in_specs=[
            per_group((nseq * R, D_MODEL)),
            per_group((nseq, R, D_CONV)),
            per_step((ns, D_SWA_KV, WINDOW)),
            per_step((ns, D_SWA_KV, WINDOW)),
            per_step((ns, D_MEMQ, N_MEM)),
            per_step((ns, D_MEMQ, N_MEM)),
            full((1, D_MODEL)),
            full((1, D_MODEL)),
            full((D_IN, D_MODEL)),
            full((CONV_WIDTH, D_CONV)),
            pl.BlockSpec(memory_space=pltpu.SMEM),
            full((D_MODEL, D_MODEL)),
        ],
        out_specs=[
            per_group((nseq, R // 2, D_MODEL)),
            per_group((nseq, CONV_WIDTH - 1, D_CONV)),
            per_step((ns, D_SWA_KV, WINDOW)),
            per_step((ns, D_SWA_KV, WINDOW)),
        ],
        out_shape=[
            jax.ShapeDtypeStruct((N, R // 2, D_MODEL), jnp.float32),
            jax.ShapeDtypeStruct((N, CONV_WIDTH - 1, D_CONV), jnp.float32),
            jax.ShapeDtypeStruct((N, D_SWA_KV, WINDOW), jnp.float32),
            jax.ShapeDtypeStruct((N, D_SWA_KV, WINDOW), jnp.float32),
        ],
        scratch_shapes=[
            pltpu.VMEM((nseq * R, D_IN), jnp.float32),
            pltpu.VMEM((nseq * R, D_MODEL), jnp.float32),
        ],
        compiler_params=pltpu.CompilerParams(
            dimension_semantics=("arbitrary", "arbitrary"),
            vmem_limit_bytes=SAMPLE_VMEM_LIMIT_BYTES),
        name="sample_layer",
    )(x8, conv_past8, ckt, cvt, mkt, mvt, pre_g, post_g, w_in_t, conv_w, sinks, w_out)


def _heads_last_to_keys_last(a):
    n, keys, heads, dim = a.shape
    return jnp.transpose(a, (0, 2, 3, 1)).reshape(n, heads * dim, keys)


def _keys_last_to_heads_last(a, heads):
    n, hd, keys = a.shape
    return jnp.transpose(a.reshape(n, heads, hd // heads, keys), (0, 3, 1, 2))[None]


def kernel(x_prompt, x_sample, mem_prompt, state_conv, cache_swa_k, cache_swa_v, cache_mem_k, cache_mem_v,
           pre_norm_g, post_norm_g, w_in, conv_w, attn_sinks, mem_norm_g, w_mem_k, w_mem_v, w_out):
    assert w_in.shape[0] == 1, "one layer, as the problem states"
    N, TS, _ = x_sample.shape
    assert TS == SAMPLE_ROWS // 2 and cache_swa_k.shape[2] == WINDOW
    l = 0

    pre_g = pre_norm_g[l].reshape(1, D_MODEL)
    post_g = post_norm_g[l].reshape(1, D_MODEL)
    w_in_t = w_in[l].astype(jnp.bfloat16).T
    w_out_bf = w_out[l].astype(jnp.bfloat16)
    sinks = attn_sinks[l].astype(jnp.float32)

    mkt, mvt, mkb, mvtb = _mem_kv(mem_prompt, mem_norm_g[l], w_mem_k[l], w_mem_v[l])
    y_p, conv_p, kt_p, vt_p = _prompt_layer(
        x_prompt, pre_g, post_g, w_in_t, conv_w[l].T, sinks, mkb, mvtb, w_out_bf, tq=PROMPT_TQ, nsub=PROMPT_NSUB)

    x8 = jnp.concatenate([x_sample, x_sample], axis=1).reshape(N * SAMPLE_ROWS, D_MODEL)
    conv_past8 = jnp.pad(state_conv[l], ((0, 0), (0, SAMPLE_ROWS - (CONV_WIDTH - 1)), (0, 0)))
    y_s, conv_s, kt_s, vt_s = _sample_layer(
        x8, conv_past8,
        _heads_last_to_keys_last(cache_swa_k[l]), _heads_last_to_keys_last(cache_swa_v[l]),
        _heads_last_to_keys_last(cache_mem_k[l]), _heads_last_to_keys_last(cache_mem_v[l]),
        pre_g, post_g, w_in_t, conv_w[l], sinks, w_out_bf, ns=SAMPLE_NS, group=SAMPLE_GROUP)

    return (y_p, y_s,
            conv_p[None],
            _keys_last_to_heads_last(kt_p, N_SWA_KV), _keys_last_to_heads_last(vt_p, N_SWA_KV),
            _keys_last_to_heads_last(mkt, N_MEM_HEADS), _keys_last_to_heads_last(mvt, N_MEM_HEADS),
            conv_s[None],
            _keys_last_to_heads_last(kt_s, N_SWA_KV), _keys_last_to_heads_last(vt_s, N_SWA_KV))
```

```python
import functools

import numpy as np
import jax
import jax.numpy as jnp
from jax import lax
from jax.experimental import pallas as pl
from jax.experimental.pallas import tpu as pltpu

D_MODEL = 1024
HEAD_DIM = 64
D_CONV = 384
N_MEM_HEADS = 4
D_MEMQ = N_MEM_HEADS * HEAD_DIM
D_SWA = 384
N_SWA_HEADS = 6
N_SWA_KV = 2
SWA_GROUP = N_SWA_HEADS // N_SWA_KV
D_SWA_KV = N_SWA_KV * HEAD_DIM
N_MEM = 256
CONV_WIDTH = 3
WINDOW = 128
BLOCK = 128
RMS_EPS = 1e-6
NEG_INF = -1e30
D_IN = 3072
QK_SCALE = HEAD_DIM ** -0.5
LOG2_E = float(np.log2(np.e))
QK_SCALE_LOG2 = QK_SCALE * LOG2_E

OFF_CB, OFF_CC, OFF_CH, OFF_CZ = 0, 384, 768, 1152
OFF_Q, OFF_K, OFF_V, OFF_SZ = 1536, 1920, 2048, 2176
OFF_MQ, OFF_MZ = 2560, 2816
YOFF_CONV, YOFF_SWA, YOFF_MEM = 0, 384, 768

PAIR = 2 * HEAD_DIM

MIB = 1024 * 1024
PROMPT_VMEM_LIMIT_BYTES = 44 * MIB
SAMPLE_VMEM_LIMIT_BYTES = 48 * MIB

MEM_KV_BATCHES = 2
PROMPT_TQ = 512
PROMPT_NSUB = 2
IN_PROJ_CHUNK = 512
ATTN_WIDTH = 2
SAMPLE_NS = 16
SAMPLE_GROUP = 2
SAMPLE_ROWS = 8
SAMPLE_WIDTH = 16
SAMPLE_NEW = 16


def _alibi_slope(h):
    return float(np.power(np.float32(2.0), np.float32(-8.0 * (h + 1) / N_SWA_HEADS)))


def _rms_norm(x, g):
    return x * lax.rsqrt(jnp.mean(x * x, axis=-1, keepdims=True) + RMS_EPS) * g


def _silu(x):
    return x * jax.nn.sigmoid(x)


_NT = (((1,), (1,)), ((), ()))
_TN = (((0,), (0,)), ((), ()))


def _mem_kv_kernel(mem_ref, g_ref, wk_ref, wv_ref, mkt_ref, mvt_ref, mkb_ref, mvtb_ref, *, nb):
    mem = mem_ref[...].reshape(nb * N_MEM, D_MODEL)
    m = _rms_norm(mem, g_ref[...]).astype(jnp.bfloat16)
    mk = jnp.dot(m, wk_ref[...].astype(jnp.bfloat16), preferred_element_type=jnp.float32)
    mv = jnp.dot(m, wv_ref[...].astype(jnp.bfloat16), preferred_element_type=jnp.float32)
    for b in range(nb):
        mk_b = mk[b * N_MEM:(b + 1) * N_MEM]
        mv_t = mv[b * N_MEM:(b + 1) * N_MEM].T
        mkt_ref[b] = mk_b.T
        mvt_ref[b] = mv_t
        mkb_ref[b] = mk_b.astype(jnp.bfloat16)
        mvtb_ref[b] = mv_t.astype(jnp.bfloat16)


def _mem_kv(mem, mem_g, w_mk, w_mv):
    B = mem.shape[0]
    nb = MEM_KV_BATCHES
    full = lambda shape: pl.BlockSpec(shape, lambda b: (0,) * len(shape))
    per_batch = pl.BlockSpec((nb, N_MEM, D_MEMQ), lambda b: (b, 0, 0))
    return pl.pallas_call(
        functools.partial(_mem_kv_kernel, nb=nb),
        grid=(B // nb,),
        in_specs=[
            pl.BlockSpec((nb, N_MEM, D_MODEL), lambda b: (b, 0, 0)),
            full((1, D_MODEL)),
            full((D_MODEL, D_MEMQ)),
            full((D_MODEL, D_MEMQ)),
        ],
        out_specs=[per_batch] * 4,
        out_shape=[
            jax.ShapeDtypeStruct((B, D_MEMQ, N_MEM), jnp.float32),
            jax.ShapeDtypeStruct((B, D_MEMQ, N_MEM), jnp.float32),
            jax.ShapeDtypeStruct((B, N_MEM, D_MEMQ), jnp.bfloat16),
            jax.ShapeDtypeStruct((B, D_MEMQ, N_MEM), jnp.bfloat16),
        ],
        compiler_params=pltpu.CompilerParams(dimension_semantics=("arbitrary",)),
        name="mem_kv",
    )(mem, mem_g.reshape(1, D_MODEL), w_mk, w_mv)


def _prompt_kernel(x_ref, x_next_ref, pre_g_ref, post_g_ref, w_in_t_ref, conv_w_t_ref, sink_ref, mkb_ref, mvt_ref,
                   w_out_ref,
                   y_ref, conv_state_ref, k_state_ref, v_state_ref,
                   zt_ref, ycat_ref, kbuf_ref, vbuf_ref, uprev_ref, bias_ref, *, tq, nsub):
    assert nsub % 2 == 0
    t = pl.program_id(1)
    nblk = tq // BLOCK

    @pl.when((pl.program_id(0) == 0) & (t == 0))
    def _():
        c = lax.broadcasted_iota(jnp.int32, (BLOCK, BLOCK), 0)
        r = lax.broadcasted_iota(jnp.int32, (BLOCK, BLOCK), 1)
        distf = (r - c + jnp.where(c > r, BLOCK, 0)).astype(jnp.float32)
        for h in range(N_SWA_HEADS):
            g, i = divmod(h, SWA_GROUP)
            bias_ref[g, :, i * BLOCK:(i + 1) * BLOCK] = (-_alibi_slope(h) * LOG2_E) * distf

    @pl.when(t == 0)
    def _():
        kbuf_ref[0:BLOCK, :] = jnp.zeros((BLOCK, D_SWA_KV), jnp.bfloat16)
        vbuf_ref[:, 0:BLOCK] = jnp.zeros((D_SWA_KV, BLOCK), jnp.bfloat16)
        uprev_ref[...] = jnp.zeros_like(uprev_ref)

    first_pen = jnp.where(t == 0, NEG_INF, 0.0)
    slot = lax.broadcasted_iota(jnp.int32, (BLOCK, SWA_GROUP * BLOCK), 0)
    query = lax.broadcasted_iota(jnp.int32, (BLOCK, SWA_GROUP * BLOCK), 1) % BLOCK
    from_prev = slot > query
    q_zero = jnp.zeros((HEAD_DIM, SWA_GROUP * BLOCK), jnp.bfloat16)
    cw = conv_w_t_ref[...]
    carry = {"u_prev": uprev_ref[...]}
    head_of_lane = lax.broadcasted_iota(jnp.int32, (1, SWA_GROUP * BLOCK), 1) // BLOCK
    sink_rows = []
    for g in range(N_SWA_KV):
        row = jnp.full((1, SWA_GROUP * BLOCK), sink_ref[g * SWA_GROUP], jnp.float32)
        for i in range(1, SWA_GROUP):
            row = jnp.where(head_of_lane == i, sink_ref[g * SWA_GROUP + i], row)
        sink_rows.append(row * LOG2_E)

    def in_proj_chunks(x_rows, zt):
        state = {}

        def norm():
            state["h"] = _rms_norm(x_rows(), pre_g_ref[...]).astype(jnp.bfloat16)

        def chunk(lo):
            rows = slice(lo, lo + IN_PROJ_CHUNK)
            zt[rows, :] = lax.dot_general(w_in_t_ref[rows, :], state["h"], _NT,
                                          preferred_element_type=jnp.float32)

        return norm, [functools.partial(chunk, lo) for lo in range(0, D_IN, IN_PROJ_CHUNK)]

    def mixer_units(sub):
        zt = zt_ref.at[sub % 2]
        ycat = ycat_ref.at[sub % 2]
        tok0 = sub * tq

        def conv_unit():
            u = zt[OFF_CC:OFF_CC + D_CONV, :] * zt[OFF_CH:OFF_CH + D_CONV, :]
            ucat = jnp.concatenate([carry["u_prev"], u], axis=1)
            conv = (cw[:, 0:1] * pltpu.roll(ucat, 2, axis=1)[:, BLOCK:]
                    + cw[:, 1:2] * pltpu.roll(ucat, 1, axis=1)[:, BLOCK:]
                    + cw[:, 2:3] * u)
            y_conv = zt[OFF_CB:OFF_CB + D_CONV, :] * conv * _silu(zt[OFF_CZ:OFF_CZ + D_CONV, :])
            ycat[YOFF_CONV:YOFF_CONV + D_CONV, :] = y_conv.astype(jnp.bfloat16)
            carry["u_prev"] = u[:, tq - BLOCK:]

        def kv_unit():
            k_nat = zt[OFF_K:OFF_K + D_SWA_KV, :].T
            kbuf_ref[BLOCK + tok0:BLOCK + tok0 + tq, :] = k_nat.astype(jnp.bfloat16)
            vbuf_ref[:, BLOCK + tok0:BLOCK + tok0 + tq] = zt[OFF_V:OFF_V + D_SWA_KV, :].astype(jnp.bfloat16)

        def swa_unit(j, g):
            cols = slice(j * BLOCK, (j + 1) * BLOCK)
            band = slice(tok0 + j * BLOCK, tok0 + (j + 2) * BLOCK)
            q0 = OFF_Q + g * SWA_GROUP * HEAD_DIM
            qt = jnp.concatenate(
                [zt[q0 + i * HEAD_DIM:q0 + (i + 1) * HEAD_DIM, cols] for i in range(SWA_GROUP)],
                axis=1)
            qt = (qt * QK_SCALE_LOG2).astype(jnp.bfloat16)
            qt = jnp.concatenate([qt, q_zero] if g == 0 else [q_zero, qt], axis=0)
            s = jnp.dot(kbuf_ref[band, :], qt, preferred_element_type=jnp.float32)
            yield
            s = jnp.where(from_prev, s[0:BLOCK], s[BLOCK:]) + bias_ref[g]
            if sub == 0 and j == 0:
                s = s + jnp.where(from_prev, first_pen, 0.0)
            sink = sink_rows[g]
            m = jnp.maximum(jnp.max(s, axis=0, keepdims=True), sink)
            p = jnp.exp2(s - m)
            l = jnp.sum(p, axis=0, keepdims=True) + jnp.exp2(sink - m)
            p = jnp.concatenate([jnp.where(from_prev, p, 0.0), jnp.where(from_prev, 0.0, p)],
                                axis=0).astype(jnp.bfloat16)
            yield
            vband = vbuf_ref[g * HEAD_DIM:(g + 1) * HEAD_DIM, band]
            o = jnp.dot(vband, p, preferred_element_type=jnp.float32)
            o = o / l
            for i in range(SWA_GROUP):
                hh = g * SWA_GROUP + i
                gate = _silu(zt[OFF_SZ + hh * HEAD_DIM:OFF_SZ + (hh + 1) * HEAD_DIM, cols])
                ycat[YOFF_SWA + hh * HEAD_DIM:YOFF_SWA + (hh + 1) * HEAD_DIM, cols] = (
                    o[:, i * BLOCK:(i + 1) * BLOCK] * gate).astype(jnp.bfloat16)

        def mem_unit(hh):
            rows = slice(OFF_MQ + hh * HEAD_DIM, OFF_MQ + (hh + 1) * HEAD_DIM)
            qt = (zt[rows, :] * QK_SCALE_LOG2).astype(jnp.bfloat16)
            pieces = [jnp.zeros((HEAD_DIM, tq), jnp.bfloat16)] * N_MEM_HEADS
            pieces[hh] = qt
            s = jnp.dot(mkb_ref[0], jnp.concatenate(pieces, axis=0),
                        preferred_element_type=jnp.float32)
            yield
            m = jnp.max(s, axis=0, keepdims=True)
            p = jnp.exp2(s - m)
            l = jnp.sum(p, axis=0, keepdims=True)
            p = p.astype(jnp.bfloat16)
            yield
            o = jnp.dot(mvt_ref[0, hh * HEAD_DIM:(hh + 1) * HEAD_DIM, :], p,
                        preferred_element_type=jnp.float32)
            gate = _silu(zt[OFF_MZ + hh * HEAD_DIM:OFF_MZ + (hh + 1) * HEAD_DIM, :])
            ycat[YOFF_MEM + hh * HEAD_DIM:YOFF_MEM + (hh + 1) * HEAD_DIM, :] = (
                o / l * gate).astype(jnp.bfloat16)

        def out_proj():
            carry["y"] = lax.dot_general(ycat[...], w_out_ref[...], _TN,
                                         preferred_element_type=jnp.float32)

        def post_norm(blk):
            rows = slice(blk * BLOCK, (blk + 1) * BLOCK)
            out_rows = slice(tok0 + blk * BLOCK, tok0 + (blk + 1) * BLOCK)
            y_ref[0, out_rows, :] = x_ref[0, out_rows, :] + _rms_norm(carry["y"][rows], post_g_ref[...])

        attention = [functools.partial(swa_unit, j, g) for j in range(nblk) for g in range(N_SWA_KV)]
        attention += [functools.partial(mem_unit, hh) for hh in range(N_MEM_HEADS)]
        return [conv_unit, kv_unit], attention, (out_proj, [functools.partial(post_norm, b) for b in range(nblk)])

    def interleave(units, chunks, norm_first, norm_after):
        (conv_unit, kv_unit), attention, (out_proj, post_norms) = units
        pending = list(chunks)
        kv_unit()
        n_rounds = -(-len(attention) // ATTN_WIDTH) + 2
        chunk_rounds = [k * n_rounds // len(pending) for k in range(len(pending))]
        todo = list(attention)
        active = []
        for rnd in range(n_rounds):
            for _ in range(ATTN_WIDTH):
                if todo:
                    active.append(todo.pop(0)())
            for gen in list(active):
                if next(gen, "done") == "done":
                    active.remove(gen)
            if rnd == 0 and norm_first is not None:
                norm_first()
            for _ in range(chunk_rounds.count(rnd)):
                pending.pop(0)()
            if rnd == 0:
                conv_unit()
        assert not todo and not active and not pending
        if norm_after is not None:
            norm_after()
        out_proj()
        for post_norm in post_norms:
            post_norm()

    @pl.when((pl.program_id(0) == 0) & (t == 0))
    def _():
        norm, chunks = in_proj_chunks(lambda: x_ref[0, 0:tq, :], zt_ref.at[0])
        norm()
        for chunk in chunks:
            chunk()

    proj = [in_proj_chunks(lambda sub=sub: x_ref[0, sub * tq:(sub + 1) * tq, :], zt_ref.at[sub % 2])
            for sub in range(1, nsub)]
    proj.append(in_proj_chunks(lambda: x_next_ref[0], zt_ref.at[0]))
    for sub in range(nsub):
        norm_first = proj[0][0] if sub == 0 else None
        norm_after = proj[sub + 1][0] if sub + 1 < nsub else None
        interleave(mixer_units(sub), proj[sub][1], norm_first, norm_after)

    tile = nsub * tq
    uprev_ref[...] = carry["u_prev"]
    conv_state_ref[0] = carry["u_prev"].T[BLOCK - (CONV_WIDTH - 1):, :]
    kbuf_ref[0:BLOCK, :] = kbuf_ref[tile:tile + BLOCK, :]
    vbuf_ref[:, 0:BLOCK] = vbuf_ref[:, tile:tile + BLOCK]
    k_state_ref[0] = zt_ref[1, OFF_K:OFF_K + D_SWA_KV, tq - BLOCK:]
    v_state_ref[0] = zt_ref[1, OFF_V:OFF_V + D_SWA_KV, tq - BLOCK:]


def _prompt_layer(x, pre_g, post_g, w_in_t, conv_w_t, sinks, mkb, mvtb, w_out, *, tq, nsub):
    B, T, _ = x.shape
    tile = tq * nsub
    full = lambda shape: pl.BlockSpec(shape, lambda b, t: (0,) * len(shape))
    kernel = functools.partial(_prompt_kernel, tq=tq, nsub=nsub)
    steps = T // tile

    def next_first_sub_tile(b, t):
        nxt = jnp.minimum(b * steps + t + 1, B * steps - 1)
        return nxt // steps, (nxt % steps) * nsub, 0

    resident = pl.BlockSpec(memory_space=pltpu.VMEM)
    return pl.pallas_call(
        kernel,
        grid=(B, steps),
        in_specs=[
            pl.BlockSpec((1, tile, D_MODEL), lambda b, t: (b, t, 0)),
            pl.BlockSpec((1, tq, D_MODEL), next_first_sub_tile),
            full((1, D_MODEL)),
            full((1, D_MODEL)),
            resident,
            full((D_CONV, CONV_WIDTH)),
            pl.BlockSpec(memory_space=pltpu.SMEM),
            pl.BlockSpec((1, N_MEM, D_MEMQ), lambda b, t: (b, 0, 0)),
            pl.BlockSpec((1, D_MEMQ, N_MEM), lambda b, t: (b, 0, 0)),
            resident,
        ],
        out_specs=[
            pl.BlockSpec((1, tile, D_MODEL), lambda b, t: (b, t, 0)),
            pl.BlockSpec((1, CONV_WIDTH - 1, D_CONV), lambda b, t: (b, 0, 0)),
            pl.BlockSpec((1, D_SWA_KV, BLOCK), lambda b, t: (b, 0, 0)),
            pl.BlockSpec((1, D_SWA_KV, BLOCK), lambda b, t: (b, 0, 0)),
        ],
        out_shape=[
            jax.ShapeDtypeStruct((B, T, D_MODEL), jnp.float32),
            jax.ShapeDtypeStruct((B, CONV_WIDTH - 1, D_CONV), jnp.float32),
            jax.ShapeDtypeStruct((B, D_SWA_KV, BLOCK), jnp.float32),
            jax.ShapeDtypeStruct((B, D_SWA_KV, BLOCK), jnp.float32),
        ],
        scratch_shapes=[
            pltpu.VMEM((2, D_IN, tq), jnp.float32),
            pltpu.VMEM((2, D_MODEL, tq), jnp.bfloat16),
            pltpu.VMEM((BLOCK + tile, D_SWA_KV), jnp.bfloat16),
            pltpu.VMEM((D_SWA_KV, BLOCK + tile), jnp.bfloat16),
            pltpu.VMEM((D_CONV, BLOCK), jnp.float32),
            pltpu.VMEM((N_SWA_KV, BLOCK, SWA_GROUP * BLOCK), jnp.float32),
        ],
        compiler_params=pltpu.CompilerParams(
            dimension_semantics=("arbitrary", "arbitrary"),
            vmem_limit_bytes=PROMPT_VMEM_LIMIT_BYTES),
        name="prompt_layer",
    )(x, x, pre_g, post_g, w_in_t, conv_w_t, sinks, mkb, mvtb, w_out)


def _sample_kernel(x_ref, conv_past_ref, ckt_ref, cvt_ref, mkt_ref, mvt_ref,
                   pre_g_ref, post_g_ref, w_in_t_ref, conv_w_ref, sink_ref, w_out_ref,
                   y_ref, conv_state_ref, kt_state_ref, vt_state_ref,
                   z_ref, ycat_ref, *, ns, group):
    R = SAMPLE_ROWS
    half = R // 2
    nseq = ns * group
    step = pl.program_id(1)

    @pl.when(step == 0)
    def _():
        h = _rms_norm(x_ref[...], pre_g_ref[...]).astype(jnp.bfloat16)
        z_ref[...] = lax.dot_general(h, w_in_t_ref[...], _NT, preferred_element_type=jnp.float32)
        u = (z_ref[:, OFF_CC:OFF_CC + D_CONV] * z_ref[:, OFF_CH:OFF_CH + D_CONV]).reshape(nseq, R, D_CONV)
        row3 = lax.broadcasted_iota(jnp.int32, (nseq, R, D_CONV), 1)
        u_full = jnp.where(row3 < CONV_WIDTH - 1, conv_past_ref[...], pltpu.roll(u, CONV_WIDTH - 1, axis=1))
        cw = conv_w_ref[...]
        conv = (cw[0:1, :] * u_full
                + cw[1:2, :] * pltpu.roll(u_full, R - 1, axis=1)
                + cw[2:3, :] * pltpu.roll(u_full, R - 2, axis=1))
        conv_state_ref[...] = pltpu.roll(u_full, R - half, axis=1)[:, 0:CONV_WIDTH - 1, :]
        y_conv = (z_ref[:, OFF_CB:OFF_CB + D_CONV] * conv.reshape(nseq * R, D_CONV)
                  * _silu(z_ref[:, OFF_CZ:OFF_CZ + D_CONV]))
        ycat_ref[:, YOFF_CONV:YOFF_CONV + D_CONV] = y_conv

    row = lax.broadcasted_iota(jnp.int32, (R, PAIR), 0)
    lane = lax.broadcasted_iota(jnp.int32, (R, PAIR), 1)
    lo_row = row < half
    lo_lane = lane < HEAD_DIM
    diag = lo_row == lo_lane
    lane_sq = lax.broadcasted_iota(jnp.int32, (WINDOW, WINDOW), 1)

    def pair_bias(ncols, dist_of):
        rr = lax.broadcasted_iota(jnp.int32, (R, ncols), 0)
        cc = lax.broadcasted_iota(jnp.int32, (R, ncols), 1)
        dist, valid = dist_of(rr % half, cc)
        distf = dist.astype(jnp.float32)
        tiles = []
        for pair in range(N_SWA_HEADS // 2):
            slope = jnp.where(rr < half, _alibi_slope(2 * pair), _alibi_slope(2 * pair + 1))
            tiles.append(jnp.where(valid, -slope * distf, NEG_INF))
        return jnp.concatenate(tiles, axis=0)

    def cached_dist(tok, c):
        d = tok + WINDOW - c
        return d, d < WINDOW

    def new_dist(tok, c):
        d = tok - c
        return d, (d >= 0) & (c < half)

    bias_c = pair_bias(WINDOW, cached_dist)
    bias_n = pair_bias(SAMPLE_NEW, new_dist)
    head_of_row = lax.broadcasted_iota(jnp.int32, (N_SWA_HEADS * half, 1), 0) // half
    sink_col = jnp.full((N_SWA_HEADS * half, 1), sink_ref[0], jnp.float32)
    for hh in range(1, N_SWA_HEADS):
        sink_col = jnp.where(head_of_row == hh, sink_ref[hh], sink_col)
    state_pad = jnp.zeros((WINDOW - R, D_SWA_KV), jnp.float32)

    def seq_rows(n):
        return pl.ds(pl.multiple_of((step * ns + n) * R, R), R)

    def swa_unit(n):
        rows = seq_rows(n)
        qa = z_ref[rows, OFF_Q:OFF_Q + PAIR] * QK_SCALE
        qb = z_ref[rows, OFF_Q + PAIR:OFF_Q + 2 * PAIR] * QK_SCALE
        qc = z_ref[rows, OFF_Q + 2 * PAIR:OFF_Q + 3 * PAIR] * QK_SCALE
        t0 = jnp.where(lo_lane, jnp.where(lo_row, qa, pltpu.roll(qa, HEAD_DIM, axis=1)), 0.0)
        t1 = jnp.where(diag, qb, 0.0)
        t2 = jnp.where(lo_lane, 0.0, jnp.where(lo_row, pltpu.roll(qc, HEAD_DIM, axis=1), qc))
        qs = jnp.concatenate([t0, t1, t2], axis=0).astype(jnp.bfloat16)

        k_new = z_ref[rows, OFF_K:OFF_K + D_SWA_KV]
        v_new = z_ref[rows, OFF_V:OFF_V + D_SWA_KV]
        k_new_b = jnp.concatenate([k_new, k_new], axis=0).astype(jnp.bfloat16)
        v_new_b = jnp.concatenate([v_new, v_new], axis=0).astype(jnp.bfloat16)
        kt_old = ckt_ref[n].astype(jnp.bfloat16)

        s_c = jnp.dot(qs, kt_old, preferred_element_type=jnp.float32) + bias_c
        s_n = lax.dot_general(qs, k_new_b, _NT, preferred_element_type=jnp.float32) + bias_n
        yield
        m = jnp.maximum(jnp.maximum(jnp.max(s_c, axis=1, keepdims=True), jnp.max(s_n, axis=1, keepdims=True)),
                        sink_col)
        p_c = jnp.exp(s_c - m)
        p_n = jnp.exp(s_n - m)
        l = (jnp.sum(p_c, axis=1, keepdims=True) + jnp.sum(p_n, axis=1, keepdims=True) + jnp.exp(sink_col - m))
        p_c = p_c.astype(jnp.bfloat16)
        p_n = p_n.astype(jnp.bfloat16)
        yield
        o = (lax.dot_general(p_c, cvt_ref[n].astype(jnp.bfloat16), _NT, preferred_element_type=jnp.float32)
             + jnp.dot(p_n, v_new_b, preferred_element_type=jnp.float32)) / l
        yield
        o0, o1, o2 = o[0:R], o[R:2 * R], o[2 * R:3 * R]
        ya = jnp.where(lo_lane, o0, pltpu.roll(pltpu.roll(o0, HEAD_DIM, axis=1), half, axis=0))
        yb = jnp.where(lo_lane, o1, pltpu.roll(o1, half, axis=0))
        yc = jnp.where(lo_lane, pltpu.roll(o2, HEAD_DIM, axis=1), pltpu.roll(o2, half, axis=0))
        y_swa = jnp.concatenate([ya, yb, yc], axis=1) * _silu(z_ref[rows, OFF_SZ:OFF_SZ + D_SWA])
        ycat_ref[rows, YOFF_SWA:YOFF_SWA + D_SWA] = y_swa

    def state_unit(n):
        rows = seq_rows(n)
        k_new_t = jnp.concatenate([state_pad, z_ref[rows, OFF_K:OFF_K + D_SWA_KV]], axis=0).T
        v_new_t = jnp.concatenate([state_pad, z_ref[rows, OFF_V:OFF_V + D_SWA_KV]], axis=0).T
        keep = lane_sq < WINDOW - half
        kt_state_ref[n] = jnp.where(keep, pltpu.roll(ckt_ref[n], WINDOW - half, axis=1), k_new_t)
        vt_state_ref[n] = jnp.where(keep, pltpu.roll(cvt_ref[n], WINDOW - half, axis=1), v_new_t)

    def mem_unit(n):
        rows = seq_rows(n)
        m0 = z_ref[rows, OFF_MQ:OFF_MQ + PAIR] * QK_SCALE
        m1 = z_ref[rows, OFF_MQ + PAIR:OFF_MQ + 2 * PAIR] * QK_SCALE
        zero = jnp.zeros_like(m0)
        qm = jnp.concatenate(
            [jnp.concatenate([jnp.where(diag, m0, 0.0), zero], axis=1),
             jnp.concatenate([zero, jnp.where(diag, m1, 0.0)], axis=1)], axis=0).astype(jnp.bfloat16)
        s = jnp.dot(qm, mkt_ref[n].astype(jnp.bfloat16), preferred_element_type=jnp.float32)
        yield
        m = jnp.max(s, axis=1, keepdims=True)
        p = jnp.exp(s - m)
        l = jnp.sum(p, axis=1, keepdims=True)
        p = p.astype(jnp.bfloat16)
        yield
        o = lax.dot_general(p, mvt_ref[n].astype(jnp.bfloat16), _NT,
                            preferred_element_type=jnp.float32) / l
        yield
        oa, ob = o[0:R, 0:PAIR], o[R:2 * R, PAIR:2 * PAIR]
        y_mem = jnp.concatenate([jnp.where(lo_lane, oa, pltpu.roll(oa, half, axis=0)),
                                 jnp.where(lo_lane, ob, pltpu.roll(ob, half, axis=0))], axis=1)
        ycat_ref[rows, YOFF_MEM:YOFF_MEM + D_MEMQ] = y_mem * _silu(z_ref[rows, OFF_MZ:OFF_MZ + D_MEMQ])

    todo = [functools.partial(unit, n) for n in range(ns) for unit in (swa_unit, mem_unit)]
    states = [functools.partial(state_unit, n) for n in range(ns)]
    active = []
    while todo or active:
        for _ in range(SAMPLE_WIDTH):
            if todo:
                active.append(todo.pop(0)())
        for gen in list(active):
            if next(gen, "done") == "done":
                active.remove(gen)
        if states:
            states.pop(0)()
    for unit in states:
        unit()

    @pl.when(step == group - 1)
    def _():
        y = jnp.dot(ycat_ref[...].astype(jnp.bfloat16), w_out_ref[...], preferred_element_type=jnp.float32)
        out = x_ref[...] + _rms_norm(y, post_g_ref[...])
        y_ref[...] = out.reshape(nseq, R, D_MODEL)[:, 0:half, :]


def _sample_layer(x8, conv_past8, ckt, cvt, mkt, mvt, pre_g, post_g, w_in_t, conv_w, sinks, w_out, *, ns, group):
    N = ckt.shape[0]
    R = SAMPLE_ROWS
    nseq = ns * group
    full = lambda shape: pl.BlockSpec(shape, lambda o, i: (0,) * len(shape))
    per_group = lambda shape: pl.BlockSpec(shape, lambda o, i: (o,) + (0,) * (len(shape) - 1))
    per_step = lambda shape: pl.BlockSpec(shape, lambda o, i: (o * group + i,) + (0,) * (len(shape) - 1))
    kernel = functools.partial(_sample_kernel, ns=ns, group=group)
    return pl.pallas_call(
        kernel,
        grid=(N // nseq, group),
        in_specs=[
            per_group((nseq * R, D_MODEL)),
            per_group((nseq, R, D_CONV)),
            per_step((ns, D_SWA_KV, WINDOW)),
            per_step((ns, D_SWA_KV, WINDOW)),
            per_step((ns, D_MEMQ, N_MEM)),
            per_step((ns, D_MEMQ, N_MEM)),
            full((1, D_MODEL)),
            full((1, D_MODEL)),
            pl.BlockSpec(memory_space=pltpu.VMEM),
            full((CONV_WIDTH, D_CONV)),
            pl.BlockSpec(memory_space=pltpu.SMEM),
            pl.BlockSpec(memory_space=pltpu.VMEM),
        ],
        out_specs=[
            per_group((nseq, R // 2, D_MODEL)),
            per_group((nseq, CONV_WIDTH - 1, D_CONV)),
            per_step((ns, D_SWA_KV, WINDOW)),
            per_step((ns, D_SWA_KV, WINDOW)),
        ],
        out_shape=[
            jax.ShapeDtypeStruct((N, R // 2, D_MODEL), jnp.float32),
            jax.ShapeDtypeStruct((N, CONV_WIDTH - 1, D_CONV), jnp.float32),
            jax.ShapeDtypeStruct((N, D_SWA_KV, WINDOW), jnp.float32),
            jax.ShapeDtypeStruct((N, D_SWA_KV, WINDOW), jnp.float32),
        ],
        scratch_shapes=[
            pltpu.VMEM((nseq * R, D_IN), jnp.float32),
            pltpu.VMEM((nseq * R, D_MODEL), jnp.float32),
        ],
        compiler_params=pltpu.CompilerParams(
            dimension_semantics=("arbitrary", "arbitrary"),
            vmem_limit_bytes=SAMPLE_VMEM_LIMIT_BYTES),
        name="sample_layer",
    )(x8, conv_past8, ckt, cvt, mkt, mvt, pre_g, post_g, w_in_t, conv_w, sinks, w_out)


def _heads_last_to_keys_last(a):
    n, keys, heads, dim = a.shape
    return jnp.transpose(a, (0, 2, 3, 1)).reshape(n, heads * dim, keys)


def _keys_last_to_heads_last(a, heads):
    n, hd, keys = a.shape
    return jnp.transpose(a.reshape(n, heads, hd // heads, keys), (0, 3, 1, 2))[None]


def kernel(x_prompt, x_sample, mem_prompt, state_conv, cache_swa_k, cache_swa_v, cache_mem_k, cache_mem_v,
           pre_norm_g, post_norm_g, w_in, conv_w, attn_sinks, mem_norm_g, w_mem_k, w_mem_v, w_out):
    assert w_in.shape[0] == 1, "one layer, as the problem states"
    N, TS, _ = x_sample.shape
    assert TS == SAMPLE_ROWS // 2 and cache_swa_k.shape[2] == WINDOW
    l = 0

    pre_g = pre_norm_g[l].reshape(1, D_MODEL)
    post_g = post_norm_g[l].reshape(1, D_MODEL)
    w_in_t = w_in[l].astype(jnp.bfloat16).T
    w_out_bf = w_out[l].astype(jnp.bfloat16)
    sinks = attn_sinks[l].astype(jnp.float32)

    mkt, mvt, mkb, mvtb = _mem_kv(mem_prompt, mem_norm_g[l], w_mem_k[l], w_mem_v[l])
    y_p, conv_p, kt_p, vt_p = _prompt_layer(
        x_prompt, pre_g, post_g, w_in_t, conv_w[l].T, sinks, mkb, mvtb, w_out_bf, tq=PROMPT_TQ, nsub=PROMPT_NSUB)

    x8 = jnp.concatenate([x_sample, x_sample], axis=1).reshape(N * SAMPLE_ROWS, D_MODEL)
    conv_past8 = jnp.pad(state_conv[l], ((0, 0), (0, SAMPLE_ROWS - (CONV_WIDTH - 1)), (0, 0)))
    y_s, conv_s, kt_s, vt_s = _sample_layer(
        x8, conv_past8,
        _heads_last_to_keys_last(cache_swa_k[l]), _heads_last_to_keys_last(cache_swa_v[l]),
        _heads_last_to_keys_last(cache_mem_k[l]), _heads_last_to_keys_last(cache_mem_v[l]),
        pre_g, post_g, w_in_t, conv_w[l], sinks, w_out_bf, ns=SAMPLE_NS, group=SAMPLE_GROUP)

    return (y_p, y_s,
            conv_p[None],
            _keys_last_to_heads_last(kt_p, N_SWA_KV), _keys_last_to_heads_last(vt_p, N_SWA_KV),
            _keys_last_to_heads_last(mkt, N_MEM_HEADS), _keys_last_to_heads_last(mvt, N_MEM_HEADS),
            conv_s[None],
            _keys_last_to_heads_last(kt_s, N_SWA_KV), _keys_last_to_heads_last(vt_s, N_SWA_KV))
```

```python
import functools

import numpy as np
import jax
import jax.numpy as jnp
from jax import lax
from jax.experimental import pallas as pl
from jax.experimental.pallas import tpu as pltpu

D_MODEL = 1024
HEAD_DIM = 64
D_CONV = 384
N_MEM_HEADS = 4
D_MEMQ = N_MEM_HEADS * HEAD_DIM
D_SWA = 384
N_SWA_HEADS = 6
N_SWA_KV = 2
SWA_GROUP = N_SWA_HEADS // N_SWA_KV
D_SWA_KV = N_SWA_KV * HEAD_DIM
N_MEM = 256
CONV_WIDTH = 3
WINDOW = 128
BLOCK = 128
RMS_EPS = 1e-6
NEG_INF = -1e30
D_IN = 3072
QK_SCALE = HEAD_DIM ** -0.5
LOG2_E = float(np.log2(np.e))
QK_SCALE_LOG2 = QK_SCALE * LOG2_E

OFF_CB, OFF_CC, OFF_CH, OFF_CZ = 0, 384, 768, 1152
OFF_Q, OFF_K, OFF_V, OFF_SZ = 1536, 1920, 2048, 2176
OFF_MQ, OFF_MZ = 2560, 2816
YOFF_CONV, YOFF_SWA, YOFF_MEM = 0, 384, 768

PAIR = 2 * HEAD_DIM

MIB = 1024 * 1024
PROMPT_VMEM_LIMIT_BYTES = 44 * MIB
SAMPLE_VMEM_LIMIT_BYTES = 48 * MIB

MEM_KV_BATCHES = 2
PROMPT_TQ = 512
PROMPT_NSUB = 2
IN_PROJ_CHUNK = 512
ATTN_WIDTH = 2
SAMPLE_NS = 16
SAMPLE_GROUP = 2
SAMPLE_ROWS = 8
SAMPLE_WIDTH = 16
SAMPLE_NEW = 16


def _alibi_slope(h):
    return float(np.power(np.float32(2.0), np.float32(-8.0 * (h + 1) / N_SWA_HEADS)))


def _rms_norm(x, g):
    return x * lax.rsqrt(jnp.mean(x * x, axis=-1, keepdims=True) + RMS_EPS) * g


def _silu(x):
    return x * jax.nn.sigmoid(x)


_NT = (((1,), (1,)), ((), ()))
_TN = (((0,), (0,)), ((), ()))


def _mem_kv_kernel(mem_ref, g_ref, wk_ref, wv_ref, mkt_ref, mvt_ref, mkb_ref, mvtb_ref, *, nb):
    mem = mem_ref[...].reshape(nb * N_MEM, D_MODEL)
    m = _rms_norm(mem, g_ref[...]).astype(jnp.bfloat16)
    mk = jnp.dot(m, wk_ref[...].astype(jnp.bfloat16), preferred_element_type=jnp.float32)
    mv = jnp.dot(m, wv_ref[...].astype(jnp.bfloat16), preferred_element_type=jnp.float32)
    for b in range(nb):
        mk_b = mk[b * N_MEM:(b + 1) * N_MEM]
        mv_t = mv[b * N_MEM:(b + 1) * N_MEM].T
        mkt_ref[b] = mk_b.T
        mvt_ref[b] = mv_t
        mkb_ref[b] = mk_b.astype(jnp.bfloat16)
        mvtb_ref[b] = mv_t.astype(jnp.bfloat16)


def _mem_kv(mem, mem_g, w_mk, w_mv):
    B = mem.shape[0]
    nb = MEM_KV_BATCHES
    full = lambda shape: pl.BlockSpec(shape, lambda b: (0,) * len(shape))
    per_batch = pl.BlockSpec((nb, N_MEM, D_MEMQ), lambda b: (b, 0, 0))
    return pl.pallas_call(
        functools.partial(_mem_kv_kernel, nb=nb),
        grid=(B // nb,),
        in_specs=[
            pl.BlockSpec((nb, N_MEM, D_MODEL), lambda b: (b, 0, 0)),
            full((1, D_MODEL)),
            full((D_MODEL, D_MEMQ)),
            full((D_MODEL, D_MEMQ)),
        ],
        out_specs=[per_batch] * 4,
        out_shape=[
            jax.ShapeDtypeStruct((B, D_MEMQ, N_MEM), jnp.float32),
            jax.ShapeDtypeStruct((B, D_MEMQ, N_MEM), jnp.float32),
            jax.ShapeDtypeStruct((B, N_MEM, D_MEMQ), jnp.bfloat16),
            jax.ShapeDtypeStruct((B, D_MEMQ, N_MEM), jnp.bfloat16),
        ],
        compiler_params=pltpu.CompilerParams(dimension_semantics=("arbitrary",)),
        name="mem_kv",
    )(mem, mem_g.reshape(1, D_MODEL), w_mk, w_mv)


def _prompt_kernel(x_ref, x_next_ref, pre_g_ref, post_g_ref, w_in_t_ref, conv_w_t_ref, sink_ref, mkb_ref, mvt_ref,
                   w_out_ref,
                   y_ref, conv_state_ref, k_state_ref, v_state_ref,
                   zt_ref, ycat_ref, kbuf_ref, vbuf_ref, uprev_ref, bias_ref, *, tq, nsub):
    assert nsub % 2 == 0
    t = pl.program_id(1)
    nblk = tq // BLOCK

    @pl.when((pl.program_id(0) == 0) & (t == 0))
    def _():
        c = lax.broadcasted_iota(jnp.int32, (BLOCK, BLOCK), 0)
        r = lax.broadcasted_iota(jnp.int32, (BLOCK, BLOCK), 1)
        distf = (r - c + jnp.where(c > r, BLOCK, 0)).astype(jnp.float32)
        for h in range(N_SWA_HEADS):
            g, i = divmod(h, SWA_GROUP)
            bias_ref[g, :, i * BLOCK:(i + 1) * BLOCK] = (-_alibi_slope(h) * LOG2_E) * distf

    @pl.when(t == 0)
    def _():
        kbuf_ref[0:BLOCK, :] = jnp.zeros((BLOCK, D_SWA_KV), jnp.bfloat16)
        vbuf_ref[:, 0:BLOCK] = jnp.zeros((D_SWA_KV, BLOCK), jnp.bfloat16)
        uprev_ref[...] = jnp.zeros_like(uprev_ref)

    first_pen = jnp.where(t == 0, NEG_INF, 0.0)
    slot = lax.broadcasted_iota(jnp.int32, (BLOCK, SWA_GROUP * BLOCK), 0)
    query = lax.broadcasted_iota(jnp.int32, (BLOCK, SWA_GROUP * BLOCK), 1) % BLOCK
    from_prev = slot > query
    q_zero = jnp.zeros((HEAD_DIM, SWA_GROUP * BLOCK), jnp.bfloat16)
    cw = conv_w_t_ref[...]
    carry = {"u_prev": uprev_ref[...]}
    head_of_lane = lax.broadcasted_iota(jnp.int32, (1, SWA_GROUP * BLOCK), 1) // BLOCK
    sink_rows = []
    for g in range(N_SWA_KV):
        row = jnp.full((1, SWA_GROUP * BLOCK), sink_ref[g * SWA_GROUP], jnp.float32)
        for i in range(1, SWA_GROUP):
            row = jnp.where(head_of_lane == i, sink_ref[g * SWA_GROUP + i], row)
        sink_rows.append(row * LOG2_E)

    def in_proj_chunks(x_rows, zt):
        state = {}

        def norm():
            state["h"] = _rms_norm(x_rows(), pre_g_ref[...]).astype(jnp.bfloat16)

        def chunk(lo):
            rows = slice(lo, lo + IN_PROJ_CHUNK)
            zt[rows, :] = lax.dot_general(w_in_t_ref[rows, :], state["h"], _NT,
                                          preferred_element_type=jnp.float32)

        return norm, [functools.partial(chunk, lo) for lo in range(0, D_IN, IN_PROJ_CHUNK)]

    def mixer_units(sub):
        zt = zt_ref.at[sub % 2]
        ycat = ycat_ref.at[sub % 2]
        tok0 = sub * tq

        def conv_unit():
            u = zt[OFF_CC:OFF_CC + D_CONV, :] * zt[OFF_CH:OFF_CH + D_CONV, :]
            ucat = jnp.concatenate([carry["u_prev"], u], axis=1)
            conv = (cw[:, 0:1] * pltpu.roll(ucat, 2, axis=1)[:, BLOCK:]
                    + cw[:, 1:2] * pltpu.roll(ucat, 1, axis=1)[:, BLOCK:]
                    + cw[:, 2:3] * u)
            y_conv = zt[OFF_CB:OFF_CB + D_CONV, :] * conv * _silu(zt[OFF_CZ:OFF_CZ + D_CONV, :])
            ycat[YOFF_CONV:YOFF_CONV + D_CONV, :] = y_conv.astype(jnp.bfloat16)
            carry["u_prev"] = u[:, tq - BLOCK:]

        def kv_unit():
            k_nat = zt[OFF_K:OFF_K + D_SWA_KV, :].T
            kbuf_ref[BLOCK + tok0:BLOCK + tok0 + tq, :] = k_nat.astype(jnp.bfloat16)
            vbuf_ref[:, BLOCK + tok0:BLOCK + tok0 + tq] = zt[OFF_V:OFF_V + D_SWA_KV, :].astype(jnp.bfloat16)

        def swa_unit(j, g):
            cols = slice(j * BLOCK, (j + 1) * BLOCK)
            band = slice(tok0 + j * BLOCK, tok0 + (j + 2) * BLOCK)
            q0 = OFF_Q + g * SWA_GROUP * HEAD_DIM
            qt = jnp.concatenate(
                [zt[q0 + i * HEAD_DIM:q0 + (i + 1) * HEAD_DIM, cols] for i in range(SWA_GROUP)],
                axis=1)
            qt = (qt * QK_SCALE_LOG2).astype(jnp.bfloat16)
            qt = jnp.concatenate([qt, q_zero] if g == 0 else [q_zero, qt], axis=0)
            s = jnp.dot(kbuf_ref[band, :], qt, preferred_element_type=jnp.float32)
            yield
            s = jnp.where(from_prev, s[0:BLOCK], s[BLOCK:]) + bias_ref[g]
            if sub == 0 and j == 0:
                s = s + jnp.where(from_prev, first_pen, 0.0)
            sink = sink_rows[g]
            m = jnp.maximum(jnp.max(s, axis=0, keepdims=True), sink)
            p = jnp.exp2(s - m)
            l = jnp.sum(p, axis=0, keepdims=True) + jnp.exp2(sink - m)
            p = jnp.concatenate([jnp.where(from_prev, p, 0.0), jnp.where(from_prev, 0.0, p)],
                                axis=0).astype(jnp.bfloat16)
            yield
            vband = vbuf_ref[g * HEAD_DIM:(g + 1) * HEAD_DIM, band]
            o = jnp.dot(vband, p, preferred_element_type=jnp.float32)
            o = o / l
            for i in range(SWA_GROUP):
                hh = g * SWA_GROUP + i
                gate = _silu(zt[OFF_SZ + hh * HEAD_DIM:OFF_SZ + (hh + 1) * HEAD_DIM, cols])
                ycat[YOFF_SWA + hh * HEAD_DIM:YOFF_SWA + (hh + 1) * HEAD_DIM, cols] = (
                    o[:, i * BLOCK:(i + 1) * BLOCK] * gate).astype(jnp.bfloat16)

        def mem_unit(hh):
            rows = slice(OFF_MQ + hh * HEAD_DIM, OFF_MQ + (hh + 1) * HEAD_DIM)
            qt = (zt[rows, :] * QK_SCALE_LOG2).astype(jnp.bfloat16)
            pieces = [jnp.zeros((HEAD_DIM, tq), jnp.bfloat16)] * N_MEM_HEADS
            pieces[hh] = qt
            s = jnp.dot(mkb_ref[0], jnp.concatenate(pieces, axis=0),
                        preferred_element_type=jnp.float32)
            yield
            m = jnp.max(s, axis=0, keepdims=True)
            p = jnp.exp2(s - m)
            l = jnp.sum(p, axis=0, keepdims=True)
            p = p.astype(jnp.bfloat16)
            yield
            o = jnp.dot(mvt_ref[0, hh * HEAD_DIM:(hh + 1) * HEAD_DIM, :], p,
                        preferred_element_type=jnp.float32)
            gate = _silu(zt[OFF_MZ + hh * HEAD_DIM:OFF_MZ + (hh + 1) * HEAD_DIM, :])
            ycat[YOFF_MEM + hh * HEAD_DIM:YOFF_MEM + (hh + 1) * HEAD_DIM, :] = (
                o / l * gate).astype(jnp.bfloat16)

        def out_proj():
            carry["y"] = lax.dot_general(ycat[...], w_out_ref[...], _TN,
                                         preferred_element_type=jnp.float32)

        def post_norm(blk):
            rows = slice(blk * BLOCK, (blk + 1) * BLOCK)
            out_rows = slice(tok0 + blk * BLOCK, tok0 + (blk + 1) * BLOCK)
            y_ref[0, out_rows, :] = x_ref[0, out_rows, :] + _rms_norm(carry["y"][rows], post_g_ref[...])

        attention = [functools.partial(swa_unit, j, g) for j in range(nblk) for g in range(N_SWA_KV)]
        attention += [functools.partial(mem_unit, hh) for hh in range(N_MEM_HEADS)]
        return [conv_unit, kv_unit], attention, (out_proj, [functools.partial(post_norm, b) for b in range(nblk)])

    def interleave(units, chunks, norm_first, norm_after):
        (conv_unit, kv_unit), attention, (out_proj, post_norms) = units
        pending = list(chunks)
        kv_unit()
        n_rounds = -(-len(attention) // ATTN_WIDTH) + 2
        chunk_rounds = [k * n_rounds // len(pending) for k in range(len(pending))]
        todo = list(attention)
        active = []
        for rnd in range(n_rounds):
            for _ in range(ATTN_WIDTH):
                if todo:
                    active.append(todo.pop(0)())
            for gen in list(active):
                if next(gen, "done") == "done":
                    active.remove(gen)
            if rnd == 0 and norm_first is not None:
                norm_first()
            for _ in range(chunk_rounds.count(rnd)):
                pending.pop(0)()
            if rnd == 0:
                conv_unit()
        assert not todo and not active and not pending
        if norm_after is not None:
            norm_after()
        out_proj()
        for post_norm in post_norms:
            post_norm()

    @pl.when((pl.program_id(0) == 0) & (t == 0))
    def _():
        norm, chunks = in_proj_chunks(lambda: x_ref[0, 0:tq, :], zt_ref.at[0])
        norm()
        for chunk in chunks:
            chunk()

    proj = [in_proj_chunks(lambda sub=sub: x_ref[0, sub * tq:(sub + 1) * tq, :], zt_ref.at[sub % 2])
            for sub in range(1, nsub)]
    proj.append(in_proj_chunks(lambda: x_next_ref[0], zt_ref.at[0]))
    for sub in range(nsub):
        norm_first = proj[0][0] if sub == 0 else None
        norm_after = proj[sub + 1][0] if sub + 1 < nsub else None
        interleave(mixer_units(sub), proj[sub][1], norm_first, norm_after)

    tile = nsub * tq
    uprev_ref[...] = carry["u_prev"]
    conv_state_ref[0] = carry["u_prev"].T[BLOCK - (CONV_WIDTH - 1):, :]
    kbuf_ref[0:BLOCK, :] = kbuf_ref[tile:tile + BLOCK, :]
    vbuf_ref[:, 0:BLOCK] = vbuf_ref[:, tile:tile + BLOCK]
    k_state_ref[0] = zt_ref[1, OFF_K:OFF_K + D_SWA_KV, tq - BLOCK:]
    v_state_ref[0] = zt_ref[1, OFF_V:OFF_V + D_SWA_KV, tq - BLOCK:]


def _prompt_layer(x, pre_g, post_g, w_in_t, conv_w_t, sinks, mkb, mvtb, w_out, *, tq, nsub):
    B, T, _ = x.shape
    tile = tq * nsub
    full = lambda shape: pl.BlockSpec(shape, lambda b, t: (0,) * len(shape))
    kernel = functools.partial(_prompt_kernel, tq=tq, nsub=nsub)
    steps = T // tile

    def next_first_sub_tile(b, t):
        nxt = jnp.minimum(b * steps + t + 1, B * steps - 1)
        return nxt // steps, (nxt % steps) * nsub, 0

    resident = pl.BlockSpec(memory_space=pltpu.VMEM)
    return pl.pallas_call(
        kernel,
        grid=(B, steps),
        in_specs=[
            pl.BlockSpec((1, tile, D_MODEL), lambda b, t: (b, t, 0)),
            pl.BlockSpec((1, tq, D_MODEL), next_first_sub_tile),
            full((1, D_MODEL)),
            full((1, D_MODEL)),
            resident,
            full((D_CONV, CONV_WIDTH)),
            pl.BlockSpec(memory_space=pltpu.SMEM),
            pl.BlockSpec((1, N_MEM, D_MEMQ), lambda b, t: (b, 0, 0)),
            pl.BlockSpec((1, D_MEMQ, N_MEM), lambda b, t: (b, 0, 0)),
            resident,
        ],
        out_specs=[
            pl.BlockSpec((1, tile, D_MODEL), lambda b, t: (b, t, 0)),
            pl.BlockSpec((1, CONV_WIDTH - 1, D_CONV), lambda b, t: (b, 0, 0)),
            pl.BlockSpec((1, D_SWA_KV, BLOCK), lambda b, t: (b, 0, 0)),
            pl.BlockSpec((1, D_SWA_KV, BLOCK), lambda b, t: (b, 0, 0)),
        ],
        out_shape=[
            jax.ShapeDtypeStruct((B, T, D_MODEL), jnp.float32),
            jax.ShapeDtypeStruct((B, CONV_WIDTH - 1, D_CONV), jnp.float32),
            jax.ShapeDtypeStruct((B, D_SWA_KV, BLOCK), jnp.float32),
            jax.ShapeDtypeStruct((B, D_SWA_KV, BLOCK), jnp.float32),
        ],
        scratch_shapes=[
            pltpu.VMEM((2, D_IN, tq), jnp.float32),
            pltpu.VMEM((2, D_MODEL, tq), jnp.bfloat16),
            pltpu.VMEM((BLOCK + tile, D_SWA_KV), jnp.bfloat16),
            pltpu.VMEM((D_SWA_KV, BLOCK + tile), jnp.bfloat16),
            pltpu.VMEM((D_CONV, BLOCK), jnp.float32),
            pltpu.VMEM((N_SWA_KV, BLOCK, SWA_GROUP * BLOCK), jnp.float32),
        ],
        compiler_params=pltpu.CompilerParams(
            dimension_semantics=("arbitrary", "arbitrary"),
            vmem_limit_bytes=PROMPT_VMEM_LIMIT_BYTES),
        name="prompt_layer",
    )(x, x, pre_g, post_g, w_in_t, conv_w_t, sinks, mkb, mvtb, w_out)


def _sample_kernel(x_ref, conv_past_ref, ckt_ref, cvt_ref, mkt_ref, mvt_ref,
                   pre_g_ref, post_g_ref, w_in_t_ref, conv_w_ref, sink_ref, w_out_ref,
                   y_ref, conv_state_ref, kt_state_ref, vt_state_ref,
                   z_ref, ycat_ref, *, ns, group):
    R = SAMPLE_ROWS
    half = R // 2
    nseq = ns * group
    step = pl.program_id(1)

    @pl.when(step == 0)
    def _():
        x4 = x_ref[...]
        x8 = jnp.concatenate([x4, x4], axis=1).reshape(nseq * R, D_MODEL)
        h = _rms_norm(x8, pre_g_ref[...]).astype(jnp.bfloat16)
        z_ref[...] = lax.dot_general(h, w_in_t_ref[...], _NT, preferred_element_type=jnp.float32)
        u = (z_ref[:, OFF_CC:OFF_CC + D_CONV] * z_ref[:, OFF_CH:OFF_CH + D_CONV]).reshape(nseq, R, D_CONV)
        row3 = lax.broadcasted_iota(jnp.int32, (nseq, R, D_CONV), 1)
        past = conv_past_ref[...]
        past = jnp.concatenate([past, jnp.zeros((nseq, R - (CONV_WIDTH - 1), D_CONV), past.dtype)], axis=1)
        u_full = jnp.where(row3 < CONV_WIDTH - 1, past, pltpu.roll(u, CONV_WIDTH - 1, axis=1))
        cw = conv_w_ref[...]
        conv = (cw[0:1, :] * u_full
                + cw[1:2, :] * pltpu.roll(u_full, R - 1, axis=1)
                + cw[2:3, :] * pltpu.roll(u_full, R - 2, axis=1))
        conv_state_ref[...] = pltpu.roll(u_full, R - half, axis=1)[:, 0:CONV_WIDTH - 1, :]
        y_conv = (z_ref[:, OFF_CB:OFF_CB + D_CONV] * conv.reshape(nseq * R, D_CONV)
                  * _silu(z_ref[:, OFF_CZ:OFF_CZ + D_CONV]))
        ycat_ref[:, YOFF_CONV:YOFF_CONV + D_CONV] = y_conv

    row = lax.broadcasted_iota(jnp.int32, (R, PAIR), 0)
    lane = lax.broadcasted_iota(jnp.int32, (R, PAIR), 1)
    lo_row = row < half
    lo_lane = lane < HEAD_DIM
    diag = lo_row == lo_lane
    lane_sq = lax.broadcasted_iota(jnp.int32, (WINDOW, WINDOW), 1)

    def pair_bias(ncols, dist_of):
        rr = lax.broadcasted_iota(jnp.int32, (R, ncols), 0)
        cc = lax.broadcasted_iota(jnp.int32, (R, ncols), 1)
        dist, valid = dist_of(rr % half, cc)
        distf = dist.astype(jnp.float32)
        tiles = []
        for pair in range(N_SWA_HEADS // 2):
            slope = jnp.where(rr < half, _alibi_slope(2 * pair), _alibi_slope(2 * pair + 1))
            tiles.append(jnp.where(valid, -slope * distf, NEG_INF))
        return jnp.concatenate(tiles, axis=0)

    def cached_dist(tok, c):
        d = tok + WINDOW - c
        return d, d < WINDOW

    def new_dist(tok, c):
        d = tok - c
        return d, (d >= 0) & (c < half)

    bias_c = pair_bias(WINDOW, cached_dist)
    bias_n = pair_bias(SAMPLE_NEW, new_dist)
    head_of_row = lax.broadcasted_iota(jnp.int32, (N_SWA_HEADS * half, 1), 0) // half
    sink_col = jnp.full((N_SWA_HEADS * half, 1), sink_ref[0], jnp.float32)
    for hh in range(1, N_SWA_HEADS):
        sink_col = jnp.where(head_of_row == hh, sink_ref[hh], sink_col)
    state_pad = jnp.zeros((WINDOW - R, D_SWA_KV), jnp.float32)

    def seq_rows(n):
        return pl.ds(pl.multiple_of((step * ns + n) * R, R), R)

    def swa_unit(n):
        rows = seq_rows(n)
        qa = z_ref[rows, OFF_Q:OFF_Q + PAIR] * QK_SCALE
        qb = z_ref[rows, OFF_Q + PAIR:OFF_Q + 2 * PAIR] * QK_SCALE
        qc = z_ref[rows, OFF_Q + 2 * PAIR:OFF_Q + 3 * PAIR] * QK_SCALE
        t0 = jnp.where(lo_lane, jnp.where(lo_row, qa, pltpu.roll(qa, HEAD_DIM, axis=1)), 0.0)
        t1 = jnp.where(diag, qb, 0.0)
        t2 = jnp.where(lo_lane, 0.0, jnp.where(lo_row, pltpu.roll(qc, HEAD_DIM, axis=1), qc))
        qs = jnp.concatenate([t0, t1, t2], axis=0).astype(jnp.bfloat16)

        k_new = z_ref[rows, OFF_K:OFF_K + D_SWA_KV]
        v_new = z_ref[rows, OFF_V:OFF_V + D_SWA_KV]
        k_new_b = jnp.concatenate([k_new, k_new], axis=0).astype(jnp.bfloat16)
        v_new_b = jnp.concatenate([v_new, v_new], axis=0).astype(jnp.bfloat16)
        kt_old = ckt_ref[n].astype(jnp.bfloat16)

        s_c = jnp.dot(qs, kt_old, preferred_element_type=jnp.float32) + bias_c
        s_n = lax.dot_general(qs, k_new_b, _NT, preferred_element_type=jnp.float32) + bias_n
        yield
        m = jnp.maximum(jnp.maximum(jnp.max(s_c, axis=1, keepdims=True), jnp.max(s_n, axis=1, keepdims=True)),
                        sink_col)
        p_c = jnp.exp(s_c - m)
        p_n = jnp.exp(s_n - m)
        l = (jnp.sum(p_c, axis=1, keepdims=True) + jnp.sum(p_n, axis=1, keepdims=True) + jnp.exp(sink_col - m))
        p_c = p_c.astype(jnp.bfloat16)
        p_n = p_n.astype(jnp.bfloat16)
        yield
        o = (lax.dot_general(p_c, cvt_ref[n].astype(jnp.bfloat16), _NT, preferred_element_type=jnp.float32)
             + jnp.dot(p_n, v_new_b, preferred_element_type=jnp.float32)) / l
        yield
        o0, o1, o2 = o[0:R], o[R:2 * R], o[2 * R:3 * R]
        ya = jnp.where(lo_lane, o0, pltpu.roll(pltpu.roll(o0, HEAD_DIM, axis=1), half, axis=0))
        yb = jnp.where(lo_lane, o1, pltpu.roll(o1, half, axis=0))
        yc = jnp.where(lo_lane, pltpu.roll(o2, HEAD_DIM, axis=1), pltpu.roll(o2, half, axis=0))
        y_swa = jnp.concatenate([ya, yb, yc], axis=1) * _silu(z_ref[rows, OFF_SZ:OFF_SZ + D_SWA])
        ycat_ref[rows, YOFF_SWA:YOFF_SWA + D_SWA] = y_swa

    def state_unit(n):
        rows = seq_rows(n)
        k_new_t = jnp.concatenate([state_pad, z_ref[rows, OFF_K:OFF_K + D_SWA_KV]], axis=0).T
        v_new_t = jnp.concatenate([state_pad, z_ref[rows, OFF_V:OFF_V + D_SWA_KV]], axis=0).T
        keep = lane_sq < WINDOW - half
        kt_state_ref[n] = jnp.where(keep, pltpu.roll(ckt_ref[n], WINDOW - half, axis=1), k_new_t)
        vt_state_ref[n] = jnp.where(keep, pltpu.roll(cvt_ref[n], WINDOW - half, axis=1), v_new_t)

    def mem_unit(n):
        rows = seq_rows(n)
        m0 = z_ref[rows, OFF_MQ:OFF_MQ + PAIR] * QK_SCALE
        m1 = z_ref[rows, OFF_MQ + PAIR:OFF_MQ + 2 * PAIR] * QK_SCALE
        zero = jnp.zeros_like(m0)
        qm = jnp.concatenate(
            [jnp.concatenate([jnp.where(diag, m0, 0.0), zero], axis=1),
             jnp.concatenate([zero, jnp.where(diag, m1, 0.0)], axis=1)], axis=0).astype(jnp.bfloat16)
        s = jnp.dot(qm, mkt_ref[n].astype(jnp.bfloat16), preferred_element_type=jnp.float32)
        yield
        m = jnp.max(s, axis=1, keepdims=True)
        p = jnp.exp(s - m)
        l = jnp.sum(p, axis=1, keepdims=True)
        p = p.astype(jnp.bfloat16)
        yield
        o = lax.dot_general(p, mvt_ref[n].astype(jnp.bfloat16), _NT,
                            preferred_element_type=jnp.float32) / l
        yield
        oa, ob = o[0:R, 0:PAIR], o[R:2 * R, PAIR:2 * PAIR]
        y_mem = jnp.concatenate([jnp.where(lo_lane, oa, pltpu.roll(oa, half, axis=0)),
                                 jnp.where(lo_lane, ob, pltpu.roll(ob, half, axis=0))], axis=1)
        ycat_ref[rows, YOFF_MEM:YOFF_MEM + D_MEMQ] = y_mem * _silu(z_ref[rows, OFF_MZ:OFF_MZ + D_MEMQ])

    todo = [functools.partial(unit, n) for n in range(ns) for unit in (swa_unit, mem_unit)]
    states = [functools.partial(state_unit, n) for n in range(ns)]
    active = []
    while todo or active:
        for _ in range(SAMPLE_WIDTH):
            if todo:
                active.append(todo.pop(0)())
        for gen in list(active):
            if next(gen, "done") == "done":
                active.remove(gen)
        if states:
            states.pop(0)()
    for unit in states:
        unit()

    @pl.when(step == group - 1)
    def _():
        y = jnp.dot(ycat_ref[...].astype(jnp.bfloat16), w_out_ref[...], preferred_element_type=jnp.float32)
        y_ref[...] = x_ref[...] + _rms_norm(y, post_g_ref[...]).reshape(nseq, R, D_MODEL)[:, 0:half, :]


def _sample_layer(x, conv_past, ckt, cvt, mkt, mvt, pre_g, post_g, w_in_t, conv_w, sinks, w_out, *, ns, group):
    N = ckt.shape[0]
    R = SAMPLE_ROWS
    nseq = ns * group
    full = lambda shape: pl.BlockSpec(shape, lambda o, i: (0,) * len(shape))
    per_group = lambda shape: pl.BlockSpec(shape, lambda o, i: (o,) + (0,) * (len(shape) - 1))
    per_step = lambda shape: pl.BlockSpec(shape, lambda o, i: (o * group + i,) + (0,) * (len(shape) - 1))
    kernel = functools.partial(_sample_kernel, ns=ns, group=group)
    return pl.pallas_call(
        kernel,
        grid=(N // nseq, group),
        in_specs=[
            per_group((nseq, R // 2, D_MODEL)),
            per_group((nseq, CONV_WIDTH - 1, D_CONV)),
            per_step((ns, D_SWA_KV, WINDOW)),
            per_step((ns, D_SWA_KV, WINDOW)),
            per_step((ns, D_MEMQ, N_MEM)),
            per_step((ns, D_MEMQ, N_MEM)),
            full((1, D_MODEL)),
            full((1, D_MODEL)),
            pl.BlockSpec(memory_space=pltpu.VMEM),
            full((CONV_WIDTH, D_CONV)),
            pl.BlockSpec(memory_space=pltpu.SMEM),
            pl.BlockSpec(memory_space=pltpu.VMEM),
        ],
        out_specs=[
            per_group((nseq, R // 2, D_MODEL)),
            per_group((nseq, CONV_WIDTH - 1, D_CONV)),
            per_step((ns, D_SWA_KV, WINDOW)),
            per_step((ns, D_SWA_KV, WINDOW)),
        ],
        out_shape=[
            jax.ShapeDtypeStruct((N, R // 2, D_MODEL), jnp.float32),
            jax.ShapeDtypeStruct((N, CONV_WIDTH - 1, D_CONV), jnp.float32),
            jax.ShapeDtypeStruct((N, D_SWA_KV, WINDOW), jnp.float32),
            jax.ShapeDtypeStruct((N, D_SWA_KV, WINDOW), jnp.float32),
        ],
        scratch_shapes=[
            pltpu.VMEM((nseq * R, D_IN), jnp.float32),
            pltpu.VMEM((nseq * R, D_MODEL), jnp.float32),
        ],
        compiler_params=pltpu.CompilerParams(
            dimension_semantics=("arbitrary", "arbitrary"),
            vmem_limit_bytes=SAMPLE_VMEM_LIMIT_BYTES),
        name="sample_layer",
    )(x, conv_past, ckt, cvt, mkt, mvt, pre_g, post_g, w_in_t, conv_w, sinks, w_out)


def _heads_last_to_keys_last(a):
    n, keys, heads, dim = a.shape
    return jnp.transpose(a, (0, 2, 3, 1)).reshape(n, heads * dim, keys)


def _keys_last_to_heads_last(a, heads):
    n, hd, keys = a.shape
    return jnp.transpose(a.reshape(n, heads, hd // heads, keys), (0, 3, 1, 2))[None]


def kernel(x_prompt, x_sample, mem_prompt, state_conv, cache_swa_k, cache_swa_v, cache_mem_k, cache_mem_v,
           pre_norm_g, post_norm_g, w_in, conv_w, attn_sinks, mem_norm_g, w_mem_k, w_mem_v, w_out):
    assert w_in.shape[0] == 1, "one layer, as the problem states"
    N, TS, _ = x_sample.shape
    assert TS == SAMPLE_ROWS // 2 and cache_swa_k.shape[2] == WINDOW
    l = 0

    pre_g = pre_norm_g[l].reshape(1, D_MODEL)
    post_g = post_norm_g[l].reshape(1, D_MODEL)
    w_in_t = w_in[l].astype(jnp.bfloat16).T
    w_out_bf = w_out[l].astype(jnp.bfloat16)
    sinks = attn_sinks[l].astype(jnp.float32)

    mkt, mvt, mkb, mvtb = _mem_kv(mem_prompt, mem_norm_g[l], w_mem_k[l], w_mem_v[l])
    y_p, conv_p, kt_p, vt_p = _prompt_layer(
        x_prompt, pre_g, post_g, w_in_t, conv_w[l].T, sinks, mkb, mvtb, w_out_bf, tq=PROMPT_TQ, nsub=PROMPT_NSUB)

    y_s, conv_s, kt_s, vt_s = _sample_layer(
        x_sample, state_conv[l],
        _heads_last_to_keys_last(cache_swa_k[l]), _heads_last_to_keys_last(cache_swa_v[l]),
        _heads_last_to_keys_last(cache_mem_k[l]), _heads_last_to_keys_last(cache_mem_v[l]),
        pre_g, post_g, w_in_t, conv_w[l], sinks, w_out_bf, ns=SAMPLE_NS, group=SAMPLE_GROUP)

    return (y_p, y_s,
            conv_p[None],
            _keys_last_to_heads_last(kt_p, N_SWA_KV), _keys_last_to_heads_last(vt_p, N_SWA_KV),
            _keys_last_to_heads_last(mkt, N_MEM_HEADS), _keys_last_to_heads_last(mvt, N_MEM_HEADS),
            conv_s[None],
            _keys_last_to_heads_last(kt_s, N_SWA_KV), _keys_last_to_heads_last(vt_s, N_SWA_KV))
```

```python
import functools

import numpy as np
import jax
import jax.numpy as jnp
from jax import lax
from jax.experimental import pallas as pl
from jax.experimental.pallas import tpu as pltpu

D_MODEL = 1024
HEAD_DIM = 64
D_CONV = 384
N_MEM_HEADS = 4
D_MEMQ = N_MEM_HEADS * HEAD_DIM
D_SWA = 384
N_SWA_HEADS = 6
N_SWA_KV = 2
SWA_GROUP = N_SWA_HEADS // N_SWA_KV
D_SWA_KV = N_SWA_KV * HEAD_DIM
N_MEM = 256
CONV_WIDTH = 3
WINDOW = 128
BLOCK = 128
RMS_EPS = 1e-6
NEG_INF = -1e30
D_IN = 3072
QK_SCALE = HEAD_DIM ** -0.5
LOG2_E = float(np.log2(np.e))
QK_SCALE_LOG2 = QK_SCALE * LOG2_E

OFF_CB, OFF_CC, OFF_CH, OFF_CZ = 0, 384, 768, 1152
OFF_Q, OFF_K, OFF_V, OFF_SZ = 1536, 1920, 2048, 2176
OFF_MQ, OFF_MZ = 2560, 2816
YOFF_CONV, YOFF_SWA, YOFF_MEM = 0, 384, 768

PAIR = 2 * HEAD_DIM

MIB = 1024 * 1024
PROMPT_VMEM_LIMIT_BYTES = 44 * MIB
SAMPLE_VMEM_LIMIT_BYTES = 48 * MIB

MEM_KV_BATCHES = 2
PROMPT_TQ = 512
PROMPT_NSUB = 2
IN_PROJ_CHUNK = 512
ATTN_WIDTH = 2
SAMPLE_NS = 16
SAMPLE_GROUP = 2
SAMPLE_ROWS = 8
SAMPLE_WIDTH = 16
SAMPLE_NEW = 16


def _alibi_slope(h):
    return float(np.power(np.float32(2.0), np.float32(-8.0 * (h + 1) / N_SWA_HEADS)))


def _rms_norm(x, g):
    return x * lax.rsqrt(jnp.mean(x * x, axis=-1, keepdims=True) + RMS_EPS) * g


def _silu(x):
    return x * jax.nn.sigmoid(x)


_NT = (((1,), (1,)), ((), ()))
_TN = (((0,), (0,)), ((), ()))


def _mem_kv_kernel(mem_ref, g_ref, wk_ref, wv_ref, mkt_ref, mvt_ref, mkb_ref, mvtb_ref, *, nb):
    mem = mem_ref[...].reshape(nb * N_MEM, D_MODEL)
    m = _rms_norm(mem, g_ref[...]).astype(jnp.bfloat16)
    mk = jnp.dot(m, wk_ref[...].astype(jnp.bfloat16), preferred_element_type=jnp.float32)
    mv = jnp.dot(m, wv_ref[...].astype(jnp.bfloat16), preferred_element_type=jnp.float32)
    for b in range(nb):
        mk_b = mk[b * N_MEM:(b + 1) * N_MEM]
        mv_t = mv[b * N_MEM:(b + 1) * N_MEM].T
        mkt_ref[b] = mk_b.T
        mvt_ref[b] = mv_t
        mkb_ref[b] = mk_b.astype(jnp.bfloat16)
        mvtb_ref[b] = mv_t.astype(jnp.bfloat16)


def _mem_kv(mem, mem_g, w_mk, w_mv):
    B = mem.shape[0]
    nb = MEM_KV_BATCHES
    full = lambda shape: pl.BlockSpec(shape, lambda b: (0,) * len(shape))
    per_batch = pl.BlockSpec((nb, N_MEM, D_MEMQ), lambda b: (b, 0, 0))
    return pl.pallas_call(
        functools.partial(_mem_kv_kernel, nb=nb),
        grid=(B // nb,),
        in_specs=[
            pl.BlockSpec((nb, N_MEM, D_MODEL), lambda b: (b, 0, 0)),
            full((1, D_MODEL)),
            full((D_MODEL, D_MEMQ)),
            full((D_MODEL, D_MEMQ)),
        ],
        out_specs=[per_batch] * 4,
        out_shape=[
            jax.ShapeDtypeStruct((B, D_MEMQ, N_MEM), jnp.float32),
            jax.ShapeDtypeStruct((B, D_MEMQ, N_MEM), jnp.float32),
            jax.ShapeDtypeStruct((B, N_MEM, D_MEMQ), jnp.bfloat16),
            jax.ShapeDtypeStruct((B, D_MEMQ, N_MEM), jnp.bfloat16),
        ],
        compiler_params=pltpu.CompilerParams(dimension_semantics=("arbitrary",)),
        name="mem_kv",
    )(mem, mem_g.reshape(1, D_MODEL), w_mk, w_mv)


def _prompt_kernel(x_ref, x_next_ref, pre_g_ref, post_g_ref, w_in_t_ref, conv_w_t_ref, sink_ref, mkb_ref, mvt_ref,
                   w_out_ref, run_after_ref,
                   y_ref, conv_state_ref, k_state_ref, v_state_ref,
                   zt_ref, ycat_ref, kbuf_ref, vbuf_ref, uprev_ref, bias_ref, *, tq, nsub):
    assert nsub % 2 == 0
    t = pl.program_id(1)
    nblk = tq // BLOCK

    @pl.when((pl.program_id(0) == 0) & (t == 0))
    def _():
        c = lax.broadcasted_iota(jnp.int32, (BLOCK, BLOCK), 0)
        r = lax.broadcasted_iota(jnp.int32, (BLOCK, BLOCK), 1)
        distf = (r - c + jnp.where(c > r, BLOCK, 0)).astype(jnp.float32)
        for h in range(N_SWA_HEADS):
            g, i = divmod(h, SWA_GROUP)
            bias_ref[g, :, i * BLOCK:(i + 1) * BLOCK] = (-_alibi_slope(h) * LOG2_E) * distf

    @pl.when(t == 0)
    def _():
        kbuf_ref[0:BLOCK, :] = jnp.zeros((BLOCK, D_SWA_KV), jnp.bfloat16)
        vbuf_ref[:, 0:BLOCK] = jnp.zeros((D_SWA_KV, BLOCK), jnp.bfloat16)
        uprev_ref[...] = jnp.zeros_like(uprev_ref)

    first_pen = jnp.where(t == 0, NEG_INF, 0.0)
    slot = lax.broadcasted_iota(jnp.int32, (BLOCK, SWA_GROUP * BLOCK), 0)
    query = lax.broadcasted_iota(jnp.int32, (BLOCK, SWA_GROUP * BLOCK), 1) % BLOCK
    from_prev = slot > query
    q_zero = jnp.zeros((HEAD_DIM, SWA_GROUP * BLOCK), jnp.bfloat16)
    cw = conv_w_t_ref[...]
    carry = {"u_prev": uprev_ref[...]}
    head_of_lane = lax.broadcasted_iota(jnp.int32, (1, SWA_GROUP * BLOCK), 1) // BLOCK
    sink_rows = []
    for g in range(N_SWA_KV):
        row = jnp.full((1, SWA_GROUP * BLOCK), sink_ref[g * SWA_GROUP], jnp.float32)
        for i in range(1, SWA_GROUP):
            row = jnp.where(head_of_lane == i, sink_ref[g * SWA_GROUP + i], row)
        sink_rows.append(row * LOG2_E)

    def in_proj_chunks(x_rows, zt):
        state = {}

        def norm():
            state["h"] = _rms_norm(x_rows(), pre_g_ref[...]).astype(jnp.bfloat16)

        def chunk(lo):
            rows = slice(lo, lo + IN_PROJ_CHUNK)
            zt[rows, :] = lax.dot_general(w_in_t_ref[rows, :], state["h"], _NT,
                                          preferred_element_type=jnp.float32)

        return norm, [functools.partial(chunk, lo) for lo in range(0, D_IN, IN_PROJ_CHUNK)]

    def mixer_units(sub):
        zt = zt_ref.at[sub % 2]
        ycat = ycat_ref.at[sub % 2]
        tok0 = sub * tq

        def conv_unit():
            u = zt[OFF_CC:OFF_CC + D_CONV, :] * zt[OFF_CH:OFF_CH + D_CONV, :]
            ucat = jnp.concatenate([carry["u_prev"], u], axis=1)
            conv = (cw[:, 0:1] * pltpu.roll(ucat, 2, axis=1)[:, BLOCK:]
                    + cw[:, 1:2] * pltpu.roll(ucat, 1, axis=1)[:, BLOCK:]
                    + cw[:, 2:3] * u)
            y_conv = zt[OFF_CB:OFF_CB + D_CONV, :] * conv * _silu(zt[OFF_CZ:OFF_CZ + D_CONV, :])
            ycat[YOFF_CONV:YOFF_CONV + D_CONV, :] = y_conv.astype(jnp.bfloat16)
            carry["u_prev"] = u[:, tq - BLOCK:]

        def kv_unit():
            k_nat = zt[OFF_K:OFF_K + D_SWA_KV, :].T
            kbuf_ref[BLOCK + tok0:BLOCK + tok0 + tq, :] = k_nat.astype(jnp.bfloat16)
            vbuf_ref[:, BLOCK + tok0:BLOCK + tok0 + tq] = zt[OFF_V:OFF_V + D_SWA_KV, :].astype(jnp.bfloat16)

        def swa_unit(j, g):
            cols = slice(j * BLOCK, (j + 1) * BLOCK)
            band = slice(tok0 + j * BLOCK, tok0 + (j + 2) * BLOCK)
            q0 = OFF_Q + g * SWA_GROUP * HEAD_DIM
            qt = jnp.concatenate(
                [zt[q0 + i * HEAD_DIM:q0 + (i + 1) * HEAD_DIM, cols] for i in range(SWA_GROUP)],
                axis=1)
            qt = (qt * QK_SCALE_LOG2).astype(jnp.bfloat16)
            qt = jnp.concatenate([qt, q_zero] if g == 0 else [q_zero, qt], axis=0)
            s = jnp.dot(kbuf_ref[band, :], qt, preferred_element_type=jnp.float32)
            yield
            s = jnp.where(from_prev, s[0:BLOCK], s[BLOCK:]) + bias_ref[g]
            if sub == 0 and j == 0:
                s = s + jnp.where(from_prev, first_pen, 0.0)
            sink = sink_rows[g]
            m = jnp.maximum(jnp.max(s, axis=0, keepdims=True), sink)
            p = jnp.exp2(s - m)
            l = jnp.sum(p, axis=0, keepdims=True) + jnp.exp2(sink - m)
            p = jnp.concatenate([jnp.where(from_prev, p, 0.0), jnp.where(from_prev, 0.0, p)],
                                axis=0).astype(jnp.bfloat16)
            yield
            vband = vbuf_ref[g * HEAD_DIM:(g + 1) * HEAD_DIM, band]
            o = jnp.dot(vband, p, preferred_element_type=jnp.float32)
            o = o / l
            for i in range(SWA_GROUP):
                hh = g * SWA_GROUP + i
                gate = _silu(zt[OFF_SZ + hh * HEAD_DIM:OFF_SZ + (hh + 1) * HEAD_DIM, cols])
                ycat[YOFF_SWA + hh * HEAD_DIM:YOFF_SWA + (hh + 1) * HEAD_DIM, cols] = (
                    o[:, i * BLOCK:(i + 1) * BLOCK] * gate).astype(jnp.bfloat16)

        def mem_unit(hh):
            rows = slice(OFF_MQ + hh * HEAD_DIM, OFF_MQ + (hh + 1) * HEAD_DIM)
            qt = (zt[rows, :] * QK_SCALE_LOG2).astype(jnp.bfloat16)
            pieces = [jnp.zeros((HEAD_DIM, tq), jnp.bfloat16)] * N_MEM_HEADS
            pieces[hh] = qt
            s = jnp.dot(mkb_ref[0], jnp.concatenate(pieces, axis=0),
                        preferred_element_type=jnp.float32)
            yield
            m = jnp.max(s, axis=0, keepdims=True)
            p = jnp.exp2(s - m)
            l = jnp.sum(p, axis=0, keepdims=True)
            p = p.astype(jnp.bfloat16)
            yield
            o = jnp.dot(mvt_ref[0, hh * HEAD_DIM:(hh + 1) * HEAD_DIM, :], p,
                        preferred_element_type=jnp.float32)
            gate = _silu(zt[OFF_MZ + hh * HEAD_DIM:OFF_MZ + (hh + 1) * HEAD_DIM, :])
            ycat[YOFF_MEM + hh * HEAD_DIM:YOFF_MEM + (hh + 1) * HEAD_DIM, :] = (
                o / l * gate).astype(jnp.bfloat16)

        def out_proj():
            carry["y"] = lax.dot_general(ycat[...], w_out_ref[...], _TN,
                                         preferred_element_type=jnp.float32)

        def post_norm(blk):
            rows = slice(blk * BLOCK, (blk + 1) * BLOCK)
            out_rows = slice(tok0 + blk * BLOCK, tok0 + (blk + 1) * BLOCK)
            y_ref[0, out_rows, :] = x_ref[0, out_rows, :] + _rms_norm(carry["y"][rows], post_g_ref[...])

        attention = [functools.partial(swa_unit, j, g) for j in range(nblk) for g in range(N_SWA_KV)]
        attention += [functools.partial(mem_unit, hh) for hh in range(N_MEM_HEADS)]
        return [conv_unit, kv_unit], attention, (out_proj, [functools.partial(post_norm, b) for b in range(nblk)])

    def interleave(units, chunks, norm_first, norm_after):
        (conv_unit, kv_unit), attention, (out_proj, post_norms) = units
        pending = list(chunks)
        kv_unit()
        n_rounds = -(-len(attention) // ATTN_WIDTH) + 2
        chunk_rounds = [k * n_rounds // len(pending) for k in range(len(pending))]
        todo = list(attention)
        active = []
        for rnd in range(n_rounds):
            for _ in range(ATTN_WIDTH):
                if todo:
                    active.append(todo.pop(0)())
            for gen in list(active):
                if next(gen, "done") == "done":
                    active.remove(gen)
            if rnd == 0 and norm_first is not None:
                norm_first()
            for _ in range(chunk_rounds.count(rnd)):
                pending.pop(0)()
            if rnd == 0:
                conv_unit()
        assert not todo and not active and not pending
        if norm_after is not None:
            norm_after()
        out_proj()
        for post_norm in post_norms:
            post_norm()

    @pl.when((pl.program_id(0) == 0) & (t == 0))
    def _():
        norm, chunks = in_proj_chunks(lambda: x_ref[0, 0:tq, :], zt_ref.at[0])
        norm()
        for chunk in chunks:
            chunk()

    proj = [in_proj_chunks(lambda sub=sub: x_ref[0, sub * tq:(sub + 1) * tq, :], zt_ref.at[sub % 2])
            for sub in range(1, nsub)]
    proj.append(in_proj_chunks(lambda: x_next_ref[0], zt_ref.at[0]))
    for sub in range(nsub):
        norm_first = proj[0][0] if sub == 0 else None
        norm_after = proj[sub + 1][0] if sub + 1 < nsub else None
        interleave(mixer_units(sub), proj[sub][1], norm_first, norm_after)

    tile = nsub * tq
    uprev_ref[...] = carry["u_prev"]
    conv_state_ref[0] = carry["u_prev"].T[BLOCK - (CONV_WIDTH - 1):, :]
    kbuf_ref[0:BLOCK, :] = kbuf_ref[tile:tile + BLOCK, :]
    vbuf_ref[:, 0:BLOCK] = vbuf_ref[:, tile:tile + BLOCK]
    k_state_ref[0] = zt_ref[1, OFF_K:OFF_K + D_SWA_KV, tq - BLOCK:]
    v_state_ref[0] = zt_ref[1, OFF_V:OFF_V + D_SWA_KV, tq - BLOCK:]


def _prompt_layer(x, pre_g, post_g, w_in_t, conv_w_t, sinks, mkb, mvtb, w_out, run_after, *, tq, nsub):
    B, T, _ = x.shape
    tile = tq * nsub
    full = lambda shape: pl.BlockSpec(shape, lambda b, t: (0,) * len(shape))
    kernel = functools.partial(_prompt_kernel, tq=tq, nsub=nsub)
    steps = T // tile

    def next_first_sub_tile(b, t):
        nxt = jnp.minimum(b * steps + t + 1, B * steps - 1)
        return nxt // steps, (nxt % steps) * nsub, 0

    resident = pl.BlockSpec(memory_space=pltpu.VMEM)
    return pl.pallas_call(
        kernel,
        grid=(B, steps),
        in_specs=[
            pl.BlockSpec((1, tile, D_MODEL), lambda b, t: (b, t, 0)),
            pl.BlockSpec((1, tq, D_MODEL), next_first_sub_tile),
            full((1, D_MODEL)),
            full((1, D_MODEL)),
            resident,
            full((D_CONV, CONV_WIDTH)),
            pl.BlockSpec(memory_space=pltpu.SMEM),
            pl.BlockSpec((1, N_MEM, D_MEMQ), lambda b, t: (b, 0, 0)),
            pl.BlockSpec((1, D_MEMQ, N_MEM), lambda b, t: (b, 0, 0)),
            resident,
            pl.BlockSpec(memory_space=pl.ANY),
        ],
        out_specs=[
            pl.BlockSpec((1, tile, D_MODEL), lambda b, t: (b, t, 0)),
            pl.BlockSpec((1, CONV_WIDTH - 1, D_CONV), lambda b, t: (b, 0, 0)),
            pl.BlockSpec((1, D_SWA_KV, BLOCK), lambda b, t: (b, 0, 0)),
            pl.BlockSpec((1, D_SWA_KV, BLOCK), lambda b, t: (b, 0, 0)),
        ],
        out_shape=[
            jax.ShapeDtypeStruct((B, T, D_MODEL), jnp.float32),
            jax.ShapeDtypeStruct((B, CONV_WIDTH - 1, D_CONV), jnp.float32),
            jax.ShapeDtypeStruct((B, D_SWA_KV, BLOCK), jnp.float32),
            jax.ShapeDtypeStruct((B, D_SWA_KV, BLOCK), jnp.float32),
        ],
        scratch_shapes=[
            pltpu.VMEM((2, D_IN, tq), jnp.float32),
            pltpu.VMEM((2, D_MODEL, tq), jnp.bfloat16),
            pltpu.VMEM((BLOCK + tile, D_SWA_KV), jnp.bfloat16),
            pltpu.VMEM((D_SWA_KV, BLOCK + tile), jnp.bfloat16),
            pltpu.VMEM((D_CONV, BLOCK), jnp.float32),
            pltpu.VMEM((N_SWA_KV, BLOCK, SWA_GROUP * BLOCK), jnp.float32),
        ],
        compiler_params=pltpu.CompilerParams(
            dimension_semantics=("arbitrary", "arbitrary"),
            vmem_limit_bytes=PROMPT_VMEM_LIMIT_BYTES),
        name="prompt_layer",
    )(x, x, pre_g, post_g, w_in_t, conv_w_t, sinks, mkb, mvtb, w_out, run_after)


def _sample_kernel(x_ref, conv_past_ref, ckt_ref, cvt_ref, mkt_ref, mvt_ref,
                   pre_g_ref, post_g_ref, w_in_t_ref, conv_w_ref, sink_ref, w_out_ref,
                   y_ref, conv_state_ref, kt_state_ref, vt_state_ref,
                   z_ref, ycat_ref, *, ns, group):
    R = SAMPLE_ROWS
    half = R // 2
    nseq = ns * group
    step = pl.program_id(1)

    @pl.when(step == 0)
    def _():
        x4 = x_ref[...]
        x8 = jnp.concatenate([x4, x4], axis=1).reshape(nseq * R, D_MODEL)
        h = _rms_norm(x8, pre_g_ref[...]).astype(jnp.bfloat16)
        z_ref[...] = lax.dot_general(h, w_in_t_ref[...], _NT, preferred_element_type=jnp.float32)
        u = (z_ref[:, OFF_CC:OFF_CC + D_CONV] * z_ref[:, OFF_CH:OFF_CH + D_CONV]).reshape(nseq, R, D_CONV)
        row3 = lax.broadcasted_iota(jnp.int32, (nseq, R, D_CONV), 1)
        past = conv_past_ref[...]
        past = jnp.concatenate([past, jnp.zeros((nseq, R - (CONV_WIDTH - 1), D_CONV), past.dtype)], axis=1)
        u_full = jnp.where(row3 < CONV_WIDTH - 1, past, pltpu.roll(u, CONV_WIDTH - 1, axis=1))
        cw = conv_w_ref[...]
        conv = (cw[0:1, :] * u_full
                + cw[1:2, :] * pltpu.roll(u_full, R - 1, axis=1)
                + cw[2:3, :] * pltpu.roll(u_full, R - 2, axis=1))
        conv_state_ref[...] = pltpu.roll(u_full, R - half, axis=1)[:, 0:CONV_WIDTH - 1, :]
        y_conv = (z_ref[:, OFF_CB:OFF_CB + D_CONV] * conv.reshape(nseq * R, D_CONV)
                  * _silu(z_ref[:, OFF_CZ:OFF_CZ + D_CONV]))
        ycat_ref[:, YOFF_CONV:YOFF_CONV + D_CONV] = y_conv

    row = lax.broadcasted_iota(jnp.int32, (R, PAIR), 0)
    lane = lax.broadcasted_iota(jnp.int32, (R, PAIR), 1)
    lo_row = row < half
    lo_lane = lane < HEAD_DIM
    diag = lo_row == lo_lane
    lane_sq = lax.broadcasted_iota(jnp.int32, (WINDOW, WINDOW), 1)

    def pair_bias(ncols, dist_of):
        rr = lax.broadcasted_iota(jnp.int32, (R, ncols), 0)
        cc = lax.broadcasted_iota(jnp.int32, (R, ncols), 1)
        dist, valid = dist_of(rr % half, cc)
        distf = dist.astype(jnp.float32)
        tiles = []
        for pair in range(N_SWA_HEADS // 2):
            slope = jnp.where(rr < half, _alibi_slope(2 * pair), _alibi_slope(2 * pair + 1))
            tiles.append(jnp.where(valid, -slope * distf, NEG_INF))
        return jnp.concatenate(tiles, axis=0)

    def cached_dist(tok, c):
        d = tok + WINDOW - c
        return d, d < WINDOW

    def new_dist(tok, c):
        d = tok - c
        return d, (d >= 0) & (c < half)

    bias_c = pair_bias(WINDOW, cached_dist)
    bias_n = pair_bias(SAMPLE_NEW, new_dist)
    head_of_row = lax.broadcasted_iota(jnp.int32, (N_SWA_HEADS * half, 1), 0) // half
    sink_col = jnp.full((N_SWA_HEADS * half, 1), sink_ref[0], jnp.float32)
    for hh in range(1, N_SWA_HEADS):
        sink_col = jnp.where(head_of_row == hh, sink_ref[hh], sink_col)
    state_pad = jnp.zeros((WINDOW - R, D_SWA_KV), jnp.float32)

    def seq_rows(n):
        return pl.ds(pl.multiple_of((step * ns + n) * R, R), R)

    def swa_unit(n):
        rows = seq_rows(n)
        qa = z_ref[rows, OFF_Q:OFF_Q + PAIR] * QK_SCALE
        qb = z_ref[rows, OFF_Q + PAIR:OFF_Q + 2 * PAIR] * QK_SCALE
        qc = z_ref[rows, OFF_Q + 2 * PAIR:OFF_Q + 3 * PAIR] * QK_SCALE
        t0 = jnp.where(lo_lane, jnp.where(lo_row, qa, pltpu.roll(qa, HEAD_DIM, axis=1)), 0.0)
        t1 = jnp.where(diag, qb, 0.0)
        t2 = jnp.where(lo_lane, 0.0, jnp.where(lo_row, pltpu.roll(qc, HEAD_DIM, axis=1), qc))
        qs = jnp.concatenate([t0, t1, t2], axis=0).astype(jnp.bfloat16)

        k_new = z_ref[rows, OFF_K:OFF_K + D_SWA_KV]
        v_new = z_ref[rows, OFF_V:OFF_V + D_SWA_KV]
        k_new_b = jnp.concatenate([k_new, k_new], axis=0).astype(jnp.bfloat16)
        v_new_b = jnp.concatenate([v_new, v_new], axis=0).astype(jnp.bfloat16)
        kt_old = ckt_ref[n].astype(jnp.bfloat16)

        s_c = jnp.dot(qs, kt_old, preferred_element_type=jnp.float32) + bias_c
        s_n = lax.dot_general(qs, k_new_b, _NT, preferred_element_type=jnp.float32) + bias_n
        yield
        m = jnp.maximum(jnp.maximum(jnp.max(s_c, axis=1, keepdims=True), jnp.max(s_n, axis=1, keepdims=True)),
                        sink_col)
        p_c = jnp.exp(s_c - m)
        p_n = jnp.exp(s_n - m)
        l = (jnp.sum(p_c, axis=1, keepdims=True) + jnp.sum(p_n, axis=1, keepdims=True) + jnp.exp(sink_col - m))
        p_c = p_c.astype(jnp.bfloat16)
        p_n = p_n.astype(jnp.bfloat16)
        yield
        o = (lax.dot_general(p_c, cvt_ref[n].astype(jnp.bfloat16), _NT, preferred_element_type=jnp.float32)
             + jnp.dot(p_n, v_new_b, preferred_element_type=jnp.float32)) / l
        yield
        o0, o1, o2 = o[0:R], o[R:2 * R], o[2 * R:3 * R]
        ya = jnp.where(lo_lane, o0, pltpu.roll(pltpu.roll(o0, HEAD_DIM, axis=1), half, axis=0))
        yb = jnp.where(lo_lane, o1, pltpu.roll(o1, half, axis=0))
        yc = jnp.where(lo_lane, pltpu.roll(o2, HEAD_DIM, axis=1), pltpu.roll(o2, half, axis=0))
        y_swa = jnp.concatenate([ya, yb, yc], axis=1) * _silu(z_ref[rows, OFF_SZ:OFF_SZ + D_SWA])
        ycat_ref[rows, YOFF_SWA:YOFF_SWA + D_SWA] = y_swa

    def state_unit(n):
        rows = seq_rows(n)
        k_new_t = jnp.concatenate([state_pad, z_ref[rows, OFF_K:OFF_K + D_SWA_KV]], axis=0).T
        v_new_t = jnp.concatenate([state_pad, z_ref[rows, OFF_V:OFF_V + D_SWA_KV]], axis=0).T
        keep = lane_sq < WINDOW - half
        kt_state_ref[n] = jnp.where(keep, pltpu.roll(ckt_ref[n], WINDOW - half, axis=1), k_new_t)
        vt_state_ref[n] = jnp.where(keep, pltpu.roll(cvt_ref[n], WINDOW - half, axis=1), v_new_t)

    def mem_unit(n):
        rows = seq_rows(n)
        m0 = z_ref[rows, OFF_MQ:OFF_MQ + PAIR] * QK_SCALE
        m1 = z_ref[rows, OFF_MQ + PAIR:OFF_MQ + 2 * PAIR] * QK_SCALE
        zero = jnp.zeros_like(m0)
        qm = jnp.concatenate(
            [jnp.concatenate([jnp.where(diag, m0, 0.0), zero], axis=1),
             jnp.concatenate([zero, jnp.where(diag, m1, 0.0)], axis=1)], axis=0).astype(jnp.bfloat16)
        s = jnp.dot(qm, mkt_ref[n].astype(jnp.bfloat16), preferred_element_type=jnp.float32)
        yield
        m = jnp.max(s, axis=1, keepdims=True)
        p = jnp.exp(s - m)
        l = jnp.sum(p, axis=1, keepdims=True)
        p = p.astype(jnp.bfloat16)
        yield
        o = lax.dot_general(p, mvt_ref[n].astype(jnp.bfloat16), _NT,
                            preferred_element_type=jnp.float32) / l
        yield
        oa, ob = o[0:R, 0:PAIR], o[R:2 * R, PAIR:2 * PAIR]
        y_mem = jnp.concatenate([jnp.where(lo_lane, oa, pltpu.roll(oa, half, axis=0)),
                                 jnp.where(lo_lane, ob, pltpu.roll(ob, half, axis=0))], axis=1)
        ycat_ref[rows, YOFF_MEM:YOFF_MEM + D_MEMQ] = y_mem * _silu(z_ref[rows, OFF_MZ:OFF_MZ + D_MEMQ])

    todo = [functools.partial(unit, n) for n in range(ns) for unit in (swa_unit, mem_unit)]
    states = [functools.partial(state_unit, n) for n in range(ns)]
    active = []
    while todo or active:
        for _ in range(SAMPLE_WIDTH):
            if todo:
                active.append(todo.pop(0)())
        for gen in list(active):
            if next(gen, "done") == "done":
                active.remove(gen)
        if states:
            states.pop(0)()
    for unit in states:
        unit()

    @pl.when(step == group - 1)
    def _():
        y = jnp.dot(ycat_ref[...].astype(jnp.bfloat16), w_out_ref[...], preferred_element_type=jnp.float32)
        y_ref[...] = x_ref[...] + _rms_norm(y, post_g_ref[...]).reshape(nseq, R, D_MODEL)[:, 0:half, :]


def _sample_layer(x, conv_past, ckt, cvt, mkt, mvt, pre_g, post_g, w_in_t, conv_w, sinks, w_out, *, ns, group):
    N = ckt.shape[0]
    R = SAMPLE_ROWS
    nseq = ns * group
    full = lambda shape: pl.BlockSpec(shape, lambda o, i: (0,) * len(shape))
    per_group = lambda shape: pl.BlockSpec(shape, lambda o, i: (o,) + (0,) * (len(shape) - 1))
    per_step = lambda shape: pl.BlockSpec(shape, lambda o, i: (o * group + i,) + (0,) * (len(shape) - 1))
    kernel = functools.partial(_sample_kernel, ns=ns, group=group)
    return pl.pallas_call(
        kernel,
        grid=(N // nseq, group),
        in_specs=[
            per_group((nseq, R // 2, D_MODEL)),
            per_group((nseq, CONV_WIDTH - 1, D_CONV)),
            per_step((ns, D_SWA_KV, WINDOW)),
            per_step((ns, D_SWA_KV, WINDOW)),
            per_step((ns, D_MEMQ, N_MEM)),
            per_step((ns, D_MEMQ, N_MEM)),
            full((1, D_MODEL)),
            full((1, D_MODEL)),
            pl.BlockSpec(memory_space=pltpu.VMEM),
            full((CONV_WIDTH, D_CONV)),
            pl.BlockSpec(memory_space=pltpu.SMEM),
            pl.BlockSpec(memory_space=pltpu.VMEM),
        ],
        out_specs=[
            per_group((nseq, R // 2, D_MODEL)),
            per_group((nseq, CONV_WIDTH - 1, D_CONV)),
            per_step((ns, D_SWA_KV, WINDOW)),
            per_step((ns, D_SWA_KV, WINDOW)),
        ],
        out_shape=[
            jax.ShapeDtypeStruct((N, R // 2, D_MODEL), jnp.float32),
            jax.ShapeDtypeStruct((N, CONV_WIDTH - 1, D_CONV), jnp.float32),
            jax.ShapeDtypeStruct((N, D_SWA_KV, WINDOW), jnp.float32),
            jax.ShapeDtypeStruct((N, D_SWA_KV, WINDOW), jnp.float32),
        ],
        scratch_shapes=[
            pltpu.VMEM((nseq * R, D_IN), jnp.float32),
            pltpu.VMEM((nseq * R, D_MODEL), jnp.float32),
        ],
        compiler_params=pltpu.CompilerParams(
            dimension_semantics=("arbitrary", "arbitrary"),
            vmem_limit_bytes=SAMPLE_VMEM_LIMIT_BYTES),
        name="sample_layer",
    )(x, conv_past, ckt, cvt, mkt, mvt, pre_g, post_g, w_in_t, conv_w, sinks, w_out)


def _heads_last_to_keys_last(a):
    n, keys, heads, dim = a.shape
    return jnp.transpose(a, (0, 2, 3, 1)).reshape(n, heads * dim, keys)


def _keys_last_to_heads_last(a, heads):
    n, hd, keys = a.shape
    return jnp.transpose(a.reshape(n, heads, hd // heads, keys), (0, 3, 1, 2))[None]


def kernel(x_prompt, x_sample, mem_prompt, state_conv, cache_swa_k, cache_swa_v, cache_mem_k, cache_mem_v,
           pre_norm_g, post_norm_g, w_in, conv_w, attn_sinks, mem_norm_g, w_mem_k, w_mem_v, w_out):
    assert w_in.shape[0] == 1, "one layer, as the problem states"
    N, TS, _ = x_sample.shape
    assert TS == SAMPLE_ROWS // 2 and cache_swa_k.shape[2] == WINDOW
    l = 0

    pre_g = pre_norm_g[l].reshape(1, D_MODEL)
    post_g = post_norm_g[l].reshape(1, D_MODEL)
    w_in_t = w_in[l].astype(jnp.bfloat16).T
    w_out_bf = w_out[l].astype(jnp.bfloat16)
    sinks = attn_sinks[l].astype(jnp.float32)

    y_s, conv_s, kt_s, vt_s = _sample_layer(
        x_sample, state_conv[l],
        _heads_last_to_keys_last(cache_swa_k[l]), _heads_last_to_keys_last(cache_swa_v[l]),
        _heads_last_to_keys_last(cache_mem_k[l]), _heads_last_to_keys_last(cache_mem_v[l]),
        pre_g, post_g, w_in_t, conv_w[l], sinks, w_out_bf, ns=SAMPLE_NS, group=SAMPLE_GROUP)

    mkt, mvt, mkb, mvtb = _mem_kv(mem_prompt, mem_norm_g[l], w_mem_k[l], w_mem_v[l])
    y_p, conv_p, kt_p, vt_p = _prompt_layer(
        x_prompt, pre_g, post_g, w_in_t, conv_w[l].T, sinks, mkb, mvtb, w_out_bf, conv_s,
        tq=PROMPT_TQ, nsub=PROMPT_NSUB)

    return (y_p, y_s,
            conv_p[None],
            _keys_last_to_heads_last(kt_p, N_SWA_KV), _keys_last_to_heads_last(vt_p, N_SWA_KV),
            _keys_last_to_heads_last(mkt, N_MEM_HEADS), _keys_last_to_heads_last(mvt, N_MEM_HEADS),
            conv_s[None],
            _keys_last_to_heads_last(kt_s, N_SWA_KV), _keys_last_to_heads_last(vt_s, N_SWA_KV))
```

```python
import functools

import numpy as np
import jax
import jax.numpy as jnp
from jax import lax
from jax.experimental import pallas as pl
from jax.experimental.pallas import tpu as pltpu

D_MODEL = 1024
HEAD_DIM = 64
D_CONV = 384
N_MEM_HEADS = 4
D_MEMQ = N_MEM_HEADS * HEAD_DIM
D_SWA = 384
N_SWA_HEADS = 6
N_SWA_KV = 2
SWA_GROUP = N_SWA_HEADS // N_SWA_KV
D_SWA_KV = N_SWA_KV * HEAD_DIM
N_MEM = 256
CONV_WIDTH = 3
WINDOW = 128
BLOCK = 128
RMS_EPS = 1e-6
NEG_INF = -1e30
D_IN = 3072
QK_SCALE = HEAD_DIM ** -0.5
LOG2_E = float(np.log2(np.e))
QK_SCALE_LOG2 = QK_SCALE * LOG2_E

OFF_CB, OFF_CC, OFF_CH, OFF_CZ = 0, 384, 768, 1152
OFF_Q, OFF_K, OFF_V, OFF_SZ = 1536, 1920, 2048, 2176
OFF_MQ, OFF_MZ = 2560, 2816
YOFF_CONV, YOFF_SWA, YOFF_MEM = 0, 384, 768

PAIR = 2 * HEAD_DIM

MIB = 1024 * 1024
PROMPT_VMEM_LIMIT_BYTES = 44 * MIB
SAMPLE_VMEM_LIMIT_BYTES = 48 * MIB

MEM_KV_BATCHES = 2
PROMPT_TQ = 512
PROMPT_NSUB = 2
IN_PROJ_CHUNK = 512
ATTN_WIDTH = 2
SAMPLE_NS = 16
SAMPLE_GROUP = 2
SAMPLE_ROWS = 8
SAMPLE_WIDTH = 16
SAMPLE_NEW = 16


def _alibi_slope(h):
    return float(np.power(np.float32(2.0), np.float32(-8.0 * (h + 1) / N_SWA_HEADS)))


def _rms_norm(x, g):
    return x * lax.rsqrt(jnp.mean(x * x, axis=-1, keepdims=True) + RMS_EPS) * g


def _silu(x):
    return x * jax.nn.sigmoid(x)


_NT = (((1,), (1,)), ((), ()))
_TN = (((0,), (0,)), ((), ()))


def _transpose_cast_kernel(w_ref, wt_ref):
    wt_ref[...] = w_ref[...].T.astype(jnp.bfloat16)


def _transposed_bf16(w, cols):
    K, N = w.shape
    return pl.pallas_call(
        _transpose_cast_kernel,
        grid=(N // cols,),
        in_specs=[pl.BlockSpec((K, cols), lambda i: (0, i))],
        out_specs=pl.BlockSpec((cols, K), lambda i: (i, 0)),
        out_shape=jax.ShapeDtypeStruct((N, K), jnp.bfloat16),
        compiler_params=pltpu.CompilerParams(dimension_semantics=("arbitrary",)),
        name="transpose_cast",
    )(w)


def _mem_kv_kernel(mem_ref, g_ref, wk_ref, wv_ref, mkt_ref, mvt_ref, mkb_ref, mvtb_ref, *, nb):
    mem = mem_ref[...].reshape(nb * N_MEM, D_MODEL)
    m = _rms_norm(mem, g_ref[...]).astype(jnp.bfloat16)
    mk = jnp.dot(m, wk_ref[...].astype(jnp.bfloat16), preferred_element_type=jnp.float32)
    mv = jnp.dot(m, wv_ref[...].astype(jnp.bfloat16), preferred_element_type=jnp.float32)
    for b in range(nb):
        mk_b = mk[b * N_MEM:(b + 1) * N_MEM]
        mv_t = mv[b * N_MEM:(b + 1) * N_MEM].T
        mkt_ref[b] = mk_b.T
        mvt_ref[b] = mv_t
        mkb_ref[b] = mk_b.astype(jnp.bfloat16)
        mvtb_ref[b] = mv_t.astype(jnp.bfloat16)


def _mem_kv(mem, mem_g, w_mk, w_mv):
    B = mem.shape[0]
    nb = MEM_KV_BATCHES
    full = lambda shape: pl.BlockSpec(shape, lambda b: (0,) * len(shape))
    per_batch = pl.BlockSpec((nb, N_MEM, D_MEMQ), lambda b: (b, 0, 0))
    return pl.pallas_call(
        functools.partial(_mem_kv_kernel, nb=nb),
        grid=(B // nb,),
        in_specs=[
            pl.BlockSpec((nb, N_MEM, D_MODEL), lambda b: (b, 0, 0)),
            full((1, D_MODEL)),
            full((D_MODEL, D_MEMQ)),
            full((D_MODEL, D_MEMQ)),
        ],
        out_specs=[per_batch] * 4,
        out_shape=[
            jax.ShapeDtypeStruct((B, D_MEMQ, N_MEM), jnp.float32),
            jax.ShapeDtypeStruct((B, D_MEMQ, N_MEM), jnp.float32),
            jax.ShapeDtypeStruct((B, N_MEM, D_MEMQ), jnp.bfloat16),
            jax.ShapeDtypeStruct((B, D_MEMQ, N_MEM), jnp.bfloat16),
        ],
        compiler_params=pltpu.CompilerParams(dimension_semantics=("arbitrary",)),
        name="mem_kv",
    )(mem, mem_g.reshape(1, D_MODEL), w_mk, w_mv)


def _prompt_kernel(x_ref, x_next_ref, pre_g_ref, post_g_ref, w_in_t_ref, conv_w_t_ref, sink_ref, mkb_ref, mvt_ref,
                   w_out_ref, run_after_ref,
                   y_ref, conv_state_ref, k_state_ref, v_state_ref,
                   zt_ref, ycat_ref, kbuf_ref, vbuf_ref, uprev_ref, bias_ref, *, tq, nsub):
    assert nsub % 2 == 0
    t = pl.program_id(1)
    nblk = tq // BLOCK

    @pl.when((pl.program_id(0) == 0) & (t == 0))
    def _():
        c = lax.broadcasted_iota(jnp.int32, (BLOCK, BLOCK), 0)
        r = lax.broadcasted_iota(jnp.int32, (BLOCK, BLOCK), 1)
        distf = (r - c + jnp.where(c > r, BLOCK, 0)).astype(jnp.float32)
        for h in range(N_SWA_HEADS):
            g, i = divmod(h, SWA_GROUP)
            bias_ref[g, :, i * BLOCK:(i + 1) * BLOCK] = (-_alibi_slope(h) * LOG2_E) * distf

    @pl.when(t == 0)
    def _():
        kbuf_ref[0:BLOCK, :] = jnp.zeros((BLOCK, D_SWA_KV), jnp.bfloat16)
        vbuf_ref[:, 0:BLOCK] = jnp.zeros((D_SWA_KV, BLOCK), jnp.bfloat16)
        uprev_ref[...] = jnp.zeros_like(uprev_ref)

    first_pen = jnp.where(t == 0, NEG_INF, 0.0)
    slot = lax.broadcasted_iota(jnp.int32, (BLOCK, SWA_GROUP * BLOCK), 0)
    query = lax.broadcasted_iota(jnp.int32, (BLOCK, SWA_GROUP * BLOCK), 1) % BLOCK
    from_prev = slot > query
    q_zero = jnp.zeros((HEAD_DIM, SWA_GROUP * BLOCK), jnp.bfloat16)
    cw = conv_w_t_ref[...]
    carry = {"u_prev": uprev_ref[...]}
    head_of_lane = lax.broadcasted_iota(jnp.int32, (1, SWA_GROUP * BLOCK), 1) // BLOCK
    sink_rows = []
    for g in range(N_SWA_KV):
        row = jnp.full((1, SWA_GROUP * BLOCK), sink_ref[g * SWA_GROUP], jnp.float32)
        for i in range(1, SWA_GROUP):
            row = jnp.where(head_of_lane == i, sink_ref[g * SWA_GROUP + i], row)
        sink_rows.append(row * LOG2_E)

    def in_proj_chunks(x_rows, zt):
        state = {}

        def norm():
            state["h"] = _rms_norm(x_rows(), pre_g_ref[...]).astype(jnp.bfloat16)

        def chunk(lo):
            rows = slice(lo, lo + IN_PROJ_CHUNK)
            zt[rows, :] = lax.dot_general(w_in_t_ref[rows, :], state["h"], _NT,
                                          preferred_element_type=jnp.float32)

        return norm, [functools.partial(chunk, lo) for lo in range(0, D_IN, IN_PROJ_CHUNK)]

    def mixer_units(sub):
        zt = zt_ref.at[sub % 2]
        ycat = ycat_ref.at[sub % 2]
        tok0 = sub * tq

        def conv_unit():
            u = zt[OFF_CC:OFF_CC + D_CONV, :] * zt[OFF_CH:OFF_CH + D_CONV, :]
            ucat = jnp.concatenate([carry["u_prev"], u], axis=1)
            conv = (cw[:, 0:1] * pltpu.roll(ucat, 2, axis=1)[:, BLOCK:]
                    + cw[:, 1:2] * pltpu.roll(ucat, 1, axis=1)[:, BLOCK:]
                    + cw[:, 2:3] * u)
            y_conv = zt[OFF_CB:OFF_CB + D_CONV, :] * conv * _silu(zt[OFF_CZ:OFF_CZ + D_CONV, :])
            ycat[YOFF_CONV:YOFF_CONV + D_CONV, :] = y_conv.astype(jnp.bfloat16)
            carry["u_prev"] = u[:, tq - BLOCK:]

        def kv_unit():
            k_nat = zt[OFF_K:OFF_K + D_SWA_KV, :].T
            kbuf_ref[BLOCK + tok0:BLOCK + tok0 + tq, :] = k_nat.astype(jnp.bfloat16)
            vbuf_ref[:, BLOCK + tok0:BLOCK + tok0 + tq] = zt[OFF_V:OFF_V + D_SWA_KV, :].astype(jnp.bfloat16)

        def swa_unit(j, g):
            cols = slice(j * BLOCK, (j + 1) * BLOCK)
            band = slice(tok0 + j * BLOCK, tok0 + (j + 2) * BLOCK)
            q0 = OFF_Q + g * SWA_GROUP * HEAD_DIM
            qt = jnp.concatenate(
                [zt[q0 + i * HEAD_DIM:q0 + (i + 1) * HEAD_DIM, cols] for i in range(SWA_GROUP)],
                axis=1)
            qt = (qt * QK_SCALE_LOG2).astype(jnp.bfloat16)
            qt = jnp.concatenate([qt, q_zero] if g == 0 else [q_zero, qt], axis=0)
            s = jnp.dot(kbuf_ref[band, :], qt, preferred_element_type=jnp.float32)
            yield
            s = jnp.where(from_prev, s[0:BLOCK], s[BLOCK:]) + bias_ref[g]
            if sub == 0 and j == 0:
                s = s + jnp.where(from_prev, first_pen, 0.0)
            sink = sink_rows[g]
            m = jnp.maximum(jnp.max(s, axis=0, keepdims=True), sink)
            p = jnp.exp2(s - m)
            l = jnp.sum(p, axis=0, keepdims=True) + jnp.exp2(sink - m)
            p = jnp.concatenate([jnp.where(from_prev, p, 0.0), jnp.where(from_prev, 0.0, p)],
                                axis=0).astype(jnp.bfloat16)
            yield
            vband = vbuf_ref[g * HEAD_DIM:(g + 1) * HEAD_DIM, band]
            o = jnp.dot(vband, p, preferred_element_type=jnp.float32)
            o = o / l
            for i in range(SWA_GROUP):
                hh = g * SWA_GROUP + i
                gate = _silu(zt[OFF_SZ + hh * HEAD_DIM:OFF_SZ + (hh + 1) * HEAD_DIM, cols])
                ycat[YOFF_SWA + hh * HEAD_DIM:YOFF_SWA + (hh + 1) * HEAD_DIM, cols] = (
                    o[:, i * BLOCK:(i + 1) * BLOCK] * gate).astype(jnp.bfloat16)

        def mem_unit(hh):
            rows = slice(OFF_MQ + hh * HEAD_DIM, OFF_MQ + (hh + 1) * HEAD_DIM)
            qt = (zt[rows, :] * QK_SCALE_LOG2).astype(jnp.bfloat16)
            pieces = [jnp.zeros((HEAD_DIM, tq), jnp.bfloat16)] * N_MEM_HEADS
            pieces[hh] = qt
            s = jnp.dot(mkb_ref[0], jnp.concatenate(pieces, axis=0),
                        preferred_element_type=jnp.float32)
            yield
            m = jnp.max(s, axis=0, keepdims=True)
            p = jnp.exp2(s - m)
            l = jnp.sum(p, axis=0, keepdims=True)
            p = p.astype(jnp.bfloat16)
            yield
            o = jnp.dot(mvt_ref[0, hh * HEAD_DIM:(hh + 1) * HEAD_DIM, :], p,
                        preferred_element_type=jnp.float32)
            gate = _silu(zt[OFF_MZ + hh * HEAD_DIM:OFF_MZ + (hh + 1) * HEAD_DIM, :])
            ycat[YOFF_MEM + hh * HEAD_DIM:YOFF_MEM + (hh + 1) * HEAD_DIM, :] = (
                o / l * gate).astype(jnp.bfloat16)

        def out_proj():
            carry["y"] = lax.dot_general(ycat[...], w_out_ref[...], _TN,
                                         preferred_element_type=jnp.float32)

        def post_norm(blk):
            rows = slice(blk * BLOCK, (blk + 1) * BLOCK)
            out_rows = slice(tok0 + blk * BLOCK, tok0 + (blk + 1) * BLOCK)
            y_ref[0, out_rows, :] = x_ref[0, out_rows, :] + _rms_norm(carry["y"][rows], post_g_ref[...])

        attention = [functools.partial(swa_unit, j, g) for j in range(nblk) for g in range(N_SWA_KV)]
        attention += [functools.partial(mem_unit, hh) for hh in range(N_MEM_HEADS)]
        return [conv_unit, kv_unit], attention, (out_proj, [functools.partial(post_norm, b) for b in range(nblk)])

    def interleave(units, chunks, norm_first, norm_after):
        (conv_unit, kv_unit), attention, (out_proj, post_norms) = units
        pending = list(chunks)
        kv_unit()
        n_rounds = -(-len(attention) // ATTN_WIDTH) + 2
        chunk_rounds = [k * n_rounds // len(pending) for k in range(len(pending))]
        todo = list(attention)
        active = []
        for rnd in range(n_rounds):
            for _ in range(ATTN_WIDTH):
                if todo:
                    active.append(todo.pop(0)())
            for gen in list(active):
                if next(gen, "done") == "done":
                    active.remove(gen)
            if rnd == 0 and norm_first is not None:
                norm_first()
            for _ in range(chunk_rounds.count(rnd)):
                pending.pop(0)()
            if rnd == 0:
                conv_unit()
        assert not todo and not active and not pending
        if norm_after is not None:
            norm_after()
        out_proj()
        for post_norm in post_norms:
            post_norm()

    @pl.when((pl.program_id(0) == 0) & (t == 0))
    def _():
        norm, chunks = in_proj_chunks(lambda: x_ref[0, 0:tq, :], zt_ref.at[0])
        norm()
        for chunk in chunks:
            chunk()

    proj = [in_proj_chunks(lambda sub=sub: x_ref[0, sub * tq:(sub + 1) * tq, :], zt_ref.at[sub % 2])
            for sub in range(1, nsub)]
    proj.append(in_proj_chunks(lambda: x_next_ref[0], zt_ref.at[0]))
    for sub in range(nsub):
        norm_first = proj[0][0] if sub == 0 else None
        norm_after = proj[sub + 1][0] if sub + 1 < nsub else None
        interleave(mixer_units(sub), proj[sub][1], norm_first, norm_after)

    tile = nsub * tq
    uprev_ref[...] = carry["u_prev"]
    conv_state_ref[0] = carry["u_prev"].T[BLOCK - (CONV_WIDTH - 1):, :]
    kbuf_ref[0:BLOCK, :] = kbuf_ref[tile:tile + BLOCK, :]
    vbuf_ref[:, 0:BLOCK] = vbuf_ref[:, tile:tile + BLOCK]
    k_state_ref[0] = zt_ref[1, OFF_K:OFF_K + D_SWA_KV, tq - BLOCK:]
    v_state_ref[0] = zt_ref[1, OFF_V:OFF_V + D_SWA_KV, tq - BLOCK:]


def _prompt_layer(x, pre_g, post_g, w_in_t, conv_w_t, sinks, mkb, mvtb, w_out, run_after, *, tq, nsub):
    B, T, _ = x.shape
    tile = tq * nsub
    full = lambda shape: pl.BlockSpec(shape, lambda b, t: (0,) * len(shape))
    kernel = functools.partial(_prompt_kernel, tq=tq, nsub=nsub)
    steps = T // tile

    def next_first_sub_tile(b, t):
        nxt = jnp.minimum(b * steps + t + 1, B * steps - 1)
        return nxt // steps, (nxt % steps) * nsub, 0

    resident = pl.BlockSpec(memory_space=pltpu.VMEM)
    return pl.pallas_call(
        kernel,
        grid=(B, steps),
        in_specs=[
            pl.BlockSpec((1, tile, D_MODEL), lambda b, t: (b, t, 0)),
            pl.BlockSpec((1, tq, D_MODEL), next_first_sub_tile),
            full((1, D_MODEL)),
            full((1, D_MODEL)),
            resident,
            full((D_CONV, CONV_WIDTH)),
            pl.BlockSpec(memory_space=pltpu.SMEM),
            pl.BlockSpec((1, N_MEM, D_MEMQ), lambda b, t: (b, 0, 0)),
            pl.BlockSpec((1, D_MEMQ, N_MEM), lambda b, t: (b, 0, 0)),
            resident,
            pl.BlockSpec(memory_space=pl.ANY),
        ],
        out_specs=[
            pl.BlockSpec((1, tile, D_MODEL), lambda b, t: (b, t, 0)),
            pl.BlockSpec((1, CONV_WIDTH - 1, D_CONV), lambda b, t: (b, 0, 0)),
            pl.BlockSpec((1, D_SWA_KV, BLOCK), lambda b, t: (b, 0, 0)),
            pl.BlockSpec((1, D_SWA_KV, BLOCK), lambda b, t: (b, 0, 0)),
        ],
        out_shape=[
            jax.ShapeDtypeStruct((B, T, D_MODEL), jnp.float32),
            jax.ShapeDtypeStruct((B, CONV_WIDTH - 1, D_CONV), jnp.float32),
            jax.ShapeDtypeStruct((B, D_SWA_KV, BLOCK), jnp.float32),
            jax.ShapeDtypeStruct((B, D_SWA_KV, BLOCK), jnp.float32),
        ],
        scratch_shapes=[
            pltpu.VMEM((2, D_IN, tq), jnp.float32),
            pltpu.VMEM((2, D_MODEL, tq), jnp.bfloat16),
            pltpu.VMEM((BLOCK + tile, D_SWA_KV), jnp.bfloat16),
            pltpu.VMEM((D_SWA_KV, BLOCK + tile), jnp.bfloat16),
            pltpu.VMEM((D_CONV, BLOCK), jnp.float32),
            pltpu.VMEM((N_SWA_KV, BLOCK, SWA_GROUP * BLOCK), jnp.float32),
        ],
        compiler_params=pltpu.CompilerParams(
            dimension_semantics=("arbitrary", "arbitrary"),
            vmem_limit_bytes=PROMPT_VMEM_LIMIT_BYTES),
        name="prompt_layer",
    )(x, x, pre_g, post_g, w_in_t, conv_w_t, sinks, mkb, mvtb, w_out, run_after)


def _sample_kernel(x_ref, conv_past_ref, ckt_ref, cvt_ref, mkt_ref, mvt_ref,
                   pre_g_ref, post_g_ref, w_in_t_ref, conv_w_ref, sink_ref, w_out_ref,
                   y_ref, conv_state_ref, kt_state_ref, vt_state_ref,
                   z_ref, ycat_ref, *, ns, group):
    R = SAMPLE_ROWS
    half = R // 2
    nseq = ns * group
    step = pl.program_id(1)

    @pl.when(step == 0)
    def _():
        x4 = x_ref[...]
        x8 = jnp.concatenate([x4, x4], axis=1).reshape(nseq * R, D_MODEL)
        h = _rms_norm(x8, pre_g_ref[...]).astype(jnp.bfloat16)
        z_ref[...] = lax.dot_general(h, w_in_t_ref[...], _NT, preferred_element_type=jnp.float32)
        u = (z_ref[:, OFF_CC:OFF_CC + D_CONV] * z_ref[:, OFF_CH:OFF_CH + D_CONV]).reshape(nseq, R, D_CONV)
        row3 = lax.broadcasted_iota(jnp.int32, (nseq, R, D_CONV), 1)
        past = conv_past_ref[...]
        past = jnp.concatenate([past, jnp.zeros((nseq, R - (CONV_WIDTH - 1), D_CONV), past.dtype)], axis=1)
        u_full = jnp.where(row3 < CONV_WIDTH - 1, past, pltpu.roll(u, CONV_WIDTH - 1, axis=1))
        cw = conv_w_ref[...]
        conv = (cw[0:1, :] * u_full
                + cw[1:2, :] * pltpu.roll(u_full, R - 1, axis=1)
                + cw[2:3, :] * pltpu.roll(u_full, R - 2, axis=1))
        conv_state_ref[...] = pltpu.roll(u_full, R - half, axis=1)[:, 0:CONV_WIDTH - 1, :]
        y_conv = (z_ref[:, OFF_CB:OFF_CB + D_CONV] * conv.reshape(nseq * R, D_CONV)
                  * _silu(z_ref[:, OFF_CZ:OFF_CZ + D_CONV]))
        ycat_ref[:, YOFF_CONV:YOFF_CONV + D_CONV] = y_conv

    row = lax.broadcasted_iota(jnp.int32, (R, PAIR), 0)
    lane = lax.broadcasted_iota(jnp.int32, (R, PAIR), 1)
    lo_row = row < half
    lo_lane = lane < HEAD_DIM
    diag = lo_row == lo_lane
    lane_sq = lax.broadcasted_iota(jnp.int32, (WINDOW, WINDOW), 1)

    def pair_bias(ncols, dist_of):
        rr = lax.broadcasted_iota(jnp.int32, (R, ncols), 0)
        cc = lax.broadcasted_iota(jnp.int32, (R, ncols), 1)
        dist, valid = dist_of(rr % half, cc)
        distf = dist.astype(jnp.float32)
        tiles = []
        for pair in range(N_SWA_HEADS // 2):
            slope = jnp.where(rr < half, _alibi_slope(2 * pair), _alibi_slope(2 * pair + 1))
            tiles.append(jnp.where(valid, -slope * distf, NEG_INF))
        return jnp.concatenate(tiles, axis=0)

    def cached_dist(tok, c):
        d = tok + WINDOW - c
        return d, d < WINDOW

    def new_dist(tok, c):
        d = tok - c
        return d, (d >= 0) & (c < half)

    bias_c = pair_bias(WINDOW, cached_dist)
    bias_n = pair_bias(SAMPLE_NEW, new_dist)
    head_of_row = lax.broadcasted_iota(jnp.int32, (N_SWA_HEADS * half, 1), 0) // half
    sink_col = jnp.full((N_SWA_HEADS * half, 1), sink_ref[0], jnp.float32)
    for hh in range(1, N_SWA_HEADS):
        sink_col = jnp.where(head_of_row == hh, sink_ref[hh], sink_col)
    state_pad = jnp.zeros((WINDOW - R, D_SWA_KV), jnp.float32)

    def seq_rows(n):
        return pl.ds(pl.multiple_of((step * ns + n) * R, R), R)

    def swa_unit(n):
        rows = seq_rows(n)
        qa = z_ref[rows, OFF_Q:OFF_Q + PAIR] * QK_SCALE
        qb = z_ref[rows, OFF_Q + PAIR:OFF_Q + 2 * PAIR] * QK_SCALE
        qc = z_ref[rows, OFF_Q + 2 * PAIR:OFF_Q + 3 * PAIR] * QK_SCALE
        t0 = jnp.where(lo_lane, jnp.where(lo_row, qa, pltpu.roll(qa, HEAD_DIM, axis=1)), 0.0)
        t1 = jnp.where(diag, qb, 0.0)
        t2 = jnp.where(lo_lane, 0.0, jnp.where(lo_row, pltpu.roll(qc, HEAD_DIM, axis=1), qc))
        qs = jnp.concatenate([t0, t1, t2], axis=0).astype(jnp.bfloat16)

        k_new = z_ref[rows, OFF_K:OFF_K + D_SWA_KV]
        v_new = z_ref[rows, OFF_V:OFF_V + D_SWA_KV]
        k_new_b = jnp.concatenate([k_new, k_new], axis=0).astype(jnp.bfloat16)
        v_new_b = jnp.concatenate([v_new, v_new], axis=0).astype(jnp.bfloat16)
        kt_old = ckt_ref[n].astype(jnp.bfloat16)

        s_c = jnp.dot(qs, kt_old, preferred_element_type=jnp.float32) + bias_c
        s_n = lax.dot_general(qs, k_new_b, _NT, preferred_element_type=jnp.float32) + bias_n
        yield
        m = jnp.maximum(jnp.maximum(jnp.max(s_c, axis=1, keepdims=True), jnp.max(s_n, axis=1, keepdims=True)),
                        sink_col)
        p_c = jnp.exp(s_c - m)
        p_n = jnp.exp(s_n - m)
        l = (jnp.sum(p_c, axis=1, keepdims=True) + jnp.sum(p_n, axis=1, keepdims=True) + jnp.exp(sink_col - m))
        p_c = p_c.astype(jnp.bfloat16)
        p_n = p_n.astype(jnp.bfloat16)
        yield
        o = (lax.dot_general(p_c, cvt_ref[n].astype(jnp.bfloat16), _NT, preferred_element_type=jnp.float32)
             + jnp.dot(p_n, v_new_b, preferred_element_type=jnp.float32)) / l
        yield
        o0, o1, o2 = o[0:R], o[R:2 * R], o[2 * R:3 * R]
        ya = jnp.where(lo_lane, o0, pltpu.roll(pltpu.roll(o0, HEAD_DIM, axis=1), half, axis=0))
        yb = jnp.where(lo_lane, o1, pltpu.roll(o1, half, axis=0))
        yc = jnp.where(lo_lane, pltpu.roll(o2, HEAD_DIM, axis=1), pltpu.roll(o2, half, axis=0))
        y_swa = jnp.concatenate([ya, yb, yc], axis=1) * _silu(z_ref[rows, OFF_SZ:OFF_SZ + D_SWA])
        ycat_ref[rows, YOFF_SWA:YOFF_SWA + D_SWA] = y_swa

    def state_unit(n):
        rows = seq_rows(n)
        k_new_t = jnp.concatenate([state_pad, z_ref[rows, OFF_K:OFF_K + D_SWA_KV]], axis=0).T
        v_new_t = jnp.concatenate([state_pad, z_ref[rows, OFF_V:OFF_V + D_SWA_KV]], axis=0).T
        keep = lane_sq < WINDOW - half
        kt_state_ref[n] = jnp.where(keep, pltpu.roll(ckt_ref[n], WINDOW - half, axis=1), k_new_t)
        vt_state_ref[n] = jnp.where(keep, pltpu.roll(cvt_ref[n], WINDOW - half, axis=1), v_new_t)

    def mem_unit(n):
        rows = seq_rows(n)
        m0 = z_ref[rows, OFF_MQ:OFF_MQ + PAIR] * QK_SCALE
        m1 = z_ref[rows, OFF_MQ + PAIR:OFF_MQ + 2 * PAIR] * QK_SCALE
        zero = jnp.zeros_like(m0)
        qm = jnp.concatenate(
            [jnp.concatenate([jnp.where(diag, m0, 0.0), zero], axis=1),
             jnp.concatenate([zero, jnp.where(diag, m1, 0.0)], axis=1)], axis=0).astype(jnp.bfloat16)
        s = jnp.dot(qm, mkt_ref[n].astype(jnp.bfloat16), preferred_element_type=jnp.float32)
        yield
        m = jnp.max(s, axis=1, keepdims=True)
        p = jnp.exp(s - m)
        l = jnp.sum(p, axis=1, keepdims=True)
        p = p.astype(jnp.bfloat16)
        yield
        o = lax.dot_general(p, mvt_ref[n].astype(jnp.bfloat16), _NT,
                            preferred_element_type=jnp.float32) / l
        yield
        oa, ob = o[0:R, 0:PAIR], o[R:2 * R, PAIR:2 * PAIR]
        y_mem = jnp.concatenate([jnp.where(lo_lane, oa, pltpu.roll(oa, half, axis=0)),
                                 jnp.where(lo_lane, ob, pltpu.roll(ob, half, axis=0))], axis=1)
        ycat_ref[rows, YOFF_MEM:YOFF_MEM + D_MEMQ] = y_mem * _silu(z_ref[rows, OFF_MZ:OFF_MZ + D_MEMQ])

    todo = [functools.partial(unit, n) for n in range(ns) for unit in (swa_unit, mem_unit)]
    states = [functools.partial(state_unit, n) for n in range(ns)]
    active = []
    while todo or active:
        for _ in range(SAMPLE_WIDTH):
            if todo:
                active.append(todo.pop(0)())
        for gen in list(active):
            if next(gen, "done") == "done":
                active.remove(gen)
        if states:
            states.pop(0)()
    for unit in states:
        unit()

    @pl.when(step == group - 1)
    def _():
        y = jnp.dot(ycat_ref[...].astype(jnp.bfloat16), w_out_ref[...], preferred_element_type=jnp.float32)
        y_ref[...] = x_ref[...] + _rms_norm(y, post_g_ref[...]).reshape(nseq, R, D_MODEL)[:, 0:half, :]


def _sample_layer(x, conv_past, ckt, cvt, mkt, mvt, pre_g, post_g, w_in_t, conv_w, sinks, w_out, *, ns, group):
    N = ckt.shape[0]
    R = SAMPLE_ROWS
    nseq = ns * group
    full = lambda shape: pl.BlockSpec(shape, lambda o, i: (0,) * len(shape))
    per_group = lambda shape: pl.BlockSpec(shape, lambda o, i: (o,) + (0,) * (len(shape) - 1))
    per_step = lambda shape: pl.BlockSpec(shape, lambda o, i: (o * group + i,) + (0,) * (len(shape) - 1))
    kernel = functools.partial(_sample_kernel, ns=ns, group=group)
    return pl.pallas_call(
        kernel,
        grid=(N // nseq, group),
        in_specs=[
            per_group((nseq, R // 2, D_MODEL)),
            per_group((nseq, CONV_WIDTH - 1, D_CONV)),
            per_step((ns, D_SWA_KV, WINDOW)),
            per_step((ns, D_SWA_KV, WINDOW)),
            per_step((ns, D_MEMQ, N_MEM)),
            per_step((ns, D_MEMQ, N_MEM)),
            full((1, D_MODEL)),
            full((1, D_MODEL)),
            pl.BlockSpec(memory_space=pltpu.VMEM),
            full((CONV_WIDTH, D_CONV)),
            pl.BlockSpec(memory_space=pltpu.SMEM),
            pl.BlockSpec(memory_space=pltpu.VMEM),
        ],
        out_specs=[
            per_group((nseq, R // 2, D_MODEL)),
            per_group((nseq, CONV_WIDTH - 1, D_CONV)),
            per_step((ns, D_SWA_KV, WINDOW)),
            per_step((ns, D_SWA_KV, WINDOW)),
        ],
        out_shape=[
            jax.ShapeDtypeStruct((N, R // 2, D_MODEL), jnp.float32),
            jax.ShapeDtypeStruct((N, CONV_WIDTH - 1, D_CONV), jnp.float32),
            jax.ShapeDtypeStruct((N, D_SWA_KV, WINDOW), jnp.float32),
            jax.ShapeDtypeStruct((N, D_SWA_KV, WINDOW), jnp.float32),
        ],
        scratch_shapes=[
            pltpu.VMEM((nseq * R, D_IN), jnp.float32),
            pltpu.VMEM((nseq * R, D_MODEL), jnp.float32),
        ],
        compiler_params=pltpu.CompilerParams(
            dimension_semantics=("arbitrary", "arbitrary"),
            vmem_limit_bytes=SAMPLE_VMEM_LIMIT_BYTES),
        name="sample_layer",
    )(x, conv_past, ckt, cvt, mkt, mvt, pre_g, post_g, w_in_t, conv_w, sinks, w_out)


def _heads_last_to_keys_last(a):
    n, keys, heads, dim = a.shape
    return jnp.transpose(a, (0, 2, 3, 1)).reshape(n, heads * dim, keys)


def _keys_last_to_heads_last(a, heads):
    n, hd, keys = a.shape
    return jnp.transpose(a.reshape(n, heads, hd // heads, keys), (0, 3, 1, 2))[None]


def kernel(x_prompt, x_sample, mem_prompt, state_conv, cache_swa_k, cache_swa_v, cache_mem_k, cache_mem_v,
           pre_norm_g, post_norm_g, w_in, conv_w, attn_sinks, mem_norm_g, w_mem_k, w_mem_v, w_out):
    assert w_in.shape[0] == 1, "one layer, as the problem states"
    N, TS, _ = x_sample.shape
    assert TS == SAMPLE_ROWS // 2 and cache_swa_k.shape[2] == WINDOW
    l = 0

    pre_g = pre_norm_g[l].reshape(1, D_MODEL)
    post_g = post_norm_g[l].reshape(1, D_MODEL)
    w_in_t = _transposed_bf16(w_in[l], IN_PROJ_CHUNK)
    w_out_bf = w_out[l].astype(jnp.bfloat16)
    sinks = attn_sinks[l].astype(jnp.float32)

    y_s, conv_s, kt_s, vt_s = _sample_layer(
        x_sample, state_conv[l],
        _heads_last_to_keys_last(cache_swa_k[l]), _heads_last_to_keys_last(cache_swa_v[l]),
        _heads_last_to_keys_last(cache_mem_k[l]), _heads_last_to_keys_last(cache_mem_v[l]),
        pre_g, post_g, w_in_t, conv_w[l], sinks, w_out_bf, ns=SAMPLE_NS, group=SAMPLE_GROUP)

    mkt, mvt, mkb, mvtb = _mem_kv(mem_prompt, mem_norm_g[l], w_mem_k[l], w_mem_v[l])
    y_p, conv_p, kt_p, vt_p = _prompt_layer(
        x_prompt, pre_g, post_g, w_in_t, conv_w[l].T, sinks, mkb, mvtb, w_out_bf, conv_s,
        tq=PROMPT_TQ, nsub=PROMPT_NSUB)

    return (y_p, y_s,
            conv_p[None],
            _keys_last_to_heads_last(kt_p, N_SWA_KV), _keys_last_to_heads_last(vt_p, N_SWA_KV),
            _keys_last_to_heads_last(mkt, N_MEM_HEADS), _keys_last_to_heads_last(mvt, N_MEM_HEADS),
            conv_s[None],
            _keys_last_to_heads_last(kt_s, N_SWA_KV), _keys_last_to_heads_last(vt_s, N_SWA_KV))
```

```python
import functools

import numpy as np
import jax
import jax.numpy as jnp
from jax import lax
from jax.experimental import pallas as pl
from jax.experimental.pallas import tpu as pltpu

D_MODEL = 1024
HEAD_DIM = 64
D_CONV = 384
N_MEM_HEADS = 4
D_MEMQ = N_MEM_HEADS * HEAD_DIM
D_SWA = 384
N_SWA_HEADS = 6
N_SWA_KV = 2
SWA_GROUP = N_SWA_HEADS // N_SWA_KV
D_SWA_KV = N_SWA_KV * HEAD_DIM
N_MEM = 256
CONV_WIDTH = 3
WINDOW = 128
BLOCK = 128
RMS_EPS = 1e-6
NEG_INF = -1e30
D_IN = 3072
QK_SCALE = HEAD_DIM ** -0.5
LOG2_E = float(np.log2(np.e))
QK_SCALE_LOG2 = QK_SCALE * LOG2_E

OFF_CB, OFF_CC, OFF_CH, OFF_CZ = 0, 384, 768, 1152
OFF_Q, OFF_K, OFF_V, OFF_SZ = 1536, 1920, 2048, 2176
OFF_MQ, OFF_MZ = 2560, 2816
YOFF_CONV, YOFF_SWA, YOFF_MEM = 0, 384, 768

PAIR = 2 * HEAD_DIM

MIB = 1024 * 1024
PROMPT_VMEM_LIMIT_BYTES = 44 * MIB
SAMPLE_VMEM_LIMIT_BYTES = 48 * MIB

MEM_KV_BATCHES = 2
PROMPT_TQ = 512
PROMPT_NSUB = 2
IN_PROJ_CHUNK = 512
ATTN_WIDTH = 2
SAMPLE_NS = 16
SAMPLE_GROUP = 2
SAMPLE_ROWS = 8
SAMPLE_WIDTH = 16
SAMPLE_NEW = 16


def _alibi_slope(h):
    return float(np.power(np.float32(2.0), np.float32(-8.0 * (h + 1) / N_SWA_HEADS)))


def _rms_norm(x, g):
    return x * lax.rsqrt(jnp.mean(x * x, axis=-1, keepdims=True) + RMS_EPS) * g


def _silu(x):
    return x * jax.nn.sigmoid(x)


_NT = (((1,), (1,)), ((), ()))
_TN = (((0,), (0,)), ((), ()))


def _mem_kv_kernel(mem_ref, g_ref, wk_ref, wv_ref, mkt_ref, mvt_ref, mkb_ref, mvtb_ref, *, nb):
    mem = mem_ref[...].reshape(nb * N_MEM, D_MODEL)
    m = _rms_norm(mem, g_ref[...]).astype(jnp.bfloat16)
    mk = jnp.dot(m, wk_ref[...].astype(jnp.bfloat16), preferred_element_type=jnp.float32)
    mv = jnp.dot(m, wv_ref[...].astype(jnp.bfloat16), preferred_element_type=jnp.float32)
    for b in range(nb):
        mk_b = mk[b * N_MEM:(b + 1) * N_MEM]
        mv_t = mv[b * N_MEM:(b + 1) * N_MEM].T
        mkt_ref[b] = mk_b.T
        mvt_ref[b] = mv_t
        mkb_ref[b] = mk_b.astype(jnp.bfloat16)
        mvtb_ref[b] = mv_t.astype(jnp.bfloat16)


def _mem_kv(mem, mem_g, w_mk, w_mv):
    B = mem.shape[0]
    nb = MEM_KV_BATCHES
    full = lambda shape: pl.BlockSpec(shape, lambda b: (0,) * len(shape))
    per_batch = pl.BlockSpec((nb, N_MEM, D_MEMQ), lambda b: (b, 0, 0))
    return pl.pallas_call(
        functools.partial(_mem_kv_kernel, nb=nb),
        grid=(B // nb,),
        in_specs=[
            pl.BlockSpec((nb, N_MEM, D_MODEL), lambda b: (b, 0, 0)),
            full((1, D_MODEL)),
            full((D_MODEL, D_MEMQ)),
            full((D_MODEL, D_MEMQ)),
        ],
        out_specs=[per_batch] * 4,
        out_shape=[
            jax.ShapeDtypeStruct((B, D_MEMQ, N_MEM), jnp.float32),
            jax.ShapeDtypeStruct((B, D_MEMQ, N_MEM), jnp.float32),
            jax.ShapeDtypeStruct((B, N_MEM, D_MEMQ), jnp.bfloat16),
            jax.ShapeDtypeStruct((B, D_MEMQ, N_MEM), jnp.bfloat16),
        ],
        compiler_params=pltpu.CompilerParams(dimension_semantics=("arbitrary",)),
        name="mem_kv",
    )(mem, mem_g.reshape(1, D_MODEL), w_mk, w_mv)


def _prompt_kernel(x_ref, x_next_ref, pre_g_ref, post_g_ref, w_in_t_ref, conv_w_t_ref, sink_ref, mkb_ref, mvt_ref,
                   w_out_ref, run_after_ref,
                   y_ref, conv_state_ref, k_state_ref, v_state_ref,
                   zt_ref, ycat_ref, kbuf_ref, vbuf_ref, uprev_ref, bias_ref, *, tq, nsub):
    assert nsub % 2 == 0
    t = pl.program_id(1)
    nblk = tq // BLOCK

    @pl.when((pl.program_id(0) == 0) & (t == 0))
    def _():
        c = lax.broadcasted_iota(jnp.int32, (BLOCK, BLOCK), 0)
        r = lax.broadcasted_iota(jnp.int32, (BLOCK, BLOCK), 1)
        distf = (r - c + jnp.where(c > r, BLOCK, 0)).astype(jnp.float32)
        for h in range(N_SWA_HEADS):
            g, i = divmod(h, SWA_GROUP)
            bias_ref[g, :, i * BLOCK:(i + 1) * BLOCK] = (-_alibi_slope(h) * LOG2_E) * distf

    @pl.when(t == 0)
    def _():
        kbuf_ref[0:BLOCK, :] = jnp.zeros((BLOCK, D_SWA_KV), jnp.bfloat16)
        vbuf_ref[:, 0:BLOCK] = jnp.zeros((D_SWA_KV, BLOCK), jnp.bfloat16)
        uprev_ref[...] = jnp.zeros_like(uprev_ref)

    first_pen = jnp.where(t == 0, NEG_INF, 0.0)
    slot = lax.broadcasted_iota(jnp.int32, (BLOCK, SWA_GROUP * BLOCK), 0)
    query = lax.broadcasted_iota(jnp.int32, (BLOCK, SWA_GROUP * BLOCK), 1) % BLOCK
    from_prev = slot > query
    q_zero = jnp.zeros((HEAD_DIM, SWA_GROUP * BLOCK), jnp.bfloat16)
    cw = conv_w_t_ref[...]
    carry = {"u_prev": uprev_ref[...]}
    head_of_lane = lax.broadcasted_iota(jnp.int32, (1, SWA_GROUP * BLOCK), 1) // BLOCK
    sink_rows = []
    for g in range(N_SWA_KV):
        row = jnp.full((1, SWA_GROUP * BLOCK), sink_ref[g * SWA_GROUP], jnp.float32)
        for i in range(1, SWA_GROUP):
            row = jnp.where(head_of_lane == i, sink_ref[g * SWA_GROUP + i], row)
        sink_rows.append(row * LOG2_E)

    def in_proj_chunks(x_rows, zt):
        state = {}

        def norm():
            state["h"] = _rms_norm(x_rows(), pre_g_ref[...]).astype(jnp.bfloat16)

        def chunk(lo):
            rows = slice(lo, lo + IN_PROJ_CHUNK)
            zt[rows, :] = lax.dot_general(w_in_t_ref[rows, :], state["h"], _NT,
                                          preferred_element_type=jnp.float32)

        return norm, [functools.partial(chunk, lo) for lo in range(0, D_IN, IN_PROJ_CHUNK)]

    def mixer_units(sub):
        zt = zt_ref.at[sub % 2]
        ycat = ycat_ref.at[sub % 2]
        tok0 = sub * tq

        def conv_unit():
            u = zt[OFF_CC:OFF_CC + D_CONV, :] * zt[OFF_CH:OFF_CH + D_CONV, :]
            ucat = jnp.concatenate([carry["u_prev"], u], axis=1)
            conv = (cw[:, 0:1] * pltpu.roll(ucat, 2, axis=1)[:, BLOCK:]
                    + cw[:, 1:2] * pltpu.roll(ucat, 1, axis=1)[:, BLOCK:]
                    + cw[:, 2:3] * u)
            y_conv = zt[OFF_CB:OFF_CB + D_CONV, :] * conv * _silu(zt[OFF_CZ:OFF_CZ + D_CONV, :])
            ycat[YOFF_CONV:YOFF_CONV + D_CONV, :] = y_conv.astype(jnp.bfloat16)
            carry["u_prev"] = u[:, tq - BLOCK:]

        def kv_unit():
            k_nat = zt[OFF_K:OFF_K + D_SWA_KV, :].T
            kbuf_ref[BLOCK + tok0:BLOCK + tok0 + tq, :] = k_nat.astype(jnp.bfloat16)
            vbuf_ref[:, BLOCK + tok0:BLOCK + tok0 + tq] = zt[OFF_V:OFF_V + D_SWA_KV, :].astype(jnp.bfloat16)

        def swa_unit(j, g):
            cols = slice(j * BLOCK, (j + 1) * BLOCK)
            band = slice(tok0 + j * BLOCK, tok0 + (j + 2) * BLOCK)
            q0 = OFF_Q + g * SWA_GROUP * HEAD_DIM
            qt = jnp.concatenate(
                [zt[q0 + i * HEAD_DIM:q0 + (i + 1) * HEAD_DIM, cols] for i in range(SWA_GROUP)],
                axis=1)
            qt = (qt * QK_SCALE_LOG2).astype(jnp.bfloat16)
            qt = jnp.concatenate([qt, q_zero] if g == 0 else [q_zero, qt], axis=0)
            s = jnp.dot(kbuf_ref[band, :], qt, preferred_element_type=jnp.float32)
            yield
            s = jnp.where(from_prev, s[0:BLOCK], s[BLOCK:]) + bias_ref[g]
            if sub == 0 and j == 0:
                s = s + jnp.where(from_prev, first_pen, 0.0)
            sink = sink_rows[g]
            m = jnp.maximum(jnp.max(s, axis=0, keepdims=True), sink)
            p = jnp.exp2(s - m)
            l = jnp.sum(p, axis=0, keepdims=True) + jnp.exp2(sink - m)
            p = jnp.concatenate([jnp.where(from_prev, p, 0.0), jnp.where(from_prev, 0.0, p)],
                                axis=0).astype(jnp.bfloat16)
            yield
            vband = vbuf_ref[g * HEAD_DIM:(g + 1) * HEAD_DIM, band]
            o = jnp.dot(vband, p, preferred_element_type=jnp.float32)
            o = o / l
            for i in range(SWA_GROUP):
                hh = g * SWA_GROUP + i
                gate = _silu(zt[OFF_SZ + hh * HEAD_DIM:OFF_SZ + (hh + 1) * HEAD_DIM, cols])
                ycat[YOFF_SWA + hh * HEAD_DIM:YOFF_SWA + (hh + 1) * HEAD_DIM, cols] = (
                    o[:, i * BLOCK:(i + 1) * BLOCK] * gate).astype(jnp.bfloat16)

        def mem_unit(hh):
            rows = slice(OFF_MQ + hh * HEAD_DIM, OFF_MQ + (hh + 1) * HEAD_DIM)
            qt = (zt[rows, :] * QK_SCALE_LOG2).astype(jnp.bfloat16)
            pieces = [jnp.zeros((HEAD_DIM, tq), jnp.bfloat16)] * N_MEM_HEADS
            pieces[hh] = qt
            s = jnp.dot(mkb_ref[0], jnp.concatenate(pieces, axis=0),
                        preferred_element_type=jnp.float32)
            yield
            m = jnp.max(s, axis=0, keepdims=True)
            p = jnp.exp2(s - m)
            l = jnp.sum(p, axis=0, keepdims=True)
            p = p.astype(jnp.bfloat16)
            yield
            o = jnp.dot(mvt_ref[0, hh * HEAD_DIM:(hh + 1) * HEAD_DIM, :], p,
                        preferred_element_type=jnp.float32)
            gate = _silu(zt[OFF_MZ + hh * HEAD_DIM:OFF_MZ + (hh + 1) * HEAD_DIM, :])
            ycat[YOFF_MEM + hh * HEAD_DIM:YOFF_MEM + (hh + 1) * HEAD_DIM, :] = (
                o / l * gate).astype(jnp.bfloat16)

        def out_proj():
            carry["y"] = lax.dot_general(ycat[...], w_out_ref[...], _TN,
                                         preferred_element_type=jnp.float32)

        def post_norm(blk):
            rows = slice(blk * BLOCK, (blk + 1) * BLOCK)
            out_rows = slice(tok0 + blk * BLOCK, tok0 + (blk + 1) * BLOCK)
            y_ref[0, out_rows, :] = x_ref[0, out_rows, :] + _rms_norm(carry["y"][rows], post_g_ref[...])

        attention = [functools.partial(swa_unit, j, g) for j in range(nblk) for g in range(N_SWA_KV)]
        attention += [functools.partial(mem_unit, hh) for hh in range(N_MEM_HEADS)]
        return [conv_unit, kv_unit], attention, (out_proj, [functools.partial(post_norm, b) for b in range(nblk)])

    def interleave(units, chunks, norm_first, norm_after):
        (conv_unit, kv_unit), attention, (out_proj, post_norms) = units
        pending = list(chunks)
        kv_unit()
        n_rounds = -(-len(attention) // ATTN_WIDTH) + 2
        chunk_rounds = [k * n_rounds // len(pending) for k in range(len(pending))]
        todo = list(attention)
        active = []
        for rnd in range(n_rounds):
            for _ in range(ATTN_WIDTH):
                if todo:
                    active.append(todo.pop(0)())
            for gen in list(active):
                if next(gen, "done") == "done":
                    active.remove(gen)
            if rnd == 0 and norm_first is not None:
                norm_first()
            for _ in range(chunk_rounds.count(rnd)):
                pending.pop(0)()
            if rnd == 0:
                conv_unit()
        assert not todo and not active and not pending
        if norm_after is not None:
            norm_after()
        out_proj()
        for post_norm in post_norms:
            post_norm()

    @pl.when((pl.program_id(0) == 0) & (t == 0))
    def _():
        norm, chunks = in_proj_chunks(lambda: x_ref[0, 0:tq, :], zt_ref.at[0])
        norm()
        for chunk in chunks:
            chunk()

    proj = [in_proj_chunks(lambda sub=sub: x_ref[0, sub * tq:(sub + 1) * tq, :], zt_ref.at[sub % 2])
            for sub in range(1, nsub)]
    proj.append(in_proj_chunks(lambda: x_next_ref[0], zt_ref.at[0]))
    for sub in range(nsub):
        norm_first = proj[0][0] if sub == 0 else None
        norm_after = proj[sub + 1][0] if sub + 1 < nsub else None
        interleave(mixer_units(sub), proj[sub][1], norm_first, norm_after)

    tile = nsub * tq
    uprev_ref[...] = carry["u_prev"]
    conv_state_ref[0] = carry["u_prev"].T[BLOCK - (CONV_WIDTH - 1):, :]
    kbuf_ref[0:BLOCK, :] = kbuf_ref[tile:tile + BLOCK, :]
    vbuf_ref[:, 0:BLOCK] = vbuf_ref[:, tile:tile + BLOCK]
    k_state_ref[0] = zt_ref[1, OFF_K:OFF_K + D_SWA_KV, tq - BLOCK:]
    v_state_ref[0] = zt_ref[1, OFF_V:OFF_V + D_SWA_KV, tq - BLOCK:]


def _prompt_layer(x, pre_g, post_g, w_in_t, conv_w_t, sinks, mkb, mvtb, w_out, run_after, *, tq, nsub):
    B, T, _ = x.shape
    tile = tq * nsub
    full = lambda shape: pl.BlockSpec(shape, lambda b, t: (0,) * len(shape))
    kernel = functools.partial(_prompt_kernel, tq=tq, nsub=nsub)
    steps = T // tile

    def next_first_sub_tile(b, t):
        nxt = jnp.minimum(b * steps + t + 1, B * steps - 1)
        return nxt // steps, (nxt % steps) * nsub, 0

    resident = pl.BlockSpec(memory_space=pltpu.VMEM)
    return pl.pallas_call(
        kernel,
        grid=(B, steps),
        in_specs=[
            pl.BlockSpec((1, tile, D_MODEL), lambda b, t: (b, t, 0)),
            pl.BlockSpec((1, tq, D_MODEL), next_first_sub_tile),
            resident,
            resident,
            resident,
            resident,
            pl.BlockSpec(memory_space=pltpu.SMEM),
            pl.BlockSpec((1, N_MEM, D_MEMQ), lambda b, t: (b, 0, 0)),
            pl.BlockSpec((1, D_MEMQ, N_MEM), lambda b, t: (b, 0, 0)),
            resident,
            pl.BlockSpec(memory_space=pl.ANY),
        ],
        out_specs=[
            pl.BlockSpec((1, tile, D_MODEL), lambda b, t: (b, t, 0)),
            pl.BlockSpec((1, CONV_WIDTH - 1, D_CONV), lambda b, t: (b, 0, 0)),
            pl.BlockSpec((1, D_SWA_KV, BLOCK), lambda b, t: (b, 0, 0)),
            pl.BlockSpec((1, D_SWA_KV, BLOCK), lambda b, t: (b, 0, 0)),
        ],
        out_shape=[
            jax.ShapeDtypeStruct((B, T, D_MODEL), jnp.float32),
            jax.ShapeDtypeStruct((B, CONV_WIDTH - 1, D_CONV), jnp.float32),
            jax.ShapeDtypeStruct((B, D_SWA_KV, BLOCK), jnp.float32),
            jax.ShapeDtypeStruct((B, D_SWA_KV, BLOCK), jnp.float32),
        ],
        scratch_shapes=[
            pltpu.VMEM((2, D_IN, tq), jnp.float32),
            pltpu.VMEM((2, D_MODEL, tq), jnp.bfloat16),
            pltpu.VMEM((BLOCK + tile, D_SWA_KV), jnp.bfloat16),
            pltpu.VMEM((D_SWA_KV, BLOCK + tile), jnp.bfloat16),
            pltpu.VMEM((D_CONV, BLOCK), jnp.float32),
            pltpu.VMEM((N_SWA_KV, BLOCK, SWA_GROUP * BLOCK), jnp.float32),
        ],
        compiler_params=pltpu.CompilerParams(
            dimension_semantics=("arbitrary", "arbitrary"),
            vmem_limit_bytes=PROMPT_VMEM_LIMIT_BYTES),
        name="prompt_layer",
    )(x, x, pre_g, post_g, w_in_t, conv_w_t, sinks, mkb, mvtb, w_out, run_after)


def _sample_kernel(x_ref, conv_past_ref, ckt_ref, cvt_ref, mkt_ref, mvt_ref,
                   pre_g_ref, post_g_ref, w_in_t_ref, conv_w_ref, sink_ref, w_out_ref,
                   y_ref, conv_state_ref, kt_state_ref, vt_state_ref,
                   z_ref, ycat_ref, *, ns, group):
    R = SAMPLE_ROWS
    half = R // 2
    nseq = ns * group
    step = pl.program_id(1)

    @pl.when(step == 0)
    def _():
        x4 = x_ref[...]
        x8 = jnp.concatenate([x4, x4], axis=1).reshape(nseq * R, D_MODEL)
        h = _rms_norm(x8, pre_g_ref[...]).astype(jnp.bfloat16)
        z_ref[...] = lax.dot_general(h, w_in_t_ref[...], _NT, preferred_element_type=jnp.float32)
        u = (z_ref[:, OFF_CC:OFF_CC + D_CONV] * z_ref[:, OFF_CH:OFF_CH + D_CONV]).reshape(nseq, R, D_CONV)
        row3 = lax.broadcasted_iota(jnp.int32, (nseq, R, D_CONV), 1)
        past = conv_past_ref[...]
        past = jnp.concatenate([past, jnp.zeros((nseq, R - (CONV_WIDTH - 1), D_CONV), past.dtype)], axis=1)
        u_full = jnp.where(row3 < CONV_WIDTH - 1, past, pltpu.roll(u, CONV_WIDTH - 1, axis=1))
        cw = conv_w_ref[...]
        conv = (cw[0:1, :] * u_full
                + cw[1:2, :] * pltpu.roll(u_full, R - 1, axis=1)
                + cw[2:3, :] * pltpu.roll(u_full, R - 2, axis=1))
        conv_state_ref[...] = pltpu.roll(u_full, R - half, axis=1)[:, 0:CONV_WIDTH - 1, :]
        y_conv = (z_ref[:, OFF_CB:OFF_CB + D_CONV] * conv.reshape(nseq * R, D_CONV)
                  * _silu(z_ref[:, OFF_CZ:OFF_CZ + D_CONV]))
        ycat_ref[:, YOFF_CONV:YOFF_CONV + D_CONV] = y_conv

    row = lax.broadcasted_iota(jnp.int32, (R, PAIR), 0)
    lane = lax.broadcasted_iota(jnp.int32, (R, PAIR), 1)
    lo_row = row < half
    lo_lane = lane < HEAD_DIM
    diag = lo_row == lo_lane
    lane_sq = lax.broadcasted_iota(jnp.int32, (WINDOW, WINDOW), 1)

    def pair_bias(ncols, dist_of):
        rr = lax.broadcasted_iota(jnp.int32, (R, ncols), 0)
        cc = lax.broadcasted_iota(jnp.int32, (R, ncols), 1)
        dist, valid = dist_of(rr % half, cc)
        distf = dist.astype(jnp.float32)
        tiles = []
        for pair in range(N_SWA_HEADS // 2):
            slope = jnp.where(rr < half, _alibi_slope(2 * pair), _alibi_slope(2 * pair + 1))
            tiles.append(jnp.where(valid, -slope * distf, NEG_INF))
        return jnp.concatenate(tiles, axis=0)

    def cached_dist(tok, c):
        d = tok + WINDOW - c
        return d, d < WINDOW

    def new_dist(tok, c):
        d = tok - c
        return d, (d >= 0) & (c < half)

    bias_c = pair_bias(WINDOW, cached_dist)
    bias_n = pair_bias(SAMPLE_NEW, new_dist)
    head_of_row = lax.broadcasted_iota(jnp.int32, (N_SWA_HEADS * half, 1), 0) // half
    sink_col = jnp.full((N_SWA_HEADS * half, 1), sink_ref[0], jnp.float32)
    for hh in range(1, N_SWA_HEADS):
        sink_col = jnp.where(head_of_row == hh, sink_ref[hh], sink_col)
    state_pad = jnp.zeros((WINDOW - R, D_SWA_KV), jnp.float32)

    def seq_rows(n):
        return pl.ds(pl.multiple_of((step * ns + n) * R, R), R)

    def swa_unit(n):
        rows = seq_rows(n)
        qa = z_ref[rows, OFF_Q:OFF_Q + PAIR] * QK_SCALE
        qb = z_ref[rows, OFF_Q + PAIR:OFF_Q + 2 * PAIR] * QK_SCALE
        qc = z_ref[rows, OFF_Q + 2 * PAIR:OFF_Q + 3 * PAIR] * QK_SCALE
        t0 = jnp.where(lo_lane, jnp.where(lo_row, qa, pltpu.roll(qa, HEAD_DIM, axis=1)), 0.0)
        t1 = jnp.where(diag, qb, 0.0)
        t2 = jnp.where(lo_lane, 0.0, jnp.where(lo_row, pltpu.roll(qc, HEAD_DIM, axis=1), qc))
        qs = jnp.concatenate([t0, t1, t2], axis=0).astype(jnp.bfloat16)

        k_new = z_ref[rows, OFF_K:OFF_K + D_SWA_KV]
        v_new = z_ref[rows, OFF_V:OFF_V + D_SWA_KV]
        k_new_b = jnp.concatenate([k_new, k_new], axis=0).astype(jnp.bfloat16)
        v_new_b = jnp.concatenate([v_new, v_new], axis=0).astype(jnp.bfloat16)
        kt_old = ckt_ref[n].astype(jnp.bfloat16)

        s_c = jnp.dot(qs, kt_old, preferred_element_type=jnp.float32) + bias_c
        s_n = lax.dot_general(qs, k_new_b, _NT, preferred_element_type=jnp.float32) + bias_n
        yield
        m = jnp.maximum(jnp.maximum(jnp.max(s_c, axis=1, keepdims=True), jnp.max(s_n, axis=1, keepdims=True)),
                        sink_col)
        p_c = jnp.exp(s_c - m)
        p_n = jnp.exp(s_n - m)
        l = (jnp.sum(p_c, axis=1, keepdims=True) + jnp.sum(p_n, axis=1, keepdims=True) + jnp.exp(sink_col - m))
        p_c = p_c.astype(jnp.bfloat16)
        p_n = p_n.astype(jnp.bfloat16)
        yield
        o = (lax.dot_general(p_c, cvt_ref[n].astype(jnp.bfloat16), _NT, preferred_element_type=jnp.float32)
             + jnp.dot(p_n, v_new_b, preferred_element_type=jnp.float32)) / l
        yield
        o0, o1, o2 = o[0:R], o[R:2 * R], o[2 * R:3 * R]
        ya = jnp.where(lo_lane, o0, pltpu.roll(pltpu.roll(o0, HEAD_DIM, axis=1), half, axis=0))
        yb = jnp.where(lo_lane, o1, pltpu.roll(o1, half, axis=0))
        yc = jnp.where(lo_lane, pltpu.roll(o2, HEAD_DIM, axis=1), pltpu.roll(o2, half, axis=0))
        y_swa = jnp.concatenate([ya, yb, yc], axis=1) * _silu(z_ref[rows, OFF_SZ:OFF_SZ + D_SWA])
        ycat_ref[rows, YOFF_SWA:YOFF_SWA + D_SWA] = y_swa

    def state_unit(n):
        rows = seq_rows(n)
        k_new_t = jnp.concatenate([state_pad, z_ref[rows, OFF_K:OFF_K + D_SWA_KV]], axis=0).T
        v_new_t = jnp.concatenate([state_pad, z_ref[rows, OFF_V:OFF_V + D_SWA_KV]], axis=0).T
        keep = lane_sq < WINDOW - half
        kt_state_ref[n] = jnp.where(keep, pltpu.roll(ckt_ref[n], WINDOW - half, axis=1), k_new_t)
        vt_state_ref[n] = jnp.where(keep, pltpu.roll(cvt_ref[n], WINDOW - half, axis=1), v_new_t)

    def mem_unit(n):
        rows = seq_rows(n)
        m0 = z_ref[rows, OFF_MQ:OFF_MQ + PAIR] * QK_SCALE
        m1 = z_ref[rows, OFF_MQ + PAIR:OFF_MQ + 2 * PAIR] * QK_SCALE
        zero = jnp.zeros_like(m0)
        qm = jnp.concatenate(
            [jnp.concatenate([jnp.where(diag, m0, 0.0), zero], axis=1),
             jnp.concatenate([zero, jnp.where(diag, m1, 0.0)], axis=1)], axis=0).astype(jnp.bfloat16)
        s = jnp.dot(qm, mkt_ref[n].astype(jnp.bfloat16), preferred_element_type=jnp.float32)
        yield
        m = jnp.max(s, axis=1, keepdims=True)
        p = jnp.exp(s - m)
        l = jnp.sum(p, axis=1, keepdims=True)
        p = p.astype(jnp.bfloat16)
        yield
        o = lax.dot_general(p, mvt_ref[n].astype(jnp.bfloat16), _NT,
                            preferred_element_type=jnp.float32) / l
        yield
        oa, ob = o[0:R, 0:PAIR], o[R:2 * R, PAIR:2 * PAIR]
        y_mem = jnp.concatenate([jnp.where(lo_lane, oa, pltpu.roll(oa, half, axis=0)),
                                 jnp.where(lo_lane, ob, pltpu.roll(ob, half, axis=0))], axis=1)
        ycat_ref[rows, YOFF_MEM:YOFF_MEM + D_MEMQ] = y_mem * _silu(z_ref[rows, OFF_MZ:OFF_MZ + D_MEMQ])

    todo = [functools.partial(unit, n) for n in range(ns) for unit in (swa_unit, mem_unit)]
    states = [functools.partial(state_unit, n) for n in range(ns)]
    active = []
    while todo or active:
        for _ in range(SAMPLE_WIDTH):
            if todo:
                active.append(todo.pop(0)())
        for gen in list(active):
            if next(gen, "done") == "done":
                active.remove(gen)
        if states:
            states.pop(0)()
    for unit in states:
        unit()

    @pl.when(step == group - 1)
    def _():
        y = jnp.dot(ycat_ref[...].astype(jnp.bfloat16), w_out_ref[...], preferred_element_type=jnp.float32)
        y_ref[...] = x_ref[...] + _rms_norm(y, post_g_ref[...]).reshape(nseq, R, D_MODEL)[:, 0:half, :]


def _sample_layer(x, conv_past, ckt, cvt, mkt, mvt, pre_g, post_g, w_in_t, conv_w, sinks, w_out, *, ns, group):
    N = ckt.shape[0]
    R = SAMPLE_ROWS
    nseq = ns * group
    full = lambda shape: pl.BlockSpec(shape, lambda o, i: (0,) * len(shape))
    per_group = lambda shape: pl.BlockSpec(shape, lambda o, i: (o,) + (0,) * (len(shape) - 1))
    per_step = lambda shape: pl.BlockSpec(shape, lambda o, i: (o * group + i,) + (0,) * (len(shape) - 1))
    kernel = functools.partial(_sample_kernel, ns=ns, group=group)
    return pl.pallas_call(
        kernel,
        grid=(N // nseq, group),
        in_specs=[
            per_group((nseq, R // 2, D_MODEL)),
            per_group((nseq, CONV_WIDTH - 1, D_CONV)),
            per_step((ns, D_SWA_KV, WINDOW)),
            per_step((ns, D_SWA_KV, WINDOW)),
            per_step((ns, D_MEMQ, N_MEM)),
            per_step((ns, D_MEMQ, N_MEM)),
            full((1, D_MODEL)),
            full((1, D_MODEL)),
            pl.BlockSpec(memory_space=pltpu.VMEM),
            full((CONV_WIDTH, D_CONV)),
            pl.BlockSpec(memory_space=pltpu.SMEM),
            pl.BlockSpec(memory_space=pltpu.VMEM),
        ],
        out_specs=[
            per_group((nseq, R // 2, D_MODEL)),
            per_group((nseq, CONV_WIDTH - 1, D_CONV)),
            per_step((ns, D_SWA_KV, WINDOW)),
            per_step((ns, D_SWA_KV, WINDOW)),
        ],
        out_shape=[
            jax.ShapeDtypeStruct((N, R // 2, D_MODEL), jnp.float32),
            jax.ShapeDtypeStruct((N, CONV_WIDTH - 1, D_CONV), jnp.float32),
            jax.ShapeDtypeStruct((N, D_SWA_KV, WINDOW), jnp.float32),
            jax.ShapeDtypeStruct((N, D_SWA_KV, WINDOW), jnp.float32),
        ],
        scratch_shapes=[
            pltpu.VMEM((nseq * R, D_IN), jnp.float32),
            pltpu.VMEM((nseq * R, D_MODEL), jnp.float32),
        ],
        compiler_params=pltpu.CompilerParams(
            dimension_semantics=("arbitrary", "arbitrary"),
            vmem_limit_bytes=SAMPLE_VMEM_LIMIT_BYTES),
        name="sample_layer",
    )(x, conv_past, ckt, cvt, mkt, mvt, pre_g, post_g, w_in_t, conv_w, sinks, w_out)


def _heads_last_to_keys_last(a):
    n, keys, heads, dim = a.shape
    return jnp.transpose(a, (0, 2, 3, 1)).reshape(n, heads * dim, keys)


def _keys_last_to_heads_last(a, heads):
    n, hd, keys = a.shape
    return jnp.transpose(a.reshape(n, heads, hd // heads, keys), (0, 3, 1, 2))[None]


def kernel(x_prompt, x_sample, mem_prompt, state_conv, cache_swa_k, cache_swa_v, cache_mem_k, cache_mem_v,
           pre_norm_g, post_norm_g, w_in, conv_w, attn_sinks, mem_norm_g, w_mem_k, w_mem_v, w_out):
    assert w_in.shape[0] == 1, "one layer, as the problem states"
    N, TS, _ = x_sample.shape
    assert TS == SAMPLE_ROWS // 2 and cache_swa_k.shape[2] == WINDOW
    l = 0

    pre_g = pre_norm_g[l].reshape(1, D_MODEL)
    post_g = post_norm_g[l].reshape(1, D_MODEL)
    w_in_t = w_in[l].astype(jnp.bfloat16).T
    w_out_bf = w_out[l].astype(jnp.bfloat16)
    sinks = attn_sinks[l].astype(jnp.float32)

    y_s, conv_s, kt_s, vt_s = _sample_layer(
        x_sample, state_conv[l],
        _heads_last_to_keys_last(cache_swa_k[l]), _heads_last_to_keys_last(cache_swa_v[l]),
        _heads_last_to_keys_last(cache_mem_k[l]), _heads_last_to_keys_last(cache_mem_v[l]),
        pre_g, post_g, w_in_t, conv_w[l], sinks, w_out_bf, ns=SAMPLE_NS, group=SAMPLE_GROUP)

    mkt, mvt, mkb, mvtb = _mem_kv(mem_prompt, mem_norm_g[l], w_mem_k[l], w_mem_v[l])
    y_p, conv_p, kt_p, vt_p = _prompt_layer(
        x_prompt, pre_g, post_g, w_in_t, conv_w[l].T, sinks, mkb, mvtb, w_out_bf, conv_s,
        tq=PROMPT_TQ, nsub=PROMPT_NSUB)

    return (y_p, y_s,
            conv_p[None],
            _keys_last_to_heads_last(kt_p, N_SWA_KV), _keys_last_to_heads_last(vt_p, N_SWA_KV),
            _keys_last_to_heads_last(mkt, N_MEM_HEADS), _keys_last_to_heads_last(mvt, N_MEM_HEADS),
            conv_s[None],
            _keys_last_to_heads_last(kt_s, N_SWA_KV), _keys_last_to_heads_last(vt_s, N_SWA_KV))
```

```python
import functools

import numpy as np
import jax
import jax.numpy as jnp
from jax import lax
from jax.experimental import pallas as pl
from jax.experimental.pallas import tpu as pltpu

D_MODEL = 1024
HEAD_DIM = 64
D_CONV = 384
N_MEM_HEADS = 4
D_MEMQ = N_MEM_HEADS * HEAD_DIM
D_SWA = 384
N_SWA_HEADS = 6
N_SWA_KV = 2
SWA_GROUP = N_SWA_HEADS // N_SWA_KV
D_SWA_KV = N_SWA_KV * HEAD_DIM
N_MEM = 256
CONV_WIDTH = 3
WINDOW = 128
BLOCK = 128
RMS_EPS = 1e-6
NEG_INF = -1e30
D_IN = 3072
QK_SCALE = HEAD_DIM ** -0.5
LOG2_E = float(np.log2(np.e))
QK_SCALE_LOG2 = QK_SCALE * LOG2_E

OFF_CB, OFF_CC, OFF_CH, OFF_CZ = 0, 384, 768, 1152
OFF_Q, OFF_K, OFF_V, OFF_SZ = 1536, 1920, 2048, 2176
OFF_MQ, OFF_MZ = 2560, 2816
YOFF_CONV, YOFF_SWA, YOFF_MEM = 0, 384, 768

PAIR = 2 * HEAD_DIM

MIB = 1024 * 1024
PROMPT_VMEM_LIMIT_BYTES = 44 * MIB
SAMPLE_VMEM_LIMIT_BYTES = 48 * MIB

MEM_KV_BATCHES = 2
PROMPT_TQ = 512
PROMPT_NSUB = 2
IN_PROJ_CHUNK = 512
ATTN_WIDTH = 2
SAMPLE_NS = 16
SAMPLE_GROUP = 2
SAMPLE_ROWS = 8
SAMPLE_WIDTH = 16
SAMPLE_NEW = 16


def _alibi_slope(h):
    return float(np.power(np.float32(2.0), np.float32(-8.0 * (h + 1) / N_SWA_HEADS)))


def _rms_norm(x, g):
    return x * lax.rsqrt(jnp.mean(x * x, axis=-1, keepdims=True) + RMS_EPS) * g


def _silu(x):
    return x * jax.nn.sigmoid(x)


_NT = (((1,), (1,)), ((), ()))
_TN = (((0,), (0,)), ((), ()))


def _mem_kv_kernel(mem_ref, g_ref, wk_ref, wv_ref, mkt_ref, mvt_ref, mkb_ref, mvtb_ref, *, nb):
    mem = mem_ref[...].reshape(nb * N_MEM, D_MODEL)
    m = _rms_norm(mem, g_ref[...]).astype(jnp.bfloat16)
    mk = jnp.dot(m, wk_ref[...].astype(jnp.bfloat16), preferred_element_type=jnp.float32)
    mv = jnp.dot(m, wv_ref[...].astype(jnp.bfloat16), preferred_element_type=jnp.float32)
    for b in range(nb):
        mk_b = mk[b * N_MEM:(b + 1) * N_MEM]
        mv_t = mv[b * N_MEM:(b + 1) * N_MEM].T
        mkt_ref[b] = mk_b.T
        mvt_ref[b] = mv_t
        mkb_ref[b] = mk_b.astype(jnp.bfloat16)
        mvtb_ref[b] = mv_t.astype(jnp.bfloat16)


def _mem_kv(mem, mem_g, w_mk, w_mv):
    B = mem.shape[0]
    nb = MEM_KV_BATCHES
    full = lambda shape: pl.BlockSpec(shape, lambda b: (0,) * len(shape))
    per_batch = pl.BlockSpec((nb, N_MEM, D_MEMQ), lambda b: (b, 0, 0))
    return pl.pallas_call(
        functools.partial(_mem_kv_kernel, nb=nb),
        grid=(B // nb,),
        in_specs=[
            pl.BlockSpec((nb, N_MEM, D_MODEL), lambda b: (b, 0, 0)),
            full((1, D_MODEL)),
            full((D_MODEL, D_MEMQ)),
            full((D_MODEL, D_MEMQ)),
        ],
        out_specs=[per_batch] * 4,
        out_shape=[
            jax.ShapeDtypeStruct((B, D_MEMQ, N_MEM), jnp.float32),
            jax.ShapeDtypeStruct((B, D_MEMQ, N_MEM), jnp.float32),
            jax.ShapeDtypeStruct((B, N_MEM, D_MEMQ), jnp.bfloat16),
            jax.ShapeDtypeStruct((B, D_MEMQ, N_MEM), jnp.bfloat16),
        ],
        compiler_params=pltpu.CompilerParams(dimension_semantics=("arbitrary",)),
        name="mem_kv",
    )(mem, mem_g.reshape(1, D_MODEL), w_mk, w_mv)


def _prompt_kernel(x_ref, x_next_ref, pre_g_ref, post_g_ref, w_in_t_ref, conv_w_t_ref, sink_ref, mkb_ref, mvt_ref,
                   w_out_ref, run_after_ref,
                   y_ref, conv_state_ref, k_state_ref, v_state_ref,
                   zt_ref, ycat_ref, kbuf_ref, vbuf_ref, uprev_ref, bias_ref, *, tq, nsub):
    assert nsub % 2 == 0
    t = pl.program_id(1)
    nblk = tq // BLOCK

    @pl.when((pl.program_id(0) == 0) & (t == 0))
    def _():
        c = lax.broadcasted_iota(jnp.int32, (BLOCK, BLOCK), 0)
        r = lax.broadcasted_iota(jnp.int32, (BLOCK, BLOCK), 1)
        distf = (r - c + jnp.where(c > r, BLOCK, 0)).astype(jnp.float32)
        for h in range(N_SWA_HEADS):
            g, i = divmod(h, SWA_GROUP)
            bias_ref[g, :, i * BLOCK:(i + 1) * BLOCK] = (-_alibi_slope(h) * LOG2_E) * distf

    @pl.when(t == 0)
    def _():
        kbuf_ref[0:BLOCK, :] = jnp.zeros((BLOCK, D_SWA_KV), jnp.bfloat16)
        vbuf_ref[:, 0:BLOCK] = jnp.zeros((D_SWA_KV, BLOCK), jnp.bfloat16)
        uprev_ref[...] = jnp.zeros_like(uprev_ref)

    first_pen = jnp.where(t == 0, NEG_INF, 0.0)
    slot = lax.broadcasted_iota(jnp.int32, (BLOCK, SWA_GROUP * BLOCK), 0)
    query = lax.broadcasted_iota(jnp.int32, (BLOCK, SWA_GROUP * BLOCK), 1) % BLOCK
    from_prev = slot > query
    q_zero = jnp.zeros((HEAD_DIM, SWA_GROUP * BLOCK), jnp.bfloat16)
    cw = conv_w_t_ref[...]
    carry = {"u_prev": uprev_ref[...]}
    head_of_lane = lax.broadcasted_iota(jnp.int32, (1, SWA_GROUP * BLOCK), 1) // BLOCK
    sink_rows = []
    for g in range(N_SWA_KV):
        row = jnp.full((1, SWA_GROUP * BLOCK), sink_ref[g * SWA_GROUP], jnp.float32)
        for i in range(1, SWA_GROUP):
            row = jnp.where(head_of_lane == i, sink_ref[g * SWA_GROUP + i], row)
        sink_rows.append(row * LOG2_E)

    def in_proj_chunks(x_rows, zt):
        state = {}

        def norm():
            state["h"] = _rms_norm(x_rows(), pre_g_ref[...]).astype(jnp.bfloat16)

        def chunk(lo):
            rows = slice(lo, lo + IN_PROJ_CHUNK)
            zt[rows, :] = lax.dot_general(w_in_t_ref[rows, :], state["h"], _NT,
                                          preferred_element_type=jnp.float32)

        return norm, [functools.partial(chunk, lo) for lo in range(0, D_IN, IN_PROJ_CHUNK)]

    def mixer_units(sub):
        zt = zt_ref.at[sub % 2]
        ycat = ycat_ref.at[sub % 2]
        tok0 = sub * tq

        def conv_unit():
            u = zt[OFF_CC:OFF_CC + D_CONV, :] * zt[OFF_CH:OFF_CH + D_CONV, :]
            ucat = jnp.concatenate([carry["u_prev"], u], axis=1)
            conv = (cw[:, 0:1] * pltpu.roll(ucat, 2, axis=1)[:, BLOCK:]
                    + cw[:, 1:2] * pltpu.roll(ucat, 1, axis=1)[:, BLOCK:]
                    + cw[:, 2:3] * u)
            y_conv = zt[OFF_CB:OFF_CB + D_CONV, :] * conv * _silu(zt[OFF_CZ:OFF_CZ + D_CONV, :])
            ycat[YOFF_CONV:YOFF_CONV + D_CONV, :] = y_conv.astype(jnp.bfloat16)
            carry["u_prev"] = u[:, tq - BLOCK:]

        def kv_unit():
            k_nat = zt[OFF_K:OFF_K + D_SWA_KV, :].T
            kbuf_ref[BLOCK + tok0:BLOCK + tok0 + tq, :] = k_nat.astype(jnp.bfloat16)
            vbuf_ref[:, BLOCK + tok0:BLOCK + tok0 + tq] = zt[OFF_V:OFF_V + D_SWA_KV, :].astype(jnp.bfloat16)

        def swa_unit(j, g):
            cols = slice(j * BLOCK, (j + 1) * BLOCK)
            band = slice(tok0 + j * BLOCK, tok0 + (j + 2) * BLOCK)
            q0 = OFF_Q + g * SWA_GROUP * HEAD_DIM
            qt = jnp.concatenate(
                [zt[q0 + i * HEAD_DIM:q0 + (i + 1) * HEAD_DIM, cols] for i in range(SWA_GROUP)],
                axis=1)
            qt = (qt * QK_SCALE_LOG2).astype(jnp.bfloat16)
            qt = jnp.concatenate([qt, q_zero] if g == 0 else [q_zero, qt], axis=0)
            s = jnp.dot(kbuf_ref[band, :], qt, preferred_element_type=jnp.float32)
            yield
            s = jnp.where(from_prev, s[0:BLOCK], s[BLOCK:]) + bias_ref[g]
            if sub == 0 and j == 0:
                s = s + jnp.where(from_prev, first_pen, 0.0)
            sink = sink_rows[g]
            m = jnp.maximum(jnp.max(s, axis=0, keepdims=True), sink)
            p = jnp.exp2(s - m)
            l = jnp.sum(p, axis=0, keepdims=True) + jnp.exp2(sink - m)
            p = jnp.concatenate([jnp.where(from_prev, p, 0.0), jnp.where(from_prev, 0.0, p)],
                                axis=0).astype(jnp.bfloat16)
            yield
            vband = vbuf_ref[g * HEAD_DIM:(g + 1) * HEAD_DIM, band]
            o = jnp.dot(vband, p, preferred_element_type=jnp.float32)
            o = o / l
            for i in range(SWA_GROUP):
                hh = g * SWA_GROUP + i
                gate = _silu(zt[OFF_SZ + hh * HEAD_DIM:OFF_SZ + (hh + 1) * HEAD_DIM, cols])
                ycat[YOFF_SWA + hh * HEAD_DIM:YOFF_SWA + (hh + 1) * HEAD_DIM, cols] = (
                    o[:, i * BLOCK:(i + 1) * BLOCK] * gate).astype(jnp.bfloat16)

        def mem_unit(hh):
            rows = slice(OFF_MQ + hh * HEAD_DIM, OFF_MQ + (hh + 1) * HEAD_DIM)
            qt = (zt[rows, :] * QK_SCALE_LOG2).astype(jnp.bfloat16)
            pieces = [jnp.zeros((HEAD_DIM, tq), jnp.bfloat16)] * N_MEM_HEADS
            pieces[hh] = qt
            s = jnp.dot(mkb_ref[0], jnp.concatenate(pieces, axis=0),
                        preferred_element_type=jnp.float32)
            yield
            m = jnp.max(s, axis=0, keepdims=True)
            p = jnp.exp2(s - m)
            l = jnp.sum(p, axis=0, keepdims=True)
            p = p.astype(jnp.bfloat16)
            yield
            o = jnp.dot(mvt_ref[0, hh * HEAD_DIM:(hh + 1) * HEAD_DIM, :], p,
                        preferred_element_type=jnp.float32)
            gate = _silu(zt[OFF_MZ + hh * HEAD_DIM:OFF_MZ + (hh + 1) * HEAD_DIM, :])
            ycat[YOFF_MEM + hh * HEAD_DIM:YOFF_MEM + (hh + 1) * HEAD_DIM, :] = (
                o / l * gate).astype(jnp.bfloat16)

        def out_proj():
            carry["y"] = lax.dot_general(ycat[...], w_out_ref[...], _TN,
                                         preferred_element_type=jnp.float32)

        def post_norm(blk):
            rows = slice(blk * BLOCK, (blk + 1) * BLOCK)
            out_rows = slice(tok0 + blk * BLOCK, tok0 + (blk + 1) * BLOCK)
            y_ref[0, out_rows, :] = x_ref[0, out_rows, :] + _rms_norm(carry["y"][rows], post_g_ref[...])

        attention = [functools.partial(swa_unit, j, g) for j in range(nblk) for g in range(N_SWA_KV)]
        attention += [functools.partial(mem_unit, hh) for hh in range(N_MEM_HEADS)]
        return [conv_unit, kv_unit], attention, (out_proj, [functools.partial(post_norm, b) for b in range(nblk)])

    def interleave(units, chunks, norm_first, norm_after):
        (conv_unit, kv_unit), attention, (out_proj, post_norms) = units
        pending = list(chunks)
        kv_unit()
        n_rounds = -(-len(attention) // ATTN_WIDTH) + 2
        chunk_rounds = [k * n_rounds // len(pending) for k in range(len(pending))]
        todo = list(attention)
        active = []
        for rnd in range(n_rounds):
            for _ in range(ATTN_WIDTH):
                if todo:
                    active.append(todo.pop(0)())
            for gen in list(active):
                if next(gen, "done") == "done":
                    active.remove(gen)
            if rnd == 0 and norm_first is not None:
                norm_first()
            for _ in range(chunk_rounds.count(rnd)):
                pending.pop(0)()
            if rnd == 0:
                conv_unit()
        assert not todo and not active and not pending
        if norm_after is not None:
            norm_after()
        out_proj()
        for post_norm in post_norms:
            post_norm()

    @pl.when((pl.program_id(0) == 0) & (t == 0))
    def _():
        norm, chunks = in_proj_chunks(lambda: x_ref[0, 0:tq, :], zt_ref.at[0])
        norm()
        for chunk in chunks:
            chunk()

    proj = [in_proj_chunks(lambda sub=sub: x_ref[0, sub * tq:(sub + 1) * tq, :], zt_ref.at[sub % 2])
            for sub in range(1, nsub)]
    proj.append(in_proj_chunks(lambda: x_next_ref[0], zt_ref.at[0]))
    for sub in range(nsub):
        norm_first = proj[0][0] if sub == 0 else None
        norm_after = proj[sub + 1][0] if sub + 1 < nsub else None
        interleave(mixer_units(sub), proj[sub][1], norm_first, norm_after)

    tile = nsub * tq
    uprev_ref[...] = carry["u_prev"]
    conv_state_ref[0] = carry["u_prev"].T[BLOCK - (CONV_WIDTH - 1):, :]
    kbuf_ref[0:BLOCK, :] = kbuf_ref[tile:tile + BLOCK, :]
    vbuf_ref[:, 0:BLOCK] = vbuf_ref[:, tile:tile + BLOCK]
    k_state_ref[0] = zt_ref[1, OFF_K:OFF_K + D_SWA_KV, tq - BLOCK:]
    v_state_ref[0] = zt_ref[1, OFF_V:OFF_V + D_SWA_KV, tq - BLOCK:]


def _prompt_layer(x, pre_g, post_g, w_in_t, conv_w_t, sinks, mkb, mvtb, w_out, run_after, *, tq, nsub):
    B, T, _ = x.shape
    tile = tq * nsub
    kernel = functools.partial(_prompt_kernel, tq=tq, nsub=nsub)
    steps = T // tile

    def next_first_sub_tile(b, t):
        nxt = jnp.minimum(b * steps + t + 1, B * steps - 1)
        return nxt // steps, (nxt % steps) * nsub, 0

    resident = pl.BlockSpec(memory_space=pltpu.VMEM)
    return pl.pallas_call(
        kernel,
        grid=(B, steps),
        in_specs=[
            pl.BlockSpec((1, tile, D_MODEL), lambda b, t: (b, t, 0)),
            pl.BlockSpec((1, tq, D_MODEL), next_first_sub_tile),
            resident,
            resident,
            resident,
            resident,
            pl.BlockSpec(memory_space=pltpu.SMEM),
            pl.BlockSpec((1, N_MEM, D_MEMQ), lambda b, t: (b, 0, 0)),
            pl.BlockSpec((1, D_MEMQ, N_MEM), lambda b, t: (b, 0, 0)),
            resident,
            pl.BlockSpec(memory_space=pl.ANY),
        ],
        out_specs=[
            pl.BlockSpec((1, tile, D_MODEL), lambda b, t: (b, t, 0)),
            pl.BlockSpec((1, CONV_WIDTH - 1, D_CONV), lambda b, t: (b, 0, 0)),
            pl.BlockSpec((1, D_SWA_KV, BLOCK), lambda b, t: (b, 0, 0)),
            pl.BlockSpec((1, D_SWA_KV, BLOCK), lambda b, t: (b, 0, 0)),
        ],
        out_shape=[
            jax.ShapeDtypeStruct((B, T, D_MODEL), jnp.float32),
            jax.ShapeDtypeStruct((B, CONV_WIDTH - 1, D_CONV), jnp.float32),
            jax.ShapeDtypeStruct((B, D_SWA_KV, BLOCK), jnp.float32),
            jax.ShapeDtypeStruct((B, D_SWA_KV, BLOCK), jnp.float32),
        ],
        scratch_shapes=[
            pltpu.VMEM((2, D_IN, tq), jnp.float32),
            pltpu.VMEM((2, D_MODEL, tq), jnp.bfloat16),
            pltpu.VMEM((BLOCK + tile, D_SWA_KV), jnp.bfloat16),
            pltpu.VMEM((D_SWA_KV, BLOCK + tile), jnp.bfloat16),
            pltpu.VMEM((D_CONV, BLOCK), jnp.float32),
            pltpu.VMEM((N_SWA_KV, BLOCK, SWA_GROUP * BLOCK), jnp.float32),
        ],
        compiler_params=pltpu.CompilerParams(
            dimension_semantics=("arbitrary", "arbitrary"),
            vmem_limit_bytes=PROMPT_VMEM_LIMIT_BYTES),
        name="prompt_layer",
    )(x, x, pre_g, post_g, w_in_t, conv_w_t, sinks, mkb, mvtb, w_out, run_after)


def _sample_kernel(x_ref, conv_past_ref, ckt_ref, cvt_ref, mkt_ref, mvt_ref,
                   pre_g_ref, post_g_ref, w_in_t_ref, conv_w_ref, sink_ref, w_out_ref,
                   y_ref, conv_state_ref, kt_state_ref, vt_state_ref,
                   z_ref, ycat_ref, *, ns, group):
    R = SAMPLE_ROWS
    half = R // 2
    nseq = ns * group
    step = pl.program_id(1)

    @pl.when(step == 0)
    def _():
        x4 = x_ref[...]
        x8 = jnp.concatenate([x4, x4], axis=1).reshape(nseq * R, D_MODEL)
        h = _rms_norm(x8, pre_g_ref[...]).astype(jnp.bfloat16)
        z_ref[...] = lax.dot_general(h, w_in_t_ref[...], _NT, preferred_element_type=jnp.float32)
        u = (z_ref[:, OFF_CC:OFF_CC + D_CONV] * z_ref[:, OFF_CH:OFF_CH + D_CONV]).reshape(nseq, R, D_CONV)
        row3 = lax.broadcasted_iota(jnp.int32, (nseq, R, D_CONV), 1)
        past = conv_past_ref[...]
        past = jnp.concatenate([past, jnp.zeros((nseq, R - (CONV_WIDTH - 1), D_CONV), past.dtype)], axis=1)
        u_full = jnp.where(row3 < CONV_WIDTH - 1, past, pltpu.roll(u, CONV_WIDTH - 1, axis=1))
        cw = conv_w_ref[...]
        conv = (cw[0:1, :] * u_full
                + cw[1:2, :] * pltpu.roll(u_full, R - 1, axis=1)
                + cw[2:3, :] * pltpu.roll(u_full, R - 2, axis=1))
        conv_state_ref[...] = pltpu.roll(u_full, R - half, axis=1)[:, 0:CONV_WIDTH - 1, :]
        y_conv = (z_ref[:, OFF_CB:OFF_CB + D_CONV] * conv.reshape(nseq * R, D_CONV)
                  * _silu(z_ref[:, OFF_CZ:OFF_CZ + D_CONV]))
        ycat_ref[:, YOFF_CONV:YOFF_CONV + D_CONV] = y_conv

    row = lax.broadcasted_iota(jnp.int32, (R, PAIR), 0)
    lane = lax.broadcasted_iota(jnp.int32, (R, PAIR), 1)
    lo_row = row < half
    lo_lane = lane < HEAD_DIM
    diag = lo_row == lo_lane
    lane_sq = lax.broadcasted_iota(jnp.int32, (WINDOW, WINDOW), 1)

    def pair_bias(ncols, dist_of):
        rr = lax.broadcasted_iota(jnp.int32, (R, ncols), 0)
        cc = lax.broadcasted_iota(jnp.int32, (R, ncols), 1)
        dist, valid = dist_of(rr % half, cc)
        distf = dist.astype(jnp.float32)
        tiles = []
        for pair in range(N_SWA_HEADS // 2):
            slope = jnp.where(rr < half, _alibi_slope(2 * pair), _alibi_slope(2 * pair + 1))
            tiles.append(jnp.where(valid, -slope * distf, NEG_INF))
        return jnp.concatenate(tiles, axis=0)

    def cached_dist(tok, c):
        d = tok + WINDOW - c
        return d, d < WINDOW

    def new_dist(tok, c):
        d = tok - c
        return d, (d >= 0) & (c < half)

    bias_c = pair_bias(WINDOW, cached_dist)
    bias_n = pair_bias(SAMPLE_NEW, new_dist)
    head_of_row = lax.broadcasted_iota(jnp.int32, (N_SWA_HEADS * half, 1), 0) // half
    sink_col = jnp.full((N_SWA_HEADS * half, 1), sink_ref[0], jnp.float32)
    for hh in range(1, N_SWA_HEADS):
        sink_col = jnp.where(head_of_row == hh, sink_ref[hh], sink_col)
    state_pad = jnp.zeros((WINDOW - R, D_SWA_KV), jnp.float32)

    def seq_rows(n):
        return pl.ds(pl.multiple_of((step * ns + n) * R, R), R)

    def swa_unit(n):
        rows = seq_rows(n)
        qa = z_ref[rows, OFF_Q:OFF_Q + PAIR] * QK_SCALE
        qb = z_ref[rows, OFF_Q + PAIR:OFF_Q + 2 * PAIR] * QK_SCALE
        qc = z_ref[rows, OFF_Q + 2 * PAIR:OFF_Q + 3 * PAIR] * QK_SCALE
        t0 = jnp.where(lo_lane, jnp.where(lo_row, qa, pltpu.roll(qa, HEAD_DIM, axis=1)), 0.0)
        t1 = jnp.where(diag, qb, 0.0)
        t2 = jnp.where(lo_lane, 0.0, jnp.where(lo_row, pltpu.roll(qc, HEAD_DIM, axis=1), qc))
        qs = jnp.concatenate([t0, t1, t2], axis=0).astype(jnp.bfloat16)

        k_new = z_ref[rows, OFF_K:OFF_K + D_SWA_KV]
        v_new = z_ref[rows, OFF_V:OFF_V + D_SWA_KV]
        k_new_b = jnp.concatenate([k_new, k_new], axis=0).astype(jnp.bfloat16)
        v_new_b = jnp.concatenate([v_new, v_new], axis=0).astype(jnp.bfloat16)
        kt_old = ckt_ref[n].astype(jnp.bfloat16)

        s_c = jnp.dot(qs, kt_old, preferred_element_type=jnp.float32) + bias_c
        s_n = lax.dot_general(qs, k_new_b, _NT, preferred_element_type=jnp.float32) + bias_n
        yield
        m = jnp.maximum(jnp.maximum(jnp.max(s_c, axis=1, keepdims=True), jnp.max(s_n, axis=1, keepdims=True)),
                        sink_col)
        p_c = jnp.exp(s_c - m)
        p_n = jnp.exp(s_n - m)
        l = (jnp.sum(p_c, axis=1, keepdims=True) + jnp.sum(p_n, axis=1, keepdims=True) + jnp.exp(sink_col - m))
        p_c = p_c.astype(jnp.bfloat16)
        p_n = p_n.astype(jnp.bfloat16)
        yield
        o = (lax.dot_general(p_c, cvt_ref[n].astype(jnp.bfloat16), _NT, preferred_element_type=jnp.float32)
             + jnp.dot(p_n, v_new_b, preferred_element_type=jnp.float32)) / l
        yield
        o0, o1, o2 = o[0:R], o[R:2 * R], o[2 * R:3 * R]
        ya = jnp.where(lo_lane, o0, pltpu.roll(pltpu.roll(o0, HEAD_DIM, axis=1), half, axis=0))
        yb = jnp.where(lo_lane, o1, pltpu.roll(o1, half, axis=0))
        yc = jnp.where(lo_lane, pltpu.roll(o2, HEAD_DIM, axis=1), pltpu.roll(o2, half, axis=0))
        y_swa = jnp.concatenate([ya, yb, yc], axis=1) * _silu(z_ref[rows, OFF_SZ:OFF_SZ + D_SWA])
        ycat_ref[rows, YOFF_SWA:YOFF_SWA + D_SWA] = y_swa

    def state_unit(n):
        rows = seq_rows(n)
        k_new_t = jnp.concatenate([state_pad, z_ref[rows, OFF_K:OFF_K + D_SWA_KV]], axis=0).T
        v_new_t = jnp.concatenate([state_pad, z_ref[rows, OFF_V:OFF_V + D_SWA_KV]], axis=0).T
        keep = lane_sq < WINDOW - half
        kt_state_ref[n] = jnp.where(keep, pltpu.roll(ckt_ref[n], WINDOW - half, axis=1), k_new_t)
        vt_state_ref[n] = jnp.where(keep, pltpu.roll(cvt_ref[n], WINDOW - half, axis=1), v_new_t)

    def mem_unit(n):
        rows = seq_rows(n)
        m0 = z_ref[rows, OFF_MQ:OFF_MQ + PAIR] * QK_SCALE
        m1 = z_ref[rows, OFF_MQ + PAIR:OFF_MQ + 2 * PAIR] * QK_SCALE
        zero = jnp.zeros_like(m0)
        qm = jnp.concatenate(
            [jnp.concatenate([jnp.where(diag, m0, 0.0), zero], axis=1),
             jnp.concatenate([zero, jnp.where(diag, m1, 0.0)], axis=1)], axis=0).astype(jnp.bfloat16)
        s = jnp.dot(qm, mkt_ref[n].astype(jnp.bfloat16), preferred_element_type=jnp.float32)
        yield
        m = jnp.max(s, axis=1, keepdims=True)
        p = jnp.exp(s - m)
        l = jnp.sum(p, axis=1, keepdims=True)
        p = p.astype(jnp.bfloat16)
        yield
        o = lax.dot_general(p, mvt_ref[n].astype(jnp.bfloat16), _NT,
                            preferred_element_type=jnp.float32) / l
        yield
        oa, ob = o[0:R, 0:PAIR], o[R:2 * R, PAIR:2 * PAIR]
        y_mem = jnp.concatenate([jnp.where(lo_lane, oa, pltpu.roll(oa, half, axis=0)),
                                 jnp.where(lo_lane, ob, pltpu.roll(ob, half, axis=0))], axis=1)
        ycat_ref[rows, YOFF_MEM:YOFF_MEM + D_MEMQ] = y_mem * _silu(z_ref[rows, OFF_MZ:OFF_MZ + D_MEMQ])

    todo = [functools.partial(unit, n) for n in range(ns) for unit in (swa_unit, mem_unit)]
    states = [functools.partial(state_unit, n) for n in range(ns)]
    active = []
    while todo or active:
        for _ in range(SAMPLE_WIDTH):
            if todo:
                active.append(todo.pop(0)())
        for gen in list(active):
            if next(gen, "done") == "done":
                active.remove(gen)
        if states:
            states.pop(0)()
    for unit in states:
        unit()

    @pl.when(step == group - 1)
    def _():
        y = jnp.dot(ycat_ref[...].astype(jnp.bfloat16), w_out_ref[...], preferred_element_type=jnp.float32)
        y_ref[...] = x_ref[...] + _rms_norm(y, post_g_ref[...]).reshape(nseq, R, D_MODEL)[:, 0:half, :]


def _sample_layer(x, conv_past, ckt, cvt, mkt, mvt, pre_g, post_g, w_in_t, conv_w, sinks, w_out, *, ns, group):
    N = ckt.shape[0]
    R = SAMPLE_ROWS
    nseq = ns * group
    per_group = lambda shape: pl.BlockSpec(shape, lambda o, i: (o,) + (0,) * (len(shape) - 1))
    per_step = lambda shape: pl.BlockSpec(shape, lambda o, i: (o * group + i,) + (0,) * (len(shape) - 1))
    resident = pl.BlockSpec(memory_space=pltpu.VMEM)
    kernel = functools.partial(_sample_kernel, ns=ns, group=group)
    return pl.pallas_call(
        kernel,
        grid=(N // nseq, group),
        in_specs=[
            per_group((nseq, R // 2, D_MODEL)),
            per_group((nseq, CONV_WIDTH - 1, D_CONV)),
            per_step((ns, D_SWA_KV, WINDOW)),
            per_step((ns, D_SWA_KV, WINDOW)),
            per_step((ns, D_MEMQ, N_MEM)),
            per_step((ns, D_MEMQ, N_MEM)),
            resident,
            resident,
            resident,
            resident,
            pl.BlockSpec(memory_space=pltpu.SMEM),
            resident,
        ],
        out_specs=[
            per_group((nseq, R // 2, D_MODEL)),
            per_group((nseq, CONV_WIDTH - 1, D_CONV)),
            per_step((ns, D_SWA_KV, WINDOW)),
            per_step((ns, D_SWA_KV, WINDOW)),
        ],
        out_shape=[
            jax.ShapeDtypeStruct((N, R // 2, D_MODEL), jnp.float32),
            jax.ShapeDtypeStruct((N, CONV_WIDTH - 1, D_CONV), jnp.float32),
            jax.ShapeDtypeStruct((N, D_SWA_KV, WINDOW), jnp.float32),
            jax.ShapeDtypeStruct((N, D_SWA_KV, WINDOW), jnp.float32),
        ],
        scratch_shapes=[
            pltpu.VMEM((nseq * R, D_IN), jnp.float32),
            pltpu.VMEM((nseq * R, D_MODEL), jnp.float32),
        ],
        compiler_params=pltpu.CompilerParams(
            dimension_semantics=("arbitrary", "arbitrary"),
            vmem_limit_bytes=SAMPLE_VMEM_LIMIT_BYTES),
        name="sample_layer",
    )(x, conv_past, ckt, cvt, mkt, mvt, pre_g, post_g, w_in_t, conv_w, sinks, w_out)


def _heads_last_to_keys_last(a):
    n, keys, heads, dim = a.shape
    return jnp.transpose(a, (0, 2, 3, 1)).reshape(n, heads * dim, keys)


def _keys_last_to_heads_last(a, heads):
    n, hd, keys = a.shape
    return jnp.transpose(a.reshape(n, heads, hd // heads, keys), (0, 3, 1, 2))[None]


def kernel(x_prompt, x_sample, mem_prompt, state_conv, cache_swa_k, cache_swa_v, cache_mem_k, cache_mem_v,
           pre_norm_g, post_norm_g, w_in, conv_w, attn_sinks, mem_norm_g, w_mem_k, w_mem_v, w_out):
    assert w_in.shape[0] == 1, "one layer, as the problem states"
    N, TS, _ = x_sample.shape
    assert TS == SAMPLE_ROWS // 2 and cache_swa_k.shape[2] == WINDOW
    l = 0

    pre_g = pre_norm_g[l].reshape(1, D_MODEL)
    post_g = post_norm_g[l].reshape(1, D_MODEL)
    w_in_t = w_in[l].astype(jnp.bfloat16).T
    w_out_bf = w_out[l].astype(jnp.bfloat16)
    sinks = attn_sinks[l].astype(jnp.float32)

    y_s, conv_s, kt_s, vt_s = _sample_layer(
        x_sample, state_conv[l],
        _heads_last_to_keys_last(cache_swa_k[l]), _heads_last_to_keys_last(cache_swa_v[l]),
        _heads_last_to_keys_last(cache_mem_k[l]), _heads_last_to_keys_last(cache_mem_v[l]),
        pre_g, post_g, w_in_t, conv_w[l], sinks, w_out_bf, ns=SAMPLE_NS, group=SAMPLE_GROUP)

    mkt, mvt, mkb, mvtb = _mem_kv(mem_prompt, mem_norm_g[l], w_mem_k[l], w_mem_v[l])
    y_p, conv_p, kt_p, vt_p = _prompt_layer(
        x_prompt, pre_g, post_g, w_in_t, conv_w[l].T, sinks, mkb, mvtb, w_out_bf, conv_s,
        tq=PROMPT_TQ, nsub=PROMPT_NSUB)

    return (y_p, y_s,
            conv_p[None],
            _keys_last_to_heads_last(kt_p, N_SWA_KV), _keys_last_to_heads_last(vt_p, N_SWA_KV),
            _keys_last_to_heads_last(mkt, N_MEM_HEADS), _keys_last_to_heads_last(mvt, N_MEM_HEADS),
            conv_s[None],
            _keys_last_to_heads_last(kt_s, N_SWA_KV), _keys_last_to_heads_last(vt_s, N_SWA_KV))
```

```python
import functools

import numpy as np
import jax
import jax.numpy as jnp
from jax import lax
from jax.experimental import pallas as pl
from jax.experimental.pallas import tpu as pltpu

D_MODEL = 1024
HEAD_DIM = 64
D_CONV = 384
N_MEM_HEADS = 4
D_MEMQ = N_MEM_HEADS * HEAD_DIM
D_SWA = 384
N_SWA_HEADS = 6
N_SWA_KV = 2
SWA_GROUP = N_SWA_HEADS // N_SWA_KV
D_SWA_KV = N_SWA_KV * HEAD_DIM
N_MEM = 256
CONV_WIDTH = 3
WINDOW = 128
BLOCK = 128
RMS_EPS = 1e-6
NEG_INF = -1e30
D_IN = 3072
QK_SCALE = HEAD_DIM ** -0.5
LOG2_E = float(np.log2(np.e))
QK_SCALE_LOG2 = QK_SCALE * LOG2_E

OFF_CB, OFF_CC, OFF_CH, OFF_CZ = 0, 384, 768, 1152
OFF_Q, OFF_K, OFF_V, OFF_SZ = 1536, 1920, 2048, 2176
OFF_MQ, OFF_MZ = 2560, 2816
YOFF_CONV, YOFF_SWA, YOFF_MEM = 0, 384, 768

PAIR = 2 * HEAD_DIM

MIB = 1024 * 1024
PROMPT_VMEM_LIMIT_BYTES = 44 * MIB
SAMPLE_VMEM_LIMIT_BYTES = 48 * MIB

MEM_KV_BATCHES = 2
PROMPT_TQ = 512
PROMPT_NSUB = 2
IN_PROJ_CHUNK = 512
ATTN_WIDTH = 2
SAMPLE_NS = 16
SAMPLE_GROUP = 2
SAMPLE_ROWS = 8
SAMPLE_WIDTH = 16
SAMPLE_NEW = 16


def _alibi_slope(h):
    return float(np.power(np.float32(2.0), np.float32(-8.0 * (h + 1) / N_SWA_HEADS)))


def _rms_norm(x, g):
    return x * lax.rsqrt(jnp.mean(x * x, axis=-1, keepdims=True) + RMS_EPS) * g


def _silu(x):
    return x * jax.nn.sigmoid(x)


_NT = (((1,), (1,)), ((), ()))
_TN = (((0,), (0,)), ((), ()))


def _mem_kv_kernel(mem_ref, g_ref, wk_ref, wv_ref, mkt_ref, mvt_ref, mkb_ref, mvtb_ref, *, nb):
    mem = mem_ref[...].reshape(nb * N_MEM, D_MODEL)
    m = _rms_norm(mem, g_ref[...]).astype(jnp.bfloat16)
    mk = jnp.dot(m, wk_ref[...].astype(jnp.bfloat16), preferred_element_type=jnp.float32)
    mv = jnp.dot(m, wv_ref[...].astype(jnp.bfloat16), preferred_element_type=jnp.float32)
    for b in range(nb):
        mk_b = mk[b * N_MEM:(b + 1) * N_MEM]
        mv_t = mv[b * N_MEM:(b + 1) * N_MEM].T
        mkt_ref[b] = mk_b.T
        mvt_ref[b] = mv_t
        mkb_ref[b] = mk_b.astype(jnp.bfloat16)
        mvtb_ref[b] = mv_t.astype(jnp.bfloat16)


def _mem_kv(mem, mem_g, w_mk, w_mv):
    B = mem.shape[0]
    nb = MEM_KV_BATCHES
    full = lambda shape: pl.BlockSpec(shape, lambda b: (0,) * len(shape))
    per_batch = pl.BlockSpec((nb, N_MEM, D_MEMQ), lambda b: (b, 0, 0))
    return pl.pallas_call(
        functools.partial(_mem_kv_kernel, nb=nb),
        grid=(B // nb,),
        in_specs=[
            pl.BlockSpec((nb, N_MEM, D_MODEL), lambda b: (b, 0, 0)),
            full((1, D_MODEL)),
            full((D_MODEL, D_MEMQ)),
            full((D_MODEL, D_MEMQ)),
        ],
        out_specs=[per_batch] * 4,
        out_shape=[
            jax.ShapeDtypeStruct((B, D_MEMQ, N_MEM), jnp.float32),
            jax.ShapeDtypeStruct((B, D_MEMQ, N_MEM), jnp.float32),
            jax.ShapeDtypeStruct((B, N_MEM, D_MEMQ), jnp.bfloat16),
            jax.ShapeDtypeStruct((B, D_MEMQ, N_MEM), jnp.bfloat16),
        ],
        compiler_params=pltpu.CompilerParams(dimension_semantics=("arbitrary",)),
        name="mem_kv",
    )(mem, mem_g.reshape(1, D_MODEL), w_mk, w_mv)


def _prompt_kernel(x_ref, x_next_ref, pre_g_ref, post_g_ref, w_in_t_ref, conv_w_t_ref, sink_ref, mkb_ref, mvt_ref,
                   w_out_ref, run_after_ref,
                   y_ref, conv_state_ref, k_state_ref, v_state_ref,
                   zt_ref, ycat_ref, kbuf_ref, vbuf_ref, uprev_ref, bias_ref, *, tq, nsub):
    assert nsub % 2 == 0
    t = pl.program_id(1)
    nblk = tq // BLOCK

    @pl.when((pl.program_id(0) == 0) & (t == 0))
    def _():
        c = lax.broadcasted_iota(jnp.int32, (BLOCK, BLOCK), 0)
        r = lax.broadcasted_iota(jnp.int32, (BLOCK, BLOCK), 1)
        distf = (r - c + jnp.where(c > r, BLOCK, 0)).astype(jnp.float32)
        for h in range(N_SWA_HEADS):
            g, i = divmod(h, SWA_GROUP)
            bias_ref[g, :, i * BLOCK:(i + 1) * BLOCK] = (-_alibi_slope(h) * LOG2_E) * distf

    @pl.when(t == 0)
    def _():
        kbuf_ref[0:BLOCK, :] = jnp.zeros((BLOCK, D_SWA_KV), jnp.bfloat16)
        vbuf_ref[:, 0:BLOCK] = jnp.zeros((D_SWA_KV, BLOCK), jnp.bfloat16)
        uprev_ref[...] = jnp.zeros_like(uprev_ref)

    first_pen = jnp.where(t == 0, NEG_INF, 0.0)
    slot = lax.broadcasted_iota(jnp.int32, (BLOCK, SWA_GROUP * BLOCK), 0)
    query = lax.broadcasted_iota(jnp.int32, (BLOCK, SWA_GROUP * BLOCK), 1) % BLOCK
    from_prev = slot > query
    q_zero = jnp.zeros((HEAD_DIM, SWA_GROUP * BLOCK), jnp.bfloat16)
    cw = conv_w_t_ref[...]
    carry = {"u_prev": uprev_ref[...]}
    head_of_lane = lax.broadcasted_iota(jnp.int32, (1, SWA_GROUP * BLOCK), 1) // BLOCK
    sink_rows = []
    for g in range(N_SWA_KV):
        row = jnp.full((1, SWA_GROUP * BLOCK), sink_ref[g * SWA_GROUP], jnp.float32)
        for i in range(1, SWA_GROUP):
            row = jnp.where(head_of_lane == i, sink_ref[g * SWA_GROUP + i], row)
        sink_rows.append(row * LOG2_E)

    def in_proj_chunks(x_rows, zt):
        state = {}

        def norm():
            state["h"] = _rms_norm(x_rows(), pre_g_ref[...]).astype(jnp.bfloat16)

        def chunk(lo):
            rows = slice(lo, lo + IN_PROJ_CHUNK)
            zt[rows, :] = lax.dot_general(w_in_t_ref[rows, :], state["h"], _NT,
                                          preferred_element_type=jnp.float32)

        return norm, [functools.partial(chunk, lo) for lo in range(0, D_IN, IN_PROJ_CHUNK)]

    def mixer_units(sub):
        zt = zt_ref.at[sub % 2]
        ycat = ycat_ref.at[sub % 2]
        tok0 = sub * tq

        def conv_unit():
            u = zt[OFF_CC:OFF_CC + D_CONV, :] * zt[OFF_CH:OFF_CH + D_CONV, :]
            ucat = jnp.concatenate([carry["u_prev"], u], axis=1)
            conv = (cw[:, 0:1] * pltpu.roll(ucat, 2, axis=1)[:, BLOCK:]
                    + cw[:, 1:2] * pltpu.roll(ucat, 1, axis=1)[:, BLOCK:]
                    + cw[:, 2:3] * u)
            y_conv = zt[OFF_CB:OFF_CB + D_CONV, :] * conv * _silu(zt[OFF_CZ:OFF_CZ + D_CONV, :])
            ycat[YOFF_CONV:YOFF_CONV + D_CONV, :] = y_conv.astype(jnp.bfloat16)
            carry["u_prev"] = u[:, tq - BLOCK:]

        def kv_unit():
            k_nat = zt[OFF_K:OFF_K + D_SWA_KV, :].T
            kbuf_ref[BLOCK + tok0:BLOCK + tok0 + tq, :] = k_nat.astype(jnp.bfloat16)
            vbuf_ref[:, BLOCK + tok0:BLOCK + tok0 + tq] = zt[OFF_V:OFF_V + D_SWA_KV, :].astype(jnp.bfloat16)

        def swa_unit(j, g):
            cols = slice(j * BLOCK, (j + 1) * BLOCK)
            band = slice(tok0 + j * BLOCK, tok0 + (j + 2) * BLOCK)
            q0 = OFF_Q + g * SWA_GROUP * HEAD_DIM
            qt = jnp.concatenate(
                [zt[q0 + i * HEAD_DIM:q0 + (i + 1) * HEAD_DIM, cols] for i in range(SWA_GROUP)],
                axis=1)
            qt = (qt * QK_SCALE_LOG2).astype(jnp.bfloat16)
            qt = jnp.concatenate([qt, q_zero] if g == 0 else [q_zero, qt], axis=0)
            s = jnp.dot(kbuf_ref[band, :], qt, preferred_element_type=jnp.float32)
            yield
            s = jnp.where(from_prev, s[0:BLOCK], s[BLOCK:]) + bias_ref[g]
            if sub == 0 and j == 0:
                s = s + jnp.where(from_prev, first_pen, 0.0)
            sink = sink_rows[g]
            m = jnp.maximum(jnp.max(s, axis=0, keepdims=True), sink)
            p = jnp.exp2(s - m)
            l = jnp.sum(p, axis=0, keepdims=True) + jnp.exp2(sink - m)
            p = jnp.concatenate([jnp.where(from_prev, p, 0.0), jnp.where(from_prev, 0.0, p)],
                                axis=0).astype(jnp.bfloat16)
            yield
            vband = vbuf_ref[g * HEAD_DIM:(g + 1) * HEAD_DIM, band]
            o = jnp.dot(vband, p, preferred_element_type=jnp.float32)
            o = o / l
            for i in range(SWA_GROUP):
                hh = g * SWA_GROUP + i
                gate = _silu(zt[OFF_SZ + hh * HEAD_DIM:OFF_SZ + (hh + 1) * HEAD_DIM, cols])
                ycat[YOFF_SWA + hh * HEAD_DIM:YOFF_SWA + (hh + 1) * HEAD_DIM, cols] = (
                    o[:, i * BLOCK:(i + 1) * BLOCK] * gate).astype(jnp.bfloat16)

        def mem_unit(hh):
            rows = slice(OFF_MQ + hh * HEAD_DIM, OFF_MQ + (hh + 1) * HEAD_DIM)
            qt = (zt[rows, :] * QK_SCALE_LOG2).astype(jnp.bfloat16)
            pieces = [jnp.zeros((HEAD_DIM, tq), jnp.bfloat16)] * N_MEM_HEADS
            pieces[hh] = qt
            s = jnp.dot(mkb_ref[0], jnp.concatenate(pieces, axis=0),
                        preferred_element_type=jnp.float32)
            yield
            m = jnp.max(s, axis=0, keepdims=True)
            p = jnp.exp2(s - m)
            l = jnp.sum(p, axis=0, keepdims=True)
            p = p.astype(jnp.bfloat16)
            yield
            o = jnp.dot(mvt_ref[0, hh * HEAD_DIM:(hh + 1) * HEAD_DIM, :], p,
                        preferred_element_type=jnp.float32)
            gate = _silu(zt[OFF_MZ + hh * HEAD_DIM:OFF_MZ + (hh + 1) * HEAD_DIM, :])
            ycat[YOFF_MEM + hh * HEAD_DIM:YOFF_MEM + (hh + 1) * HEAD_DIM, :] = (
                o / l * gate).astype(jnp.bfloat16)

        def out_proj():
            carry["y"] = lax.dot_general(ycat[...], w_out_ref[...], _TN,
                                         preferred_element_type=jnp.float32)

        def post_norm(blk):
            rows = slice(blk * BLOCK, (blk + 1) * BLOCK)
            out_rows = slice(tok0 + blk * BLOCK, tok0 + (blk + 1) * BLOCK)
            y_ref[0, out_rows, :] = x_ref[0, out_rows, :] + _rms_norm(carry["y"][rows], post_g_ref[...])

        attention = [functools.partial(swa_unit, j, g) for j in range(nblk) for g in range(N_SWA_KV)]
        attention += [functools.partial(mem_unit, hh) for hh in range(N_MEM_HEADS)]
        return [conv_unit, kv_unit], attention, (out_proj, [functools.partial(post_norm, b) for b in range(nblk)])

    def interleave(units, chunks, norm_first, norm_after):
        (conv_unit, kv_unit), attention, (out_proj, post_norms) = units
        pending = list(chunks)
        kv_unit()
        n_rounds = -(-len(attention) // ATTN_WIDTH) + 2
        chunk_rounds = [k * n_rounds // len(pending) for k in range(len(pending))]
        todo = list(attention)
        active = []
        for rnd in range(n_rounds):
            for _ in range(ATTN_WIDTH):
                if todo:
                    active.append(todo.pop(0)())
            for gen in list(active):
                if next(gen, "done") == "done":
                    active.remove(gen)
            if rnd == 0 and norm_first is not None:
                norm_first()
            for _ in range(chunk_rounds.count(rnd)):
                pending.pop(0)()
            if rnd == 0:
                conv_unit()
        assert not todo and not active and not pending
        if norm_after is not None:
            norm_after()
        out_proj()
        for post_norm in post_norms:
            post_norm()

    @pl.when((pl.program_id(0) == 0) & (t == 0))
    def _():
        norm, chunks = in_proj_chunks(lambda: x_ref[0, 0:tq, :], zt_ref.at[0])
        norm()
        for chunk in chunks:
            chunk()

    proj = [in_proj_chunks(lambda sub=sub: x_ref[0, sub * tq:(sub + 1) * tq, :], zt_ref.at[sub % 2])
            for sub in range(1, nsub)]
    proj.append(in_proj_chunks(lambda: x_next_ref[0], zt_ref.at[0]))
    for sub in range(nsub):
        norm_first = proj[0][0] if sub == 0 else None
        norm_after = proj[sub + 1][0] if sub + 1 < nsub else None
        interleave(mixer_units(sub), proj[sub][1], norm_first, norm_after)

    tile = nsub * tq
    uprev_ref[...] = carry["u_prev"]
    conv_state_ref[0] = carry["u_prev"].T[BLOCK - (CONV_WIDTH - 1):, :]
    kbuf_ref[0:BLOCK, :] = kbuf_ref[tile:tile + BLOCK, :]
    vbuf_ref[:, 0:BLOCK] = vbuf_ref[:, tile:tile + BLOCK]
    k_state_ref[0] = zt_ref[1, OFF_K:OFF_K + D_SWA_KV, tq - BLOCK:]
    v_state_ref[0] = zt_ref[1, OFF_V:OFF_V + D_SWA_KV, tq - BLOCK:]


def _prompt_layer(x, pre_g, post_g, w_in_t, conv_w_t, sinks, mkb, mvtb, w_out, run_after, *, tq, nsub):
    B, T, _ = x.shape
    tile = tq * nsub
    kernel = functools.partial(_prompt_kernel, tq=tq, nsub=nsub)
    steps = T // tile

    def next_first_sub_tile(b, t):
        nxt = jnp.minimum(b * steps + t + 1, B * steps - 1)
        return nxt // steps, (nxt % steps) * nsub, 0

    resident = pl.BlockSpec(memory_space=pltpu.VMEM)
    return pl.pallas_call(
        kernel,
        grid=(B, steps),
        in_specs=[
            pl.BlockSpec((1, tile, D_MODEL), lambda b, t: (b, t, 0)),
            pl.BlockSpec((1, tq, D_MODEL), next_first_sub_tile),
            resident,
            resident,
            resident,
            resident,
            pl.BlockSpec(memory_space=pltpu.SMEM),
            pl.BlockSpec((1, N_MEM, D_MEMQ), lambda b, t: (b, 0, 0)),
            pl.BlockSpec((1, D_MEMQ, N_MEM), lambda b, t: (b, 0, 0)),
            resident,
            pl.BlockSpec(memory_space=pl.ANY),
        ],
        out_specs=[
            pl.BlockSpec((1, tile, D_MODEL), lambda b, t: (b, t, 0)),
            pl.BlockSpec((1, CONV_WIDTH - 1, D_CONV), lambda b, t: (b, 0, 0)),
            pl.BlockSpec((1, D_SWA_KV, BLOCK), lambda b, t: (b, 0, 0)),
            pl.BlockSpec((1, D_SWA_KV, BLOCK), lambda b, t: (b, 0, 0)),
        ],
        out_shape=[
            jax.ShapeDtypeStruct((B, T, D_MODEL), jnp.float32),
            jax.ShapeDtypeStruct((B, CONV_WIDTH - 1, D_CONV), jnp.float32),
            jax.ShapeDtypeStruct((B, D_SWA_KV, BLOCK), jnp.float32),
            jax.ShapeDtypeStruct((B, D_SWA_KV, BLOCK), jnp.float32),
        ],
        scratch_shapes=[
            pltpu.VMEM((2, D_IN, tq), jnp.float32),
            pltpu.VMEM((2, D_MODEL, tq), jnp.bfloat16),
            pltpu.VMEM((BLOCK + tile, D_SWA_KV), jnp.bfloat16),
            pltpu.VMEM((D_SWA_KV, BLOCK + tile), jnp.bfloat16),
            pltpu.VMEM((D_CONV, BLOCK), jnp.float32),
            pltpu.VMEM((N_SWA_KV, BLOCK, SWA_GROUP * BLOCK), jnp.float32),
        ],
        compiler_params=pltpu.CompilerParams(
            dimension_semantics=("arbitrary", "arbitrary"),
            vmem_limit_bytes=PROMPT_VMEM_LIMIT_BYTES),
        name="prompt_layer",
    )(x, x, pre_g, post_g, w_in_t, conv_w_t, sinks, mkb, mvtb, w_out, run_after)


def _sample_kernel(x_ref, conv_past_ref, ckt_ref, cvt_ref, mkt_ref, mvt_ref,
                   pre_g_ref, post_g_ref, w_in_t_ref, conv_w_ref, sink_ref, w_out_ref,
                   y_ref, conv_state_ref, kt_state_ref, vt_state_ref, w_out_bf_ref,
                   z_ref, ycat_ref, *, ns, group):
    R = SAMPLE_ROWS
    half = R // 2
    nseq = ns * group
    step = pl.program_id(1)

    @pl.when((pl.program_id(0) == 0) & (step == 0))
    def _():
        w_out_bf_ref[...] = w_out_ref[...].astype(jnp.bfloat16)

    @pl.when(step == 0)
    def _():
        x4 = x_ref[...]
        x8 = jnp.concatenate([x4, x4], axis=1).reshape(nseq * R, D_MODEL)
        h = _rms_norm(x8, pre_g_ref[...]).astype(jnp.bfloat16)
        z_ref[...] = lax.dot_general(h, w_in_t_ref[...], _NT, preferred_element_type=jnp.float32)
        u = (z_ref[:, OFF_CC:OFF_CC + D_CONV] * z_ref[:, OFF_CH:OFF_CH + D_CONV]).reshape(nseq, R, D_CONV)
        row3 = lax.broadcasted_iota(jnp.int32, (nseq, R, D_CONV), 1)
        past = conv_past_ref[...]
        past = jnp.concatenate([past, jnp.zeros((nseq, R - (CONV_WIDTH - 1), D_CONV), past.dtype)], axis=1)
        u_full = jnp.where(row3 < CONV_WIDTH - 1, past, pltpu.roll(u, CONV_WIDTH - 1, axis=1))
        cw = conv_w_ref[...]
        conv = (cw[0:1, :] * u_full
                + cw[1:2, :] * pltpu.roll(u_full, R - 1, axis=1)
                + cw[2:3, :] * pltpu.roll(u_full, R - 2, axis=1))
        conv_state_ref[...] = pltpu.roll(u_full, R - half, axis=1)[:, 0:CONV_WIDTH - 1, :]
        y_conv = (z_ref[:, OFF_CB:OFF_CB + D_CONV] * conv.reshape(nseq * R, D_CONV)
                  * _silu(z_ref[:, OFF_CZ:OFF_CZ + D_CONV]))
        ycat_ref[:, YOFF_CONV:YOFF_CONV + D_CONV] = y_conv

    row = lax.broadcasted_iota(jnp.int32, (R, PAIR), 0)
    lane = lax.broadcasted_iota(jnp.int32, (R, PAIR), 1)
    lo_row = row < half
    lo_lane = lane < HEAD_DIM
    diag = lo_row == lo_lane
    lane_sq = lax.broadcasted_iota(jnp.int32, (WINDOW, WINDOW), 1)

    def pair_bias(ncols, dist_of):
        rr = lax.broadcasted_iota(jnp.int32, (R, ncols), 0)
        cc = lax.broadcasted_iota(jnp.int32, (R, ncols), 1)
        dist, valid = dist_of(rr % half, cc)
        distf = dist.astype(jnp.float32)
        tiles = []
        for pair in range(N_SWA_HEADS // 2):
            slope = jnp.where(rr < half, _alibi_slope(2 * pair), _alibi_slope(2 * pair + 1))
            tiles.append(jnp.where(valid, -slope * distf, NEG_INF))
        return jnp.concatenate(tiles, axis=0)

    def cached_dist(tok, c):
        d = tok + WINDOW - c
        return d, d < WINDOW

    def new_dist(tok, c):
        d = tok - c
        return d, (d >= 0) & (c < half)

    bias_c = pair_bias(WINDOW, cached_dist)
    bias_n = pair_bias(SAMPLE_NEW, new_dist)
    head_of_row = lax.broadcasted_iota(jnp.int32, (N_SWA_HEADS * half, 1), 0) // half
    sink_col = jnp.full((N_SWA_HEADS * half, 1), sink_ref[0], jnp.float32)
    for hh in range(1, N_SWA_HEADS):
        sink_col = jnp.where(head_of_row == hh, sink_ref[hh], sink_col)
    state_pad = jnp.zeros((WINDOW - R, D_SWA_KV), jnp.float32)

    def seq_rows(n):
        return pl.ds(pl.multiple_of((step * ns + n) * R, R), R)

    def swa_unit(n):
        rows = seq_rows(n)
        qa = z_ref[rows, OFF_Q:OFF_Q + PAIR] * QK_SCALE
        qb = z_ref[rows, OFF_Q + PAIR:OFF_Q + 2 * PAIR] * QK_SCALE
        qc = z_ref[rows, OFF_Q + 2 * PAIR:OFF_Q + 3 * PAIR] * QK_SCALE
        t0 = jnp.where(lo_lane, jnp.where(lo_row, qa, pltpu.roll(qa, HEAD_DIM, axis=1)), 0.0)
        t1 = jnp.where(diag, qb, 0.0)
        t2 = jnp.where(lo_lane, 0.0, jnp.where(lo_row, pltpu.roll(qc, HEAD_DIM, axis=1), qc))
        qs = jnp.concatenate([t0, t1, t2], axis=0).astype(jnp.bfloat16)

        k_new = z_ref[rows, OFF_K:OFF_K + D_SWA_KV]
        v_new = z_ref[rows, OFF_V:OFF_V + D_SWA_KV]
        k_new_b = jnp.concatenate([k_new, k_new], axis=0).astype(jnp.bfloat16)
        v_new_b = jnp.concatenate([v_new, v_new], axis=0).astype(jnp.bfloat16)
        kt_old = ckt_ref[n].astype(jnp.bfloat16)

        s_c = jnp.dot(qs, kt_old, preferred_element_type=jnp.float32) + bias_c
        s_n = lax.dot_general(qs, k_new_b, _NT, preferred_element_type=jnp.float32) + bias_n
        yield
        m = jnp.maximum(jnp.maximum(jnp.max(s_c, axis=1, keepdims=True), jnp.max(s_n, axis=1, keepdims=True)),
                        sink_col)
        p_c = jnp.exp(s_c - m)
        p_n = jnp.exp(s_n - m)
        l = (jnp.sum(p_c, axis=1, keepdims=True) + jnp.sum(p_n, axis=1, keepdims=True) + jnp.exp(sink_col - m))
        p_c = p_c.astype(jnp.bfloat16)
        p_n = p_n.astype(jnp.bfloat16)
        yield
        o = (lax.dot_general(p_c, cvt_ref[n].astype(jnp.bfloat16), _NT, preferred_element_type=jnp.float32)
             + jnp.dot(p_n, v_new_b, preferred_element_type=jnp.float32)) / l
        yield
        o0, o1, o2 = o[0:R], o[R:2 * R], o[2 * R:3 * R]
        ya = jnp.where(lo_lane, o0, pltpu.roll(pltpu.roll(o0, HEAD_DIM, axis=1), half, axis=0))
        yb = jnp.where(lo_lane, o1, pltpu.roll(o1, half, axis=0))
        yc = jnp.where(lo_lane, pltpu.roll(o2, HEAD_DIM, axis=1), pltpu.roll(o2, half, axis=0))
        y_swa = jnp.concatenate([ya, yb, yc], axis=1) * _silu(z_ref[rows, OFF_SZ:OFF_SZ + D_SWA])
        ycat_ref[rows, YOFF_SWA:YOFF_SWA + D_SWA] = y_swa

    def state_unit(n):
        rows = seq_rows(n)
        k_new_t = jnp.concatenate([state_pad, z_ref[rows, OFF_K:OFF_K + D_SWA_KV]], axis=0).T
        v_new_t = jnp.concatenate([state_pad, z_ref[rows, OFF_V:OFF_V + D_SWA_KV]], axis=0).T
        keep = lane_sq < WINDOW - half
        kt_state_ref[n] = jnp.where(keep, pltpu.roll(ckt_ref[n], WINDOW - half, axis=1), k_new_t)
        vt_state_ref[n] = jnp.where(keep, pltpu.roll(cvt_ref[n], WINDOW - half, axis=1), v_new_t)

    def mem_unit(n):
        rows = seq_rows(n)
        m0 = z_ref[rows, OFF_MQ:OFF_MQ + PAIR] * QK_SCALE
        m1 = z_ref[rows, OFF_MQ + PAIR:OFF_MQ + 2 * PAIR] * QK_SCALE
        zero = jnp.zeros_like(m0)
        qm = jnp.concatenate(
            [jnp.concatenate([jnp.where(diag, m0, 0.0), zero], axis=1),
             jnp.concatenate([zero, jnp.where(diag, m1, 0.0)], axis=1)], axis=0).astype(jnp.bfloat16)
        s = jnp.dot(qm, mkt_ref[n].astype(jnp.bfloat16), preferred_element_type=jnp.float32)
        yield
        m = jnp.max(s, axis=1, keepdims=True)
        p = jnp.exp(s - m)
        l = jnp.sum(p, axis=1, keepdims=True)
        p = p.astype(jnp.bfloat16)
        yield
        o = lax.dot_general(p, mvt_ref[n].astype(jnp.bfloat16), _NT,
                            preferred_element_type=jnp.float32) / l
        yield
        oa, ob = o[0:R, 0:PAIR], o[R:2 * R, PAIR:2 * PAIR]
        y_mem = jnp.concatenate([jnp.where(lo_lane, oa, pltpu.roll(oa, half, axis=0)),
                                 jnp.where(lo_lane, ob, pltpu.roll(ob, half, axis=0))], axis=1)
        ycat_ref[rows, YOFF_MEM:YOFF_MEM + D_MEMQ] = y_mem * _silu(z_ref[rows, OFF_MZ:OFF_MZ + D_MEMQ])

    todo = [functools.partial(unit, n) for n in range(ns) for unit in (swa_unit, mem_unit)]
    states = [functools.partial(state_unit, n) for n in range(ns)]
    active = []
    while todo or active:
        for _ in range(SAMPLE_WIDTH):
            if todo:
                active.append(todo.pop(0)())
        for gen in list(active):
            if next(gen, "done") == "done":
                active.remove(gen)
        if states:
            states.pop(0)()
    for unit in states:
        unit()

    @pl.when(step == group - 1)
    def _():
        y = jnp.dot(ycat_ref[...].astype(jnp.bfloat16), w_out_bf_ref[...], preferred_element_type=jnp.float32)
        y_ref[...] = x_ref[...] + _rms_norm(y, post_g_ref[...]).reshape(nseq, R, D_MODEL)[:, 0:half, :]


def _sample_layer(x, conv_past, ckt, cvt, mkt, mvt, pre_g, post_g, w_in_t, conv_w, sinks, w_out, *, ns, group):
    N = ckt.shape[0]
    R = SAMPLE_ROWS
    nseq = ns * group
    per_group = lambda shape: pl.BlockSpec(shape, lambda o, i: (o,) + (0,) * (len(shape) - 1))
    per_step = lambda shape: pl.BlockSpec(shape, lambda o, i: (o * group + i,) + (0,) * (len(shape) - 1))
    resident = pl.BlockSpec(memory_space=pltpu.VMEM)
    kernel = functools.partial(_sample_kernel, ns=ns, group=group)
    return pl.pallas_call(
        kernel,
        grid=(N // nseq, group),
        in_specs=[
            per_group((nseq, R // 2, D_MODEL)),
            per_group((nseq, CONV_WIDTH - 1, D_CONV)),
            per_step((ns, D_SWA_KV, WINDOW)),
            per_step((ns, D_SWA_KV, WINDOW)),
            per_step((ns, D_MEMQ, N_MEM)),
            per_step((ns, D_MEMQ, N_MEM)),
            resident,
            resident,
            resident,
            resident,
            pl.BlockSpec(memory_space=pltpu.SMEM),
            resident,
        ],
        out_specs=[
            per_group((nseq, R // 2, D_MODEL)),
            per_group((nseq, CONV_WIDTH - 1, D_CONV)),
            per_step((ns, D_SWA_KV, WINDOW)),
            per_step((ns, D_SWA_KV, WINDOW)),
            pl.BlockSpec((D_MODEL, D_MODEL), lambda o, i: (0, 0)),
        ],
        out_shape=[
            jax.ShapeDtypeStruct((N, R // 2, D_MODEL), jnp.float32),
            jax.ShapeDtypeStruct((N, CONV_WIDTH - 1, D_CONV), jnp.float32),
            jax.ShapeDtypeStruct((N, D_SWA_KV, WINDOW), jnp.float32),
            jax.ShapeDtypeStruct((N, D_SWA_KV, WINDOW), jnp.float32),
            jax.ShapeDtypeStruct((D_MODEL, D_MODEL), jnp.bfloat16),
        ],
        scratch_shapes=[
            pltpu.VMEM((nseq * R, D_IN), jnp.float32),
            pltpu.VMEM((nseq * R, D_MODEL), jnp.float32),
        ],
        compiler_params=pltpu.CompilerParams(
            dimension_semantics=("arbitrary", "arbitrary"),
            vmem_limit_bytes=SAMPLE_VMEM_LIMIT_BYTES),
        name="sample_layer",
    )(x, conv_past, ckt, cvt, mkt, mvt, pre_g, post_g, w_in_t, conv_w, sinks, w_out)


def _heads_last_to_keys_last(a):
    n, keys, heads, dim = a.shape
    return jnp.transpose(a, (0, 2, 3, 1)).reshape(n, heads * dim, keys)


def _keys_last_to_heads_last(a, heads):
    n, hd, keys = a.shape
    return jnp.transpose(a.reshape(n, heads, hd // heads, keys), (0, 3, 1, 2))[None]


def kernel(x_prompt, x_sample, mem_prompt, state_conv, cache_swa_k, cache_swa_v, cache_mem_k, cache_mem_v,
           pre_norm_g, post_norm_g, w_in, conv_w, attn_sinks, mem_norm_g, w_mem_k, w_mem_v, w_out):
    assert w_in.shape[0] == 1, "one layer, as the problem states"
    N, TS, _ = x_sample.shape
    assert TS == SAMPLE_ROWS // 2 and cache_swa_k.shape[2] == WINDOW
    l = 0

    pre_g = pre_norm_g[l].reshape(1, D_MODEL)
    post_g = post_norm_g[l].reshape(1, D_MODEL)
    w_in_t = w_in[l].astype(jnp.bfloat16).T
    sinks = attn_sinks[l].astype(jnp.float32)

    y_s, conv_s, kt_s, vt_s, w_out_bf = _sample_layer(
        x_sample, state_conv[l],
        _heads_last_to_keys_last(cache_swa_k[l]), _heads_last_to_keys_last(cache_swa_v[l]),
        _heads_last_to_keys_last(cache_mem_k[l]), _heads_last_to_keys_last(cache_mem_v[l]),
        pre_g, post_g, w_in_t, conv_w[l], sinks, w_out[l], ns=SAMPLE_NS, group=SAMPLE_GROUP)

    mkt, mvt, mkb, mvtb = _mem_kv(mem_prompt, mem_norm_g[l], w_mem_k[l], w_mem_v[l])
    y_p, conv_p, kt_p, vt_p = _prompt_layer(
        x_prompt, pre_g, post_g, w_in_t, conv_w[l].T, sinks, mkb, mvtb, w_out_bf, conv_s,
        tq=PROMPT_TQ, nsub=PROMPT_NSUB)

    return (y_p, y_s,
            conv_p[None],
            _keys_last_to_heads_last(kt_p, N_SWA_KV), _keys_last_to_heads_last(vt_p, N_SWA_KV),
            _keys_last_to_heads_last(mkt, N_MEM_HEADS), _keys_last_to_heads_last(mvt, N_MEM_HEADS),
            conv_s[None],
            _keys_last_to_heads_last(kt_s, N_SWA_KV), _keys_last_to_heads_last(vt_s, N_SWA_KV))
```

```python
import functools

import numpy as np
import jax
import jax.numpy as jnp
from jax import lax
from jax.experimental import pallas as pl
from jax.experimental.pallas import tpu as pltpu

D_MODEL = 1024
HEAD_DIM = 64
D_CONV = 384
N_MEM_HEADS = 4
D_MEMQ = N_MEM_HEADS * HEAD_DIM
D_SWA = 384
N_SWA_HEADS = 6
N_SWA_KV = 2
SWA_GROUP = N_SWA_HEADS // N_SWA_KV
D_SWA_KV = N_SWA_KV * HEAD_DIM
N_MEM = 256
CONV_WIDTH = 3
WINDOW = 128
BLOCK = 128
RMS_EPS = 1e-6
NEG_INF = -1e30
D_IN = 3072
QK_SCALE = HEAD_DIM ** -0.5
LOG2_E = float(np.log2(np.e))
QK_SCALE_LOG2 = QK_SCALE * LOG2_E

OFF_CB, OFF_CC, OFF_CH, OFF_CZ = 0, 384, 768, 1152
OFF_Q, OFF_K, OFF_V, OFF_SZ = 1536, 1920, 2048, 2176
OFF_MQ, OFF_MZ = 2560, 2816
YOFF_CONV, YOFF_SWA, YOFF_MEM = 0, 384, 768

PAIR = 2 * HEAD_DIM

MIB = 1024 * 1024
PROMPT_VMEM_LIMIT_BYTES = 44 * MIB
SAMPLE_VMEM_LIMIT_BYTES = 48 * MIB

MEM_KV_BATCHES = 2
PROMPT_TQ = 512
PROMPT_NSUB = 2
IN_PROJ_CHUNK = 512
ATTN_WIDTH = 2
SAMPLE_NS = 16
SAMPLE_GROUP = 2
SAMPLE_ROWS = 8
SAMPLE_WIDTH = 16
SAMPLE_NEW = 16


def _alibi_slope(h):
    return float(np.power(np.float32(2.0), np.float32(-8.0 * (h + 1) / N_SWA_HEADS)))


def _rms_norm(x, g):
    return x * lax.rsqrt(jnp.mean(x * x, axis=-1, keepdims=True) + RMS_EPS) * g


def _silu(x):
    return x * jax.nn.sigmoid(x)


_NT = (((1,), (1,)), ((), ()))
_TN = (((0,), (0,)), ((), ()))


def _mem_kv_kernel(mem_ref, g_ref, wk_ref, wv_ref, mkt_ref, mvt_ref, mkb_ref, mvtb_ref, *, nb):
    mem = mem_ref[...].reshape(nb * N_MEM, D_MODEL)
    m = _rms_norm(mem, g_ref[...]).astype(jnp.bfloat16)
    mk = jnp.dot(m, wk_ref[...].astype(jnp.bfloat16), preferred_element_type=jnp.float32)
    mv = jnp.dot(m, wv_ref[...].astype(jnp.bfloat16), preferred_element_type=jnp.float32)
    for b in range(nb):
        mk_b = mk[b * N_MEM:(b + 1) * N_MEM]
        mv_t = mv[b * N_MEM:(b + 1) * N_MEM].T
        mkt_ref[b] = mk_b.T
        mvt_ref[b] = mv_t
        mkb_ref[b] = mk_b.astype(jnp.bfloat16)
        mvtb_ref[b] = mv_t.astype(jnp.bfloat16)


def _mem_kv(mem, mem_g, w_mk, w_mv):
    B = mem.shape[0]
    nb = MEM_KV_BATCHES
    full = lambda shape: pl.BlockSpec(shape, lambda b: (0,) * len(shape))
    per_batch = pl.BlockSpec((nb, N_MEM, D_MEMQ), lambda b: (b, 0, 0))
    return pl.pallas_call(
        functools.partial(_mem_kv_kernel, nb=nb),
        grid=(B // nb,),
        in_specs=[
            pl.BlockSpec((nb, N_MEM, D_MODEL), lambda b: (b, 0, 0)),
            full((1, D_MODEL)),
            full((D_MODEL, D_MEMQ)),
            full((D_MODEL, D_MEMQ)),
        ],
        out_specs=[per_batch] * 4,
        out_shape=[
            jax.ShapeDtypeStruct((B, D_MEMQ, N_MEM), jnp.float32),
            jax.ShapeDtypeStruct((B, D_MEMQ, N_MEM), jnp.float32),
            jax.ShapeDtypeStruct((B, N_MEM, D_MEMQ), jnp.bfloat16),
            jax.ShapeDtypeStruct((B, D_MEMQ, N_MEM), jnp.bfloat16),
        ],
        compiler_params=pltpu.CompilerParams(dimension_semantics=("arbitrary",)),
        name="mem_kv",
    )(mem, mem_g.reshape(1, D_MODEL), w_mk, w_mv)


def _prompt_kernel(x_ref, x_next_ref, pre_g_ref, post_g_ref, w_in_t_ref, conv_w_t_ref, sink_ref, mkb_ref, mvt_ref,
                   w_out_ref, run_after_ref,
                   y_ref, conv_state_ref, k_state_ref, v_state_ref,
                   zt_ref, ycat_ref, kbuf_ref, vbuf_ref, uprev_ref, bias_ref, *, tq, nsub):
    assert nsub % 2 == 0
    t = pl.program_id(1)
    nblk = tq // BLOCK

    @pl.when((pl.program_id(0) == 0) & (t == 0))
    def _():
        c = lax.broadcasted_iota(jnp.int32, (BLOCK, BLOCK), 0)
        r = lax.broadcasted_iota(jnp.int32, (BLOCK, BLOCK), 1)
        distf = (r - c + jnp.where(c > r, BLOCK, 0)).astype(jnp.float32)
        for h in range(N_SWA_HEADS):
            g, i = divmod(h, SWA_GROUP)
            bias_ref[g, :, i * BLOCK:(i + 1) * BLOCK] = (-_alibi_slope(h) * LOG2_E) * distf

    @pl.when(t == 0)
    def _():
        kbuf_ref[0:BLOCK, :] = jnp.zeros((BLOCK, D_SWA_KV), jnp.bfloat16)
        vbuf_ref[:, 0:BLOCK] = jnp.zeros((D_SWA_KV, BLOCK), jnp.bfloat16)
        uprev_ref[...] = jnp.zeros_like(uprev_ref)

    first_pen = jnp.where(t == 0, NEG_INF, 0.0)
    slot = lax.broadcasted_iota(jnp.int32, (BLOCK, SWA_GROUP * BLOCK), 0)
    query = lax.broadcasted_iota(jnp.int32, (BLOCK, SWA_GROUP * BLOCK), 1) % BLOCK
    from_prev = slot > query
    q_zero = jnp.zeros((HEAD_DIM, SWA_GROUP * BLOCK), jnp.bfloat16)
    cw = conv_w_t_ref[...]
    carry = {"u_prev": uprev_ref[...]}
    head_of_lane = lax.broadcasted_iota(jnp.int32, (1, SWA_GROUP * BLOCK), 1) // BLOCK
    sink_rows = []
    for g in range(N_SWA_KV):
        row = jnp.full((1, SWA_GROUP * BLOCK), sink_ref[g * SWA_GROUP], jnp.float32)
        for i in range(1, SWA_GROUP):
            row = jnp.where(head_of_lane == i, sink_ref[g * SWA_GROUP + i], row)
        sink_rows.append(row * LOG2_E)

    def in_proj_chunks(x_rows, zt):
        state = {}

        def norm():
            state["h"] = _rms_norm(x_rows(), pre_g_ref[...]).astype(jnp.bfloat16)

        def chunk(lo):
            rows = slice(lo, lo + IN_PROJ_CHUNK)
            zt[rows, :] = lax.dot_general(w_in_t_ref[rows, :], state["h"], _NT,
                                          preferred_element_type=jnp.float32)

        return norm, [functools.partial(chunk, lo) for lo in range(0, D_IN, IN_PROJ_CHUNK)]

    def mixer_units(sub):
        zt = zt_ref.at[sub % 2]
        ycat = ycat_ref.at[sub % 2]
        tok0 = sub * tq

        def conv_unit():
            u = zt[OFF_CC:OFF_CC + D_CONV, :] * zt[OFF_CH:OFF_CH + D_CONV, :]
            ucat = jnp.concatenate([carry["u_prev"], u], axis=1)
            conv = (cw[:, 0:1] * pltpu.roll(ucat, 2, axis=1)[:, BLOCK:]
                    + cw[:, 1:2] * pltpu.roll(ucat, 1, axis=1)[:, BLOCK:]
                    + cw[:, 2:3] * u)
            y_conv = zt[OFF_CB:OFF_CB + D_CONV, :] * conv * _silu(zt[OFF_CZ:OFF_CZ + D_CONV, :])
            ycat[YOFF_CONV:YOFF_CONV + D_CONV, :] = y_conv.astype(jnp.bfloat16)
            carry["u_prev"] = u[:, tq - BLOCK:]

        def kv_unit():
            k_nat = zt[OFF_K:OFF_K + D_SWA_KV, :].T
            kbuf_ref[BLOCK + tok0:BLOCK + tok0 + tq, :] = k_nat.astype(jnp.bfloat16)
            vbuf_ref[:, BLOCK + tok0:BLOCK + tok0 + tq] = zt[OFF_V:OFF_V + D_SWA_KV, :].astype(jnp.bfloat16)

        def swa_unit(j, g):
            cols = slice(j * BLOCK, (j + 1) * BLOCK)
            band = slice(tok0 + j * BLOCK, tok0 + (j + 2) * BLOCK)
            q0 = OFF_Q + g * SWA_GROUP * HEAD_DIM
            qt = jnp.concatenate(
                [zt[q0 + i * HEAD_DIM:q0 + (i + 1) * HEAD_DIM, cols] for i in range(SWA_GROUP)],
                axis=1)
            qt = (qt * QK_SCALE_LOG2).astype(jnp.bfloat16)
            qt = jnp.concatenate([qt, q_zero] if g == 0 else [q_zero, qt], axis=0)
            s = jnp.dot(kbuf_ref[band, :], qt, preferred_element_type=jnp.float32)
            yield
            s = jnp.where(from_prev, s[0:BLOCK], s[BLOCK:]) + bias_ref[g]
            if sub == 0 and j == 0:
                s = s + jnp.where(from_prev, first_pen, 0.0)
            sink = sink_rows[g]
            m = jnp.maximum(jnp.max(s, axis=0, keepdims=True), sink)
            p = jnp.exp2(s - m)
            l = jnp.sum(p, axis=0, keepdims=True) + jnp.exp2(sink - m)
            p = jnp.concatenate([jnp.where(from_prev, p, 0.0), jnp.where(from_prev, 0.0, p)],
                                axis=0).astype(jnp.bfloat16)
            yield
            vband = vbuf_ref[g * HEAD_DIM:(g + 1) * HEAD_DIM, band]
            o = jnp.dot(vband, p, preferred_element_type=jnp.float32)
            o = o / l
            for i in range(SWA_GROUP):
                hh = g * SWA_GROUP + i
                gate = _silu(zt[OFF_SZ + hh * HEAD_DIM:OFF_SZ + (hh + 1) * HEAD_DIM, cols])
                ycat[YOFF_SWA + hh * HEAD_DIM:YOFF_SWA + (hh + 1) * HEAD_DIM, cols] = (
                    o[:, i * BLOCK:(i + 1) * BLOCK] * gate).astype(jnp.bfloat16)

        def mem_unit(hh):
            rows = slice(OFF_MQ + hh * HEAD_DIM, OFF_MQ + (hh + 1) * HEAD_DIM)
            qt = (zt[rows, :] * QK_SCALE_LOG2).astype(jnp.bfloat16)
            pieces = [jnp.zeros((HEAD_DIM, tq), jnp.bfloat16)] * N_MEM_HEADS
            pieces[hh] = qt
            s = jnp.dot(mkb_ref[0], jnp.concatenate(pieces, axis=0),
                        preferred_element_type=jnp.float32)
            yield
            m = jnp.max(s, axis=0, keepdims=True)
            p = jnp.exp2(s - m)
            l = jnp.sum(p, axis=0, keepdims=True)
            p = p.astype(jnp.bfloat16)
            yield
            o = jnp.dot(mvt_ref[0, hh * HEAD_DIM:(hh + 1) * HEAD_DIM, :], p,
                        preferred_element_type=jnp.float32)
            gate = _silu(zt[OFF_MZ + hh * HEAD_DIM:OFF_MZ + (hh + 1) * HEAD_DIM, :])
            ycat[YOFF_MEM + hh * HEAD_DIM:YOFF_MEM + (hh + 1) * HEAD_DIM, :] = (
                o / l * gate).astype(jnp.bfloat16)

        def out_proj():
            carry["y"] = lax.dot_general(ycat[...], w_out_ref[...], _TN,
                                         preferred_element_type=jnp.float32)

        def post_norm(blk):
            rows = slice(blk * BLOCK, (blk + 1) * BLOCK)
            out_rows = slice(tok0 + blk * BLOCK, tok0 + (blk + 1) * BLOCK)
            y_ref[0, out_rows, :] = x_ref[0, out_rows, :] + _rms_norm(carry["y"][rows], post_g_ref[...])

        attention = [functools.partial(swa_unit, j, g) for j in range(nblk) for g in range(N_SWA_KV)]
        attention += [functools.partial(mem_unit, hh) for hh in range(N_MEM_HEADS)]
        return [conv_unit, kv_unit], attention, (out_proj, [functools.partial(post_norm, b) for b in range(nblk)])

    def interleave(units, chunks, norm_first, norm_after):
        (conv_unit, kv_unit), attention, (out_proj, post_norms) = units
        pending = list(chunks)
        kv_unit()
        n_rounds = -(-len(attention) // ATTN_WIDTH) + 2
        chunk_rounds = [k * n_rounds // len(pending) for k in range(len(pending))]
        todo = list(attention)
        active = []
        for rnd in range(n_rounds):
            for _ in range(ATTN_WIDTH):
                if todo:
                    active.append(todo.pop(0)())
            for gen in list(active):
                if next(gen, "done") == "done":
                    active.remove(gen)
            if rnd == 0 and norm_first is not None:
                norm_first()
            for _ in range(chunk_rounds.count(rnd)):
                pending.pop(0)()
            if rnd == 0:
                conv_unit()
        assert not todo and not active and not pending
        if norm_after is not None:
            norm_after()
        out_proj()
        for post_norm in post_norms:
            post_norm()

    @pl.when((pl.program_id(0) == 0) & (t == 0))
    def _():
        norm, chunks = in_proj_chunks(lambda: x_ref[0, 0:tq, :], zt_ref.at[0])
        norm()
        for chunk in chunks:
            chunk()

    proj = [in_proj_chunks(lambda sub=sub: x_ref[0, sub * tq:(sub + 1) * tq, :], zt_ref.at[sub % 2])
            for sub in range(1, nsub)]
    proj.append(in_proj_chunks(lambda: x_next_ref[0], zt_ref.at[0]))
    for sub in range(nsub):
        norm_first = proj[0][0] if sub == 0 else None
        norm_after = proj[sub + 1][0] if sub + 1 < nsub else None
        interleave(mixer_units(sub), proj[sub][1], norm_first, norm_after)

    tile = nsub * tq
    uprev_ref[...] = carry["u_prev"]
    conv_state_ref[pl.program_id(0)] = carry["u_prev"].T[BLOCK - (CONV_WIDTH - 1):, :]
    kbuf_ref[0:BLOCK, :] = kbuf_ref[tile:tile + BLOCK, :]
    vbuf_ref[:, 0:BLOCK] = vbuf_ref[:, tile:tile + BLOCK]
    k_state_ref[0] = zt_ref[1, OFF_K:OFF_K + D_SWA_KV, tq - BLOCK:]
    v_state_ref[0] = zt_ref[1, OFF_V:OFF_V + D_SWA_KV, tq - BLOCK:]


def _prompt_layer(x, pre_g, post_g, w_in_t, conv_w_t, sinks, mkb, mvtb, w_out, run_after, *, tq, nsub):
    B, T, _ = x.shape
    tile = tq * nsub
    kernel = functools.partial(_prompt_kernel, tq=tq, nsub=nsub)
    steps = T // tile

    def next_first_sub_tile(b, t):
        nxt = jnp.minimum(b * steps + t + 1, B * steps - 1)
        return nxt // steps, (nxt % steps) * nsub, 0

    resident = pl.BlockSpec(memory_space=pltpu.VMEM)
    return pl.pallas_call(
        kernel,
        grid=(B, steps),
        in_specs=[
            pl.BlockSpec((1, tile, D_MODEL), lambda b, t: (b, t, 0)),
            pl.BlockSpec((1, tq, D_MODEL), next_first_sub_tile),
            resident,
            resident,
            resident,
            resident,
            pl.BlockSpec(memory_space=pltpu.SMEM),
            pl.BlockSpec((1, N_MEM, D_MEMQ), lambda b, t: (b, 0, 0)),
            pl.BlockSpec((1, D_MEMQ, N_MEM), lambda b, t: (b, 0, 0)),
            resident,
            pl.BlockSpec(memory_space=pl.ANY),
        ],
        out_specs=[
            pl.BlockSpec((1, tile, D_MODEL), lambda b, t: (b, t, 0)),
            resident,
            pl.BlockSpec((1, D_SWA_KV, BLOCK), lambda b, t: (b, 0, 0)),
            pl.BlockSpec((1, D_SWA_KV, BLOCK), lambda b, t: (b, 0, 0)),
        ],
        out_shape=[
            jax.ShapeDtypeStruct((B, T, D_MODEL), jnp.float32),
            jax.ShapeDtypeStruct((B, CONV_WIDTH - 1, D_CONV), jnp.float32),
            jax.ShapeDtypeStruct((B, D_SWA_KV, BLOCK), jnp.float32),
            jax.ShapeDtypeStruct((B, D_SWA_KV, BLOCK), jnp.float32),
        ],
        scratch_shapes=[
            pltpu.VMEM((2, D_IN, tq), jnp.float32),
            pltpu.VMEM((2, D_MODEL, tq), jnp.bfloat16),
            pltpu.VMEM((BLOCK + tile, D_SWA_KV), jnp.bfloat16),
            pltpu.VMEM((D_SWA_KV, BLOCK + tile), jnp.bfloat16),
            pltpu.VMEM((D_CONV, BLOCK), jnp.float32),
            pltpu.VMEM((N_SWA_KV, BLOCK, SWA_GROUP * BLOCK), jnp.float32),
        ],
        compiler_params=pltpu.CompilerParams(
            dimension_semantics=("arbitrary", "arbitrary"),
            vmem_limit_bytes=PROMPT_VMEM_LIMIT_BYTES),
        name="prompt_layer",
    )(x, x, pre_g, post_g, w_in_t, conv_w_t, sinks, mkb, mvtb, w_out, run_after)


def _sample_kernel(x_ref, conv_past_ref, ckt_ref, cvt_ref, mkt_ref, mvt_ref,
                   pre_g_ref, post_g_ref, w_in_t_ref, conv_w_ref, sink_ref, w_out_ref,
                   y_ref, conv_state_ref, kt_state_ref, vt_state_ref, w_out_bf_ref,
                   z_ref, ycat_ref, *, ns, group):
    R = SAMPLE_ROWS
    half = R // 2
    nseq = ns * group
    step = pl.program_id(1)

    @pl.when((pl.program_id(0) == 0) & (step == 0))
    def _():
        w_out_bf_ref[...] = w_out_ref[...].astype(jnp.bfloat16)

    @pl.when(step == 0)
    def _():
        x4 = x_ref[...]
        x8 = jnp.concatenate([x4, x4], axis=1).reshape(nseq * R, D_MODEL)
        h = _rms_norm(x8, pre_g_ref[...]).astype(jnp.bfloat16)
        z_ref[...] = lax.dot_general(h, w_in_t_ref[...], _NT, preferred_element_type=jnp.float32)
        u = (z_ref[:, OFF_CC:OFF_CC + D_CONV] * z_ref[:, OFF_CH:OFF_CH + D_CONV]).reshape(nseq, R, D_CONV)
        row3 = lax.broadcasted_iota(jnp.int32, (nseq, R, D_CONV), 1)
        past = conv_past_ref[...]
        past = jnp.concatenate([past, jnp.zeros((nseq, R - (CONV_WIDTH - 1), D_CONV), past.dtype)], axis=1)
        u_full = jnp.where(row3 < CONV_WIDTH - 1, past, pltpu.roll(u, CONV_WIDTH - 1, axis=1))
        cw = conv_w_ref[...]
        conv = (cw[0:1, :] * u_full
                + cw[1:2, :] * pltpu.roll(u_full, R - 1, axis=1)
                + cw[2:3, :] * pltpu.roll(u_full, R - 2, axis=1))
        conv_state_ref[...] = pltpu.roll(u_full, R - half, axis=1)[:, 0:CONV_WIDTH - 1, :]
        y_conv = (z_ref[:, OFF_CB:OFF_CB + D_CONV] * conv.reshape(nseq * R, D_CONV)
                  * _silu(z_ref[:, OFF_CZ:OFF_CZ + D_CONV]))
        ycat_ref[:, YOFF_CONV:YOFF_CONV + D_CONV] = y_conv

    row = lax.broadcasted_iota(jnp.int32, (R, PAIR), 0)
    lane = lax.broadcasted_iota(jnp.int32, (R, PAIR), 1)
    lo_row = row < half
    lo_lane = lane < HEAD_DIM
    diag = lo_row == lo_lane
    lane_sq = lax.broadcasted_iota(jnp.int32, (WINDOW, WINDOW), 1)

    def pair_bias(ncols, dist_of):
        rr = lax.broadcasted_iota(jnp.int32, (R, ncols), 0)
        cc = lax.broadcasted_iota(jnp.int32, (R, ncols), 1)
        dist, valid = dist_of(rr % half, cc)
        distf = dist.astype(jnp.float32)
        tiles = []
        for pair in range(N_SWA_HEADS // 2):
            slope = jnp.where(rr < half, _alibi_slope(2 * pair), _alibi_slope(2 * pair + 1))
            tiles.append(jnp.where(valid, -slope * distf, NEG_INF))
        return jnp.concatenate(tiles, axis=0)

    def cached_dist(tok, c):
        d = tok + WINDOW - c
        return d, d < WINDOW

    def new_dist(tok, c):
        d = tok - c
        return d, (d >= 0) & (c < half)

    bias_c = pair_bias(WINDOW, cached_dist)
    bias_n = pair_bias(SAMPLE_NEW, new_dist)
    head_of_row = lax.broadcasted_iota(jnp.int32, (N_SWA_HEADS * half, 1), 0) // half
    sink_col = jnp.full((N_SWA_HEADS * half, 1), sink_ref[0], jnp.float32)
    for hh in range(1, N_SWA_HEADS):
        sink_col = jnp.where(head_of_row == hh, sink_ref[hh], sink_col)
    state_pad = jnp.zeros((WINDOW - R, D_SWA_KV), jnp.float32)

    def seq_rows(n):
        return pl.ds(pl.multiple_of((step * ns + n) * R, R), R)

    def swa_unit(n):
        rows = seq_rows(n)
        qa = z_ref[rows, OFF_Q:OFF_Q + PAIR] * QK_SCALE
        qb = z_ref[rows, OFF_Q + PAIR:OFF_Q + 2 * PAIR] * QK_SCALE
        qc = z_ref[rows, OFF_Q + 2 * PAIR:OFF_Q + 3 * PAIR] * QK_SCALE
        t0 = jnp.where(lo_lane, jnp.where(lo_row, qa, pltpu.roll(qa, HEAD_DIM, axis=1)), 0.0)
        t1 = jnp.where(diag, qb, 0.0)
        t2 = jnp.where(lo_lane, 0.0, jnp.where(lo_row, pltpu.roll(qc, HEAD_DIM, axis=1), qc))
        qs = jnp.concatenate([t0, t1, t2], axis=0).astype(jnp.bfloat16)

        k_new = z_ref[rows, OFF_K:OFF_K + D_SWA_KV]
        v_new = z_ref[rows, OFF_V:OFF_V + D_SWA_KV]
        k_new_b = jnp.concatenate([k_new, k_new], axis=0).astype(jnp.bfloat16)
        v_new_b = jnp.concatenate([v_new, v_new], axis=0).astype(jnp.bfloat16)
        kt_old = ckt_ref[n].astype(jnp.bfloat16)

        s_c = jnp.dot(qs, kt_old, preferred_element_type=jnp.float32) + bias_c
        s_n = lax.dot_general(qs, k_new_b, _NT, preferred_element_type=jnp.float32) + bias_n
        yield
        m = jnp.maximum(jnp.maximum(jnp.max(s_c, axis=1, keepdims=True), jnp.max(s_n, axis=1, keepdims=True)),
                        sink_col)
        p_c = jnp.exp(s_c - m)
        p_n = jnp.exp(s_n - m)
        l = (jnp.sum(p_c, axis=1, keepdims=True) + jnp.sum(p_n, axis=1, keepdims=True) + jnp.exp(sink_col - m))
        p_c = p_c.astype(jnp.bfloat16)
        p_n = p_n.astype(jnp.bfloat16)
        yield
        o = (lax.dot_general(p_c, cvt_ref[n].astype(jnp.bfloat16), _NT, preferred_element_type=jnp.float32)
             + jnp.dot(p_n, v_new_b, preferred_element_type=jnp.float32)) / l
        yield
        o0, o1, o2 = o[0:R], o[R:2 * R], o[2 * R:3 * R]
        ya = jnp.where(lo_lane, o0, pltpu.roll(pltpu.roll(o0, HEAD_DIM, axis=1), half, axis=0))
        yb = jnp.where(lo_lane, o1, pltpu.roll(o1, half, axis=0))
        yc = jnp.where(lo_lane, pltpu.roll(o2, HEAD_DIM, axis=1), pltpu.roll(o2, half, axis=0))
        y_swa = jnp.concatenate([ya, yb, yc], axis=1) * _silu(z_ref[rows, OFF_SZ:OFF_SZ + D_SWA])
        ycat_ref[rows, YOFF_SWA:YOFF_SWA + D_SWA] = y_swa

    def state_unit(n):
        rows = seq_rows(n)
        k_new_t = jnp.concatenate([state_pad, z_ref[rows, OFF_K:OFF_K + D_SWA_KV]], axis=0).T
        v_new_t = jnp.concatenate([state_pad, z_ref[rows, OFF_V:OFF_V + D_SWA_KV]], axis=0).T
        keep = lane_sq < WINDOW - half
        kt_state_ref[n] = jnp.where(keep, pltpu.roll(ckt_ref[n], WINDOW - half, axis=1), k_new_t)
        vt_state_ref[n] = jnp.where(keep, pltpu.roll(cvt_ref[n], WINDOW - half, axis=1), v_new_t)

    def mem_unit(n):
        rows = seq_rows(n)
        m0 = z_ref[rows, OFF_MQ:OFF_MQ + PAIR] * QK_SCALE
        m1 = z_ref[rows, OFF_MQ + PAIR:OFF_MQ + 2 * PAIR] * QK_SCALE
        zero = jnp.zeros_like(m0)
        qm = jnp.concatenate(
            [jnp.concatenate([jnp.where(diag, m0, 0.0), zero], axis=1),
             jnp.concatenate([zero, jnp.where(diag, m1, 0.0)], axis=1)], axis=0).astype(jnp.bfloat16)
        s = jnp.dot(qm, mkt_ref[n].astype(jnp.bfloat16), preferred_element_type=jnp.float32)
        yield
        m = jnp.max(s, axis=1, keepdims=True)
        p = jnp.exp(s - m)
        l = jnp.sum(p, axis=1, keepdims=True)
        p = p.astype(jnp.bfloat16)
        yield
        o = lax.dot_general(p, mvt_ref[n].astype(jnp.bfloat16), _NT,
                            preferred_element_type=jnp.float32) / l
        yield
        oa, ob = o[0:R, 0:PAIR], o[R:2 * R, PAIR:2 * PAIR]
        y_mem = jnp.concatenate([jnp.where(lo_lane, oa, pltpu.roll(oa, half, axis=0)),
                                 jnp.where(lo_lane, ob, pltpu.roll(ob, half, axis=0))], axis=1)
        ycat_ref[rows, YOFF_MEM:YOFF_MEM + D_MEMQ] = y_mem * _silu(z_ref[rows, OFF_MZ:OFF_MZ + D_MEMQ])

    todo = [functools.partial(unit, n) for n in range(ns) for unit in (swa_unit, mem_unit)]
    states = [functools.partial(state_unit, n) for n in range(ns)]
    active = []
    while todo or active:
        for _ in range(SAMPLE_WIDTH):
            if todo:
                active.append(todo.pop(0)())
        for gen in list(active):
            if next(gen, "done") == "done":
                active.remove(gen)
        if states:
            states.pop(0)()
    for unit in states:
        unit()

    @pl.when(step == group - 1)
    def _():
        y = jnp.dot(ycat_ref[...].astype(jnp.bfloat16), w_out_bf_ref[...], preferred_element_type=jnp.float32)
        y_ref[...] = x_ref[...] + _rms_norm(y, post_g_ref[...]).reshape(nseq, R, D_MODEL)[:, 0:half, :]


def _sample_layer(x, conv_past, ckt, cvt, mkt, mvt, pre_g, post_g, w_in_t, conv_w, sinks, w_out, *, ns, group):
    N = ckt.shape[0]
    R = SAMPLE_ROWS
    nseq = ns * group
    per_group = lambda shape: pl.BlockSpec(shape, lambda o, i: (o,) + (0,) * (len(shape) - 1))
    per_step = lambda shape: pl.BlockSpec(shape, lambda o, i: (o * group + i,) + (0,) * (len(shape) - 1))
    resident = pl.BlockSpec(memory_space=pltpu.VMEM)
    kernel = functools.partial(_sample_kernel, ns=ns, group=group)
    return pl.pallas_call(
        kernel,
        grid=(N // nseq, group),
        in_specs=[
            per_group((nseq, R // 2, D_MODEL)),
            per_group((nseq, CONV_WIDTH - 1, D_CONV)),
            per_step((ns, D_SWA_KV, WINDOW)),
            per_step((ns, D_SWA_KV, WINDOW)),
            per_step((ns, D_MEMQ, N_MEM)),
            per_step((ns, D_MEMQ, N_MEM)),
            resident,
            resident,
            resident,
            resident,
            pl.BlockSpec(memory_space=pltpu.SMEM),
            resident,
        ],
        out_specs=[
            per_group((nseq, R // 2, D_MODEL)),
            per_group((nseq, CONV_WIDTH - 1, D_CONV)),
            per_step((ns, D_SWA_KV, WINDOW)),
            per_step((ns, D_SWA_KV, WINDOW)),
            pl.BlockSpec((D_MODEL, D_MODEL), lambda o, i: (0, 0)),
        ],
        out_shape=[
            jax.ShapeDtypeStruct((N, R // 2, D_MODEL), jnp.float32),
            jax.ShapeDtypeStruct((N, CONV_WIDTH - 1, D_CONV), jnp.float32),
            jax.ShapeDtypeStruct((N, D_SWA_KV, WINDOW), jnp.float32),
            jax.ShapeDtypeStruct((N, D_SWA_KV, WINDOW), jnp.float32),
            jax.ShapeDtypeStruct((D_MODEL, D_MODEL), jnp.bfloat16),
        ],
        scratch_shapes=[
            pltpu.VMEM((nseq * R, D_IN), jnp.float32),
            pltpu.VMEM((nseq * R, D_MODEL), jnp.float32),
        ],
        compiler_params=pltpu.CompilerParams(
            dimension_semantics=("arbitrary", "arbitrary"),
            vmem_limit_bytes=SAMPLE_VMEM_LIMIT_BYTES),
        name="sample_layer",
    )(x, conv_past, ckt, cvt, mkt, mvt, pre_g, post_g, w_in_t, conv_w, sinks, w_out)


def _heads_last_to_keys_last(a):
    n, keys, heads, dim = a.shape
    return jnp.transpose(a, (0, 2, 3, 1)).reshape(n, heads * dim, keys)


def _keys_last_to_heads_last(a, heads):
    n, hd, keys = a.shape
    return jnp.transpose(a.reshape(n, heads, hd // heads, keys), (0, 3, 1, 2))[None]


def kernel(x_prompt, x_sample, mem_prompt, state_conv, cache_swa_k, cache_swa_v, cache_mem_k, cache_mem_v,
           pre_norm_g, post_norm_g, w_in, conv_w, attn_sinks, mem_norm_g, w_mem_k, w_mem_v, w_out):
    assert w_in.shape[0] == 1, "one layer, as the problem states"
    N, TS, _ = x_sample.shape
    assert TS == SAMPLE_ROWS // 2 and cache_swa_k.shape[2] == WINDOW
    l = 0

    pre_g = pre_norm_g[l].reshape(1, D_MODEL)
    post_g = post_norm_g[l].reshape(1, D_MODEL)
    w_in_t = w_in[l].astype(jnp.bfloat16).T
    sinks = attn_sinks[l].astype(jnp.float32)

    y_s, conv_s, kt_s, vt_s, w_out_bf = _sample_layer(
        x_sample, state_conv[l],
        _heads_last_to_keys_last(cache_swa_k[l]), _heads_last_to_keys_last(cache_swa_v[l]),
        _heads_last_to_keys_last(cache_mem_k[l]), _heads_last_to_keys_last(cache_mem_v[l]),
        pre_g, post_g, w_in_t, conv_w[l], sinks, w_out[l], ns=SAMPLE_NS, group=SAMPLE_GROUP)

    mkt, mvt, mkb, mvtb = _mem_kv(mem_prompt, mem_norm_g[l], w_mem_k[l], w_mem_v[l])
    y_p, conv_p, kt_p, vt_p = _prompt_layer(
        x_prompt, pre_g, post_g, w_in_t, conv_w[l].T, sinks, mkb, mvtb, w_out_bf, conv_s,
        tq=PROMPT_TQ, nsub=PROMPT_NSUB)

    return (y_p, y_s,
            conv_p[None],
            _keys_last_to_heads_last(kt_p, N_SWA_KV), _keys_last_to_heads_last(vt_p, N_SWA_KV),
            _keys_last_to_heads_last(mkt, N_MEM_HEADS), _keys_last_to_heads_last(mvt, N_MEM_HEADS),
            conv_s[None],
            _keys_last_to_heads_last(kt_s, N_SWA_KV), _keys_last_to_heads_last(vt_s, N_SWA_KV))
```

```python
import functools

import numpy as np
import jax
import jax.numpy as jnp
from jax import lax
from jax.experimental import pallas as pl
from jax.experimental.pallas import tpu as pltpu

D_MODEL = 1024
HEAD_DIM = 64
D_CONV = 384
N_MEM_HEADS = 4
D_MEMQ = N_MEM_HEADS * HEAD_DIM
D_SWA = 384
N_SWA_HEADS = 6
N_SWA_KV = 2
SWA_GROUP = N_SWA_HEADS // N_SWA_KV
D_SWA_KV = N_SWA_KV * HEAD_DIM
N_MEM = 256
CONV_WIDTH = 3
WINDOW = 128
BLOCK = 128
RMS_EPS = 1e-6
NEG_INF = -1e30
D_IN = 3072
QK_SCALE = HEAD_DIM ** -0.5
LOG2_E = float(np.log2(np.e))
QK_SCALE_LOG2 = QK_SCALE * LOG2_E

OFF_CB, OFF_CC, OFF_CH, OFF_CZ = 0, 384, 768, 1152
OFF_Q, OFF_K, OFF_V, OFF_SZ = 1536, 1920, 2048, 2176
OFF_MQ, OFF_MZ = 2560, 2816
YOFF_CONV, YOFF_SWA, YOFF_MEM = 0, 384, 768

PAIR = 2 * HEAD_DIM

MIB = 1024 * 1024
PROMPT_VMEM_LIMIT_BYTES = 44 * MIB
SAMPLE_VMEM_LIMIT_BYTES = 48 * MIB

MEM_KV_BATCHES = 2
PROMPT_TQ = 512
PROMPT_NSUB = 2
IN_PROJ_CHUNK = 512
MEM_TOKEN_PARTS = 2
ATTN_WIDTH = 2
SAMPLE_NS = 16
SAMPLE_GROUP = 2
SAMPLE_ROWS = 8
SAMPLE_WIDTH = 16
SAMPLE_NEW = 16


def _alibi_slope(h):
    return float(np.power(np.float32(2.0), np.float32(-8.0 * (h + 1) / N_SWA_HEADS)))


def _rms_norm(x, g):
    return x * lax.rsqrt(jnp.mean(x * x, axis=-1, keepdims=True) + RMS_EPS) * g


def _silu(x):
    return x * jax.nn.sigmoid(x)


_NT = (((1,), (1,)), ((), ()))
_TN = (((0,), (0,)), ((), ()))


def _mem_kv_kernel(mem_ref, g_ref, wk_ref, wv_ref, mkt_ref, mvt_ref, mkb_ref, mvtb_ref, *, nb):
    mem = mem_ref[...].reshape(nb * N_MEM, D_MODEL)
    m = _rms_norm(mem, g_ref[...]).astype(jnp.bfloat16)
    mk = jnp.dot(m, wk_ref[...].astype(jnp.bfloat16), preferred_element_type=jnp.float32)
    mv = jnp.dot(m, wv_ref[...].astype(jnp.bfloat16), preferred_element_type=jnp.float32)
    for b in range(nb):
        mk_b = mk[b * N_MEM:(b + 1) * N_MEM]
        mv_t = mv[b * N_MEM:(b + 1) * N_MEM].T
        mkt_ref[b] = mk_b.T
        mvt_ref[b] = mv_t
        mkb_ref[b] = mk_b.astype(jnp.bfloat16)
        mvtb_ref[b] = mv_t.astype(jnp.bfloat16)


def _mem_kv(mem, mem_g, w_mk, w_mv):
    B = mem.shape[0]
    nb = MEM_KV_BATCHES
    full = lambda shape: pl.BlockSpec(shape, lambda b: (0,) * len(shape))
    per_batch = pl.BlockSpec((nb, N_MEM, D_MEMQ), lambda b: (b, 0, 0))
    return pl.pallas_call(
        functools.partial(_mem_kv_kernel, nb=nb),
        grid=(B // nb,),
        in_specs=[
            pl.BlockSpec((nb, N_MEM, D_MODEL), lambda b: (b, 0, 0)),
            full((1, D_MODEL)),
            full((D_MODEL, D_MEMQ)),
            full((D_MODEL, D_MEMQ)),
        ],
        out_specs=[per_batch] * 4,
        out_shape=[
            jax.ShapeDtypeStruct((B, D_MEMQ, N_MEM), jnp.float32),
            jax.ShapeDtypeStruct((B, D_MEMQ, N_MEM), jnp.float32),
            jax.ShapeDtypeStruct((B, N_MEM, D_MEMQ), jnp.bfloat16),
            jax.ShapeDtypeStruct((B, D_MEMQ, N_MEM), jnp.bfloat16),
        ],
        compiler_params=pltpu.CompilerParams(dimension_semantics=("arbitrary",)),
        name="mem_kv",
    )(mem, mem_g.reshape(1, D_MODEL), w_mk, w_mv)


def _prompt_kernel(x_ref, x_next_ref, pre_g_ref, post_g_ref, w_in_t_ref, conv_w_t_ref, sink_ref, mkb_ref, mvt_ref,
                   w_out_ref, run_after_ref,
                   y_ref, conv_state_ref, k_state_ref, v_state_ref,
                   zt_ref, ycat_ref, kbuf_ref, vbuf_ref, uprev_ref, bias_ref, *, tq, nsub):
    assert nsub % 2 == 0
    t = pl.program_id(1)
    nblk = tq // BLOCK

    @pl.when((pl.program_id(0) == 0) & (t == 0))
    def _():
        c = lax.broadcasted_iota(jnp.int32, (BLOCK, BLOCK), 0)
        r = lax.broadcasted_iota(jnp.int32, (BLOCK, BLOCK), 1)
        distf = (r - c + jnp.where(c > r, BLOCK, 0)).astype(jnp.float32)
        for h in range(N_SWA_HEADS):
            g, i = divmod(h, SWA_GROUP)
            bias_ref[g, :, i * BLOCK:(i + 1) * BLOCK] = (-_alibi_slope(h) * LOG2_E) * distf

    @pl.when(t == 0)
    def _():
        kbuf_ref[0:BLOCK, :] = jnp.zeros((BLOCK, D_SWA_KV), jnp.bfloat16)
        vbuf_ref[:, 0:BLOCK] = jnp.zeros((D_SWA_KV, BLOCK), jnp.bfloat16)
        uprev_ref[...] = jnp.zeros_like(uprev_ref)

    first_pen = jnp.where(t == 0, NEG_INF, 0.0)
    slot = lax.broadcasted_iota(jnp.int32, (BLOCK, SWA_GROUP * BLOCK), 0)
    query = lax.broadcasted_iota(jnp.int32, (BLOCK, SWA_GROUP * BLOCK), 1) % BLOCK
    from_prev = slot > query
    q_zero = jnp.zeros((HEAD_DIM, SWA_GROUP * BLOCK), jnp.bfloat16)
    cw = conv_w_t_ref[...]
    carry = {"u_prev": uprev_ref[...]}
    head_of_lane = lax.broadcasted_iota(jnp.int32, (1, SWA_GROUP * BLOCK), 1) // BLOCK
    sink_rows = []
    for g in range(N_SWA_KV):
        row = jnp.full((1, SWA_GROUP * BLOCK), sink_ref[g * SWA_GROUP], jnp.float32)
        for i in range(1, SWA_GROUP):
            row = jnp.where(head_of_lane == i, sink_ref[g * SWA_GROUP + i], row)
        sink_rows.append(row * LOG2_E)

    def in_proj_chunks(x_rows, zt):
        state = {}

        def norm():
            state["h"] = _rms_norm(x_rows(), pre_g_ref[...]).astype(jnp.bfloat16)

        def chunk(lo):
            rows = slice(lo, lo + IN_PROJ_CHUNK)
            zt[rows, :] = lax.dot_general(w_in_t_ref[rows, :], state["h"], _NT,
                                          preferred_element_type=jnp.float32)

        return norm, [functools.partial(chunk, lo) for lo in range(0, D_IN, IN_PROJ_CHUNK)]

    def mixer_units(sub):
        zt = zt_ref.at[sub % 2]
        ycat = ycat_ref.at[sub % 2]
        tok0 = sub * tq

        def conv_unit():
            u = zt[OFF_CC:OFF_CC + D_CONV, :] * zt[OFF_CH:OFF_CH + D_CONV, :]
            ucat = jnp.concatenate([carry["u_prev"], u], axis=1)
            conv = (cw[:, 0:1] * pltpu.roll(ucat, 2, axis=1)[:, BLOCK:]
                    + cw[:, 1:2] * pltpu.roll(ucat, 1, axis=1)[:, BLOCK:]
                    + cw[:, 2:3] * u)
            y_conv = zt[OFF_CB:OFF_CB + D_CONV, :] * conv * _silu(zt[OFF_CZ:OFF_CZ + D_CONV, :])
            ycat[YOFF_CONV:YOFF_CONV + D_CONV, :] = y_conv.astype(jnp.bfloat16)
            carry["u_prev"] = u[:, tq - BLOCK:]

        def kv_unit():
            k_nat = zt[OFF_K:OFF_K + D_SWA_KV, :].T
            kbuf_ref[BLOCK + tok0:BLOCK + tok0 + tq, :] = k_nat.astype(jnp.bfloat16)
            vbuf_ref[:, BLOCK + tok0:BLOCK + tok0 + tq] = zt[OFF_V:OFF_V + D_SWA_KV, :].astype(jnp.bfloat16)

        def swa_unit(j, g):
            cols = slice(j * BLOCK, (j + 1) * BLOCK)
            band = slice(tok0 + j * BLOCK, tok0 + (j + 2) * BLOCK)
            q0 = OFF_Q + g * SWA_GROUP * HEAD_DIM
            qt = jnp.concatenate(
                [zt[q0 + i * HEAD_DIM:q0 + (i + 1) * HEAD_DIM, cols] for i in range(SWA_GROUP)],
                axis=1)
            qt = (qt * QK_SCALE_LOG2).astype(jnp.bfloat16)
            qt = jnp.concatenate([qt, q_zero] if g == 0 else [q_zero, qt], axis=0)
            s = jnp.dot(kbuf_ref[band, :], qt, preferred_element_type=jnp.float32)
            yield
            s = jnp.where(from_prev, s[0:BLOCK], s[BLOCK:]) + bias_ref[g]
            if sub == 0 and j == 0:
                s = s + jnp.where(from_prev, first_pen, 0.0)
            sink = sink_rows[g]
            m = jnp.maximum(jnp.max(s, axis=0, keepdims=True), sink)
            p = jnp.exp2(s - m)
            l = jnp.sum(p, axis=0, keepdims=True) + jnp.exp2(sink - m)
            p = jnp.concatenate([jnp.where(from_prev, p, 0.0), jnp.where(from_prev, 0.0, p)],
                                axis=0).astype(jnp.bfloat16)
            yield
            vband = vbuf_ref[g * HEAD_DIM:(g + 1) * HEAD_DIM, band]
            o = jnp.dot(vband, p, preferred_element_type=jnp.float32)
            o = o / l
            for i in range(SWA_GROUP):
                hh = g * SWA_GROUP + i
                gate = _silu(zt[OFF_SZ + hh * HEAD_DIM:OFF_SZ + (hh + 1) * HEAD_DIM, cols])
                ycat[YOFF_SWA + hh * HEAD_DIM:YOFF_SWA + (hh + 1) * HEAD_DIM, cols] = (
                    o[:, i * BLOCK:(i + 1) * BLOCK] * gate).astype(jnp.bfloat16)

        def mem_unit(hh, part):
            rows = slice(OFF_MQ + hh * HEAD_DIM, OFF_MQ + (hh + 1) * HEAD_DIM)
            tw = tq // MEM_TOKEN_PARTS
            cols = slice(part * tw, (part + 1) * tw)
            qt = (zt[rows, cols] * QK_SCALE_LOG2).astype(jnp.bfloat16)
            pieces = [jnp.zeros((HEAD_DIM, tw), jnp.bfloat16)] * N_MEM_HEADS
            pieces[hh] = qt
            s = jnp.dot(mkb_ref[0], jnp.concatenate(pieces, axis=0),
                        preferred_element_type=jnp.float32)
            yield
            m = jnp.max(s, axis=0, keepdims=True)
            p = jnp.exp2(s - m)
            l = jnp.sum(p, axis=0, keepdims=True)
            p = p.astype(jnp.bfloat16)
            yield
            o = jnp.dot(mvt_ref[0, hh * HEAD_DIM:(hh + 1) * HEAD_DIM, :], p,
                        preferred_element_type=jnp.float32)
            gate = _silu(zt[OFF_MZ + hh * HEAD_DIM:OFF_MZ + (hh + 1) * HEAD_DIM, cols])
            ycat[YOFF_MEM + hh * HEAD_DIM:YOFF_MEM + (hh + 1) * HEAD_DIM, cols] = (
                o / l * gate).astype(jnp.bfloat16)

        def out_proj():
            carry["y"] = lax.dot_general(ycat[...], w_out_ref[...], _TN,
                                         preferred_element_type=jnp.float32)

        def post_norm(blk):
            rows = slice(blk * BLOCK, (blk + 1) * BLOCK)
            out_rows = slice(tok0 + blk * BLOCK, tok0 + (blk + 1) * BLOCK)
            y_ref[0, out_rows, :] = x_ref[0, out_rows, :] + _rms_norm(carry["y"][rows], post_g_ref[...])

        attention = [functools.partial(swa_unit, j, g) for j in range(nblk) for g in range(N_SWA_KV)]
        attention += [functools.partial(mem_unit, hh, part)
                      for hh in range(N_MEM_HEADS) for part in range(MEM_TOKEN_PARTS)]
        return [conv_unit, kv_unit], attention, (out_proj, [functools.partial(post_norm, b) for b in range(nblk)])

    def interleave(units, chunks, norm_first, norm_after):
        (conv_unit, kv_unit), attention, (out_proj, post_norms) = units
        pending = list(chunks)
        kv_unit()
        n_rounds = -(-len(attention) // ATTN_WIDTH) + 2
        chunk_rounds = [k * n_rounds // len(pending) for k in range(len(pending))]
        todo = list(attention)
        active = []
        for rnd in range(n_rounds):
            for _ in range(ATTN_WIDTH):
                if todo:
                    active.append(todo.pop(0)())
            for gen in list(active):
                if next(gen, "done") == "done":
                    active.remove(gen)
            if rnd == 0 and norm_first is not None:
                norm_first()
            for _ in range(chunk_rounds.count(rnd)):
                pending.pop(0)()
            if rnd == 0:
                conv_unit()
        assert not todo and not active and not pending
        if norm_after is not None:
            norm_after()
        out_proj()
        for post_norm in post_norms:
            post_norm()

    @pl.when((pl.program_id(0) == 0) & (t == 0))
    def _():
        norm, chunks = in_proj_chunks(lambda: x_ref[0, 0:tq, :], zt_ref.at[0])
        norm()
        for chunk in chunks:
            chunk()

    proj = [in_proj_chunks(lambda sub=sub: x_ref[0, sub * tq:(sub + 1) * tq, :], zt_ref.at[sub % 2])
            for sub in range(1, nsub)]
    proj.append(in_proj_chunks(lambda: x_next_ref[0], zt_ref.at[0]))
    for sub in range(nsub):
        norm_first = proj[0][0] if sub == 0 else None
        norm_after = proj[sub + 1][0] if sub + 1 < nsub else None
        interleave(mixer_units(sub), proj[sub][1], norm_first, norm_after)

    tile = nsub * tq
    uprev_ref[...] = carry["u_prev"]
    conv_state_ref[0] = carry["u_prev"].T[BLOCK - (CONV_WIDTH - 1):, :]
    kbuf_ref[0:BLOCK, :] = kbuf_ref[tile:tile + BLOCK, :]
    vbuf_ref[:, 0:BLOCK] = vbuf_ref[:, tile:tile + BLOCK]
    k_state_ref[0] = zt_ref[1, OFF_K:OFF_K + D_SWA_KV, tq - BLOCK:]
    v_state_ref[0] = zt_ref[1, OFF_V:OFF_V + D_SWA_KV, tq - BLOCK:]


def _prompt_layer(x, pre_g, post_g, w_in_t, conv_w_t, sinks, mkb, mvtb, w_out, run_after, *, tq, nsub):
    B, T, _ = x.shape
    tile = tq * nsub
    kernel = functools.partial(_prompt_kernel, tq=tq, nsub=nsub)
    steps = T // tile

    def next_first_sub_tile(b, t):
        nxt = jnp.minimum(b * steps + t + 1, B * steps - 1)
        return nxt // steps, (nxt % steps) * nsub, 0

    resident = pl.BlockSpec(memory_space=pltpu.VMEM)
    return pl.pallas_call(
        kernel,
        grid=(B, steps),
        in_specs=[
            pl.BlockSpec((1, tile, D_MODEL), lambda b, t: (b, t, 0)),
            pl.BlockSpec((1, tq, D_MODEL), next_first_sub_tile),
            resident,
            resident,
            resident,
            resident,
            pl.BlockSpec(memory_space=pltpu.SMEM),
            pl.BlockSpec((1, N_MEM, D_MEMQ), lambda b, t: (b, 0, 0)),
            pl.BlockSpec((1, D_MEMQ, N_MEM), lambda b, t: (b, 0, 0)),
            resident,
            pl.BlockSpec(memory_space=pl.ANY),
        ],
        out_specs=[
            pl.BlockSpec((1, tile, D_MODEL), lambda b, t: (b, t, 0)),
            pl.BlockSpec((1, CONV_WIDTH - 1, D_CONV), lambda b, t: (b, 0, 0)),
            pl.BlockSpec((1, D_SWA_KV, BLOCK), lambda b, t: (b, 0, 0)),
            pl.BlockSpec((1, D_SWA_KV, BLOCK), lambda b, t: (b, 0, 0)),
        ],
        out_shape=[
            jax.ShapeDtypeStruct((B, T, D_MODEL), jnp.float32),
            jax.ShapeDtypeStruct((B, CONV_WIDTH - 1, D_CONV), jnp.float32),
            jax.ShapeDtypeStruct((B, D_SWA_KV, BLOCK), jnp.float32),
            jax.ShapeDtypeStruct((B, D_SWA_KV, BLOCK), jnp.float32),
        ],
        scratch_shapes=[
            pltpu.VMEM((2, D_IN, tq), jnp.float32),
            pltpu.VMEM((2, D_MODEL, tq), jnp.bfloat16),
            pltpu.VMEM((BLOCK + tile, D_SWA_KV), jnp.bfloat16),
            pltpu.VMEM((D_SWA_KV, BLOCK + tile), jnp.bfloat16),
            pltpu.VMEM((D_CONV, BLOCK), jnp.float32),
            pltpu.VMEM((N_SWA_KV, BLOCK, SWA_GROUP * BLOCK), jnp.float32),
        ],
        compiler_params=pltpu.CompilerParams(
            dimension_semantics=("arbitrary", "arbitrary"),
            vmem_limit_bytes=PROMPT_VMEM_LIMIT_BYTES),
        name="prompt_layer",
    )(x, x, pre_g, post_g, w_in_t, conv_w_t, sinks, mkb, mvtb, w_out, run_after)


def _sample_kernel(x_ref, conv_past_ref, ckt_ref, cvt_ref, mkt_ref, mvt_ref,
                   pre_g_ref, post_g_ref, w_in_t_ref, conv_w_ref, sink_ref, w_out_ref,
                   y_ref, conv_state_ref, kt_state_ref, vt_state_ref, w_out_bf_ref,
                   z_ref, ycat_ref, *, ns, group):
    R = SAMPLE_ROWS
    half = R // 2
    nseq = ns * group
    step = pl.program_id(1)

    @pl.when((pl.program_id(0) == 0) & (step == 0))
    def _():
        w_out_bf_ref[...] = w_out_ref[...].astype(jnp.bfloat16)

    @pl.when(step == 0)
    def _():
        x4 = x_ref[...]
        x8 = jnp.concatenate([x4, x4], axis=1).reshape(nseq * R, D_MODEL)
        h = _rms_norm(x8, pre_g_ref[...]).astype(jnp.bfloat16)
        z_ref[...] = lax.dot_general(h, w_in_t_ref[...], _NT, preferred_element_type=jnp.float32)
        u = (z_ref[:, OFF_CC:OFF_CC + D_CONV] * z_ref[:, OFF_CH:OFF_CH + D_CONV]).reshape(nseq, R, D_CONV)
        row3 = lax.broadcasted_iota(jnp.int32, (nseq, R, D_CONV), 1)
        past = conv_past_ref[...]
        past = jnp.concatenate([past, jnp.zeros((nseq, R - (CONV_WIDTH - 1), D_CONV), past.dtype)], axis=1)
        u_full = jnp.where(row3 < CONV_WIDTH - 1, past, pltpu.roll(u, CONV_WIDTH - 1, axis=1))
        cw = conv_w_ref[...]
        conv = (cw[0:1, :] * u_full
                + cw[1:2, :] * pltpu.roll(u_full, R - 1, axis=1)
                + cw[2:3, :] * pltpu.roll(u_full, R - 2, axis=1))
        conv_state_ref[...] = pltpu.roll(u_full, R - half, axis=1)[:, 0:CONV_WIDTH - 1, :]
        y_conv = (z_ref[:, OFF_CB:OFF_CB + D_CONV] * conv.reshape(nseq * R, D_CONV)
                  * _silu(z_ref[:, OFF_CZ:OFF_CZ + D_CONV]))
        ycat_ref[:, YOFF_CONV:YOFF_CONV + D_CONV] = y_conv

    row = lax.broadcasted_iota(jnp.int32, (R, PAIR), 0)
    lane = lax.broadcasted_iota(jnp.int32, (R, PAIR), 1)
    lo_row = row < half
    lo_lane = lane < HEAD_DIM
    diag = lo_row == lo_lane
    lane_sq = lax.broadcasted_iota(jnp.int32, (WINDOW, WINDOW), 1)

    def pair_bias(ncols, dist_of):
        rr = lax.broadcasted_iota(jnp.int32, (R, ncols), 0)
        cc = lax.broadcasted_iota(jnp.int32, (R, ncols), 1)
        dist, valid = dist_of(rr % half, cc)
        distf = dist.astype(jnp.float32)
        tiles = []
        for pair in range(N_SWA_HEADS // 2):
            slope = jnp.where(rr < half, _alibi_slope(2 * pair), _alibi_slope(2 * pair + 1))
            tiles.append(jnp.where(valid, -slope * distf, NEG_INF))
        return jnp.concatenate(tiles, axis=0)

    def cached_dist(tok, c):
        d = tok + WINDOW - c
        return d, d < WINDOW

    def new_dist(tok, c):
        d = tok - c
        return d, (d >= 0) & (c < half)

    bias_c = pair_bias(WINDOW, cached_dist)
    bias_n = pair_bias(SAMPLE_NEW, new_dist)
    head_of_row = lax.broadcasted_iota(jnp.int32, (N_SWA_HEADS * half, 1), 0) // half
    sink_col = jnp.full((N_SWA_HEADS * half, 1), sink_ref[0], jnp.float32)
    for hh in range(1, N_SWA_HEADS):
        sink_col = jnp.where(head_of_row == hh, sink_ref[hh], sink_col)
    state_pad = jnp.zeros((WINDOW - R, D_SWA_KV), jnp.float32)

    def seq_rows(n):
        return pl.ds(pl.multiple_of((step * ns + n) * R, R), R)

    def swa_unit(n):
        rows = seq_rows(n)
        qa = z_ref[rows, OFF_Q:OFF_Q + PAIR] * QK_SCALE
        qb = z_ref[rows, OFF_Q + PAIR:OFF_Q + 2 * PAIR] * QK_SCALE
        qc = z_ref[rows, OFF_Q + 2 * PAIR:OFF_Q + 3 * PAIR] * QK_SCALE
        t0 = jnp.where(lo_lane, jnp.where(lo_row, qa, pltpu.roll(qa, HEAD_DIM, axis=1)), 0.0)
        t1 = jnp.where(diag, qb, 0.0)
        t2 = jnp.where(lo_lane, 0.0, jnp.where(lo_row, pltpu.roll(qc, HEAD_DIM, axis=1), qc))
        qs = jnp.concatenate([t0, t1, t2], axis=0).astype(jnp.bfloat16)

        k_new = z_ref[rows, OFF_K:OFF_K + D_SWA_KV]
        v_new = z_ref[rows, OFF_V:OFF_V + D_SWA_KV]
        k_new_b = jnp.concatenate([k_new, k_new], axis=0).astype(jnp.bfloat16)
        v_new_b = jnp.concatenate([v_new, v_new], axis=0).astype(jnp.bfloat16)
        kt_old = ckt_ref[n].astype(jnp.bfloat16)

        s_c = jnp.dot(qs, kt_old, preferred_element_type=jnp.float32) + bias_c
        s_n = lax.dot_general(qs, k_new_b, _NT, preferred_element_type=jnp.float32) + bias_n
        yield
        m = jnp.maximum(jnp.maximum(jnp.max(s_c, axis=1, keepdims=True), jnp.max(s_n, axis=1, keepdims=True)),
                        sink_col)
        p_c = jnp.exp(s_c - m)
        p_n = jnp.exp(s_n - m)
        l = (jnp.sum(p_c, axis=1, keepdims=True) + jnp.sum(p_n, axis=1, keepdims=True) + jnp.exp(sink_col - m))
        p_c = p_c.astype(jnp.bfloat16)
        p_n = p_n.astype(jnp.bfloat16)
        yield
        o = (lax.dot_general(p_c, cvt_ref[n].astype(jnp.bfloat16), _NT, preferred_element_type=jnp.float32)
             + jnp.dot(p_n, v_new_b, preferred_element_type=jnp.float32)) / l
        yield
        o0, o1, o2 = o[0:R], o[R:2 * R], o[2 * R:3 * R]
        ya = jnp.where(lo_lane, o0, pltpu.roll(pltpu.roll(o0, HEAD_DIM, axis=1), half, axis=0))
        yb = jnp.where(lo_lane, o1, pltpu.roll(o1, half, axis=0))
        yc = jnp.where(lo_lane, pltpu.roll(o2, HEAD_DIM, axis=1), pltpu.roll(o2, half, axis=0))
        y_swa = jnp.concatenate([ya, yb, yc], axis=1) * _silu(z_ref[rows, OFF_SZ:OFF_SZ + D_SWA])
        ycat_ref[rows, YOFF_SWA:YOFF_SWA + D_SWA] = y_swa

    def state_unit(n):
        rows = seq_rows(n)
        k_new_t = jnp.concatenate([state_pad, z_ref[rows, OFF_K:OFF_K + D_SWA_KV]], axis=0).T
        v_new_t = jnp.concatenate([state_pad, z_ref[rows, OFF_V:OFF_V + D_SWA_KV]], axis=0).T
        keep = lane_sq < WINDOW - half
        kt_state_ref[n] = jnp.where(keep, pltpu.roll(ckt_ref[n], WINDOW - half, axis=1), k_new_t)
        vt_state_ref[n] = jnp.where(keep, pltpu.roll(cvt_ref[n], WINDOW - half, axis=1), v_new_t)

    def mem_unit(n):
        rows = seq_rows(n)
        m0 = z_ref[rows, OFF_MQ:OFF_MQ + PAIR] * QK_SCALE
        m1 = z_ref[rows, OFF_MQ + PAIR:OFF_MQ + 2 * PAIR] * QK_SCALE
        zero = jnp.zeros_like(m0)
        qm = jnp.concatenate(
            [jnp.concatenate([jnp.where(diag, m0, 0.0), zero], axis=1),
             jnp.concatenate([zero, jnp.where(diag, m1, 0.0)], axis=1)], axis=0).astype(jnp.bfloat16)
        s = jnp.dot(qm, mkt_ref[n].astype(jnp.bfloat16), preferred_element_type=jnp.float32)
        yield
        m = jnp.max(s, axis=1, keepdims=True)
        p = jnp.exp(s - m)
        l = jnp.sum(p, axis=1, keepdims=True)
        p = p.astype(jnp.bfloat16)
        yield
        o = lax.dot_general(p, mvt_ref[n].astype(jnp.bfloat16), _NT,
                            preferred_element_type=jnp.float32) / l
        yield
        oa, ob = o[0:R, 0:PAIR], o[R:2 * R, PAIR:2 * PAIR]
        y_mem = jnp.concatenate([jnp.where(lo_lane, oa, pltpu.roll(oa, half, axis=0)),
                                 jnp.where(lo_lane, ob, pltpu.roll(ob, half, axis=0))], axis=1)
        ycat_ref[rows, YOFF_MEM:YOFF_MEM + D_MEMQ] = y_mem * _silu(z_ref[rows, OFF_MZ:OFF_MZ + D_MEMQ])

    todo = [functools.partial(unit, n) for n in range(ns) for unit in (swa_unit, mem_unit)]
    states = [functools.partial(state_unit, n) for n in range(ns)]
    active = []
    while todo or active:
        for _ in range(SAMPLE_WIDTH):
            if todo:
                active.append(todo.pop(0)())
        for gen in list(active):
            if next(gen, "done") == "done":
                active.remove(gen)
        if states:
            states.pop(0)()
    for unit in states:
        unit()

    @pl.when(step == group - 1)
    def _():
        y = jnp.dot(ycat_ref[...].astype(jnp.bfloat16), w_out_bf_ref[...], preferred_element_type=jnp.float32)
        y_ref[...] = x_ref[...] + _rms_norm(y, post_g_ref[...]).reshape(nseq, R, D_MODEL)[:, 0:half, :]


def _sample_layer(x, conv_past, ckt, cvt, mkt, mvt, pre_g, post_g, w_in_t, conv_w, sinks, w_out, *, ns, group):
    N = ckt.shape[0]
    R = SAMPLE_ROWS
    nseq = ns * group
    per_group = lambda shape: pl.BlockSpec(shape, lambda o, i: (o,) + (0,) * (len(shape) - 1))
    per_step = lambda shape: pl.BlockSpec(shape, lambda o, i: (o * group + i,) + (0,) * (len(shape) - 1))
    resident = pl.BlockSpec(memory_space=pltpu.VMEM)
    kernel = functools.partial(_sample_kernel, ns=ns, group=group)
    return pl.pallas_call(
        kernel,
        grid=(N // nseq, group),
        in_specs=[
            per_group((nseq, R // 2, D_MODEL)),
            per_group((nseq, CONV_WIDTH - 1, D_CONV)),
            per_step((ns, D_SWA_KV, WINDOW)),
            per_step((ns, D_SWA_KV, WINDOW)),
            per_step((ns, D_MEMQ, N_MEM)),
            per_step((ns, D_MEMQ, N_MEM)),
            resident,
            resident,
            resident,
            resident,
            pl.BlockSpec(memory_space=pltpu.SMEM),
            resident,
        ],
        out_specs=[
            per_group((nseq, R // 2, D_MODEL)),
            per_group((nseq, CONV_WIDTH - 1, D_CONV)),
            per_step((ns, D_SWA_KV, WINDOW)),
            per_step((ns, D_SWA_KV, WINDOW)),
            pl.BlockSpec((D_MODEL, D_MODEL), lambda o, i: (0, 0)),
        ],
        out_shape=[
            jax.ShapeDtypeStruct((N, R // 2, D_MODEL), jnp.float32),
            jax.ShapeDtypeStruct((N, CONV_WIDTH - 1, D_CONV), jnp.float32),
            jax.ShapeDtypeStruct((N, D_SWA_KV, WINDOW), jnp.float32),
            jax.ShapeDtypeStruct((N, D_SWA_KV, WINDOW), jnp.float32),
            jax.ShapeDtypeStruct((D_MODEL, D_MODEL), jnp.bfloat16),
        ],
        scratch_shapes=[
            pltpu.VMEM((nseq * R, D_IN), jnp.float32),
            pltpu.VMEM((nseq * R, D_MODEL), jnp.float32),
        ],
        compiler_params=pltpu.CompilerParams(
            dimension_semantics=("arbitrary", "arbitrary"),
            vmem_limit_bytes=SAMPLE_VMEM_LIMIT_BYTES),
        name="sample_layer",
    )(x, conv_past, ckt, cvt, mkt, mvt, pre_g, post_g, w_in_t, conv_w, sinks, w_out)


def _heads_last_to_keys_last(a):
    n, keys, heads, dim = a.shape
    return jnp.transpose(a, (0, 2, 3, 1)).reshape(n, heads * dim, keys)


def _keys_last_to_heads_last(a, heads):
    n, hd, keys = a.shape
    return jnp.transpose(a.reshape(n, heads, hd // heads, keys), (0, 3, 1, 2))[None]


def kernel(x_prompt, x_sample, mem_prompt, state_conv, cache_swa_k, cache_swa_v, cache_mem_k, cache_mem_v,
           pre_norm_g, post_norm_g, w_in, conv_w, attn_sinks, mem_norm_g, w_mem_k, w_mem_v, w_out):
    assert w_in.shape[0] == 1, "one layer, as the problem states"
    N, TS, _ = x_sample.shape
    assert TS == SAMPLE_ROWS // 2 and cache_swa_k.shape[2] == WINDOW
    l = 0

    pre_g = pre_norm_g[l].reshape(1, D_MODEL)
    post_g = post_norm_g[l].reshape(1, D_MODEL)
    w_in_t = w_in[l].astype(jnp.bfloat16).T
    sinks = attn_sinks[l].astype(jnp.float32)

    y_s, conv_s, kt_s, vt_s, w_out_bf = _sample_layer(
        x_sample, state_conv[l],
        _heads_last_to_keys_last(cache_swa_k[l]), _heads_last_to_keys_last(cache_swa_v[l]),
        _heads_last_to_keys_last(cache_mem_k[l]), _heads_last_to_keys_last(cache_mem_v[l]),
        pre_g, post_g, w_in_t, conv_w[l], sinks, w_out[l], ns=SAMPLE_NS, group=SAMPLE_GROUP)

    mkt, mvt, mkb, mvtb = _mem_kv(mem_prompt, mem_norm_g[l], w_mem_k[l], w_mem_v[l])
    y_p, conv_p, kt_p, vt_p = _prompt_layer(
        x_prompt, pre_g, post_g, w_in_t, conv_w[l].T, sinks, mkb, mvtb, w_out_bf, conv_s,
        tq=PROMPT_TQ, nsub=PROMPT_NSUB)

    return (y_p, y_s,
            conv_p[None],
            _keys_last_to_heads_last(kt_p, N_SWA_KV), _keys_last_to_heads_last(vt_p, N_SWA_KV),
            _keys_last_to_heads_last(mkt, N_MEM_HEADS), _keys_last_to_heads_last(mvt, N_MEM_HEADS),
            conv_s[None],
            _keys_last_to_heads_last(kt_s, N_SWA_KV), _keys_last_to_heads_last(vt_s, N_SWA_KV))
```

```python
import functools

import numpy as np
import jax
import jax.numpy as jnp
from jax import lax
from jax.experimental import pallas as pl
from jax.experimental.pallas import tpu as pltpu

D_MODEL = 1024
HEAD_DIM = 64
D_CONV = 384
N_MEM_HEADS = 4
D_MEMQ = N_MEM_HEADS * HEAD_DIM
D_SWA = 384
N_SWA_HEADS = 6
N_SWA_KV = 2
SWA_GROUP = N_SWA_HEADS // N_SWA_KV
D_SWA_KV = N_SWA_KV * HEAD_DIM
N_MEM = 256
CONV_WIDTH = 3
WINDOW = 128
BLOCK = 128
RMS_EPS = 1e-6
NEG_INF = -1e30
D_IN = 3072
QK_SCALE = HEAD_DIM ** -0.5
LOG2_E = float(np.log2(np.e))
QK_SCALE_LOG2 = QK_SCALE * LOG2_E

OFF_CB, OFF_CC, OFF_CH, OFF_CZ = 0, 384, 768, 1152
OFF_Q, OFF_K, OFF_V, OFF_SZ = 1536, 1920, 2048, 2176
OFF_MQ, OFF_MZ = 2560, 2816
YOFF_CONV, YOFF_SWA, YOFF_MEM = 0, 384, 768

PAIR = 2 * HEAD_DIM

MIB = 1024 * 1024
PROMPT_VMEM_LIMIT_BYTES = 44 * MIB
SAMPLE_VMEM_LIMIT_BYTES = 48 * MIB

MEM_KV_BATCHES = 2
PROMPT_TQ = 512
PROMPT_NSUB = 2
IN_PROJ_CHUNK = 512
ATTN_WIDTH = 2
SAMPLE_NS = 16
SAMPLE_GROUP = 2
SAMPLE_ROWS = 8
SAMPLE_WIDTH = 16
SAMPLE_NEW = 16


def _alibi_slope(h):
    return float(np.power(np.float32(2.0), np.float32(-8.0 * (h + 1) / N_SWA_HEADS)))


def _rms_norm(x, g):
    return x * lax.rsqrt(jnp.mean(x * x, axis=-1, keepdims=True) + RMS_EPS) * g


def _silu(x):
    return x * jax.nn.sigmoid(x)


_NT = (((1,), (1,)), ((), ()))
_TN = (((0,), (0,)), ((), ()))


def _transpose_cast_kernel(w_ref, wt_ref):
    wt_ref[...] = w_ref[...].T.astype(jnp.bfloat16)


def _transposed_bf16(w, cols):
    K, N = w.shape
    return pl.pallas_call(
        _transpose_cast_kernel,
        grid=(N // cols,),
        in_specs=[pl.BlockSpec((K, cols), lambda i: (0, i))],
        out_specs=pl.BlockSpec((cols, K), lambda i: (i, 0)),
        out_shape=jax.ShapeDtypeStruct((N, K), jnp.bfloat16),
        compiler_params=pltpu.CompilerParams(dimension_semantics=("arbitrary",)),
        name="transpose_cast",
    )(w)


def _mem_kv_kernel(mem_ref, g_ref, wk_ref, wv_ref, mkt_ref, mvt_ref, mkb_ref, mvtb_ref, *, nb):
    mem = mem_ref[...].reshape(nb * N_MEM, D_MODEL)
    m = _rms_norm(mem, g_ref[...]).astype(jnp.bfloat16)
    mk = jnp.dot(m, wk_ref[...].astype(jnp.bfloat16), preferred_element_type=jnp.float32)
    mv = jnp.dot(m, wv_ref[...].astype(jnp.bfloat16), preferred_element_type=jnp.float32)
    for b in range(nb):
        mk_b = mk[b * N_MEM:(b + 1) * N_MEM]
        mv_t = mv[b * N_MEM:(b + 1) * N_MEM].T
        mkt_ref[b] = mk_b.T
        mvt_ref[b] = mv_t
        mkb_ref[b] = mk_b.astype(jnp.bfloat16)
        mvtb_ref[b] = mv_t.astype(jnp.bfloat16)


def _mem_kv(mem, mem_g, w_mk, w_mv):
    B = mem.shape[0]
    nb = MEM_KV_BATCHES
    full = lambda shape: pl.BlockSpec(shape, lambda b: (0,) * len(shape))
    per_batch = pl.BlockSpec((nb, N_MEM, D_MEMQ), lambda b: (b, 0, 0))
    return pl.pallas_call(
        functools.partial(_mem_kv_kernel, nb=nb),
        grid=(B // nb,),
        in_specs=[
            pl.BlockSpec((nb, N_MEM, D_MODEL), lambda b: (b, 0, 0)),
            full((1, D_MODEL)),
            full((D_MODEL, D_MEMQ)),
            full((D_MODEL, D_MEMQ)),
        ],
        out_specs=[per_batch] * 4,
        out_shape=[
            jax.ShapeDtypeStruct((B, D_MEMQ, N_MEM), jnp.float32),
            jax.ShapeDtypeStruct((B, D_MEMQ, N_MEM), jnp.float32),
            jax.ShapeDtypeStruct((B, N_MEM, D_MEMQ), jnp.bfloat16),
            jax.ShapeDtypeStruct((B, D_MEMQ, N_MEM), jnp.bfloat16),
        ],
        compiler_params=pltpu.CompilerParams(dimension_semantics=("arbitrary",)),
        name="mem_kv",
    )(mem, mem_g.reshape(1, D_MODEL), w_mk, w_mv)


def _prompt_kernel(x_ref, x_next_ref, pre_g_ref, post_g_ref, w_in_t_ref, conv_w_t_ref, sink_ref, mkb_ref, mvt_ref,
                   w_out_ref, run_after_ref,
                   y_ref, conv_state_ref, k_state_ref, v_state_ref,
                   zt_ref, ycat_ref, kbuf_ref, vbuf_ref, uprev_ref, bias_ref, *, tq, nsub):
    assert nsub % 2 == 0
    t = pl.program_id(1)
    nblk = tq // BLOCK

    @pl.when((pl.program_id(0) == 0) & (t == 0))
    def _():
        c = lax.broadcasted_iota(jnp.int32, (BLOCK, BLOCK), 0)
        r = lax.broadcasted_iota(jnp.int32, (BLOCK, BLOCK), 1)
        distf = (r - c + jnp.where(c > r, BLOCK, 0)).astype(jnp.float32)
        for h in range(N_SWA_HEADS):
            g, i = divmod(h, SWA_GROUP)
            bias_ref[g, :, i * BLOCK:(i + 1) * BLOCK] = (-_alibi_slope(h) * LOG2_E) * distf

    @pl.when(t == 0)
    def _():
        kbuf_ref[0:BLOCK, :] = jnp.zeros((BLOCK, D_SWA_KV), jnp.bfloat16)
        vbuf_ref[:, 0:BLOCK] = jnp.zeros((D_SWA_KV, BLOCK), jnp.bfloat16)
        uprev_ref[...] = jnp.zeros_like(uprev_ref)

    first_pen = jnp.where(t == 0, NEG_INF, 0.0)
    slot = lax.broadcasted_iota(jnp.int32, (BLOCK, SWA_GROUP * BLOCK), 0)
    query = lax.broadcasted_iota(jnp.int32, (BLOCK, SWA_GROUP * BLOCK), 1) % BLOCK
    from_prev = slot > query
    q_zero = jnp.zeros((HEAD_DIM, SWA_GROUP * BLOCK), jnp.bfloat16)
    cw = conv_w_t_ref[...]
    carry = {"u_prev": uprev_ref[...]}
    head_of_lane = lax.broadcasted_iota(jnp.int32, (1, SWA_GROUP * BLOCK), 1) // BLOCK
    sink_rows = []
    for g in range(N_SWA_KV):
        row = jnp.full((1, SWA_GROUP * BLOCK), sink_ref[g * SWA_GROUP], jnp.float32)
        for i in range(1, SWA_GROUP):
            row = jnp.where(head_of_lane == i, sink_ref[g * SWA_GROUP + i], row)
        sink_rows.append(row * LOG2_E)

    def in_proj_chunks(x_rows, zt):
        state = {}

        def norm():
            state["h"] = _rms_norm(x_rows(), pre_g_ref[...]).astype(jnp.bfloat16)

        def chunk(lo):
            rows = slice(lo, lo + IN_PROJ_CHUNK)
            zt[rows, :] = lax.dot_general(w_in_t_ref[rows, :], state["h"], _NT,
                                          preferred_element_type=jnp.float32)

        return norm, [functools.partial(chunk, lo) for lo in range(0, D_IN, IN_PROJ_CHUNK)]

    def mixer_units(sub):
        zt = zt_ref.at[sub % 2]
        ycat = ycat_ref.at[sub % 2]
        tok0 = sub * tq

        def conv_unit():
            u = zt[OFF_CC:OFF_CC + D_CONV, :] * zt[OFF_CH:OFF_CH + D_CONV, :]
            ucat = jnp.concatenate([carry["u_prev"], u], axis=1)
            conv = (cw[:, 0:1] * pltpu.roll(ucat, 2, axis=1)[:, BLOCK:]
                    + cw[:, 1:2] * pltpu.roll(ucat, 1, axis=1)[:, BLOCK:]
                    + cw[:, 2:3] * u)
            y_conv = zt[OFF_CB:OFF_CB + D_CONV, :] * conv * _silu(zt[OFF_CZ:OFF_CZ + D_CONV, :])
            ycat[YOFF_CONV:YOFF_CONV + D_CONV, :] = y_conv.astype(jnp.bfloat16)
            carry["u_prev"] = u[:, tq - BLOCK:]

        def kv_unit():
            k_nat = zt[OFF_K:OFF_K + D_SWA_KV, :].T
            kbuf_ref[BLOCK + tok0:BLOCK + tok0 + tq, :] = k_nat.astype(jnp.bfloat16)
            vbuf_ref[:, BLOCK + tok0:BLOCK + tok0 + tq] = zt[OFF_V:OFF_V + D_SWA_KV, :].astype(jnp.bfloat16)

        def swa_unit(j, g):
            cols = slice(j * BLOCK, (j + 1) * BLOCK)
            band = slice(tok0 + j * BLOCK, tok0 + (j + 2) * BLOCK)
            q0 = OFF_Q + g * SWA_GROUP * HEAD_DIM
            qt = jnp.concatenate(
                [zt[q0 + i * HEAD_DIM:q0 + (i + 1) * HEAD_DIM, cols] for i in range(SWA_GROUP)],
                axis=1)
            qt = (qt * QK_SCALE_LOG2).astype(jnp.bfloat16)
            qt = jnp.concatenate([qt, q_zero] if g == 0 else [q_zero, qt], axis=0)
            s = jnp.dot(kbuf_ref[band, :], qt, preferred_element_type=jnp.float32)
            yield
            s = jnp.where(from_prev, s[0:BLOCK], s[BLOCK:]) + bias_ref[g]
            if sub == 0 and j == 0:
                s = s + jnp.where(from_prev, first_pen, 0.0)
            sink = sink_rows[g]
            m = jnp.maximum(jnp.max(s, axis=0, keepdims=True), sink)
            p = jnp.exp2(s - m)
            l = jnp.sum(p, axis=0, keepdims=True) + jnp.exp2(sink - m)
            p = jnp.concatenate([jnp.where(from_prev, p, 0.0), jnp.where(from_prev, 0.0, p)],
                                axis=0).astype(jnp.bfloat16)
            yield
            vband = vbuf_ref[g * HEAD_DIM:(g + 1) * HEAD_DIM, band]
            o = jnp.dot(vband, p, preferred_element_type=jnp.float32)
            o = o / l
            for i in range(SWA_GROUP):
                hh = g * SWA_GROUP + i
                gate = _silu(zt[OFF_SZ + hh * HEAD_DIM:OFF_SZ + (hh + 1) * HEAD_DIM, cols])
                ycat[YOFF_SWA + hh * HEAD_DIM:YOFF_SWA + (hh + 1) * HEAD_DIM, cols] = (
                    o[:, i * BLOCK:(i + 1) * BLOCK] * gate).astype(jnp.bfloat16)

        def mem_unit(hh):
            rows = slice(OFF_MQ + hh * HEAD_DIM, OFF_MQ + (hh + 1) * HEAD_DIM)
            qt = (zt[rows, :] * QK_SCALE_LOG2).astype(jnp.bfloat16)
            pieces = [jnp.zeros((HEAD_DIM, tq), jnp.bfloat16)] * N_MEM_HEADS
            pieces[hh] = qt
            s = jnp.dot(mkb_ref[0], jnp.concatenate(pieces, axis=0),
                        preferred_element_type=jnp.float32)
            yield
            m = jnp.max(s, axis=0, keepdims=True)
            p = jnp.exp2(s - m)
            l = jnp.sum(p, axis=0, keepdims=True)
            p = p.astype(jnp.bfloat16)
            yield
            o = jnp.dot(mvt_ref[0, hh * HEAD_DIM:(hh + 1) * HEAD_DIM, :], p,
                        preferred_element_type=jnp.float32)
            gate = _silu(zt[OFF_MZ + hh * HEAD_DIM:OFF_MZ + (hh + 1) * HEAD_DIM, :])
            ycat[YOFF_MEM + hh * HEAD_DIM:YOFF_MEM + (hh + 1) * HEAD_DIM, :] = (
                o / l * gate).astype(jnp.bfloat16)

        def out_proj():
            carry["y"] = lax.dot_general(ycat[...], w_out_ref[...], _TN,
                                         preferred_element_type=jnp.float32)

        def post_norm(blk):
            rows = slice(blk * BLOCK, (blk + 1) * BLOCK)
            out_rows = slice(tok0 + blk * BLOCK, tok0 + (blk + 1) * BLOCK)
            y_ref[0, out_rows, :] = x_ref[0, out_rows, :] + _rms_norm(carry["y"][rows], post_g_ref[...])

        attention = [functools.partial(swa_unit, j, g) for j in range(nblk) for g in range(N_SWA_KV)]
        attention += [functools.partial(mem_unit, hh) for hh in range(N_MEM_HEADS)]
        return [conv_unit, kv_unit], attention, (out_proj, [functools.partial(post_norm, b) for b in range(nblk)])

    def interleave(units, chunks, norm_first, norm_after):
        (conv_unit, kv_unit), attention, (out_proj, post_norms) = units
        pending = list(chunks)
        kv_unit()
        n_rounds = -(-len(attention) // ATTN_WIDTH) + 2
        chunk_rounds = [k * n_rounds // len(pending) for k in range(len(pending))]
        todo = list(attention)
        active = []
        for rnd in range(n_rounds):
            for _ in range(ATTN_WIDTH):
                if todo:
                    active.append(todo.pop(0)())
            for gen in list(active):
                if next(gen, "done") == "done":
                    active.remove(gen)
            if rnd == 0 and norm_first is not None:
                norm_first()
            for _ in range(chunk_rounds.count(rnd)):
                pending.pop(0)()
            if rnd == 0:
                conv_unit()
        assert not todo and not active and not pending
        if norm_after is not None:
            norm_after()
        out_proj()
        for post_norm in post_norms:
            post_norm()

    @pl.when((pl.program_id(0) == 0) & (t == 0))
    def _():
        norm, chunks = in_proj_chunks(lambda: x_ref[0, 0:tq, :], zt_ref.at[0])
        norm()
        for chunk in chunks:
            chunk()

    proj = [in_proj_chunks(lambda sub=sub: x_ref[0, sub * tq:(sub + 1) * tq, :], zt_ref.at[sub % 2])
            for sub in range(1, nsub)]
    proj.append(in_proj_chunks(lambda: x_next_ref[0], zt_ref.at[0]))
    for sub in range(nsub):
        norm_first = proj[0][0] if sub == 0 else None
        norm_after = proj[sub + 1][0] if sub + 1 < nsub else None
        interleave(mixer_units(sub), proj[sub][1], norm_first, norm_after)

    tile = nsub * tq
    uprev_ref[...] = carry["u_prev"]
    conv_state_ref[0] = carry["u_prev"].T[BLOCK - (CONV_WIDTH - 1):, :]
    kbuf_ref[0:BLOCK, :] = kbuf_ref[tile:tile + BLOCK, :]
    vbuf_ref[:, 0:BLOCK] = vbuf_ref[:, tile:tile + BLOCK]
    k_state_ref[0] = zt_ref[1, OFF_K:OFF_K + D_SWA_KV, tq - BLOCK:]
    v_state_ref[0] = zt_ref[1, OFF_V:OFF_V + D_SWA_KV, tq - BLOCK:]


def _prompt_layer(x, pre_g, post_g, w_in_t, conv_w_t, sinks, mkb, mvtb, w_out, run_after, *, tq, nsub):
    B, T, _ = x.shape
    tile = tq * nsub
    kernel = functools.partial(_prompt_kernel, tq=tq, nsub=nsub)
    steps = T // tile

    def next_first_sub_tile(b, t):
        nxt = jnp.minimum(b * steps + t + 1, B * steps - 1)
        return nxt // steps, (nxt % steps) * nsub, 0

    resident = pl.BlockSpec(memory_space=pltpu.VMEM)
    return pl.pallas_call(
        kernel,
        grid=(B, steps),
        in_specs=[
            pl.BlockSpec((1, tile, D_MODEL), lambda b, t: (b, t, 0)),
            pl.BlockSpec((1, tq, D_MODEL), next_first_sub_tile),
            resident,
            resident,
            resident,
            resident,
            pl.BlockSpec(memory_space=pltpu.SMEM),
            pl.BlockSpec((1, N_MEM, D_MEMQ), lambda b, t: (b, 0, 0)),
            pl.BlockSpec((1, D_MEMQ, N_MEM), lambda b, t: (b, 0, 0)),
            resident,
            pl.BlockSpec(memory_space=pl.ANY),
        ],
        out_specs=[
            pl.BlockSpec((1, tile, D_MODEL), lambda b, t: (b, t, 0)),
            pl.BlockSpec((1, CONV_WIDTH - 1, D_CONV), lambda b, t: (b, 0, 0)),
            pl.BlockSpec((1, D_SWA_KV, BLOCK), lambda b, t: (b, 0, 0)),
            pl.BlockSpec((1, D_SWA_KV, BLOCK), lambda b, t: (b, 0, 0)),
        ],
        out_shape=[
            jax.ShapeDtypeStruct((B, T, D_MODEL), jnp.float32),
            jax.ShapeDtypeStruct((B, CONV_WIDTH - 1, D_CONV), jnp.float32),
            jax.ShapeDtypeStruct((B, D_SWA_KV, BLOCK), jnp.float32),
            jax.ShapeDtypeStruct((B, D_SWA_KV, BLOCK), jnp.float32),
        ],
        scratch_shapes=[
            pltpu.VMEM((2, D_IN, tq), jnp.float32),
            pltpu.VMEM((2, D_MODEL, tq), jnp.bfloat16),
            pltpu.VMEM((BLOCK + tile, D_SWA_KV), jnp.bfloat16),
            pltpu.VMEM((D_SWA_KV, BLOCK + tile), jnp.bfloat16),
            pltpu.VMEM((D_CONV, BLOCK), jnp.float32),
            pltpu.VMEM((N_SWA_KV, BLOCK, SWA_GROUP * BLOCK), jnp.float32),
        ],
        compiler_params=pltpu.CompilerParams(
            dimension_semantics=("arbitrary", "arbitrary"),
            vmem_limit_bytes=PROMPT_VMEM_LIMIT_BYTES),
        name="prompt_layer",
    )(x, x, pre_g, post_g, w_in_t, conv_w_t, sinks, mkb, mvtb, w_out, run_after)


def _sample_kernel(x_ref, conv_past_ref, ckt_ref, cvt_ref, mkt_ref, mvt_ref,
                   pre_g_ref, post_g_ref, w_in_t_ref, conv_w_ref, sink_ref, w_out_ref,
                   y_ref, conv_state_ref, kt_state_ref, vt_state_ref, w_out_bf_ref,
                   z_ref, ycat_ref, *, ns, group):
    R = SAMPLE_ROWS
    half = R // 2
    nseq = ns * group
    step = pl.program_id(1)

    @pl.when((pl.program_id(0) == 0) & (step == 0))
    def _():
        w_out_bf_ref[...] = w_out_ref[...].astype(jnp.bfloat16)

    @pl.when(step == 0)
    def _():
        x4 = x_ref[...]
        x8 = jnp.concatenate([x4, x4], axis=1).reshape(nseq * R, D_MODEL)
        h = _rms_norm(x8, pre_g_ref[...]).astype(jnp.bfloat16)
        z_ref[...] = lax.dot_general(h, w_in_t_ref[...], _NT, preferred_element_type=jnp.float32)
        u = (z_ref[:, OFF_CC:OFF_CC + D_CONV] * z_ref[:, OFF_CH:OFF_CH + D_CONV]).reshape(nseq, R, D_CONV)
        row3 = lax.broadcasted_iota(jnp.int32, (nseq, R, D_CONV), 1)
        past = conv_past_ref[...]
        past = jnp.concatenate([past, jnp.zeros((nseq, R - (CONV_WIDTH - 1), D_CONV), past.dtype)], axis=1)
        u_full = jnp.where(row3 < CONV_WIDTH - 1, past, pltpu.roll(u, CONV_WIDTH - 1, axis=1))
        cw = conv_w_ref[...]
        conv = (cw[0:1, :] * u_full
                + cw[1:2, :] * pltpu.roll(u_full, R - 1, axis=1)
                + cw[2:3, :] * pltpu.roll(u_full, R - 2, axis=1))
        conv_state_ref[...] = pltpu.roll(u_full, R - half, axis=1)[:, 0:CONV_WIDTH - 1, :]
        y_conv = (z_ref[:, OFF_CB:OFF_CB + D_CONV] * conv.reshape(nseq * R, D_CONV)
                  * _silu(z_ref[:, OFF_CZ:OFF_CZ + D_CONV]))
        ycat_ref[:, YOFF_CONV:YOFF_CONV + D_CONV] = y_conv

    row = lax.broadcasted_iota(jnp.int32, (R, PAIR), 0)
    lane = lax.broadcasted_iota(jnp.int32, (R, PAIR), 1)
    lo_row = row < half
    lo_lane = lane < HEAD_DIM
    diag = lo_row == lo_lane
    lane_sq = lax.broadcasted_iota(jnp.int32, (WINDOW, WINDOW), 1)

    def pair_bias(ncols, dist_of):
        rr = lax.broadcasted_iota(jnp.int32, (R, ncols), 0)
        cc = lax.broadcasted_iota(jnp.int32, (R, ncols), 1)
        dist, valid = dist_of(rr % half, cc)
        distf = dist.astype(jnp.float32)
        tiles = []
        for pair in range(N_SWA_HEADS // 2):
            slope = jnp.where(rr < half, _alibi_slope(2 * pair), _alibi_slope(2 * pair + 1))
            tiles.append(jnp.where(valid, -slope * distf, NEG_INF))
        return jnp.concatenate(tiles, axis=0)

    def cached_dist(tok, c):
        d = tok + WINDOW - c
        return d, d < WINDOW

    def new_dist(tok, c):
        d = tok - c
        return d, (d >= 0) & (c < half)

    bias_c = pair_bias(WINDOW, cached_dist)
    bias_n = pair_bias(SAMPLE_NEW, new_dist)
    head_of_row = lax.broadcasted_iota(jnp.int32, (N_SWA_HEADS * half, 1), 0) // half
    sink_col = jnp.full((N_SWA_HEADS * half, 1), sink_ref[0], jnp.float32)
    for hh in range(1, N_SWA_HEADS):
        sink_col = jnp.where(head_of_row == hh, sink_ref[hh], sink_col)
    state_pad = jnp.zeros((WINDOW - R, D_SWA_KV), jnp.float32)

    def seq_rows(n):
        return pl.ds(pl.multiple_of((step * ns + n) * R, R), R)

    def swa_unit(n):
        rows = seq_rows(n)
        qa = z_ref[rows, OFF_Q:OFF_Q + PAIR] * QK_SCALE
        qb = z_ref[rows, OFF_Q + PAIR:OFF_Q + 2 * PAIR] * QK_SCALE
        qc = z_ref[rows, OFF_Q + 2 * PAIR:OFF_Q + 3 * PAIR] * QK_SCALE
        t0 = jnp.where(lo_lane, jnp.where(lo_row, qa, pltpu.roll(qa, HEAD_DIM, axis=1)), 0.0)
        t1 = jnp.where(diag, qb, 0.0)
        t2 = jnp.where(lo_lane, 0.0, jnp.where(lo_row, pltpu.roll(qc, HEAD_DIM, axis=1), qc))
        qs = jnp.concatenate([t0, t1, t2], axis=0).astype(jnp.bfloat16)

        k_new = z_ref[rows, OFF_K:OFF_K + D_SWA_KV]
        v_new = z_ref[rows, OFF_V:OFF_V + D_SWA_KV]
        k_new_b = jnp.concatenate([k_new, k_new], axis=0).astype(jnp.bfloat16)
        v_new_b = jnp.concatenate([v_new, v_new], axis=0).astype(jnp.bfloat16)
        kt_old = ckt_ref[n].astype(jnp.bfloat16)

        s_c = jnp.dot(qs, kt_old, preferred_element_type=jnp.float32) + bias_c
        s_n = lax.dot_general(qs, k_new_b, _NT, preferred_element_type=jnp.float32) + bias_n
        yield
        m = jnp.maximum(jnp.maximum(jnp.max(s_c, axis=1, keepdims=True), jnp.max(s_n, axis=1, keepdims=True)),
                        sink_col)
        p_c = jnp.exp(s_c - m)
        p_n = jnp.exp(s_n - m)
        l = (jnp.sum(p_c, axis=1, keepdims=True) + jnp.sum(p_n, axis=1, keepdims=True) + jnp.exp(sink_col - m))
        p_c = p_c.astype(jnp.bfloat16)
        p_n = p_n.astype(jnp.bfloat16)
        yield
        o = (lax.dot_general(p_c, cvt_ref[n].astype(jnp.bfloat16), _NT, preferred_element_type=jnp.float32)
             + jnp.dot(p_n, v_new_b, preferred_element_type=jnp.float32)) / l
        yield
        o0, o1, o2 = o[0:R], o[R:2 * R], o[2 * R:3 * R]
        ya = jnp.where(lo_lane, o0, pltpu.roll(pltpu.roll(o0, HEAD_DIM, axis=1), half, axis=0))
        yb = jnp.where(lo_lane, o1, pltpu.roll(o1, half, axis=0))
        yc = jnp.where(lo_lane, pltpu.roll(o2, HEAD_DIM, axis=1), pltpu.roll(o2, half, axis=0))
        y_swa = jnp.concatenate([ya, yb, yc], axis=1) * _silu(z_ref[rows, OFF_SZ:OFF_SZ + D_SWA])
        ycat_ref[rows, YOFF_SWA:YOFF_SWA + D_SWA] = y_swa

    def state_unit(n):
        rows = seq_rows(n)
        k_new_t = jnp.concatenate([state_pad, z_ref[rows, OFF_K:OFF_K + D_SWA_KV]], axis=0).T
        v_new_t = jnp.concatenate([state_pad, z_ref[rows, OFF_V:OFF_V + D_SWA_KV]], axis=0).T
        keep = lane_sq < WINDOW - half
        kt_state_ref[n] = jnp.where(keep, pltpu.roll(ckt_ref[n], WINDOW - half, axis=1), k_new_t)
        vt_state_ref[n] = jnp.where(keep, pltpu.roll(cvt_ref[n], WINDOW - half, axis=1), v_new_t)

    def mem_unit(n):
        rows = seq_rows(n)
        m0 = z_ref[rows, OFF_MQ:OFF_MQ + PAIR] * QK_SCALE
        m1 = z_ref[rows, OFF_MQ + PAIR:OFF_MQ + 2 * PAIR] * QK_SCALE
        zero = jnp.zeros_like(m0)
        qm = jnp.concatenate(
            [jnp.concatenate([jnp.where(diag, m0, 0.0), zero], axis=1),
             jnp.concatenate([zero, jnp.where(diag, m1, 0.0)], axis=1)], axis=0).astype(jnp.bfloat16)
        s = jnp.dot(qm, mkt_ref[n].astype(jnp.bfloat16), preferred_element_type=jnp.float32)
        yield
        m = jnp.max(s, axis=1, keepdims=True)
        p = jnp.exp(s - m)
        l = jnp.sum(p, axis=1, keepdims=True)
        p = p.astype(jnp.bfloat16)
        yield
        o = lax.dot_general(p, mvt_ref[n].astype(jnp.bfloat16), _NT,
                            preferred_element_type=jnp.float32) / l
        yield
        oa, ob = o[0:R, 0:PAIR], o[R:2 * R, PAIR:2 * PAIR]
        y_mem = jnp.concatenate([jnp.where(lo_lane, oa, pltpu.roll(oa, half, axis=0)),
                                 jnp.where(lo_lane, ob, pltpu.roll(ob, half, axis=0))], axis=1)
        ycat_ref[rows, YOFF_MEM:YOFF_MEM + D_MEMQ] = y_mem * _silu(z_ref[rows, OFF_MZ:OFF_MZ + D_MEMQ])

    todo = [functools.partial(unit, n) for n in range(ns) for unit in (swa_unit, mem_unit)]
    states = [functools.partial(state_unit, n) for n in range(ns)]
    active = []
    while todo or active:
        for _ in range(SAMPLE_WIDTH):
            if todo:
                active.append(todo.pop(0)())
        for gen in list(active):
            if next(gen, "done") == "done":
                active.remove(gen)
        if states:
            states.pop(0)()
    for unit in states:
        unit()

    @pl.when(step == group - 1)
    def _():
        y = jnp.dot(ycat_ref[...].astype(jnp.bfloat16), w_out_bf_ref[...], preferred_element_type=jnp.float32)
        y_ref[...] = x_ref[...] + _rms_norm(y, post_g_ref[...]).reshape(nseq, R, D_MODEL)[:, 0:half, :]


def _sample_layer(x, conv_past, ckt, cvt, mkt, mvt, pre_g, post_g, w_in_t, conv_w, sinks, w_out, *, ns, group):
    N = ckt.shape[0]
    R = SAMPLE_ROWS
    nseq = ns * group
    per_group = lambda shape: pl.BlockSpec(shape, lambda o, i: (o,) + (0,) * (len(shape) - 1))
    per_step = lambda shape: pl.BlockSpec(shape, lambda o, i: (o * group + i,) + (0,) * (len(shape) - 1))
    resident = pl.BlockSpec(memory_space=pltpu.VMEM)
    kernel = functools.partial(_sample_kernel, ns=ns, group=group)
    return pl.pallas_call(
        kernel,
        grid=(N // nseq, group),
        in_specs=[
            per_group((nseq, R // 2, D_MODEL)),
            per_group((nseq, CONV_WIDTH - 1, D_CONV)),
            per_step((ns, D_SWA_KV, WINDOW)),
            per_step((ns, D_SWA_KV, WINDOW)),
            per_step((ns, D_MEMQ, N_MEM)),
            per_step((ns, D_MEMQ, N_MEM)),
            resident,
            resident,
            resident,
            resident,
            pl.BlockSpec(memory_space=pltpu.SMEM),
            resident,
        ],
        out_specs=[
            per_group((nseq, R // 2, D_MODEL)),
            per_group((nseq, CONV_WIDTH - 1, D_CONV)),
            per_step((ns, D_SWA_KV, WINDOW)),
            per_step((ns, D_SWA_KV, WINDOW)),
            pl.BlockSpec((D_MODEL, D_MODEL), lambda o, i: (0, 0)),
        ],
        out_shape=[
            jax.ShapeDtypeStruct((N, R // 2, D_MODEL), jnp.float32),
            jax.ShapeDtypeStruct((N, CONV_WIDTH - 1, D_CONV), jnp.float32),
            jax.ShapeDtypeStruct((N, D_SWA_KV, WINDOW), jnp.float32),
            jax.ShapeDtypeStruct((N, D_SWA_KV, WINDOW), jnp.float32),
            jax.ShapeDtypeStruct((D_MODEL, D_MODEL), jnp.bfloat16),
        ],
        scratch_shapes=[
            pltpu.VMEM((nseq * R, D_IN), jnp.float32),
            pltpu.VMEM((nseq * R, D_MODEL), jnp.float32),
        ],
        compiler_params=pltpu.CompilerParams(
            dimension_semantics=("arbitrary", "arbitrary"),
            vmem_limit_bytes=SAMPLE_VMEM_LIMIT_BYTES),
        name="sample_layer",
    )(x, conv_past, ckt, cvt, mkt, mvt, pre_g, post_g, w_in_t, conv_w, sinks, w_out)


def _heads_last_to_keys_last(a):
    n, keys, heads, dim = a.shape
    return jnp.transpose(a, (0, 2, 3, 1)).reshape(n, heads * dim, keys)


def _keys_last_to_heads_last(a, heads):
    n, hd, keys = a.shape
    return jnp.transpose(a.reshape(n, heads, hd // heads, keys), (0, 3, 1, 2))[None]


def kernel(x_prompt, x_sample, mem_prompt, state_conv, cache_swa_k, cache_swa_v, cache_mem_k, cache_mem_v,
           pre_norm_g, post_norm_g, w_in, conv_w, attn_sinks, mem_norm_g, w_mem_k, w_mem_v, w_out):
    assert w_in.shape[0] == 1, "one layer, as the problem states"
    N, TS, _ = x_sample.shape
    assert TS == SAMPLE_ROWS // 2 and cache_swa_k.shape[2] == WINDOW
    l = 0

    pre_g = pre_norm_g[l].reshape(1, D_MODEL)
    post_g = post_norm_g[l].reshape(1, D_MODEL)
    w_in_t = _transposed_bf16(w_in[l], IN_PROJ_CHUNK)
    sinks = attn_sinks[l].astype(jnp.float32)

    y_s, conv_s, kt_s, vt_s, w_out_bf = _sample_layer(
        x_sample, state_conv[l],
        _heads_last_to_keys_last(cache_swa_k[l]), _heads_last_to_keys_last(cache_swa_v[l]),
        _heads_last_to_keys_last(cache_mem_k[l]), _heads_last_to_keys_last(cache_mem_v[l]),
        pre_g, post_g, w_in_t, conv_w[l], sinks, w_out[l], ns=SAMPLE_NS, group=SAMPLE_GROUP)

    mkt, mvt, mkb, mvtb = _mem_kv(mem_prompt, mem_norm_g[l], w_mem_k[l], w_mem_v[l])
    y_p, conv_p, kt_p, vt_p = _prompt_layer(
        x_prompt, pre_g, post_g, w_in_t, conv_w[l].T, sinks, mkb, mvtb, w_out_bf, conv_s,
        tq=PROMPT_TQ, nsub=PROMPT_NSUB)

    return (y_p, y_s,
            conv_p[None],
            _keys_last_to_heads_last(kt_p, N_SWA_KV), _keys_last_to_heads_last(vt_p, N_SWA_KV),
            _keys_last_to_heads_last(mkt, N_MEM_HEADS), _keys_last_to_heads_last(mvt, N_MEM_HEADS),
            conv_s[None],
            _keys_last_to_heads_last(kt_s, N_SWA_KV), _keys_last_to_heads_last(vt_s, N_SWA_KV))
```

```python
import functools

import numpy as np
import jax
import jax.numpy as jnp
from jax import lax
from jax.experimental import pallas as pl
from jax.experimental.pallas import tpu as pltpu

D_MODEL = 1024
HEAD_DIM = 64
D_CONV = 384
N_MEM_HEADS = 4
D_MEMQ = N_MEM_HEADS * HEAD_DIM
D_SWA = 384
N_SWA_HEADS = 6
N_SWA_KV = 2
SWA_GROUP = N_SWA_HEADS // N_SWA_KV
D_SWA_KV = N_SWA_KV * HEAD_DIM
N_MEM = 256
CONV_WIDTH = 3
WINDOW = 128
BLOCK = 128
RMS_EPS = 1e-6
NEG_INF = -1e30
D_IN = 3072
QK_SCALE = HEAD_DIM ** -0.5
LOG2_E = float(np.log2(np.e))
QK_SCALE_LOG2 = QK_SCALE * LOG2_E

OFF_CB, OFF_CC, OFF_CH, OFF_CZ = 0, 384, 768, 1152
OFF_Q, OFF_K, OFF_V, OFF_SZ = 1536, 1920, 2048, 2176
OFF_MQ, OFF_MZ = 2560, 2816
YOFF_CONV, YOFF_SWA, YOFF_MEM = 0, 384, 768

PAIR = 2 * HEAD_DIM

MIB = 1024 * 1024
PROMPT_VMEM_LIMIT_BYTES = 49 * MIB
SAMPLE_VMEM_LIMIT_BYTES = 48 * MIB

MEM_KV_BATCHES = 2
PROMPT_TQ = 512
PROMPT_NSUB = 2
IN_PROJ_CHUNK = 512
ATTN_WIDTH = 2
SAMPLE_NS = 16
SAMPLE_GROUP = 2
SAMPLE_ROWS = 8
SAMPLE_WIDTH = 16
SAMPLE_NEW = 16


def _alibi_slope(h):
    return float(np.power(np.float32(2.0), np.float32(-8.0 * (h + 1) / N_SWA_HEADS)))


def _rms_norm(x, g):
    return x * lax.rsqrt(jnp.mean(x * x, axis=-1, keepdims=True) + RMS_EPS) * g


def _silu(x):
    return x * jax.nn.sigmoid(x)


_NT = (((1,), (1,)), ((), ()))
_TN = (((0,), (0,)), ((), ()))


def _mem_kv_kernel(mem_ref, g_ref, wk_ref, wv_ref, mkt_ref, mvt_ref, mkb_ref, mvtb_ref, *, nb):
    mem = mem_ref[...].reshape(nb * N_MEM, D_MODEL)
    m = _rms_norm(mem, g_ref[...]).astype(jnp.bfloat16)
    mk = jnp.dot(m, wk_ref[...].astype(jnp.bfloat16), preferred_element_type=jnp.float32)
    mv = jnp.dot(m, wv_ref[...].astype(jnp.bfloat16), preferred_element_type=jnp.float32)
    for b in range(nb):
        mk_b = mk[b * N_MEM:(b + 1) * N_MEM]
        mv_t = mv[b * N_MEM:(b + 1) * N_MEM].T
        mkt_ref[b] = mk_b.T
        mvt_ref[b] = mv_t
        mkb_ref[b] = mk_b.astype(jnp.bfloat16)
        mvtb_ref[b] = mv_t.astype(jnp.bfloat16)


def _mem_kv(mem, mem_g, w_mk, w_mv):
    B = mem.shape[0]
    nb = MEM_KV_BATCHES
    full = lambda shape: pl.BlockSpec(shape, lambda b: (0,) * len(shape))
    per_batch = pl.BlockSpec((nb, N_MEM, D_MEMQ), lambda b: (b, 0, 0))
    return pl.pallas_call(
        functools.partial(_mem_kv_kernel, nb=nb),
        grid=(B // nb,),
        in_specs=[
            pl.BlockSpec((nb, N_MEM, D_MODEL), lambda b: (b, 0, 0)),
            full((1, D_MODEL)),
            full((D_MODEL, D_MEMQ)),
            full((D_MODEL, D_MEMQ)),
        ],
        out_specs=[per_batch] * 4,
        out_shape=[
            jax.ShapeDtypeStruct((B, D_MEMQ, N_MEM), jnp.float32),
            jax.ShapeDtypeStruct((B, D_MEMQ, N_MEM), jnp.float32),
            jax.ShapeDtypeStruct((B, N_MEM, D_MEMQ), jnp.bfloat16),
            jax.ShapeDtypeStruct((B, D_MEMQ, N_MEM), jnp.bfloat16),
        ],
        compiler_params=pltpu.CompilerParams(dimension_semantics=("arbitrary",)),
        name="mem_kv",
    )(mem, mem_g.reshape(1, D_MODEL), w_mk, w_mv)


def _prompt_kernel(x_ref, x_next_ref, pre_g_ref, post_g_ref, w_in_t_ref, conv_w_t_ref, sink_ref,
                   mem_ref, mem_g_ref, w_mk_ref, w_mv_ref,
                   w_out_ref, run_after_ref,
                   y_ref, conv_state_ref, k_state_ref, v_state_ref, mkt_ref, mvt_out_ref,
                   zt_ref, ycat_ref, kbuf_ref, vbuf_ref, uprev_ref, bias_ref, mkb_ref, mvt_ref, *, tq, nsub):
    assert nsub % 2 == 0
    t = pl.program_id(1)
    nblk = tq // BLOCK

    @pl.when((pl.program_id(0) == 0) & (t == 0))
    def _():
        c = lax.broadcasted_iota(jnp.int32, (BLOCK, BLOCK), 0)
        r = lax.broadcasted_iota(jnp.int32, (BLOCK, BLOCK), 1)
        distf = (r - c + jnp.where(c > r, BLOCK, 0)).astype(jnp.float32)
        for h in range(N_SWA_HEADS):
            g, i = divmod(h, SWA_GROUP)
            bias_ref[g, :, i * BLOCK:(i + 1) * BLOCK] = (-_alibi_slope(h) * LOG2_E) * distf

    @pl.when(t == 0)
    def _():
        kbuf_ref[0:BLOCK, :] = jnp.zeros((BLOCK, D_SWA_KV), jnp.bfloat16)
        vbuf_ref[:, 0:BLOCK] = jnp.zeros((D_SWA_KV, BLOCK), jnp.bfloat16)
        uprev_ref[...] = jnp.zeros_like(uprev_ref)
        m = _rms_norm(mem_ref[0], mem_g_ref[...]).astype(jnp.bfloat16)
        mk = jnp.dot(m, w_mk_ref[...].astype(jnp.bfloat16), preferred_element_type=jnp.float32)
        mv_t = jnp.dot(m, w_mv_ref[...].astype(jnp.bfloat16), preferred_element_type=jnp.float32).T
        mkt_ref[0] = mk.T
        mvt_out_ref[0] = mv_t
        mkb_ref[...] = mk.astype(jnp.bfloat16)
        mvt_ref[...] = mv_t.astype(jnp.bfloat16)

    first_pen = jnp.where(t == 0, NEG_INF, 0.0)
    slot = lax.broadcasted_iota(jnp.int32, (BLOCK, SWA_GROUP * BLOCK), 0)
    query = lax.broadcasted_iota(jnp.int32, (BLOCK, SWA_GROUP * BLOCK), 1) % BLOCK
    from_prev = slot > query
    q_zero = jnp.zeros((HEAD_DIM, SWA_GROUP * BLOCK), jnp.bfloat16)
    cw = conv_w_t_ref[...]
    carry = {"u_prev": uprev_ref[...]}
    head_of_lane = lax.broadcasted_iota(jnp.int32, (1, SWA_GROUP * BLOCK), 1) // BLOCK
    sink_rows = []
    for g in range(N_SWA_KV):
        row = jnp.full((1, SWA_GROUP * BLOCK), sink_ref[g * SWA_GROUP], jnp.float32)
        for i in range(1, SWA_GROUP):
            row = jnp.where(head_of_lane == i, sink_ref[g * SWA_GROUP + i], row)
        sink_rows.append(row * LOG2_E)

    def in_proj_chunks(x_rows, zt):
        state = {}

        def norm():
            state["h"] = _rms_norm(x_rows(), pre_g_ref[...]).astype(jnp.bfloat16)

        def chunk(lo):
            rows = slice(lo, lo + IN_PROJ_CHUNK)
            zt[rows, :] = lax.dot_general(w_in_t_ref[rows, :], state["h"], _NT,
                                          preferred_element_type=jnp.float32)

        return norm, [functools.partial(chunk, lo) for lo in range(0, D_IN, IN_PROJ_CHUNK)]

    def mixer_units(sub):
        zt = zt_ref.at[sub % 2]
        ycat = ycat_ref.at[sub % 2]
        tok0 = sub * tq

        def conv_unit():
            u = zt[OFF_CC:OFF_CC + D_CONV, :] * zt[OFF_CH:OFF_CH + D_CONV, :]
            ucat = jnp.concatenate([carry["u_prev"], u], axis=1)
            conv = (cw[:, 0:1] * pltpu.roll(ucat, 2, axis=1)[:, BLOCK:]
                    + cw[:, 1:2] * pltpu.roll(ucat, 1, axis=1)[:, BLOCK:]
                    + cw[:, 2:3] * u)
            y_conv = zt[OFF_CB:OFF_CB + D_CONV, :] * conv * _silu(zt[OFF_CZ:OFF_CZ + D_CONV, :])
            ycat[YOFF_CONV:YOFF_CONV + D_CONV, :] = y_conv.astype(jnp.bfloat16)
            carry["u_prev"] = u[:, tq - BLOCK:]

        def kv_unit():
            k_nat = zt[OFF_K:OFF_K + D_SWA_KV, :].T
            kbuf_ref[BLOCK + tok0:BLOCK + tok0 + tq, :] = k_nat.astype(jnp.bfloat16)
            vbuf_ref[:, BLOCK + tok0:BLOCK + tok0 + tq] = zt[OFF_V:OFF_V + D_SWA_KV, :].astype(jnp.bfloat16)

        def swa_unit(j, g):
            cols = slice(j * BLOCK, (j + 1) * BLOCK)
            band = slice(tok0 + j * BLOCK, tok0 + (j + 2) * BLOCK)
            q0 = OFF_Q + g * SWA_GROUP * HEAD_DIM
            qt = jnp.concatenate(
                [zt[q0 + i * HEAD_DIM:q0 + (i + 1) * HEAD_DIM, cols] for i in range(SWA_GROUP)],
                axis=1)
            qt = (qt * QK_SCALE_LOG2).astype(jnp.bfloat16)
            qt = jnp.concatenate([qt, q_zero] if g == 0 else [q_zero, qt], axis=0)
            s = jnp.dot(kbuf_ref[band, :], qt, preferred_element_type=jnp.float32)
            yield
            s = jnp.where(from_prev, s[0:BLOCK], s[BLOCK:]) + bias_ref[g]
            if sub == 0 and j == 0:
                s = s + jnp.where(from_prev, first_pen, 0.0)
            sink = sink_rows[g]
            m = jnp.maximum(jnp.max(s, axis=0, keepdims=True), sink)
            p = jnp.exp2(s - m)
            l = jnp.sum(p, axis=0, keepdims=True) + jnp.exp2(sink - m)
            p = jnp.concatenate([jnp.where(from_prev, p, 0.0), jnp.where(from_prev, 0.0, p)],
                                axis=0).astype(jnp.bfloat16)
            yield
            vband = vbuf_ref[g * HEAD_DIM:(g + 1) * HEAD_DIM, band]
            o = jnp.dot(vband, p, preferred_element_type=jnp.float32)
            o = o / l
            for i in range(SWA_GROUP):
                hh = g * SWA_GROUP + i
                gate = _silu(zt[OFF_SZ + hh * HEAD_DIM:OFF_SZ + (hh + 1) * HEAD_DIM, cols])
                ycat[YOFF_SWA + hh * HEAD_DIM:YOFF_SWA + (hh + 1) * HEAD_DIM, cols] = (
                    o[:, i * BLOCK:(i + 1) * BLOCK] * gate).astype(jnp.bfloat16)

        def mem_unit(hh):
            rows = slice(OFF_MQ + hh * HEAD_DIM, OFF_MQ + (hh + 1) * HEAD_DIM)
            qt = (zt[rows, :] * QK_SCALE_LOG2).astype(jnp.bfloat16)
            pieces = [jnp.zeros((HEAD_DIM, tq), jnp.bfloat16)] * N_MEM_HEADS
            pieces[hh] = qt
            s = jnp.dot(mkb_ref[...], jnp.concatenate(pieces, axis=0),
                        preferred_element_type=jnp.float32)
            yield
            m = jnp.max(s, axis=0, keepdims=True)
            p = jnp.exp2(s - m)
            l = jnp.sum(p, axis=0, keepdims=True)
            p = p.astype(jnp.bfloat16)
            yield
            o = jnp.dot(mvt_ref[hh * HEAD_DIM:(hh + 1) * HEAD_DIM, :], p,
                        preferred_element_type=jnp.float32)
            gate = _silu(zt[OFF_MZ + hh * HEAD_DIM:OFF_MZ + (hh + 1) * HEAD_DIM, :])
            ycat[YOFF_MEM + hh * HEAD_DIM:YOFF_MEM + (hh + 1) * HEAD_DIM, :] = (
                o / l * gate).astype(jnp.bfloat16)

        def out_proj():
            carry["y"] = lax.dot_general(ycat[...], w_out_ref[...], _TN,
                                         preferred_element_type=jnp.float32)

        def post_norm(blk):
            rows = slice(blk * BLOCK, (blk + 1) * BLOCK)
            out_rows = slice(tok0 + blk * BLOCK, tok0 + (blk + 1) * BLOCK)
            y_ref[0, out_rows, :] = x_ref[0, out_rows, :] + _rms_norm(carry["y"][rows], post_g_ref[...])

        attention = [functools.partial(swa_unit, j, g) for j in range(nblk) for g in range(N_SWA_KV)]
        attention += [functools.partial(mem_unit, hh) for hh in range(N_MEM_HEADS)]
        return [conv_unit, kv_unit], attention, (out_proj, [functools.partial(post_norm, b) for b in range(nblk)])

    def interleave(units, chunks, norm_first, norm_after):
        (conv_unit, kv_unit), attention, (out_proj, post_norms) = units
        pending = list(chunks)
        kv_unit()
        n_rounds = -(-len(attention) // ATTN_WIDTH) + 2
        chunk_rounds = [k * n_rounds // len(pending) for k in range(len(pending))]
        todo = list(attention)
        active = []
        for rnd in range(n_rounds):
            for _ in range(ATTN_WIDTH):
                if todo:
                    active.append(todo.pop(0)())
            for gen in list(active):
                if next(gen, "done") == "done":
                    active.remove(gen)
            if rnd == 0 and norm_first is not None:
                norm_first()
            for _ in range(chunk_rounds.count(rnd)):
                pending.pop(0)()
            if rnd == 0:
                conv_unit()
        assert not todo and not active and not pending
        if norm_after is not None:
            norm_after()
        out_proj()
        for post_norm in post_norms:
            post_norm()

    @pl.when((pl.program_id(0) == 0) & (t == 0))
    def _():
        norm, chunks = in_proj_chunks(lambda: x_ref[0, 0:tq, :], zt_ref.at[0])
        norm()
        for chunk in chunks:
            chunk()

    proj = [in_proj_chunks(lambda sub=sub: x_ref[0, sub * tq:(sub + 1) * tq, :], zt_ref.at[sub % 2])
            for sub in range(1, nsub)]
    proj.append(in_proj_chunks(lambda: x_next_ref[0], zt_ref.at[0]))
    for sub in range(nsub):
        norm_first = proj[0][0] if sub == 0 else None
        norm_after = proj[sub + 1][0] if sub + 1 < nsub else None
        interleave(mixer_units(sub), proj[sub][1], norm_first, norm_after)

    tile = nsub * tq
    uprev_ref[...] = carry["u_prev"]
    conv_state_ref[0] = carry["u_prev"].T[BLOCK - (CONV_WIDTH - 1):, :]
    kbuf_ref[0:BLOCK, :] = kbuf_ref[tile:tile + BLOCK, :]
    vbuf_ref[:, 0:BLOCK] = vbuf_ref[:, tile:tile + BLOCK]
    k_state_ref[0] = zt_ref[1, OFF_K:OFF_K + D_SWA_KV, tq - BLOCK:]
    v_state_ref[0] = zt_ref[1, OFF_V:OFF_V + D_SWA_KV, tq - BLOCK:]


def _prompt_layer(x, pre_g, post_g, w_in_t, conv_w_t, sinks, mem, mem_g, w_mk, w_mv, w_out, run_after, *, tq, nsub):
    B, T, _ = x.shape
    tile = tq * nsub
    kernel = functools.partial(_prompt_kernel, tq=tq, nsub=nsub)
    steps = T // tile

    def next_first_sub_tile(b, t):
        nxt = jnp.minimum(b * steps + t + 1, B * steps - 1)
        return nxt // steps, (nxt % steps) * nsub, 0

    resident = pl.BlockSpec(memory_space=pltpu.VMEM)
    return pl.pallas_call(
        kernel,
        grid=(B, steps),
        in_specs=[
            pl.BlockSpec((1, tile, D_MODEL), lambda b, t: (b, t, 0)),
            pl.BlockSpec((1, tq, D_MODEL), next_first_sub_tile, pipeline_mode=pl.Buffered(1)),
            resident,
            resident,
            resident,
            resident,
            pl.BlockSpec(memory_space=pltpu.SMEM),
            pl.BlockSpec((1, N_MEM, D_MODEL), lambda b, t: (b, 0, 0),
                         pipeline_mode=pl.Buffered(1)),
            resident,
            resident,
            resident,
            resident,
            pl.BlockSpec(memory_space=pl.ANY),
        ],
        out_specs=[
            pl.BlockSpec((1, tile, D_MODEL), lambda b, t: (b, t, 0)),
            pl.BlockSpec((1, CONV_WIDTH - 1, D_CONV), lambda b, t: (b, 0, 0)),
            pl.BlockSpec((1, D_SWA_KV, BLOCK), lambda b, t: (b, 0, 0)),
            pl.BlockSpec((1, D_SWA_KV, BLOCK), lambda b, t: (b, 0, 0)),
            pl.BlockSpec((1, D_MEMQ, N_MEM), lambda b, t: (b, 0, 0)),
            pl.BlockSpec((1, D_MEMQ, N_MEM), lambda b, t: (b, 0, 0)),
        ],
        out_shape=[
            jax.ShapeDtypeStruct((B, T, D_MODEL), jnp.float32),
            jax.ShapeDtypeStruct((B, CONV_WIDTH - 1, D_CONV), jnp.float32),
            jax.ShapeDtypeStruct((B, D_SWA_KV, BLOCK), jnp.float32),
            jax.ShapeDtypeStruct((B, D_SWA_KV, BLOCK), jnp.float32),
            jax.ShapeDtypeStruct((B, D_MEMQ, N_MEM), jnp.float32),
            jax.ShapeDtypeStruct((B, D_MEMQ, N_MEM), jnp.float32),
        ],
        scratch_shapes=[
            pltpu.VMEM((2, D_IN, tq), jnp.float32),
            pltpu.VMEM((2, D_MODEL, tq), jnp.bfloat16),
            pltpu.VMEM((BLOCK + tile, D_SWA_KV), jnp.bfloat16),
            pltpu.VMEM((D_SWA_KV, BLOCK + tile), jnp.bfloat16),
            pltpu.VMEM((D_CONV, BLOCK), jnp.float32),
            pltpu.VMEM((N_SWA_KV, BLOCK, SWA_GROUP * BLOCK), jnp.float32),
            pltpu.VMEM((N_MEM, D_MEMQ), jnp.bfloat16),
            pltpu.VMEM((D_MEMQ, N_MEM), jnp.bfloat16),
        ],
        compiler_params=pltpu.CompilerParams(
            dimension_semantics=("arbitrary", "arbitrary"),
            vmem_limit_bytes=PROMPT_VMEM_LIMIT_BYTES),
        name="prompt_layer",
    )(x, x, pre_g, post_g, w_in_t, conv_w_t, sinks, mem, mem_g, w_mk, w_mv, w_out, run_after)


def _sample_kernel(x_ref, conv_past_ref, ckt_ref, cvt_ref, mkt_ref, mvt_ref,
                   pre_g_ref, post_g_ref, w_in_t_ref, conv_w_ref, sink_ref, w_out_ref,
                   y_ref, conv_state_ref, kt_state_ref, vt_state_ref, w_out_bf_ref,
                   z_ref, ycat_ref, *, ns, group):
    R = SAMPLE_ROWS
    half = R // 2
    nseq = ns * group
    step = pl.program_id(1)

    @pl.when((pl.program_id(0) == 0) & (step == 0))
    def _():
        w_out_bf_ref[...] = w_out_ref[...].astype(jnp.bfloat16)

    @pl.when(step == 0)
    def _():
        x4 = x_ref[...]
        x8 = jnp.concatenate([x4, x4], axis=1).reshape(nseq * R, D_MODEL)
        h = _rms_norm(x8, pre_g_ref[...]).astype(jnp.bfloat16)
        z_ref[...] = lax.dot_general(h, w_in_t_ref[...], _NT, preferred_element_type=jnp.float32)
        u = (z_ref[:, OFF_CC:OFF_CC + D_CONV] * z_ref[:, OFF_CH:OFF_CH + D_CONV]).reshape(nseq, R, D_CONV)
        row3 = lax.broadcasted_iota(jnp.int32, (nseq, R, D_CONV), 1)
        past = conv_past_ref[...]
        past = jnp.concatenate([past, jnp.zeros((nseq, R - (CONV_WIDTH - 1), D_CONV), past.dtype)], axis=1)
        u_full = jnp.where(row3 < CONV_WIDTH - 1, past, pltpu.roll(u, CONV_WIDTH - 1, axis=1))
        cw = conv_w_ref[...]
        conv = (cw[0:1, :] * u_full
                + cw[1:2, :] * pltpu.roll(u_full, R - 1, axis=1)
                + cw[2:3, :] * pltpu.roll(u_full, R - 2, axis=1))
        conv_state_ref[...] = pltpu.roll(u_full, R - half, axis=1)[:, 0:CONV_WIDTH - 1, :]
        y_conv = (z_ref[:, OFF_CB:OFF_CB + D_CONV] * conv.reshape(nseq * R, D_CONV)
                  * _silu(z_ref[:, OFF_CZ:OFF_CZ + D_CONV]))
        ycat_ref[:, YOFF_CONV:YOFF_CONV + D_CONV] = y_conv

    row = lax.broadcasted_iota(jnp.int32, (R, PAIR), 0)
    lane = lax.broadcasted_iota(jnp.int32, (R, PAIR), 1)
    lo_row = row < half
    lo_lane = lane < HEAD_DIM
    diag = lo_row == lo_lane
    lane_sq = lax.broadcasted_iota(jnp.int32, (WINDOW, WINDOW), 1)

    def pair_bias(ncols, dist_of):
        rr = lax.broadcasted_iota(jnp.int32, (R, ncols), 0)
        cc = lax.broadcasted_iota(jnp.int32, (R, ncols), 1)
        dist, valid = dist_of(rr % half, cc)
        distf = dist.astype(jnp.float32)
        tiles = []
        for pair in range(N_SWA_HEADS // 2):
            slope = jnp.where(rr < half, _alibi_slope(2 * pair), _alibi_slope(2 * pair + 1))
            tiles.append(jnp.where(valid, -slope * distf, NEG_INF))
        return jnp.concatenate(tiles, axis=0)

    def cached_dist(tok, c):
        d = tok + WINDOW - c
        return d, d < WINDOW

    def new_dist(tok, c):
        d = tok - c
        return d, (d >= 0) & (c < half)

    bias_c = pair_bias(WINDOW, cached_dist)
    bias_n = pair_bias(SAMPLE_NEW, new_dist)
    head_of_row = lax.broadcasted_iota(jnp.int32, (N_SWA_HEADS * half, 1), 0) // half
    sink_col = jnp.full((N_SWA_HEADS * half, 1), sink_ref[0], jnp.float32)
    for hh in range(1, N_SWA_HEADS):
        sink_col = jnp.where(head_of_row == hh, sink_ref[hh], sink_col)
    state_pad = jnp.zeros((WINDOW - R, D_SWA_KV), jnp.float32)

    def seq_rows(n):
        return pl.ds(pl.multiple_of((step * ns + n) * R, R), R)

    def swa_unit(n):
        rows = seq_rows(n)
        qa = z_ref[rows, OFF_Q:OFF_Q + PAIR] * QK_SCALE
        qb = z_ref[rows, OFF_Q + PAIR:OFF_Q + 2 * PAIR] * QK_SCALE
        qc = z_ref[rows, OFF_Q + 2 * PAIR:OFF_Q + 3 * PAIR] * QK_SCALE
        t0 = jnp.where(lo_lane, jnp.where(lo_row, qa, pltpu.roll(qa, HEAD_DIM, axis=1)), 0.0)
        t1 = jnp.where(diag, qb, 0.0)
        t2 = jnp.where(lo_lane, 0.0, jnp.where(lo_row, pltpu.roll(qc, HEAD_DIM, axis=1), qc))
        qs = jnp.concatenate([t0, t1, t2], axis=0).astype(jnp.bfloat16)

        k_new = z_ref[rows, OFF_K:OFF_K + D_SWA_KV]
        v_new = z_ref[rows, OFF_V:OFF_V + D_SWA_KV]
        k_new_b = jnp.concatenate([k_new, k_new], axis=0).astype(jnp.bfloat16)
        v_new_b = jnp.concatenate([v_new, v_new], axis=0).astype(jnp.bfloat16)
        kt_old = ckt_ref[n].astype(jnp.bfloat16)

        s_c = jnp.dot(qs, kt_old, preferred_element_type=jnp.float32) + bias_c
        s_n = lax.dot_general(qs, k_new_b, _NT, preferred_element_type=jnp.float32) + bias_n
        yield
        m = jnp.maximum(jnp.maximum(jnp.max(s_c, axis=1, keepdims=True), jnp.max(s_n, axis=1, keepdims=True)),
                        sink_col)
        p_c = jnp.exp(s_c - m)
        p_n = jnp.exp(s_n - m)
        l = (jnp.sum(p_c, axis=1, keepdims=True) + jnp.sum(p_n, axis=1, keepdims=True) + jnp.exp(sink_col - m))
        p_c = p_c.astype(jnp.bfloat16)
        p_n = p_n.astype(jnp.bfloat16)
        yield
        o = (lax.dot_general(p_c, cvt_ref[n].astype(jnp.bfloat16), _NT, preferred_element_type=jnp.float32)
             + jnp.dot(p_n, v_new_b, preferred_element_type=jnp.float32)) / l
        yield
        o0, o1, o2 = o[0:R], o[R:2 * R], o[2 * R:3 * R]
        ya = jnp.where(lo_lane, o0, pltpu.roll(pltpu.roll(o0, HEAD_DIM, axis=1), half, axis=0))
        yb = jnp.where(lo_lane, o1, pltpu.roll(o1, half, axis=0))
        yc = jnp.where(lo_lane, pltpu.roll(o2, HEAD_DIM, axis=1), pltpu.roll(o2, half, axis=0))
        y_swa = jnp.concatenate([ya, yb, yc], axis=1) * _silu(z_ref[rows, OFF_SZ:OFF_SZ + D_SWA])
        ycat_ref[rows, YOFF_SWA:YOFF_SWA + D_SWA] = y_swa

    def state_unit(n):
        rows = seq_rows(n)
        k_new_t = jnp.concatenate([state_pad, z_ref[rows, OFF_K:OFF_K + D_SWA_KV]], axis=0).T
        v_new_t = jnp.concatenate([state_pad, z_ref[rows, OFF_V:OFF_V + D_SWA_KV]], axis=0).T
        keep = lane_sq < WINDOW - half
        kt_state_ref[n] = jnp.where(keep, pltpu.roll(ckt_ref[n], WINDOW - half, axis=1), k_new_t)
        vt_state_ref[n] = jnp.where(keep, pltpu.roll(cvt_ref[n], WINDOW - half, axis=1), v_new_t)

    def mem_unit(n):
        rows = seq_rows(n)
        m0 = z_ref[rows, OFF_MQ:OFF_MQ + PAIR] * QK_SCALE
        m1 = z_ref[rows, OFF_MQ + PAIR:OFF_MQ + 2 * PAIR] * QK_SCALE
        zero = jnp.zeros_like(m0)
        qm = jnp.concatenate(
            [jnp.concatenate([jnp.where(diag, m0, 0.0), zero], axis=1),
             jnp.concatenate([zero, jnp.where(diag, m1, 0.0)], axis=1)], axis=0).astype(jnp.bfloat16)
        s = jnp.dot(qm, mkt_ref[n].astype(jnp.bfloat16), preferred_element_type=jnp.float32)
        yield
        m = jnp.max(s, axis=1, keepdims=True)
        p = jnp.exp(s - m)
        l = jnp.sum(p, axis=1, keepdims=True)
        p = p.astype(jnp.bfloat16)
        yield
        o = lax.dot_general(p, mvt_ref[n].astype(jnp.bfloat16), _NT,
                            preferred_element_type=jnp.float32) / l
        yield
        oa, ob = o[0:R, 0:PAIR], o[R:2 * R, PAIR:2 * PAIR]
        y_mem = jnp.concatenate([jnp.where(lo_lane, oa, pltpu.roll(oa, half, axis=0)),
                                 jnp.where(lo_lane, ob, pltpu.roll(ob, half, axis=0))], axis=1)
        ycat_ref[rows, YOFF_MEM:YOFF_MEM + D_MEMQ] = y_mem * _silu(z_ref[rows, OFF_MZ:OFF_MZ + D_MEMQ])

    todo = [functools.partial(unit, n) for n in range(ns) for unit in (swa_unit, mem_unit)]
    states = [functools.partial(state_unit, n) for n in range(ns)]
    active = []
    while todo or active:
        for _ in range(SAMPLE_WIDTH):
            if todo:
                active.append(todo.pop(0)())
        for gen in list(active):
            if next(gen, "done") == "done":
                active.remove(gen)
        if states:
            states.pop(0)()
    for unit in states:
        unit()

    @pl.when(step == group - 1)
    def _():
        y = jnp.dot(ycat_ref[...].astype(jnp.bfloat16), w_out_bf_ref[...], preferred_element_type=jnp.float32)
        y_ref[...] = x_ref[...] + _rms_norm(y, post_g_ref[...]).reshape(nseq, R, D_MODEL)[:, 0:half, :]


def _sample_layer(x, conv_past, ckt, cvt, mkt, mvt, pre_g, post_g, w_in_t, conv_w, sinks, w_out, *, ns, group):
    N = ckt.shape[0]
    R = SAMPLE_ROWS
    nseq = ns * group
    per_group = lambda shape: pl.BlockSpec(shape, lambda o, i: (o,) + (0,) * (len(shape) - 1))
    per_step = lambda shape: pl.BlockSpec(shape, lambda o, i: (o * group + i,) + (0,) * (len(shape) - 1))
    resident = pl.BlockSpec(memory_space=pltpu.VMEM)
    kernel = functools.partial(_sample_kernel, ns=ns, group=group)
    return pl.pallas_call(
        kernel,
        grid=(N // nseq, group),
        in_specs=[
            per_group((nseq, R // 2, D_MODEL)),
            per_group((nseq, CONV_WIDTH - 1, D_CONV)),
            per_step((ns, D_SWA_KV, WINDOW)),
            per_step((ns, D_SWA_KV, WINDOW)),
            per_step((ns, D_MEMQ, N_MEM)),
            per_step((ns, D_MEMQ, N_MEM)),
            resident,
            resident,
            resident,
            resident,
            pl.BlockSpec(memory_space=pltpu.SMEM),
            resident,
        ],
        out_specs=[
            per_group((nseq, R // 2, D_MODEL)),
            per_group((nseq, CONV_WIDTH - 1, D_CONV)),
            per_step((ns, D_SWA_KV, WINDOW)),
            per_step((ns, D_SWA_KV, WINDOW)),
            pl.BlockSpec((D_MODEL, D_MODEL), lambda o, i: (0, 0)),
        ],
        out_shape=[
            jax.ShapeDtypeStruct((N, R // 2, D_MODEL), jnp.float32),
            jax.ShapeDtypeStruct((N, CONV_WIDTH - 1, D_CONV), jnp.float32),
            jax.ShapeDtypeStruct((N, D_SWA_KV, WINDOW), jnp.float32),
            jax.ShapeDtypeStruct((N, D_SWA_KV, WINDOW), jnp.float32),
            jax.ShapeDtypeStruct((D_MODEL, D_MODEL), jnp.bfloat16),
        ],
        scratch_shapes=[
            pltpu.VMEM((nseq * R, D_IN), jnp.float32),
            pltpu.VMEM((nseq * R, D_MODEL), jnp.float32),
        ],
        compiler_params=pltpu.CompilerParams(
            dimension_semantics=("arbitrary", "arbitrary"),
            vmem_limit_bytes=SAMPLE_VMEM_LIMIT_BYTES),
        name="sample_layer",
    )(x, conv_past, ckt, cvt, mkt, mvt, pre_g, post_g, w_in_t, conv_w, sinks, w_out)


def _heads_last_to_keys_last(a):
    n, keys, heads, dim = a.shape
    return jnp.transpose(a, (0, 2, 3, 1)).reshape(n, heads * dim, keys)


def _keys_last_to_heads_last(a, heads):
    n, hd, keys = a.shape
    return jnp.transpose(a.reshape(n, heads, hd // heads, keys), (0, 3, 1, 2))[None]


def kernel(x_prompt, x_sample, mem_prompt, state_conv, cache_swa_k, cache_swa_v, cache_mem_k, cache_mem_v,
           pre_norm_g, post_norm_g, w_in, conv_w, attn_sinks, mem_norm_g, w_mem_k, w_mem_v, w_out):
    assert w_in.shape[0] == 1, "one layer, as the problem states"
    N, TS, _ = x_sample.shape
    assert TS == SAMPLE_ROWS // 2 and cache_swa_k.shape[2] == WINDOW
    l = 0

    pre_g = pre_norm_g[l].reshape(1, D_MODEL)
    post_g = post_norm_g[l].reshape(1, D_MODEL)
    w_in_t = w_in[l].astype(jnp.bfloat16).T
    sinks = attn_sinks[l].astype(jnp.float32)

    y_s, conv_s, kt_s, vt_s, w_out_bf = _sample_layer(
        x_sample, state_conv[l],
        _heads_last_to_keys_last(cache_swa_k[l]), _heads_last_to_keys_last(cache_swa_v[l]),
        _heads_last_to_keys_last(cache_mem_k[l]), _heads_last_to_keys_last(cache_mem_v[l]),
        pre_g, post_g, w_in_t, conv_w[l], sinks, w_out[l], ns=SAMPLE_NS, group=SAMPLE_GROUP)

    y_p, conv_p, kt_p, vt_p, mkt, mvt = _prompt_layer(
        x_prompt, pre_g, post_g, w_in_t, conv_w[l].T, sinks,
        mem_prompt, mem_norm_g[l].reshape(1, D_MODEL), w_mem_k[l], w_mem_v[l], w_out_bf, conv_s,
        tq=PROMPT_TQ, nsub=PROMPT_NSUB)

    return (y_p, y_s,
            conv_p[None],
            _keys_last_to_heads_last(kt_p, N_SWA_KV), _keys_last_to_heads_last(vt_p, N_SWA_KV),
            _keys_last_to_heads_last(mkt, N_MEM_HEADS), _keys_last_to_heads_last(mvt, N_MEM_HEADS),
            conv_s[None],
            _keys_last_to_heads_last(kt_s, N_SWA_KV), _keys_last_to_heads_last(vt_s, N_SWA_KV))
```

```python
import functools

import numpy as np
import jax
import jax.numpy as jnp
from jax import lax
from jax.experimental import pallas as pl
from jax.experimental.pallas import tpu as pltpu

D_MODEL = 1024
HEAD_DIM = 64
D_CONV = 384
N_MEM_HEADS = 4
D_MEMQ = N_MEM_HEADS * HEAD_DIM
D_SWA = 384
N_SWA_HEADS = 6
N_SWA_KV = 2
SWA_GROUP = N_SWA_HEADS // N_SWA_KV
D_SWA_KV = N_SWA_KV * HEAD_DIM
N_MEM = 256
CONV_WIDTH = 3
WINDOW = 128
BLOCK = 128
RMS_EPS = 1e-6
NEG_INF = -1e30
D_IN = 3072
QK_SCALE = HEAD_DIM ** -0.5
LOG2_E = float(np.log2(np.e))
QK_SCALE_LOG2 = QK_SCALE * LOG2_E

OFF_CB, OFF_CC, OFF_CH, OFF_CZ = 0, 384, 768, 1152
OFF_Q, OFF_K, OFF_V, OFF_SZ = 1536, 1920, 2048, 2176
OFF_MQ, OFF_MZ = 2560, 2816
YOFF_CONV, YOFF_SWA, YOFF_MEM = 0, 384, 768

PAIR = 2 * HEAD_DIM

MIB = 1024 * 1024
PROMPT_VMEM_LIMIT_BYTES = 44 * MIB
SAMPLE_VMEM_LIMIT_BYTES = 48 * MIB

MEM_KV_BATCHES = 2
PROMPT_TQ = 512
PROMPT_NSUB = 2
IN_PROJ_CHUNK = 512
ATTN_WIDTH = 2
SAMPLE_NS = 16
SAMPLE_GROUP = 2
SAMPLE_ROWS = 8
SAMPLE_WIDTH = 16
SAMPLE_NEW = 16


def _alibi_slope(h):
    return float(np.power(np.float32(2.0), np.float32(-8.0 * (h + 1) / N_SWA_HEADS)))


def _rms_norm(x, g):
    return x * lax.rsqrt(jnp.mean(x * x, axis=-1, keepdims=True) + RMS_EPS) * g


def _silu(x):
    return x * jax.nn.sigmoid(x)


_NT = (((1,), (1,)), ((), ()))
_TN = (((0,), (0,)), ((), ()))


def _mem_kv_kernel(mem_ref, g_ref, wk_ref, wv_ref, mkt_ref, mvt_ref, mkb_ref, mvtb_ref, *, nb):
    mem = mem_ref[...].reshape(nb * N_MEM, D_MODEL)
    m = _rms_norm(mem, g_ref[...]).astype(jnp.bfloat16)
    mk = jnp.dot(m, wk_ref[...].astype(jnp.bfloat16), preferred_element_type=jnp.float32)
    mv = jnp.dot(m, wv_ref[...].astype(jnp.bfloat16), preferred_element_type=jnp.float32)
    for b in range(nb):
        mk_b = mk[b * N_MEM:(b + 1) * N_MEM]
        mv_t = mv[b * N_MEM:(b + 1) * N_MEM].T
        mkt_ref[b] = mk_b.T
        mvt_ref[b] = mv_t
        mkb_ref[b] = mk_b.astype(jnp.bfloat16)
        mvtb_ref[b] = mv_t.astype(jnp.bfloat16)


def _mem_kv(mem, mem_g, w_mk, w_mv):
    B = mem.shape[0]
    nb = MEM_KV_BATCHES
    full = lambda shape: pl.BlockSpec(shape, lambda b: (0,) * len(shape))
    per_batch = pl.BlockSpec((nb, N_MEM, D_MEMQ), lambda b: (b, 0, 0))
    return pl.pallas_call(
        functools.partial(_mem_kv_kernel, nb=nb),
        grid=(B // nb,),
        in_specs=[
            pl.BlockSpec((nb, N_MEM, D_MODEL), lambda b: (b, 0, 0)),
            full((1, D_MODEL)),
            full((D_MODEL, D_MEMQ)),
            full((D_MODEL, D_MEMQ)),
        ],
        out_specs=[per_batch] * 4,
        out_shape=[
            jax.ShapeDtypeStruct((B, D_MEMQ, N_MEM), jnp.float32),
            jax.ShapeDtypeStruct((B, D_MEMQ, N_MEM), jnp.float32),
            jax.ShapeDtypeStruct((B, N_MEM, D_MEMQ), jnp.bfloat16),
            jax.ShapeDtypeStruct((B, D_MEMQ, N_MEM), jnp.bfloat16),
        ],
        compiler_params=pltpu.CompilerParams(dimension_semantics=("arbitrary",)),
        name="mem_kv",
    )(mem, mem_g.reshape(1, D_MODEL), w_mk, w_mv)


def _prompt_kernel(x_ref, x_next_ref, pre_g_ref, post_g_ref, w_in_t_ref, conv_w_t_ref, sink_ref, mkb_ref, mvt_ref,
                   w_out_ref, run_after_ref,
                   y_ref, conv_state_ref, k_state_ref, v_state_ref,
                   zt_ref, ycat_ref, kbuf_ref, vbuf_ref, uprev_ref, bias_ref, *, tq, nsub):
    assert nsub % 2 == 0
    t = pl.program_id(1)
    nblk = tq // BLOCK

    @pl.when((pl.program_id(0) == 0) & (t == 0))
    def _():
        c = lax.broadcasted_iota(jnp.int32, (BLOCK, BLOCK), 0)
        r = lax.broadcasted_iota(jnp.int32, (BLOCK, BLOCK), 1)
        distf = (r - c + jnp.where(c > r, BLOCK, 0)).astype(jnp.float32)
        for h in range(N_SWA_HEADS):
            g, i = divmod(h, SWA_GROUP)
            bias_ref[g, :, i * BLOCK:(i + 1) * BLOCK] = (-_alibi_slope(h) * LOG2_E) * distf

    @pl.when(t == 0)
    def _():
        kbuf_ref[0:BLOCK, :] = jnp.zeros((BLOCK, D_SWA_KV), jnp.bfloat16)
        vbuf_ref[:, 0:BLOCK] = jnp.zeros((D_SWA_KV, BLOCK), jnp.bfloat16)
        uprev_ref[...] = jnp.zeros_like(uprev_ref)

    first_pen = jnp.where(t == 0, NEG_INF, 0.0)
    slot = lax.broadcasted_iota(jnp.int32, (BLOCK, SWA_GROUP * BLOCK), 0)
    query = lax.broadcasted_iota(jnp.int32, (BLOCK, SWA_GROUP * BLOCK), 1) % BLOCK
    from_prev = slot > query
    q_zero = jnp.zeros((HEAD_DIM, SWA_GROUP * BLOCK), jnp.bfloat16)
    cw = conv_w_t_ref[...]
    carry = {"u_prev": uprev_ref[...]}
    head_of_lane = lax.broadcasted_iota(jnp.int32, (1, SWA_GROUP * BLOCK), 1) // BLOCK
    sink_rows = []
    for g in range(N_SWA_KV):
        row = jnp.full((1, SWA_GROUP * BLOCK), sink_ref[g * SWA_GROUP], jnp.float32)
        for i in range(1, SWA_GROUP):
            row = jnp.where(head_of_lane == i, sink_ref[g * SWA_GROUP + i], row)
        sink_rows.append(row * LOG2_E)

    def in_proj_chunks(x_rows, zt):
        state = {}

        def norm():
            state["h"] = _rms_norm(x_rows(), pre_g_ref[...]).astype(jnp.bfloat16)

        def chunk(lo):
            rows = slice(lo, lo + IN_PROJ_CHUNK)
            zt[rows, :] = lax.dot_general(w_in_t_ref[rows, :], state["h"], _NT,
                                          preferred_element_type=jnp.float32)

        return norm, [functools.partial(chunk, lo) for lo in range(0, D_IN, IN_PROJ_CHUNK)]

    def mixer_units(sub):
        zt = zt_ref.at[sub % 2]
        ycat = ycat_ref.at[sub % 2]
        tok0 = sub * tq

        def conv_unit():
            u = zt[OFF_CC:OFF_CC + D_CONV, :] * zt[OFF_CH:OFF_CH + D_CONV, :]
            ucat = jnp.concatenate([carry["u_prev"], u], axis=1)
            conv = (cw[:, 0:1] * pltpu.roll(ucat, 2, axis=1)[:, BLOCK:]
                    + cw[:, 1:2] * pltpu.roll(ucat, 1, axis=1)[:, BLOCK:]
                    + cw[:, 2:3] * u)
            y_conv = zt[OFF_CB:OFF_CB + D_CONV, :] * conv * _silu(zt[OFF_CZ:OFF_CZ + D_CONV, :])
            ycat[YOFF_CONV:YOFF_CONV + D_CONV, :] = y_conv.astype(jnp.bfloat16)
            carry["u_prev"] = u[:, tq - BLOCK:]

        def kv_unit():
            k_nat = zt[OFF_K:OFF_K + D_SWA_KV, :].T
            kbuf_ref[BLOCK + tok0:BLOCK + tok0 + tq, :] = k_nat.astype(jnp.bfloat16)
            vbuf_ref[:, BLOCK + tok0:BLOCK + tok0 + tq] = zt[OFF_V:OFF_V + D_SWA_KV, :].astype(jnp.bfloat16)

        def swa_unit(j, g):
            cols = slice(j * BLOCK, (j + 1) * BLOCK)
            band = slice(tok0 + j * BLOCK, tok0 + (j + 2) * BLOCK)
            q0 = OFF_Q + g * SWA_GROUP * HEAD_DIM
            qt = jnp.concatenate(
                [zt[q0 + i * HEAD_DIM:q0 + (i + 1) * HEAD_DIM, cols] for i in range(SWA_GROUP)],
                axis=1)
            qt = (qt * QK_SCALE_LOG2).astype(jnp.bfloat16)
            qt = jnp.concatenate([qt, q_zero] if g == 0 else [q_zero, qt], axis=0)
            s = jnp.dot(kbuf_ref[band, :], qt, preferred_element_type=jnp.float32)
            yield
            s = jnp.where(from_prev, s[0:BLOCK], s[BLOCK:]) + bias_ref[g]
            if sub == 0 and j == 0:
                s = s + jnp.where(from_prev, first_pen, 0.0)
            sink = sink_rows[g]
            m = jnp.maximum(jnp.max(s, axis=0, keepdims=True), sink)
            p = jnp.exp2(s - m)
            l = jnp.sum(p, axis=0, keepdims=True) + jnp.exp2(sink - m)
            p = jnp.concatenate([jnp.where(from_prev, p, 0.0), jnp.where(from_prev, 0.0, p)],
                                axis=0).astype(jnp.bfloat16)
            yield
            vband = vbuf_ref[g * HEAD_DIM:(g + 1) * HEAD_DIM, band]
            o = jnp.dot(vband, p, preferred_element_type=jnp.float32)
            o = o / l
            for i in range(SWA_GROUP):
                hh = g * SWA_GROUP + i
                gate = _silu(zt[OFF_SZ + hh * HEAD_DIM:OFF_SZ + (hh + 1) * HEAD_DIM, cols])
                ycat[YOFF_SWA + hh * HEAD_DIM:YOFF_SWA + (hh + 1) * HEAD_DIM, cols] = (
                    o[:, i * BLOCK:(i + 1) * BLOCK] * gate).astype(jnp.bfloat16)

        def mem_unit(hh):
            rows = slice(OFF_MQ + hh * HEAD_DIM, OFF_MQ + (hh + 1) * HEAD_DIM)
            qt = (zt[rows, :] * QK_SCALE_LOG2).astype(jnp.bfloat16)
            pieces = [jnp.zeros((HEAD_DIM, tq), jnp.bfloat16)] * N_MEM_HEADS
            pieces[hh] = qt
            s = jnp.dot(mkb_ref[0], jnp.concatenate(pieces, axis=0),
                        preferred_element_type=jnp.float32)
            yield
            m = jnp.max(s, axis=0, keepdims=True)
            p = jnp.exp2(s - m)
            l = jnp.sum(p, axis=0, keepdims=True)
            p = p.astype(jnp.bfloat16)
            yield
            o = jnp.dot(mvt_ref[0, hh * HEAD_DIM:(hh + 1) * HEAD_DIM, :], p,
                        preferred_element_type=jnp.float32)
            gate = _silu(zt[OFF_MZ + hh * HEAD_DIM:OFF_MZ + (hh + 1) * HEAD_DIM, :])
            ycat[YOFF_MEM + hh * HEAD_DIM:YOFF_MEM + (hh + 1) * HEAD_DIM, :] = (
                o / l * gate).astype(jnp.bfloat16)

        def out_proj():
            carry["y"] = lax.dot_general(ycat[...], w_out_ref[...], _TN,
                                         preferred_element_type=jnp.float32)

        def post_norm(blk):
            rows = slice(blk * BLOCK, (blk + 1) * BLOCK)
            out_rows = slice(tok0 + blk * BLOCK, tok0 + (blk + 1) * BLOCK)
            y_ref[0, out_rows, :] = x_ref[0, out_rows, :] + _rms_norm(carry["y"][rows], post_g_ref[...])

        attention = [functools.partial(swa_unit, j, g) for j in range(nblk) for g in range(N_SWA_KV)]
        attention += [functools.partial(mem_unit, hh) for hh in range(N_MEM_HEADS)]
        return [conv_unit, kv_unit], attention, (out_proj, [functools.partial(post_norm, b) for b in range(nblk)])

    def interleave(units, chunks, norm_first, norm_after):
        (conv_unit, kv_unit), attention, (out_proj, post_norms) = units
        pending = list(chunks)
        kv_unit()
        n_rounds = -(-len(attention) // ATTN_WIDTH) + 2
        chunk_rounds = [k * n_rounds // len(pending) for k in range(len(pending))]
        todo = list(attention)
        active = []
        for rnd in range(n_rounds):
            for _ in range(ATTN_WIDTH):
                if todo:
                    active.append(todo.pop(0)())
            for gen in list(active):
                if next(gen, "done") == "done":
                    active.remove(gen)
            if rnd == 0 and norm_first is not None:
                norm_first()
            for _ in range(chunk_rounds.count(rnd)):
                pending.pop(0)()
            if rnd == 0:
                conv_unit()
        assert not todo and not active and not pending
        if norm_after is not None:
            norm_after()
        out_proj()
        for post_norm in post_norms:
            post_norm()

    @pl.when((pl.program_id(0) == 0) & (t == 0))
    def _():
        norm, chunks = in_proj_chunks(lambda: x_ref[0, 0:tq, :], zt_ref.at[0])
        norm()
        for chunk in chunks:
            chunk()

    proj = [in_proj_chunks(lambda sub=sub: x_ref[0, sub * tq:(sub + 1) * tq, :], zt_ref.at[sub % 2])
            for sub in range(1, nsub)]
    proj.append(in_proj_chunks(lambda: x_next_ref[0], zt_ref.at[0]))
    for sub in range(nsub):
        norm_first = proj[0][0] if sub == 0 else None
        norm_after = proj[sub + 1][0] if sub + 1 < nsub else None
        interleave(mixer_units(sub), proj[sub][1], norm_first, norm_after)

    tile = nsub * tq
    uprev_ref[...] = carry["u_prev"]
    conv_state_ref[0] = carry["u_prev"].T[BLOCK - (CONV_WIDTH - 1):, :]
    kbuf_ref[0:BLOCK, :] = kbuf_ref[tile:tile + BLOCK, :]
    vbuf_ref[:, 0:BLOCK] = vbuf_ref[:, tile:tile + BLOCK]
    k_state_ref[0] = zt_ref[1, OFF_K:OFF_K + D_SWA_KV, tq - BLOCK:]
    v_state_ref[0] = zt_ref[1, OFF_V:OFF_V + D_SWA_KV, tq - BLOCK:]


def _prompt_layer(x, pre_g, post_g, w_in_t, conv_w_t, sinks, mkb, mvtb, w_out, run_after, *, tq, nsub):
    B, T, _ = x.shape
    tile = tq * nsub
    kernel = functools.partial(_prompt_kernel, tq=tq, nsub=nsub)
    steps = T // tile

    def next_first_sub_tile(b, t):
        nxt = jnp.minimum(b * steps + t + 1, B * steps - 1)
        return nxt // steps, (nxt % steps) * nsub, 0

    resident = pl.BlockSpec(memory_space=pltpu.VMEM)
    return pl.pallas_call(
        kernel,
        grid=(B, steps),
        in_specs=[
            pl.BlockSpec((1, tile, D_MODEL), lambda b, t: (b, t, 0)),
            pl.BlockSpec((1, tq, D_MODEL), next_first_sub_tile),
            resident,
            resident,
            resident,
            resident,
            pl.BlockSpec(memory_space=pltpu.SMEM),
            pl.BlockSpec((1, N_MEM, D_MEMQ), lambda b, t: (b, 0, 0)),
            pl.BlockSpec((1, D_MEMQ, N_MEM), lambda b, t: (b, 0, 0)),
            resident,
            pl.BlockSpec(memory_space=pl.ANY),
        ],
        out_specs=[
            pl.BlockSpec((1, tile, D_MODEL), lambda b, t: (b, t, 0)),
            pl.BlockSpec((1, CONV_WIDTH - 1, D_CONV), lambda b, t: (b, 0, 0)),
            pl.BlockSpec((1, D_SWA_KV, BLOCK), lambda b, t: (b, 0, 0)),
            pl.BlockSpec((1, D_SWA_KV, BLOCK), lambda b, t: (b, 0, 0)),
        ],
        out_shape=[
            jax.ShapeDtypeStruct((B, T, D_MODEL), jnp.float32),
            jax.ShapeDtypeStruct((B, CONV_WIDTH - 1, D_CONV), jnp.float32),
            jax.ShapeDtypeStruct((B, D_SWA_KV, BLOCK), jnp.float32),
            jax.ShapeDtypeStruct((B, D_SWA_KV, BLOCK), jnp.float32),
        ],
        scratch_shapes=[
            pltpu.VMEM((2, D_IN, tq), jnp.float32),
            pltpu.VMEM((2, D_MODEL, tq), jnp.bfloat16),
            pltpu.VMEM((BLOCK + tile, D_SWA_KV), jnp.bfloat16),
            pltpu.VMEM((D_SWA_KV, BLOCK + tile), jnp.bfloat16),
            pltpu.VMEM((D_CONV, BLOCK), jnp.float32),
            pltpu.VMEM((N_SWA_KV, BLOCK, SWA_GROUP * BLOCK), jnp.float32),
        ],
        compiler_params=pltpu.CompilerParams(
            dimension_semantics=("arbitrary", "arbitrary"),
            vmem_limit_bytes=PROMPT_VMEM_LIMIT_BYTES),
        name="prompt_layer",
    )(x, x, pre_g, post_g, w_in_t, conv_w_t, sinks, mkb, mvtb, w_out, run_after)


def _sample_kernel(x_ref, conv_past_ref, ckt_ref, cvt_ref, mkt_ref, mvt_ref,
                   pre_g_ref, post_g_ref, w_in_t_ref, conv_w_ref, sink_ref, w_out_ref,
                   y_ref, conv_state_ref, kt_state_ref, vt_state_ref, w_out_bf_ref,
                   z_ref, ycat_ref, *, ns, group):
    R = SAMPLE_ROWS
    half = R // 2
    nseq = ns * group
    step = pl.program_id(1)

    @pl.when((pl.program_id(0) == 0) & (step == 0))
    def _():
        w_out_bf_ref[...] = w_out_ref[...].astype(jnp.bfloat16)

    @pl.when(step == 0)
    def _():
        x4 = x_ref[...]
        x8 = jnp.concatenate([x4, x4], axis=1).reshape(nseq * R, D_MODEL)
        h = _rms_norm(x8, pre_g_ref[...]).astype(jnp.bfloat16)
        z_ref[...] = lax.dot_general(h, w_in_t_ref[...], _NT, preferred_element_type=jnp.float32)
        u = (z_ref[:, OFF_CC:OFF_CC + D_CONV] * z_ref[:, OFF_CH:OFF_CH + D_CONV]).reshape(nseq, R, D_CONV)
        row3 = lax.broadcasted_iota(jnp.int32, (nseq, R, D_CONV), 1)
        past = conv_past_ref[...]
        past = jnp.concatenate([past, jnp.zeros((nseq, R - (CONV_WIDTH - 1), D_CONV), past.dtype)], axis=1)
        u_full = jnp.where(row3 < CONV_WIDTH - 1, past, pltpu.roll(u, CONV_WIDTH - 1, axis=1))
        cw = conv_w_ref[...]
        conv = (cw[0:1, :] * u_full
                + cw[1:2, :] * pltpu.roll(u_full, R - 1, axis=1)
                + cw[2:3, :] * pltpu.roll(u_full, R - 2, axis=1))
        conv_state_ref[...] = pltpu.roll(u_full, R - half, axis=1)[:, 0:CONV_WIDTH - 1, :]
        y_conv = (z_ref[:, OFF_CB:OFF_CB + D_CONV] * conv.reshape(nseq * R, D_CONV)
                  * _silu(z_ref[:, OFF_CZ:OFF_CZ + D_CONV]))
        ycat_ref[:, YOFF_CONV:YOFF_CONV + D_CONV] = y_conv

    row = lax.broadcasted_iota(jnp.int32, (R, PAIR), 0)
    lane = lax.broadcasted_iota(jnp.int32, (R, PAIR), 1)
    lo_row = row < half
    lo_lane = lane < HEAD_DIM
    diag = lo_row == lo_lane
    lane_sq = lax.broadcasted_iota(jnp.int32, (WINDOW, WINDOW), 1)

    def pair_bias(ncols, dist_of):
        rr = lax.broadcasted_iota(jnp.int32, (R, ncols), 0)
        cc = lax.broadcasted_iota(jnp.int32, (R, ncols), 1)
        dist, valid = dist_of(rr % half, cc)
        distf = dist.astype(jnp.float32)
        tiles = []
        for pair in range(N_SWA_HEADS // 2):
            slope = jnp.where(rr < half, _alibi_slope(2 * pair), _alibi_slope(2 * pair + 1))
            tiles.append(jnp.where(valid, -slope * distf, NEG_INF))
        return jnp.concatenate(tiles, axis=0)

    def cached_dist(tok, c):
        d = tok + WINDOW - c
        return d, d < WINDOW

    def new_dist(tok, c):
        d = tok - c
        return d, (d >= 0) & (c < half)

    bias_c = pair_bias(WINDOW, cached_dist)
    bias_n = pair_bias(SAMPLE_NEW, new_dist)
    head_of_row = lax.broadcasted_iota(jnp.int32, (N_SWA_HEADS * half, 1), 0) // half
    sink_col = jnp.full((N_SWA_HEADS * half, 1), sink_ref[0], jnp.float32)
    for hh in range(1, N_SWA_HEADS):
        sink_col = jnp.where(head_of_row == hh, sink_ref[hh], sink_col)
    state_pad = jnp.zeros((WINDOW - R, D_SWA_KV), jnp.float32)

    def seq_rows(n):
        return pl.ds(pl.multiple_of((step * ns + n) * R, R), R)

    def swa_unit(n):
        rows = seq_rows(n)
        qa = z_ref[rows, OFF_Q:OFF_Q + PAIR] * QK_SCALE
        qb = z_ref[rows, OFF_Q + PAIR:OFF_Q + 2 * PAIR] * QK_SCALE
        qc = z_ref[rows, OFF_Q + 2 * PAIR:OFF_Q + 3 * PAIR] * QK_SCALE
        t0 = jnp.where(lo_lane, jnp.where(lo_row, qa, pltpu.roll(qa, HEAD_DIM, axis=1)), 0.0)
        t1 = jnp.where(diag, qb, 0.0)
        t2 = jnp.where(lo_lane, 0.0, jnp.where(lo_row, pltpu.roll(qc, HEAD_DIM, axis=1), qc))
        qs = jnp.concatenate([t0, t1, t2], axis=0).astype(jnp.bfloat16)

        k_new = z_ref[rows, OFF_K:OFF_K + D_SWA_KV]
        v_new = z_ref[rows, OFF_V:OFF_V + D_SWA_KV]
        k_new_b = jnp.concatenate([k_new, k_new], axis=0).astype(jnp.bfloat16)
        v_new_b = jnp.concatenate([v_new, v_new], axis=0).astype(jnp.bfloat16)
        kt_old = ckt_ref[n].astype(jnp.bfloat16)

        s_c = jnp.dot(qs, kt_old, preferred_element_type=jnp.float32) + bias_c
        s_n = lax.dot_general(qs, k_new_b, _NT, preferred_element_type=jnp.float32) + bias_n
        yield
        m = jnp.maximum(jnp.maximum(jnp.max(s_c, axis=1, keepdims=True), jnp.max(s_n, axis=1, keepdims=True)),
                        sink_col)
        p_c = jnp.exp(s_c - m)
        p_n = jnp.exp(s_n - m)
        l = (jnp.sum(p_c, axis=1, keepdims=True) + jnp.sum(p_n, axis=1, keepdims=True) + jnp.exp(sink_col - m))
        p_c = p_c.astype(jnp.bfloat16)
        p_n = p_n.astype(jnp.bfloat16)
        yield
        o = (lax.dot_general(p_c, cvt_ref[n].astype(jnp.bfloat16), _NT, preferred_element_type=jnp.float32)
             + jnp.dot(p_n, v_new_b, preferred_element_type=jnp.float32)) / l
        yield
        o0, o1, o2 = o[0:R], o[R:2 * R], o[2 * R:3 * R]
        ya = jnp.where(lo_lane, o0, pltpu.roll(pltpu.roll(o0, HEAD_DIM, axis=1), half, axis=0))
        yb = jnp.where(lo_lane, o1, pltpu.roll(o1, half, axis=0))
        yc = jnp.where(lo_lane, pltpu.roll(o2, HEAD_DIM, axis=1), pltpu.roll(o2, half, axis=0))
        y_swa = jnp.concatenate([ya, yb, yc], axis=1) * _silu(z_ref[rows, OFF_SZ:OFF_SZ + D_SWA])
        ycat_ref[rows, YOFF_SWA:YOFF_SWA + D_SWA] = y_swa

    def state_unit(n):
        rows = seq_rows(n)
        k_new_t = jnp.concatenate([state_pad, z_ref[rows, OFF_K:OFF_K + D_SWA_KV]], axis=0).T
        v_new_t = jnp.concatenate([state_pad, z_ref[rows, OFF_V:OFF_V + D_SWA_KV]], axis=0).T
        keep = lane_sq < WINDOW - half
        kt_state_ref[n] = jnp.where(keep, pltpu.roll(ckt_ref[n], WINDOW - half, axis=1), k_new_t)
        vt_state_ref[n] = jnp.where(keep, pltpu.roll(cvt_ref[n], WINDOW - half, axis=1), v_new_t)

    def mem_unit(n):
        rows = seq_rows(n)
        m0 = z_ref[rows, OFF_MQ:OFF_MQ + PAIR] * QK_SCALE
        m1 = z_ref[rows, OFF_MQ + PAIR:OFF_MQ + 2 * PAIR] * QK_SCALE
        zero = jnp.zeros_like(m0)
        qm = jnp.concatenate(
            [jnp.concatenate([jnp.where(diag, m0, 0.0), zero], axis=1),
             jnp.concatenate([zero, jnp.where(diag, m1, 0.0)], axis=1)], axis=0).astype(jnp.bfloat16)
        s = jnp.dot(qm, mkt_ref[n].astype(jnp.bfloat16), preferred_element_type=jnp.float32)
        yield
        m = jnp.max(s, axis=1, keepdims=True)
        p = jnp.exp(s - m)
        l = jnp.sum(p, axis=1, keepdims=True)
        p = p.astype(jnp.bfloat16)
        yield
        o = lax.dot_general(p, mvt_ref[n].astype(jnp.bfloat16), _NT,
                            preferred_element_type=jnp.float32) / l
        yield
        oa, ob = o[0:R, 0:PAIR], o[R:2 * R, PAIR:2 * PAIR]
        y_mem = jnp.concatenate([jnp.where(lo_lane, oa, pltpu.roll(oa, half, axis=0)),
                                 jnp.where(lo_lane, ob, pltpu.roll(ob, half, axis=0))], axis=1)
        ycat_ref[rows, YOFF_MEM:YOFF_MEM + D_MEMQ] = y_mem * _silu(z_ref[rows, OFF_MZ:OFF_MZ + D_MEMQ])

    todo = [functools.partial(unit, n) for n in range(ns) for unit in (swa_unit, mem_unit)]
    states = [functools.partial(state_unit, n) for n in range(ns)]
    for unit in states[:ns // 2]:
        unit()
    states = states[ns // 2:]
    active = []
    while todo or active:
        for _ in range(SAMPLE_WIDTH):
            if todo:
                active.append(todo.pop(0)())
        for gen in list(active):
            if next(gen, "done") == "done":
                active.remove(gen)
        if states:
            states.pop(0)()
    for unit in states:
        unit()

    @pl.when(step == group - 1)
    def _():
        y = jnp.dot(ycat_ref[...].astype(jnp.bfloat16), w_out_bf_ref[...], preferred_element_type=jnp.float32)
        y_ref[...] = x_ref[...] + _rms_norm(y, post_g_ref[...]).reshape(nseq, R, D_MODEL)[:, 0:half, :]


def _sample_layer(x, conv_past, ckt, cvt, mkt, mvt, pre_g, post_g, w_in_t, conv_w, sinks, w_out, *, ns, group):
    N = ckt.shape[0]
    R = SAMPLE_ROWS
    nseq = ns * group
    per_group = lambda shape: pl.BlockSpec(shape, lambda o, i: (o,) + (0,) * (len(shape) - 1))
    per_step = lambda shape: pl.BlockSpec(shape, lambda o, i: (o * group + i,) + (0,) * (len(shape) - 1))
    resident = pl.BlockSpec(memory_space=pltpu.VMEM)
    kernel = functools.partial(_sample_kernel, ns=ns, group=group)
    return pl.pallas_call(
        kernel,
        grid=(N // nseq, group),
        in_specs=[
            per_group((nseq, R // 2, D_MODEL)),
            per_group((nseq, CONV_WIDTH - 1, D_CONV)),
            per_step((ns, D_SWA_KV, WINDOW)),
            per_step((ns, D_SWA_KV, WINDOW)),
            per_step((ns, D_MEMQ, N_MEM)),
            per_step((ns, D_MEMQ, N_MEM)),
            resident,
            resident,
            resident,
            resident,
            pl.BlockSpec(memory_space=pltpu.SMEM),
            resident,
        ],
        out_specs=[
            per_group((nseq, R // 2, D_MODEL)),
            per_group((nseq, CONV_WIDTH - 1, D_CONV)),
            per_step((ns, D_SWA_KV, WINDOW)),
            per_step((ns, D_SWA_KV, WINDOW)),
            pl.BlockSpec((D_MODEL, D_MODEL), lambda o, i: (0, 0)),
        ],
        out_shape=[
            jax.ShapeDtypeStruct((N, R // 2, D_MODEL), jnp.float32),
            jax.ShapeDtypeStruct((N, CONV_WIDTH - 1, D_CONV), jnp.float32),
            jax.ShapeDtypeStruct((N, D_SWA_KV, WINDOW), jnp.float32),
            jax.ShapeDtypeStruct((N, D_SWA_KV, WINDOW), jnp.float32),
            jax.ShapeDtypeStruct((D_MODEL, D_MODEL), jnp.bfloat16),
        ],
        scratch_shapes=[
            pltpu.VMEM((nseq * R, D_IN), jnp.float32),
            pltpu.VMEM((nseq * R, D_MODEL), jnp.float32),
        ],
        compiler_params=pltpu.CompilerParams(
            dimension_semantics=("arbitrary", "arbitrary"),
            vmem_limit_bytes=SAMPLE_VMEM_LIMIT_BYTES),
        name="sample_layer",
    )(x, conv_past, ckt, cvt, mkt, mvt, pre_g, post_g, w_in_t, conv_w, sinks, w_out)


def _heads_last_to_keys_last(a):
    n, keys, heads, dim = a.shape
    return jnp.transpose(a, (0, 2, 3, 1)).reshape(n, heads * dim, keys)


def _keys_last_to_heads_last(a, heads):
    n, hd, keys = a.shape
    return jnp.transpose(a.reshape(n, heads, hd // heads, keys), (0, 3, 1, 2))[None]


def kernel(x_prompt, x_sample, mem_prompt, state_conv, cache_swa_k, cache_swa_v, cache_mem_k, cache_mem_v,
           pre_norm_g, post_norm_g, w_in, conv_w, attn_sinks, mem_norm_g, w_mem_k, w_mem_v, w_out):
    assert w_in.shape[0] == 1, "one layer, as the problem states"
    N, TS, _ = x_sample.shape
    assert TS == SAMPLE_ROWS // 2 and cache_swa_k.shape[2] == WINDOW
    l = 0

    pre_g = pre_norm_g[l].reshape(1, D_MODEL)
    post_g = post_norm_g[l].reshape(1, D_MODEL)
    w_in_t = w_in[l].astype(jnp.bfloat16).T
    sinks = attn_sinks[l].astype(jnp.float32)

    y_s, conv_s, kt_s, vt_s, w_out_bf = _sample_layer(
        x_sample, state_conv[l],
        _heads_last_to_keys_last(cache_swa_k[l]), _heads_last_to_keys_last(cache_swa_v[l]),
        _heads_last_to_keys_last(cache_mem_k[l]), _heads_last_to_keys_last(cache_mem_v[l]),
        pre_g, post_g, w_in_t, conv_w[l], sinks, w_out[l], ns=SAMPLE_NS, group=SAMPLE_GROUP)

    mkt, mvt, mkb, mvtb = _mem_kv(mem_prompt, mem_norm_g[l], w_mem_k[l], w_mem_v[l])
    y_p, conv_p, kt_p, vt_p = _prompt_layer(
        x_prompt, pre_g, post_g, w_in_t, conv_w[l].T, sinks, mkb, mvtb, w_out_bf, conv_s,
        tq=PROMPT_TQ, nsub=PROMPT_NSUB)

    return (y_p, y_s,
            conv_p[None],
            _keys_last_to_heads_last(kt_p, N_SWA_KV), _keys_last_to_heads_last(vt_p, N_SWA_KV),
            _keys_last_to_heads_last(mkt, N_MEM_HEADS), _keys_last_to_heads_last(mvt, N_MEM_HEADS),
            conv_s[None],
            _keys_last_to_heads_last(kt_s, N_SWA_KV), _keys_last_to_heads_last(vt_s, N_SWA_KV))
```
